```python
import jax, jax.numpy as jnp
from jax import lax
import numpy as np

D_MODEL = 1024
BATCH = 8
SEQ = 2048
DEPTH = 1
DEC_BATCH = 128
DEC_SEQ = 4
PAST_LEN = 16384
PAGE_SIZE = 128

MIX_W = D_MODEL
MLSTM_W = MIX_W // 2
NH_M = 4
DV = MLSTM_W // NH_M
DK = DV // 2
CONV_CH = MIX_W - MLSTM_W
CONV_W = 3
D_FF = ((8 * D_MODEL + 3 * 256 - 1) // (3 * 256)) * 256
PLE_DIM = 256
CHUNK = 64
GATE_CAP = 15.0
EPS = 1e-6
PROJ_W = 2 * NH_M * DK + MLSTM_W + 2 * NH_M + MLSTM_W + 3 * CONV_CH

kernel_name = "hymba_mlstm_shortconv_decode_step"


def rmsnorm(x, g):
    xf = x.astype(jnp.float32)
    y = xf * lax.rsqrt(jnp.mean(xf * xf, axis=-1, keepdims=True) + EPS)
    return (y * g.astype(jnp.float32)).astype(x.dtype)


def mlstm_chunkwise(q, k, v, ig, lf, C0, n0, m0, chunk):
    Bn, H, S, _ = q.shape
    nc = S // chunk

    def to_chunks(a):
        a = a.reshape(a.shape[:2] + (nc, chunk) + a.shape[3:])
        return jnp.moveaxis(a, 2, 0)

    xs = tuple(to_chunks(a) for a in (q, k, v, ig, lf))
    causal = jnp.tril(jnp.ones((chunk, chunk), dtype=bool))

    def step(carry, inp):
        C, n, m = carry
        qc, kc, vc, ic, fc = inp
        b = jnp.cumsum(fc, axis=-1)
        g = b + m[..., None]
        Dm = b[..., :, None] - b[..., None, :] + ic[..., None, :]
        Dm = jnp.where(causal, Dm, -jnp.inf)
        mt = jnp.maximum(g, jnp.max(Dm, axis=-1))
        w_inter = jnp.exp(g - mt)
        s = jnp.einsum('bhtd,bhsd->bhts', qc, kc) * jnp.exp(Dm - mt[..., None])
        num = (w_inter[..., None] * jnp.einsum('bhtd,bhde->bhte', qc, C)
               + jnp.einsum('bhts,bhse->bhte', s, vc))
        den = w_inter * jnp.einsum('bhtd,bhd->bht', qc, n) + jnp.sum(s, axis=-1)
        h = num / jnp.maximum(jnp.abs(den), jnp.exp(-mt))[..., None]
        bL = b[..., -1]
        dL = bL[..., None] - b + ic
        m_new = jnp.maximum(bL + m, jnp.max(dL, axis=-1))
        a_st = jnp.exp(bL + m - m_new)
        wk = jnp.exp(dL - m_new[..., None])
        C_new = a_st[..., None, None] * C + jnp.einsum('bhs,bhsd,bhse->bhde', wk, kc, vc)
        n_new = a_st[..., None] * n + jnp.einsum('bhs,bhsd->bhd', wk, kc)
        return (C_new, n_new, m_new), h

    carry0 = (C0.astype(jnp.float32), n0.astype(jnp.float32), m0.astype(jnp.float32))
    (C, n, m), hs = lax.scan(step, carry0, xs)
    h = jnp.moveaxis(hs, 0, 2).reshape(Bn, H, S, v.shape[-1])
    return h, C, n, m


def hybrid_layer(x, pe, C0, n0, m0, conv0, chunk, g_mix, w_in, b_ig, b_fg, g_mh, conv_w, w_out,
                 g_ffn, w_gate, w_up, w_down, w_ple, g_ple, w_pg):
    Bn, S, _ = x.shape
    a = rmsnorm(x, g_mix)
    u = a @ w_in
    sizes = [NH_M * DK, NH_M * DK, MLSTM_W, NH_M, NH_M, MLSTM_W, CONV_CH, CONV_CH, CONV_CH]
    idx = [int(c) for c in np.cumsum(sizes)[:-1]]
    q, k, v, ig, fg, og, bg, cg, hc = jnp.split(u, idx, axis=-1)

    def heads(t, d):
        return t.reshape(Bn, S, NH_M, d).transpose(0, 2, 1, 3).astype(jnp.float32)

    qh = heads(q, DK) * (DK ** -0.5)
    kh = heads(k, DK)
    vh = heads(v, DV)
    igh = (GATE_CAP * jnp.tanh((ig + b_ig).astype(jnp.float32) / GATE_CAP)).transpose(0, 2, 1)
    lfh = jax.nn.log_sigmoid((fg + b_fg).astype(jnp.float32)).transpose(0, 2, 1)
    h, C, n, m = mlstm_chunkwise(qh, kh, vh, igh, lfh, C0, n0, m0, chunk)
    h = h.transpose(0, 2, 1, 3)
    h = h * lax.rsqrt(jnp.mean(h * h, axis=-1, keepdims=True) + EPS)
    h = h.reshape(Bn, S, MLSTM_W) * g_mh.astype(jnp.float32)
    h_m = (jax.nn.sigmoid(og.astype(jnp.float32)) * h).astype(x.dtype)

    z = cg * hc
    zp = jnp.concatenate([conv0.astype(z.dtype), z], axis=1)
    yconv = conv_w[0] * zp[:, 0:S]
    for j in range(1, CONV_W):
        yconv = yconv + conv_w[j] * zp[:, j:j + S]
    y_c = (bg * yconv).astype(x.dtype)
    new_conv = zp[:, S:]

    x = x + jnp.concatenate([h_m, y_c], axis=-1) @ w_out
    f = rmsnorm(x, g_ffn)
    x = x + (jax.nn.silu(f @ w_gate) * (f @ w_up)) @ w_down
    e = rmsnorm(pe @ w_ple, g_ple)
    x = x + jax.nn.sigmoid(x @ w_pg) * e
    return x, C, n, m, new_conv


def setup_inputs(seed: int = 0) -> dict:
    key = jax.random.key(seed)
    ks = jax.random.split(key, 24)
    nrm = jax.random.normal
    f32 = jnp.float32
    d = {}
    d["x_prompt"] = nrm(ks[0], (BATCH, SEQ, D_MODEL), f32)
    d["x_sample"] = nrm(ks[1], (DEC_BATCH, DEC_SEQ, D_MODEL), f32)
    d["p_prompt"] = nrm(ks[2], (DEPTH, BATCH, SEQ, PLE_DIM), f32)
    d["p_sample"] = nrm(ks[3], (DEPTH, DEC_BATCH, DEC_SEQ, PLE_DIM), f32)
    d["state_C"] = 0.5 * nrm(ks[4], (DEPTH, DEC_BATCH, NH_M, DK, DV), f32)
    d["state_n"] = 0.5 * nrm(ks[5], (DEPTH, DEC_BATCH, NH_M, DK), f32)
    d["state_m"] = 1.0 + 0.5 * nrm(ks[6], (DEPTH, DEC_BATCH, NH_M), f32)
    d["state_conv"] = nrm(ks[7], (DEPTH, DEC_BATCH, CONV_W - 1, CONV_CH), f32)
    d["g_mix"] = 1.0 + 0.02 * nrm(ks[8], (DEPTH, D_MODEL), f32)
    d["w_in"] = nrm(ks[9], (DEPTH, D_MODEL, PROJ_W), f32) * D_MODEL ** -0.5
    d["b_ig"] = 0.1 * nrm(ks[10], (DEPTH, NH_M), f32)
    d["b_fg"] = 3.0 + 0.1 * nrm(ks[11], (DEPTH, NH_M), f32)
    d["g_mh"] = 1.0 + 0.02 * nrm(ks[12], (DEPTH, MLSTM_W), f32)
    d["conv_w"] = nrm(ks[13], (DEPTH, CONV_W, CONV_CH), f32) * CONV_W ** -0.5
    d["w_out"] = nrm(ks[14], (DEPTH, MIX_W, D_MODEL), f32) * MIX_W ** -0.5
    d["g_ffn"] = 1.0 + 0.02 * nrm(ks[15], (DEPTH, D_MODEL), f32)
    d["w_gate"] = nrm(ks[16], (DEPTH, D_MODEL, D_FF), f32) * D_MODEL ** -0.5
    d["w_up"] = nrm(ks[17], (DEPTH, D_MODEL, D_FF), f32) * D_MODEL ** -0.5
    d["w_down"] = nrm(ks[18], (DEPTH, D_FF, D_MODEL), f32) * D_FF ** -0.5
    d["w_ple"] = nrm(ks[19], (DEPTH, PLE_DIM, D_MODEL), f32) * PLE_DIM ** -0.5
    d["g_ple"] = 1.0 + 0.02 * nrm(ks[20], (DEPTH, D_MODEL), f32)
    d["w_pg"] = nrm(ks[21], (DEPTH, D_MODEL, D_MODEL), f32) * D_MODEL ** -0.5
    d["g_final"] = 1.0 + 0.02 * nrm(ks[22], (D_MODEL,), f32)
    return d


def reference(x_prompt, x_sample, p_prompt, p_sample, state_C, state_n, state_m, state_conv,
              g_mix, w_in, b_ig, b_fg, g_mh, conv_w, w_out, g_ffn, w_gate, w_up, w_down,
              w_ple, g_ple, w_pg, g_final):
    xp, xs = x_prompt, x_sample
    Bp = xp.shape[0]
    Sp = xp.shape[1]
    chunk_p = min(CHUNK, Sp)
    chunk_s = xs.shape[1]
    Cp_l, np_l, mp_l, cvp_l = [], [], [], []
    Cs_l, ns_l, ms_l, cvs_l = [], [], [], []
    for i in range(DEPTH):
        w = (g_mix[i], w_in[i], b_ig[i], b_fg[i], g_mh[i], conv_w[i], w_out[i], g_ffn[i],
             w_gate[i], w_up[i], w_down[i], w_ple[i], g_ple[i], w_pg[i])
        C0 = jnp.zeros((Bp, NH_M, DK, DV), jnp.float32)
        n0 = jnp.zeros((Bp, NH_M, DK), jnp.float32)
        m0 = jnp.zeros((Bp, NH_M), jnp.float32)
        cv0 = jnp.zeros((Bp, CONV_W - 1, CONV_CH), xp.dtype)
        xp, Cp, n_p, mp, cvp = hybrid_layer(xp, p_prompt[i], C0, n0, m0, cv0, chunk_p, *w)
        xs, Cs, n_s, ms, cvs = hybrid_layer(xs, p_sample[i], state_C[i], state_n[i], state_m[i],
                                            state_conv[i], chunk_s, *w)
        Cp_l.append(Cp); np_l.append(n_p); mp_l.append(mp); cvp_l.append(cvp)
        Cs_l.append(Cs); ns_l.append(n_s); ms_l.append(ms); cvs_l.append(cvs)
    y_prompt = rmsnorm(xp, g_final)
    y_sample = rmsnorm(xs, g_final)
    C_prompt = jnp.stack(Cp_l); n_prompt = jnp.stack(np_l); m_prompt = jnp.stack(mp_l); conv_prompt = jnp.stack(cvp_l)
    C_sample = jnp.stack(Cs_l); n_sample = jnp.stack(ns_l); m_sample = jnp.stack(ms_l); conv_sample = jnp.stack(cvs_l)
    return (y_prompt, y_sample, C_prompt, n_prompt, m_prompt, conv_prompt, C_sample, n_sample, m_sample, conv_sample)
```

```python
import functools

import jax
import jax.numpy as jnp
from jax import lax
from jax.experimental import pallas as pl
from jax.experimental.pallas import tpu as pltpu

D_MODEL = 1024
NH = 4
DK = 64
DV = 128
MLSTM_W = NH * DV
QK_W = NH * DK
CONV_CH = 512
CONV_W = 3
D_FF = 2816
PLE_DIM = 256
PROMPT_CHUNK = 64
GATE_CAP = 15.0
EPS = 1e-6

LANES = 128
SUBLANES = 8
NEG_BIG = -1e30

_Q0, _K0, _V0, _OG0, _BG0, _CG0, _HC0, _GT0 = 0, 256, 512, 1024, 1536, 2048, 2560, 3072
PROJ_COLS = _GT0 + 2 * LANES

SAMPLE_PAD = SUBLANES
SAMPLE_LEAD = 2

VMEM_LIMIT = 56 * 1024 * 1024


def _rms(x, g):
    ms = jnp.mean(x * x, axis=-1, keepdims=True)
    return x * lax.rsqrt(ms + EPS) * g


def _seg_cumsum(x, seg):
    pos = lax.broadcasted_iota(jnp.int32, x.shape, 0) & (seg - 1)
    k = 1
    while k < seg:
        x = x + jnp.where(pos >= k, pltpu.roll(x, k, 0), 0.0)
        k *= 2
    return x


def _bdot(a, b):
    return jnp.dot(a, b, preferred_element_type=jnp.float32)


def _mixer_kernel(*refs, L, TS, sample):
    if sample:
        (x_ref, cpad_ref, c0_ref, n0_ref, m0_ref,
         gmix_ref, win_ref, gb_ref, gmh_ref, cw_ref, wout_ref,
         x1_ref, cst_ref, nst_ref, mst_ref, cvo_ref,
         q_s, k_s, v_s, og_s, b_s, ic_s, mix_s, z_s) = refs
    else:
        (x_ref,
         gmix_ref, win_ref, gb_ref, gmh_ref, cw_ref, wout_ref,
         x1_ref, cst_ref, nst_ref, mst_ref, cvo_ref,
         q_s, k_s, v_s, og_s, b_s, ic_s, mix_s, z_s) = refs
        c0_ref, n0_ref, m0_ref = cst_ref, nst_ref, mst_ref
    nc = TS // L
    j = pl.program_id(1) if not sample else None

    x = x_ref[0] if not sample else x_ref[...]
    a = _rms(x, gmix_ref[...]).astype(jnp.bfloat16)

    q_s[...] = _bdot(a, win_ref[:, _Q0:_K0]) * (DK ** -0.5)
    k_s[...] = _bdot(a, win_ref[:, _K0:_V0])
    v_s[...] = _bdot(a, win_ref[:, _V0:_OG0])
    og_s[...] = _bdot(a, win_ref[:, _OG0:_BG0])

    z = _bdot(a, win_ref[:, _CG0:_HC0]) * _bdot(a, win_ref[:, _HC0:_GT0])
    row = lax.broadcasted_iota(jnp.int32, (TS, 1), 0)
    if sample:
        pos = row & (SAMPLE_PAD - 1)
        real = (pos >= SAMPLE_LEAD) & (pos < SAMPLE_LEAD + 4)
        z = jnp.where(real, z, cpad_ref[...])
        z_s[0:SUBLANES, :] = jnp.zeros((SUBLANES, CONV_CH), jnp.float32)
    else:
        @pl.when(j == 0)
        def _():
            z_s[0:SUBLANES, :] = jnp.zeros((SUBLANES, CONV_CH), jnp.float32)
    z_s[SUBLANES:SUBLANES + TS, :] = z
    yconv = (cw_ref[0:1, :] * z_s[SUBLANES - 2:SUBLANES - 2 + TS, :]
             + cw_ref[1:2, :] * z_s[SUBLANES - 1:SUBLANES - 1 + TS, :]
             + cw_ref[2:3, :] * z)
    mix_s[:, MLSTM_W:] = _bdot(a, win_ref[:, _BG0:_CG0]) * yconv
    z_s[0:SUBLANES, :] = z[TS - SUBLANES:, :]
    if sample:
        cvo_ref[...] = z
    else:
        cvo_ref[0] = z[TS - SUBLANES:, :]

    gates = _bdot(a, win_ref[:, _GT0:PROJ_COLS]) + gb_ref[...]
    ic = GATE_CAP * jnp.tanh(gates[:, :LANES] / GATE_CAP)
    fpre = gates[:, LANES:]
    lf = jnp.minimum(fpre, 0.0) - jnp.log1p(jnp.exp(-jnp.abs(fpre)))
    if sample:
        ic = jnp.where(real, ic, NEG_BIG)
        lf = jnp.where(real, lf, 0.0)
    b_s[...] = _seg_cumsum(lf, L)
    ic_s[...] = ic

    if not sample:
        @pl.when(j == 0)
        def _():
            cst_ref[...] = jnp.zeros(cst_ref.shape, jnp.float32)
            nst_ref[...] = jnp.zeros(nst_ref.shape, jnp.float32)
            mst_ref[...] = jnp.zeros(mst_ref.shape, jnp.float32)

    ti = lax.broadcasted_iota(jnp.int32, (L, L), 0)
    si = lax.broadcasted_iota(jnp.int32, (L, L), 1)
    eye = ti == si
    causal = si <= ti

    def chunk(c, carry):
        i = c if sample else 0
        r0 = pl.multiple_of(c * L, L)
        rows = pl.ds(r0, L)
        bt = b_s[rows, :]
        ict = ic_s[rows, :]
        b_last = b_s[pl.ds(r0 + L - 1, 1), :]
        d_last = b_last - bt + ict
        m_loc2 = jnp.max(d_last, axis=0, keepdims=True)
        wk_loc = jnp.exp(d_last - m_loc2)
        m_prev = m0_ref[i]
        m_new = jnp.maximum(b_last + m_prev, m_loc2)
        a_st = jnp.exp(b_last + m_prev - m_new)
        sc = jnp.exp(m_loc2 - m_new)
        g = bt + m_prev
        for h in range(NH):
            qh = q_s[rows, h * DK:(h + 1) * DK]
            kh = k_s[rows, h * DK:(h + 1) * DK]
            vh = v_s[rows, h * DV:(h + 1) * DV].astype(jnp.bfloat16)
            qb = qh.astype(jnp.bfloat16)
            bh = bt[:, h:h + 1]
            rh = bh - ict[:, h:h + 1]
            r_row = jnp.sum(jnp.where(eye, rh, 0.0), axis=0, keepdims=True)
            dm = jnp.where(causal, bh - r_row, -jnp.inf)
            m_loc = jnp.max(dm, axis=1, keepdims=True)
            qk = lax.dot_general(qb, kh.astype(jnp.bfloat16), (((1,), (1,)), ((), ())),
                                 preferred_element_type=jnp.float32)
            p = qk * jnp.exp(dm - m_loc)
            rs = jnp.sum(p, axis=1, keepdims=True)
            pv = _bdot(p.astype(jnp.bfloat16), vh)
            gh = g[:, h:h + 1]
            mt = jnp.maximum(gh, m_loc)
            wi = jnp.exp(gh - mt)
            wl = jnp.exp(m_loc - mt)
            c_h = c0_ref[i, h]
            n_h = n0_ref[i, h:h + 1, :]
            qc = _bdot(qb, c_h.astype(jnp.bfloat16))
            qn = jnp.sum(qh * n_h, axis=1, keepdims=True)
            num = wi * qc + wl * pv
            den = wi * qn + wl * rs
            hh = num / jnp.maximum(jnp.abs(den), jnp.exp(-mt))
            hn = (hh * lax.rsqrt(jnp.mean(hh * hh, axis=1, keepdims=True) + EPS)
                  * gmh_ref[:, h * DV:(h + 1) * DV])
            mix_s[rows, h * DV:(h + 1) * DV] = jax.nn.sigmoid(og_s[rows, h * DV:(h + 1) * DV]) * hn
            kw = kh * wk_loc[:, h:h + 1]
            kv = lax.dot_general(kw.astype(jnp.bfloat16), vh, (((0,), (0,)), ((), ())),
                                 preferred_element_type=jnp.float32)
            ks = jnp.sum(kw, axis=0, keepdims=True)
            cst_ref[i, h] = a_st[:, h:h + 1] * c_h + sc[:, h:h + 1] * kv
            nst_ref[i, h:h + 1, :] = a_st[:, h:h + 1] * n_h + sc[:, h:h + 1] * ks
        mst_ref[i] = m_new
        return carry

    lax.fori_loop(0, nc, chunk, 0)

    out = x + _bdot(mix_s[...].astype(jnp.bfloat16), wout_ref[...])
    if sample:
        x1_ref[...] = out
    else:
        x1_ref[0] = out


def _const_spec(shape):
    nd = len(shape)
    return pl.BlockSpec(shape, lambda *_: (0,) * nd, pipeline_mode=pl.Buffered(1))


def _mixer_scratch(TS):
    f32 = jnp.float32
    return [
        pltpu.VMEM((TS, QK_W), f32),
        pltpu.VMEM((TS, QK_W), f32),
        pltpu.VMEM((TS, MLSTM_W), f32),
        pltpu.VMEM((TS, MLSTM_W), f32),
        pltpu.VMEM((TS, LANES), f32),
        pltpu.VMEM((TS, LANES), f32),
        pltpu.VMEM((TS, D_MODEL), f32),
        pltpu.VMEM((TS + SUBLANES, CONV_CH), f32),
    ]


def _weight_specs():
    return [
        _const_spec((1, D_MODEL)),
        _const_spec((D_MODEL, PROJ_COLS)),
        _const_spec((1, 2 * LANES)),
        _const_spec((1, MLSTM_W)),
        _const_spec((CONV_W, CONV_CH)),
        _const_spec((D_MODEL, D_MODEL)),
    ]


def _mixer_prompt(x, weights, TS):
    B, S, _ = x.shape
    nb = S // TS
    f32 = jnp.float32
    kern = functools.partial(_mixer_kernel, L=PROMPT_CHUNK, TS=TS, sample=False)
    return pl.pallas_call(
        kern,
        grid=(B, nb),
        in_specs=[pl.BlockSpec((1, TS, D_MODEL), lambda b, j: (b, j, 0))] + _weight_specs(),
        out_specs=[
            pl.BlockSpec((1, TS, D_MODEL), lambda b, j: (b, j, 0)),
            pl.BlockSpec((1, NH, DK, DV), lambda b, j: (b, 0, 0, 0)),
            pl.BlockSpec((1, NH, DK), lambda b, j: (b, 0, 0)),
            pl.BlockSpec((1, 1, LANES), lambda b, j: (b, 0, 0)),
            pl.BlockSpec((1, SUBLANES, CONV_CH), lambda b, j: (b, 0, 0)),
        ],
        out_shape=[
            jax.ShapeDtypeStruct((B, S, D_MODEL), f32),
            jax.ShapeDtypeStruct((B, NH, DK, DV), f32),
            jax.ShapeDtypeStruct((B, NH, DK), f32),
            jax.ShapeDtypeStruct((B, 1, LANES), f32),
            jax.ShapeDtypeStruct((B, SUBLANES, CONV_CH), f32),
        ],
        scratch_shapes=_mixer_scratch(TS),
        compiler_params=pltpu.CompilerParams(
            dimension_semantics=("arbitrary", "arbitrary"), vmem_limit_bytes=VMEM_LIMIT),
        name="mixer_prompt",
    )(x, *weights)


def _mixer_sample(xpad, cpad, c0, n0, m0, weights, TS):
    T = xpad.shape[0]
    nseq = TS // SAMPLE_PAD
    B = T // SAMPLE_PAD
    f32 = jnp.float32
    kern = functools.partial(_mixer_kernel, L=SAMPLE_PAD, TS=TS, sample=True)
    c_spec = pl.BlockSpec((nseq, NH, DK, DV), lambda t: (t, 0, 0, 0))
    n_spec = pl.BlockSpec((nseq, NH, DK), lambda t: (t, 0, 0))
    m_spec = pl.BlockSpec((nseq, 1, LANES), lambda t: (t, 0, 0))
    return pl.pallas_call(
        kern,
        grid=(T // TS,),
        in_specs=[
            pl.BlockSpec((TS, D_MODEL), lambda t: (t, 0)),
            pl.BlockSpec((TS, CONV_CH), lambda t: (t, 0)),
            c_spec, n_spec, m_spec,
        ] + _weight_specs(),
        out_specs=[
            pl.BlockSpec((TS, D_MODEL), lambda t: (t, 0)),
            c_spec, n_spec, m_spec,
            pl.BlockSpec((TS, CONV_CH), lambda t: (t, 0)),
        ],
        out_shape=[
            jax.ShapeDtypeStruct((T, D_MODEL), f32),
            jax.ShapeDtypeStruct((B, NH, DK, DV), f32),
            jax.ShapeDtypeStruct((B, NH, DK), f32),
            jax.ShapeDtypeStruct((B, 1, LANES), f32),
            jax.ShapeDtypeStruct((T, CONV_CH), f32),
        ],
        scratch_shapes=_mixer_scratch(TS),
        compiler_params=pltpu.CompilerParams(
            dimension_semantics=("arbitrary",), vmem_limit_bytes=VMEM_LIMIT),
        name="mixer_sample",
    )(xpad, cpad, c0, n0, m0, *weights)


def _ffn_kernel(x_ref, pe_ref, gffn_ref, wg_ref, wu_ref, wd_ref, wple_ref, gple_ref, wpg_ref,
                gfin_ref, y_ref, *, final_norm):
    x = x_ref[...]
    f = _rms(x, gffn_ref[...]).astype(jnp.bfloat16)
    gate = _bdot(f, wg_ref[...])
    up = _bdot(f, wu_ref[...])
    hmid = (gate * jax.nn.sigmoid(gate) * up).astype(jnp.bfloat16)
    x = x + _bdot(hmid, wd_ref[...])
    e = _rms(_bdot(pe_ref[...].astype(jnp.bfloat16), wple_ref[...]), gple_ref[...])
    x = x + jax.nn.sigmoid(_bdot(x.astype(jnp.bfloat16), wpg_ref[...])) * e
    if final_norm:
        x = _rms(x, gfin_ref[...])
    y_ref[...] = x


def _ffn(x, pe, weights, TM, final_norm):
    T = x.shape[0]
    kern = functools.partial(_ffn_kernel, final_norm=final_norm)
    return pl.pallas_call(
        kern,
        grid=(T // TM,),
        in_specs=[
            pl.BlockSpec((TM, D_MODEL), lambda t: (t, 0)),
            pl.BlockSpec((TM, PLE_DIM), lambda t: (t, 0)),
            _const_spec((1, D_MODEL)),
            _const_spec((D_MODEL, D_FF)),
            _const_spec((D_MODEL, D_FF)),
            _const_spec((D_FF, D_MODEL)),
            _const_spec((PLE_DIM, D_MODEL)),
            _const_spec((1, D_MODEL)),
            _const_spec((D_MODEL, D_MODEL)),
            _const_spec((1, D_MODEL)),
        ],
        out_specs=pl.BlockSpec((TM, D_MODEL), lambda t: (t, 0)),
        out_shape=jax.ShapeDtypeStruct((T, D_MODEL), jnp.float32),
        compiler_params=pltpu.CompilerParams(
            dimension_semantics=("arbitrary",), vmem_limit_bytes=VMEM_LIMIT),
        name="ffn",
    )(x, pe, *weights)


def _prep_mixer_weights(g_mix, w_in, b_ig, b_fg, g_mh, conv_w, w_out):
    bf16 = jnp.bfloat16
    ig0 = 2 * QK_W + MLSTM_W
    fg0 = ig0 + NH
    og0 = fg0 + NH
    gate_cols = jnp.zeros((D_MODEL, 2 * LANES), w_in.dtype)
    gate_cols = gate_cols.at[:, 0:NH].set(w_in[:, ig0:fg0])
    gate_cols = gate_cols.at[:, LANES:LANES + NH].set(w_in[:, fg0:og0])
    w_all = jnp.concatenate([w_in[:, :ig0], w_in[:, og0:], gate_cols], axis=1).astype(bf16)
    gb = jnp.zeros((1, 2 * LANES), jnp.float32)
    gb = gb.at[0, 0:NH].set(b_ig).at[0, LANES:LANES + NH].set(b_fg)
    return (g_mix.reshape(1, D_MODEL), w_all, gb, g_mh.reshape(1, MLSTM_W), conv_w,
            w_out.astype(bf16))


def kernel(x_prompt, x_sample, p_prompt, p_sample, state_C, state_n, state_m, state_conv,
           g_mix, w_in, b_ig, b_fg, g_mh, conv_w, w_out, g_ffn, w_gate, w_up, w_down,
           w_ple, g_ple, w_pg, g_final):
    bf16 = jnp.bfloat16
    depth = g_mix.shape[0]
    B, S, _ = x_prompt.shape
    Bs, Ss, _ = x_sample.shape
    assert Ss == 4 and S % PROMPT_CHUNK == 0
    lead, tail = SAMPLE_LEAD, SAMPLE_PAD - SAMPLE_LEAD - Ss

    xp = x_prompt
    xs = x_sample
    outs = [[] for _ in range(8)]
    for i in range(depth):
        mw = _prep_mixer_weights(g_mix[i], w_in[i], b_ig[i], b_fg[i], g_mh[i], conv_w[i], w_out[i])
        last = i == depth - 1
        fw = (g_ffn[i].reshape(1, D_MODEL), w_gate[i].astype(bf16), w_up[i].astype(bf16),
              w_down[i].astype(bf16), w_ple[i].astype(bf16), g_ple[i].reshape(1, D_MODEL),
              w_pg[i].astype(bf16), g_final.reshape(1, D_MODEL))

        x1p, cp, n_p, mp, cvp = _mixer_prompt(xp, mw, TS=256)
        xp = _ffn(x1p.reshape(B * S, D_MODEL), p_prompt[i].reshape(B * S, PLE_DIM), fw, 256,
                  last).reshape(B, S, D_MODEL)

        xs_pad = jnp.pad(xs, ((0, 0), (lead, tail), (0, 0))).reshape(Bs * SAMPLE_PAD, D_MODEL)
        cpad = jnp.pad(state_conv[i], ((0, 0), (0, SAMPLE_PAD - (CONV_W - 1)), (0, 0)))
        cpad = cpad.reshape(Bs * SAMPLE_PAD, CONV_CH)
        m0 = jnp.pad(state_m[i], ((0, 0), (0, LANES - NH))).reshape(Bs, 1, LANES)
        x1s, cs, n_s, ms, cvs = _mixer_sample(xs_pad, cpad, state_C[i], state_n[i], m0, mw, TS=256)
        x1s = x1s.reshape(Bs, SAMPLE_PAD, D_MODEL)[:, lead:lead + Ss].reshape(Bs * Ss, D_MODEL)
        xs = _ffn(x1s, p_sample[i].reshape(Bs * Ss, PLE_DIM), fw, 256, last).reshape(Bs, Ss, D_MODEL)

        new = (cp, n_p, mp[:, 0, :NH], cvp[:, SUBLANES - (CONV_W - 1):],
               cs, n_s, ms[:, 0, :NH],
               cvs.reshape(Bs, SAMPLE_PAD, CONV_CH)[:, lead + Ss - (CONV_W - 1):lead + Ss])
        for lst, v in zip(outs, new):
            lst.append(v)

    return (xp, xs) + tuple(jnp.stack(l) for l in outs)
```

```python
import functools

import jax
import jax.numpy as jnp
from jax import lax
from jax.experimental import pallas as pl
from jax.experimental.pallas import tpu as pltpu

D_MODEL = 1024
NH = 4
DK = 64
DV = 128
MLSTM_W = NH * DV
QK_W = NH * DK
CONV_CH = 512
CONV_W = 3
D_FF = 2816
PLE_DIM = 256
PROMPT_CHUNK = 64
PROMPT_L = 128
PROMPT_TS = 512
GATE_CAP = 15.0
EPS = 1e-6

LANES = 128
SUBLANES = 8
NEG_BIG = -1e30

_Q0, _K0, _V0, _OG0, _BG0, _CG0, _HC0, _GT0 = 0, 256, 512, 1024, 1536, 2048, 2560, 3072
PROJ_COLS = _GT0 + 2 * LANES

SAMPLE_PAD = SUBLANES
SAMPLE_LEAD = 2

VMEM_LIMIT = 56 * 1024 * 1024


def _rms(x, g):
    ms = jnp.mean(x * x, axis=-1, keepdims=True)
    return x * lax.rsqrt(ms + EPS) * g


def _seg_cumsum(x, seg):
    pos = lax.broadcasted_iota(jnp.int32, x.shape, 0) & (seg - 1)
    k = 1
    while k < seg:
        x = x + jnp.where(pos >= k, pltpu.roll(x, k, 0), 0.0)
        k *= 2
    return x


def _bdot(a, b):
    return jnp.dot(a, b, preferred_element_type=jnp.float32)


def _mixer_kernel(*refs, L, TS, sample):
    if sample:
        (x_ref, cpad_ref, c0_ref, n0_ref, m0_ref,
         gmix_ref, win_ref, gb_ref, gmh_ref, cw_ref, wout_ref,
         x1_ref, cst_ref, nst_ref, mst_ref, cvo_ref,
         q_s, k_s, v_s, og_s, b_s, ic_s, mix_s, z_s) = refs
    else:
        (x_ref,
         gmix_ref, win_ref, gb_ref, gmh_ref, cw_ref, wout_ref,
         x1_ref, cst_ref, nst_ref, mst_ref, cvo_ref,
         q_s, k_s, v_s, og_s, b_s, ic_s, mix_s, z_s) = refs
        c0_ref, n0_ref, m0_ref = cst_ref, nst_ref, mst_ref
    nc = TS // L
    j = pl.program_id(1) if not sample else None

    x = x_ref[0] if not sample else x_ref[...]
    a = _rms(x, gmix_ref[...]).astype(jnp.bfloat16)

    q_s[...] = _bdot(a, win_ref[:, _Q0:_K0]) * (DK ** -0.5)
    k_s[...] = _bdot(a, win_ref[:, _K0:_V0])
    v_s[...] = _bdot(a, win_ref[:, _V0:_OG0])
    og_s[...] = _bdot(a, win_ref[:, _OG0:_BG0])

    z = _bdot(a, win_ref[:, _CG0:_HC0]) * _bdot(a, win_ref[:, _HC0:_GT0])
    row = lax.broadcasted_iota(jnp.int32, (TS, 1), 0)
    if sample:
        pos = row & (SAMPLE_PAD - 1)
        real = (pos >= SAMPLE_LEAD) & (pos < SAMPLE_LEAD + 4)
        z = jnp.where(real, z, cpad_ref[...])
        z_s[0:SUBLANES, :] = jnp.zeros((SUBLANES, CONV_CH), jnp.float32)
    else:
        @pl.when(j == 0)
        def _():
            z_s[0:SUBLANES, :] = jnp.zeros((SUBLANES, CONV_CH), jnp.float32)
    z_s[SUBLANES:SUBLANES + TS, :] = z
    yconv = (cw_ref[0:1, :] * z_s[SUBLANES - 2:SUBLANES - 2 + TS, :]
             + cw_ref[1:2, :] * z_s[SUBLANES - 1:SUBLANES - 1 + TS, :]
             + cw_ref[2:3, :] * z)
    mix_s[:, MLSTM_W:] = _bdot(a, win_ref[:, _BG0:_CG0]) * yconv
    z_s[0:SUBLANES, :] = z[TS - SUBLANES:, :]
    if sample:
        cvo_ref[...] = z
    else:
        cvo_ref[0] = z[TS - SUBLANES:, :]

    gates = _bdot(a, win_ref[:, _GT0:PROJ_COLS]) + gb_ref[...]
    ic = GATE_CAP * jnp.tanh(gates[:, :LANES] / GATE_CAP)
    fpre = gates[:, LANES:]
    lf = jnp.minimum(fpre, 0.0) - jnp.log1p(jnp.exp(-jnp.abs(fpre)))
    if sample:
        ic = jnp.where(real, ic, NEG_BIG)
        lf = jnp.where(real, lf, 0.0)
    b_s[...] = _seg_cumsum(lf, L)
    ic_s[...] = ic

    if not sample:
        @pl.when(j == 0)
        def _():
            cst_ref[...] = jnp.zeros(cst_ref.shape, jnp.float32)
            nst_ref[...] = jnp.zeros(nst_ref.shape, jnp.float32)
            mst_ref[...] = jnp.zeros(mst_ref.shape, jnp.float32)

    ti = lax.broadcasted_iota(jnp.int32, (L, L), 0)
    si = lax.broadcasted_iota(jnp.int32, (L, L), 1)
    eye = ti == si
    causal = si <= ti

    def chunk(c, carry):
        i = c if sample else 0
        r0 = pl.multiple_of(c * L, L)
        rows = pl.ds(r0, L)
        bt = b_s[rows, :]
        ict = ic_s[rows, :]
        b_last = b_s[pl.ds(r0 + L - 1, 1), :]
        d_last = b_last - bt + ict
        m_loc2 = jnp.max(d_last, axis=0, keepdims=True)
        wk_loc = jnp.exp(d_last - m_loc2)
        m_prev = m0_ref[i]
        m_new = jnp.maximum(b_last + m_prev, m_loc2)
        a_st = jnp.exp(b_last + m_prev - m_new)
        sc = jnp.exp(m_loc2 - m_new)
        g = bt + m_prev
        for h in range(NH):
            qh = q_s[rows, h * DK:(h + 1) * DK]
            kh = k_s[rows, h * DK:(h + 1) * DK]
            vh = v_s[rows, h * DV:(h + 1) * DV].astype(jnp.bfloat16)
            qb = qh.astype(jnp.bfloat16)
            bh = bt[:, h:h + 1]
            rh = bh - ict[:, h:h + 1]
            r_row = jnp.sum(jnp.where(eye, rh, 0.0), axis=0, keepdims=True)
            dm = jnp.where(causal, bh - r_row, -jnp.inf)
            m_loc = jnp.max(dm, axis=1, keepdims=True)
            qk = lax.dot_general(qb, kh.astype(jnp.bfloat16), (((1,), (1,)), ((), ())),
                                 preferred_element_type=jnp.float32)
            p = qk * jnp.exp(dm - m_loc)
            rs = jnp.sum(p, axis=1, keepdims=True)
            pv = _bdot(p.astype(jnp.bfloat16), vh)
            gh = g[:, h:h + 1]
            mt = jnp.maximum(gh, m_loc)
            wi = jnp.exp(gh - mt)
            wl = jnp.exp(m_loc - mt)
            c_h = c0_ref[i, h]
            n_h = n0_ref[i, h:h + 1, :]
            qc = _bdot(qb, c_h.astype(jnp.bfloat16))
            qn = jnp.sum(qh * n_h, axis=1, keepdims=True)
            num = wi * qc + wl * pv
            den = wi * qn + wl * rs
            hh = num / jnp.maximum(jnp.abs(den), jnp.exp(-mt))
            hn = (hh * lax.rsqrt(jnp.mean(hh * hh, axis=1, keepdims=True) + EPS)
                  * gmh_ref[:, h * DV:(h + 1) * DV])
            mix_s[rows, h * DV:(h + 1) * DV] = jax.nn.sigmoid(og_s[rows, h * DV:(h + 1) * DV]) * hn
            kw = kh * wk_loc[:, h:h + 1]
            kv = lax.dot_general(kw.astype(jnp.bfloat16), vh, (((0,), (0,)), ((), ())),
                                 preferred_element_type=jnp.float32)
            ks = jnp.sum(kw, axis=0, keepdims=True)
            cst_ref[i, h] = a_st[:, h:h + 1] * c_h + sc[:, h:h + 1] * kv
            nst_ref[i, h:h + 1, :] = a_st[:, h:h + 1] * n_h + sc[:, h:h + 1] * ks
        mst_ref[i] = m_new
        return carry

    lax.fori_loop(0, nc, chunk, 0)

    out = x + _bdot(mix_s[...].astype(jnp.bfloat16), wout_ref[...])
    if sample:
        x1_ref[...] = out
    else:
        x1_ref[0] = out


def _const_spec(shape):
    nd = len(shape)
    return pl.BlockSpec(shape, lambda *_: (0,) * nd, pipeline_mode=pl.Buffered(1))


def _mixer_scratch(TS):
    f32 = jnp.float32
    return [
        pltpu.VMEM((TS, QK_W), f32),
        pltpu.VMEM((TS, QK_W), f32),
        pltpu.VMEM((TS, MLSTM_W), f32),
        pltpu.VMEM((TS, MLSTM_W), f32),
        pltpu.VMEM((TS, LANES), f32),
        pltpu.VMEM((TS, LANES), f32),
        pltpu.VMEM((TS, D_MODEL), f32),
        pltpu.VMEM((TS + SUBLANES, CONV_CH), f32),
    ]


def _weight_specs():
    return [
        _const_spec((1, D_MODEL)),
        _const_spec((D_MODEL, PROJ_COLS)),
        _const_spec((1, 2 * LANES)),
        _const_spec((1, MLSTM_W)),
        _const_spec((CONV_W, CONV_CH)),
        _const_spec((D_MODEL, D_MODEL)),
    ]


_TQ0, _TV0, _TOG0, _TG0 = 0, 256, 768, 1280
TPROJ_ROWS = _TG0 + 2 * SUBLANES
_NK0, _NBG0, _NCG0, _NHC0, NPROJ_COLS = 0, 256, 768, 1280, 1792
ST_ROWS = DV + 2 * SUBLANES


def _lane_scan(x, seg, op, fill):
    pos = lax.broadcasted_iota(jnp.int32, x.shape, 1) & (seg - 1)
    k = 1
    while k < seg:
        x = op(x, jnp.where(pos >= k, pltpu.roll(x, k, 1), fill))
        k *= 2
    return x


def _mixer_prompt_kernel(x_ref, gmix_ref, wn_ref, wt_ref, gbt_ref, gmhc_ref, cw_ref, wout_ref,
                         x1_ref, st_ref, m_ref, cvo_ref,
                         qt_s, vt_s, ogt_s, k_s, mixt_s, z_s, ucol_s, wi_s, c2_s, emt_s, wk_s, ast_s,
                         *, L, TS):
    nc = TS // L
    j = pl.program_id(1)
    bf16 = jnp.bfloat16

    @pl.when(j == 0)
    def _():
        z_s[0:SUBLANES, :] = jnp.zeros((SUBLANES, CONV_CH), jnp.float32)
        st_ref[...] = jnp.zeros(st_ref.shape, jnp.float32)
        m_ref[...] = jnp.zeros(m_ref.shape, jnp.float32)

    x = x_ref[0]
    a = _rms(x, gmix_ref[...]).astype(bf16)
    nt_dims = (((1,), (1,)), ((), ()))

    gt = lax.dot_general(wt_ref[_TG0:TPROJ_ROWS, :], a, nt_dims,
                         preferred_element_type=jnp.float32) + gbt_ref[...]
    ic = GATE_CAP * jnp.tanh(gt[0:SUBLANES] / GATE_CAP)
    fpre = gt[SUBLANES:]
    lf = jnp.minimum(fpre, 0.0) - jnp.log1p(jnp.exp(-jnp.abs(fpre)))
    b = _lane_scan(lf, L, jnp.add, 0.0)
    u = ic - b
    m_loc = b + _lane_scan(u, L, jnp.maximum, -jnp.inf)

    un = _bdot(a, wn_ref[...])
    k_s[...] = un[:, _NK0:_NBG0]
    z = un[:, _NCG0:_NHC0] * un[:, _NHC0:NPROJ_COLS]
    z_s[SUBLANES:SUBLANES + TS, :] = z
    yconv = (cw_ref[0:1, :] * z_s[SUBLANES - 2:SUBLANES - 2 + TS, :]
             + cw_ref[1:2, :] * z_s[SUBLANES - 1:SUBLANES - 1 + TS, :]
             + cw_ref[2:3, :] * z)
    yc = (un[:, _NBG0:_NCG0] * yconv).astype(bf16)
    z_s[0:SUBLANES, :] = z[TS - SUBLANES:, :]
    cvo_ref[0] = z[TS - SUBLANES:, :]

    ut = lax.dot_general(wt_ref[0:_TG0, :], a, nt_dims, preferred_element_type=jnp.float32)
    qt_s[...] = ut[_TQ0:_TV0] * (DK ** -0.5)
    vt_s[...] = ut[_TV0:_TOG0]
    ogt_s[...] = ut[_TOG0:_TG0]

    m_prev = m_ref[0]
    for c in range(nc):
        sl = slice(c * L, (c + 1) * L)
        bc, mlc = b[:, sl], m_loc[:, sl]
        b_last = jnp.broadcast_to(bc[:, L - 1:L], bc.shape)
        m_new = jnp.maximum(b_last + m_prev, jnp.broadcast_to(mlc[:, L - 1:L], bc.shape))
        g = bc + m_prev
        mt = jnp.maximum(g, mlc)
        wi_s[:, sl] = jnp.exp(g - mt)
        c2_s[:, sl] = mt - bc
        emt_s[:, sl] = jnp.exp(-mt)
        wk_s[:, sl] = jnp.exp(b_last - bc + ic[:, sl] - m_new)
        ast_s[c] = jnp.exp(b_last + m_prev - m_new)
        upad = jnp.concatenate([u[:, sl], jnp.zeros((L - SUBLANES, L), jnp.float32)], axis=0)
        ucol_s[c * L:(c + 1) * L, :] = upad.T
        m_prev = m_new
    m_ref[0] = m_prev

    s_i = lax.broadcasted_iota(jnp.int32, (L, L), 0)
    t_i = lax.broadcasted_iota(jnp.int32, (L, L), 1)
    causal = s_i <= t_i
    lane = lax.broadcasted_iota(jnp.int32, (L, LANES), 1)
    zeros_q = jnp.zeros((DK, L), jnp.float32)

    for c in range(nc):
        rows = slice(c * L, (c + 1) * L)
        for h in range(NH):
            pair, half = h // 2, h % 2
            hv = slice(h * DV, (h + 1) * DV)
            kp = k_s[rows, pair * LANES:(pair + 1) * LANES]
            qth = qt_s[h * DK:(h + 1) * DK, rows]
            qz = jnp.concatenate([qth, zeros_q] if half == 0 else [zeros_q, qth], axis=0).astype(bf16)
            st = _bdot(kp.astype(bf16), qz)
            pt = st * jnp.exp(jnp.where(causal, ucol_s[rows, h:h + 1] - c2_s[h:h + 1, rows], -jnp.inf))
            rs = jnp.sum(pt, axis=0, keepdims=True)
            vt = vt_s[hv, rows]
            state = st_ref[0, h]
            sq = _bdot(state.astype(bf16), qz) * wi_s[h:h + 1, rows]
            num = _bdot(vt.astype(bf16), pt.astype(bf16)) + sq[0:DV]
            den = sq[DV:DV + 1] + rs
            hh = num / jnp.maximum(jnp.abs(den), emt_s[h:h + 1, rows])
            hn = hh * lax.rsqrt(jnp.mean(hh * hh, axis=0, keepdims=True) + EPS) * gmhc_ref[hv, :]
            mixt_s[hv, rows] = jax.nn.sigmoid(ogt_s[hv, rows]) * hn
            wkr = wk_s[h:h + 1, rows]
            vw = jnp.concatenate([vt * wkr, jnp.broadcast_to(wkr, (2 * SUBLANES, L))], axis=0)
            km = jnp.where((lane >= half * DK) & (lane < (half + 1) * DK), kp, 0.0)
            st_ref[0, h] = ast_s[c][h:h + 1, :] * state + _bdot(vw.astype(bf16), km.astype(bf16))

    out = x + lax.dot_general(mixt_s[...].astype(bf16), wout_ref[0:MLSTM_W, :], (((0,), (0,)), ((), ())),
                              preferred_element_type=jnp.float32)
    x1_ref[0] = out + _bdot(yc, wout_ref[MLSTM_W:, :])


def _mixer_prompt(x, weights, TS, L):
    B, S, _ = x.shape
    nb = S // TS
    f32 = jnp.float32
    kern = functools.partial(_mixer_prompt_kernel, L=L, TS=TS)
    return pl.pallas_call(
        kern,
        grid=(B, nb),
        in_specs=[
            pl.BlockSpec((1, TS, D_MODEL), lambda b, j: (b, j, 0)),
            _const_spec((1, D_MODEL)),
            _const_spec((D_MODEL, NPROJ_COLS)),
            _const_spec((TPROJ_ROWS, D_MODEL)),
            _const_spec((2 * SUBLANES, TS)),
            _const_spec((MLSTM_W, LANES)),
            _const_spec((CONV_W, CONV_CH)),
            _const_spec((D_MODEL, D_MODEL)),
        ],
        out_specs=[
            pl.BlockSpec((1, TS, D_MODEL), lambda b, j: (b, j, 0)),
            pl.BlockSpec((1, NH, ST_ROWS, LANES), lambda b, j: (b, 0, 0, 0)),
            pl.BlockSpec((1, SUBLANES, LANES), lambda b, j: (b, 0, 0)),
            pl.BlockSpec((1, SUBLANES, CONV_CH), lambda b, j: (b, 0, 0)),
        ],
        out_shape=[
            jax.ShapeDtypeStruct((B, S, D_MODEL), f32),
            jax.ShapeDtypeStruct((B, NH, ST_ROWS, LANES), f32),
            jax.ShapeDtypeStruct((B, SUBLANES, LANES), f32),
            jax.ShapeDtypeStruct((B, SUBLANES, CONV_CH), f32),
        ],
        scratch_shapes=[
            pltpu.VMEM((QK_W, TS), f32),
            pltpu.VMEM((MLSTM_W, TS), f32),
            pltpu.VMEM((MLSTM_W, TS), f32),
            pltpu.VMEM((TS, QK_W), f32),
            pltpu.VMEM((MLSTM_W, TS), f32),
            pltpu.VMEM((TS + SUBLANES, CONV_CH), f32),
            pltpu.VMEM((TS, LANES), f32),
            pltpu.VMEM((SUBLANES, TS), f32),
            pltpu.VMEM((SUBLANES, TS), f32),
            pltpu.VMEM((SUBLANES, TS), f32),
            pltpu.VMEM((SUBLANES, TS), f32),
            pltpu.VMEM((TS // L, SUBLANES, LANES), f32),
        ],
        compiler_params=pltpu.CompilerParams(
            dimension_semantics=("arbitrary", "arbitrary"), vmem_limit_bytes=VMEM_LIMIT),
        name="mixer_prompt",
    )(x, *weights)


def _prep_prompt_weights(g_mix, w_in, b_ig, b_fg, g_mh, conv_w, w_out, TS):
    bf16 = jnp.bfloat16
    f32 = jnp.float32
    ig0 = 2 * QK_W + MLSTM_W
    fg0 = ig0 + NH
    og0 = fg0 + NH
    bg0 = og0 + MLSTM_W
    wn = jnp.concatenate([w_in[:, QK_W:2 * QK_W], w_in[:, bg0:]], axis=1).astype(bf16)
    gate_rows = jnp.zeros((2 * SUBLANES, D_MODEL), w_in.dtype)
    gate_rows = gate_rows.at[0:NH].set(w_in[:, ig0:fg0].T)
    gate_rows = gate_rows.at[SUBLANES:SUBLANES + NH].set(w_in[:, fg0:og0].T)
    wt = jnp.concatenate([w_in[:, 0:QK_W].T, w_in[:, 2 * QK_W:ig0].T, w_in[:, og0:bg0].T, gate_rows],
                         axis=0).astype(bf16)
    gb = jnp.zeros((2 * SUBLANES,), f32).at[0:NH].set(b_ig).at[SUBLANES:SUBLANES + NH].set(b_fg)
    gbt = jnp.broadcast_to(gb[:, None], (2 * SUBLANES, TS))
    gmhc = jnp.broadcast_to(g_mh[:, None], (MLSTM_W, LANES))
    return (g_mix.reshape(1, D_MODEL), wn, wt, gbt, gmhc, conv_w, w_out.astype(bf16))


def _unpack_prompt_state(st, m):
    B = st.shape[0]
    even = st[:, 0::2, :, 0:DK]
    odd = st[:, 1::2, :, DK:2 * DK]
    s = jnp.stack([even, odd], axis=2).reshape(B, NH, ST_ROWS, DK)
    return jnp.swapaxes(s[:, :, :DV, :], -1, -2), s[:, :, DV, :], m[:, :NH, 0]


def _mixer_sample(xpad, cpad, c0, n0, m0, weights, TS):
    T = xpad.shape[0]
    nseq = TS // SAMPLE_PAD
    B = T // SAMPLE_PAD
    f32 = jnp.float32
    kern = functools.partial(_mixer_kernel, L=SAMPLE_PAD, TS=TS, sample=True)
    c_spec = pl.BlockSpec((nseq, NH, DK, DV), lambda t: (t, 0, 0, 0))
    n_spec = pl.BlockSpec((nseq, NH, DK), lambda t: (t, 0, 0))
    m_spec = pl.BlockSpec((nseq, 1, LANES), lambda t: (t, 0, 0))
    return pl.pallas_call(
        kern,
        grid=(T // TS,),
        in_specs=[
            pl.BlockSpec((TS, D_MODEL), lambda t: (t, 0)),
            pl.BlockSpec((TS, CONV_CH), lambda t: (t, 0)),
            c_spec, n_spec, m_spec,
        ] + _weight_specs(),
        out_specs=[
            pl.BlockSpec((TS, D_MODEL), lambda t: (t, 0)),
            c_spec, n_spec, m_spec,
            pl.BlockSpec((TS, CONV_CH), lambda t: (t, 0)),
        ],
        out_shape=[
            jax.ShapeDtypeStruct((T, D_MODEL), f32),
            jax.ShapeDtypeStruct((B, NH, DK, DV), f32),
            jax.ShapeDtypeStruct((B, NH, DK), f32),
            jax.ShapeDtypeStruct((B, 1, LANES), f32),
            jax.ShapeDtypeStruct((T, CONV_CH), f32),
        ],
        scratch_shapes=_mixer_scratch(TS),
        compiler_params=pltpu.CompilerParams(
            dimension_semantics=("arbitrary",), vmem_limit_bytes=VMEM_LIMIT),
        name="mixer_sample",
    )(xpad, cpad, c0, n0, m0, *weights)


def _ffn_kernel(x_ref, pe_ref, gffn_ref, wg_ref, wu_ref, wd_ref, wple_ref, gple_ref, wpg_ref,
                gfin_ref, y_ref, *, final_norm):
    x = x_ref[...]
    f = _rms(x, gffn_ref[...]).astype(jnp.bfloat16)
    gate = _bdot(f, wg_ref[...])
    up = _bdot(f, wu_ref[...])
    hmid = (gate * jax.nn.sigmoid(gate) * up).astype(jnp.bfloat16)
    x = x + _bdot(hmid, wd_ref[...])
    e = _rms(_bdot(pe_ref[...].astype(jnp.bfloat16), wple_ref[...]), gple_ref[...])
    x = x + jax.nn.sigmoid(_bdot(x.astype(jnp.bfloat16), wpg_ref[...])) * e
    if final_norm:
        x = _rms(x, gfin_ref[...])
    y_ref[...] = x


def _ffn(x, pe, weights, TM, final_norm):
    T = x.shape[0]
    kern = functools.partial(_ffn_kernel, final_norm=final_norm)
    return pl.pallas_call(
        kern,
        grid=(T // TM,),
        in_specs=[
            pl.BlockSpec((TM, D_MODEL), lambda t: (t, 0)),
            pl.BlockSpec((TM, PLE_DIM), lambda t: (t, 0)),
            _const_spec((1, D_MODEL)),
            _const_spec((D_MODEL, D_FF)),
            _const_spec((D_MODEL, D_FF)),
            _const_spec((D_FF, D_MODEL)),
            _const_spec((PLE_DIM, D_MODEL)),
            _const_spec((1, D_MODEL)),
            _const_spec((D_MODEL, D_MODEL)),
            _const_spec((1, D_MODEL)),
        ],
        out_specs=pl.BlockSpec((TM, D_MODEL), lambda t: (t, 0)),
        out_shape=jax.ShapeDtypeStruct((T, D_MODEL), jnp.float32),
        compiler_params=pltpu.CompilerParams(
            dimension_semantics=("arbitrary",), vmem_limit_bytes=VMEM_LIMIT),
        name="ffn",
    )(x, pe, *weights)


def _prep_mixer_weights(g_mix, w_in, b_ig, b_fg, g_mh, conv_w, w_out):
    bf16 = jnp.bfloat16
    ig0 = 2 * QK_W + MLSTM_W
    fg0 = ig0 + NH
    og0 = fg0 + NH
    gate_cols = jnp.zeros((D_MODEL, 2 * LANES), w_in.dtype)
    gate_cols = gate_cols.at[:, 0:NH].set(w_in[:, ig0:fg0])
    gate_cols = gate_cols.at[:, LANES:LANES + NH].set(w_in[:, fg0:og0])
    w_all = jnp.concatenate([w_in[:, :ig0], w_in[:, og0:], gate_cols], axis=1).astype(bf16)
    gb = jnp.zeros((1, 2 * LANES), jnp.float32)
    gb = gb.at[0, 0:NH].set(b_ig).at[0, LANES:LANES + NH].set(b_fg)
    return (g_mix.reshape(1, D_MODEL), w_all, gb, g_mh.reshape(1, MLSTM_W), conv_w,
            w_out.astype(bf16))


def kernel(x_prompt, x_sample, p_prompt, p_sample, state_C, state_n, state_m, state_conv,
           g_mix, w_in, b_ig, b_fg, g_mh, conv_w, w_out, g_ffn, w_gate, w_up, w_down,
           w_ple, g_ple, w_pg, g_final):
    bf16 = jnp.bfloat16
    depth = g_mix.shape[0]
    B, S, _ = x_prompt.shape
    Bs, Ss, _ = x_sample.shape
    assert Ss == 4 and S % PROMPT_CHUNK == 0
    lead, tail = SAMPLE_LEAD, SAMPLE_PAD - SAMPLE_LEAD - Ss

    xp = x_prompt
    xs = x_sample
    outs = [[] for _ in range(8)]
    for i in range(depth):
        mw = _prep_mixer_weights(g_mix[i], w_in[i], b_ig[i], b_fg[i], g_mh[i], conv_w[i], w_out[i])
        last = i == depth - 1
        fw = (g_ffn[i].reshape(1, D_MODEL), w_gate[i].astype(bf16), w_up[i].astype(bf16),
              w_down[i].astype(bf16), w_ple[i].astype(bf16), g_ple[i].reshape(1, D_MODEL),
              w_pg[i].astype(bf16), g_final.reshape(1, D_MODEL))

        pw = _prep_prompt_weights(g_mix[i], w_in[i], b_ig[i], b_fg[i], g_mh[i], conv_w[i], w_out[i],
                                  PROMPT_TS)
        x1p, stp, mpr, cvp = _mixer_prompt(xp, pw, TS=PROMPT_TS, L=PROMPT_L)
        cp, n_p, mp = _unpack_prompt_state(stp, mpr)
        xp = _ffn(x1p.reshape(B * S, D_MODEL), p_prompt[i].reshape(B * S, PLE_DIM), fw, 256,
                  last).reshape(B, S, D_MODEL)

        xs_pad = jnp.pad(xs, ((0, 0), (lead, tail), (0, 0))).reshape(Bs * SAMPLE_PAD, D_MODEL)
        cpad = jnp.pad(state_conv[i], ((0, 0), (0, SAMPLE_PAD - (CONV_W - 1)), (0, 0)))
        cpad = cpad.reshape(Bs * SAMPLE_PAD, CONV_CH)
        m0 = jnp.pad(state_m[i], ((0, 0), (0, LANES - NH))).reshape(Bs, 1, LANES)
        x1s, cs, n_s, ms, cvs = _mixer_sample(xs_pad, cpad, state_C[i], state_n[i], m0, mw, TS=256)
        x1s = x1s.reshape(Bs, SAMPLE_PAD, D_MODEL)[:, lead:lead + Ss].reshape(Bs * Ss, D_MODEL)
        xs = _ffn(x1s, p_sample[i].reshape(Bs * Ss, PLE_DIM), fw, 256, last).reshape(Bs, Ss, D_MODEL)

        new = (cp, n_p, mp, cvp[:, SUBLANES - (CONV_W - 1):],
               cs, n_s, ms[:, 0, :NH],
               cvs.reshape(Bs, SAMPLE_PAD, CONV_CH)[:, lead + Ss - (CONV_W - 1):lead + Ss])
        for lst, v in zip(outs, new):
            lst.append(v)

    return (xp, xs) + tuple(jnp.stack(l) for l in outs)
```

```python
import functools

import jax
import jax.numpy as jnp
from jax import lax
from jax.experimental import pallas as pl
from jax.experimental.pallas import tpu as pltpu

D_MODEL = 1024
NH = 4
DK = 64
DV = 128
MLSTM_W = NH * DV
QK_W = NH * DK
CONV_CH = 512
CONV_W = 3
D_FF = 2816
PLE_DIM = 256
PROMPT_CHUNK = 64
PROMPT_L = 128
PROMPT_TS = 512
GATE_CAP = 15.0
EPS = 1e-6

LANES = 128
SUBLANES = 8
NEG_BIG = -1e30

_Q0, _K0, _V0, _OG0, _BG0, _CG0, _HC0, _GT0 = 0, 256, 512, 1024, 1536, 2048, 2560, 3072
PROJ_COLS = _GT0 + 2 * LANES

SAMPLE_PAD = SUBLANES
SAMPLE_LEAD = 2

VMEM_LIMIT = 56 * 1024 * 1024


def _rms(x, g):
    ms = jnp.mean(x * x, axis=-1, keepdims=True)
    return x * lax.rsqrt(ms + EPS) * g


def _seg_cumsum(x, seg):
    pos = lax.broadcasted_iota(jnp.int32, x.shape, 0) & (seg - 1)
    k = 1
    while k < seg:
        x = x + jnp.where(pos >= k, pltpu.roll(x, k, 0), 0.0)
        k *= 2
    return x


def _bdot(a, b):
    return jnp.dot(a, b, preferred_element_type=jnp.float32)


def _mixer_kernel(*refs, L, TS, sample):
    if sample:
        (x_ref, cpad_ref, c0_ref, n0_ref, m0_ref,
         gmix_ref, win_ref, gb_ref, gmh_ref, cw_ref, wout_ref,
         x1_ref, cst_ref, nst_ref, mst_ref, cvo_ref,
         q_s, k_s, v_s, og_s, b_s, ic_s, mix_s, z_s) = refs
    else:
        (x_ref,
         gmix_ref, win_ref, gb_ref, gmh_ref, cw_ref, wout_ref,
         x1_ref, cst_ref, nst_ref, mst_ref, cvo_ref,
         q_s, k_s, v_s, og_s, b_s, ic_s, mix_s, z_s) = refs
        c0_ref, n0_ref, m0_ref = cst_ref, nst_ref, mst_ref
    nc = TS // L
    j = pl.program_id(1) if not sample else None

    x = x_ref[0] if not sample else x_ref[...]
    a = _rms(x, gmix_ref[...]).astype(jnp.bfloat16)

    q_s[...] = _bdot(a, win_ref[:, _Q0:_K0]) * (DK ** -0.5)
    k_s[...] = _bdot(a, win_ref[:, _K0:_V0])
    v_s[...] = _bdot(a, win_ref[:, _V0:_OG0])
    og_s[...] = _bdot(a, win_ref[:, _OG0:_BG0])

    z = _bdot(a, win_ref[:, _CG0:_HC0]) * _bdot(a, win_ref[:, _HC0:_GT0])
    row = lax.broadcasted_iota(jnp.int32, (TS, 1), 0)
    if sample:
        pos = row & (SAMPLE_PAD - 1)
        real = (pos >= SAMPLE_LEAD) & (pos < SAMPLE_LEAD + 4)
        z = jnp.where(real, z, cpad_ref[...])
        z_s[0:SUBLANES, :] = jnp.zeros((SUBLANES, CONV_CH), jnp.float32)
    else:
        @pl.when(j == 0)
        def _():
            z_s[0:SUBLANES, :] = jnp.zeros((SUBLANES, CONV_CH), jnp.float32)
    z_s[SUBLANES:SUBLANES + TS, :] = z
    yconv = (cw_ref[0:1, :] * z_s[SUBLANES - 2:SUBLANES - 2 + TS, :]
             + cw_ref[1:2, :] * z_s[SUBLANES - 1:SUBLANES - 1 + TS, :]
             + cw_ref[2:3, :] * z)
    mix_s[:, MLSTM_W:] = _bdot(a, win_ref[:, _BG0:_CG0]) * yconv
    z_s[0:SUBLANES, :] = z[TS - SUBLANES:, :]
    if sample:
        cvo_ref[...] = z
    else:
        cvo_ref[0] = z[TS - SUBLANES:, :]

    gates = _bdot(a, win_ref[:, _GT0:PROJ_COLS]) + gb_ref[...]
    ic = GATE_CAP * jnp.tanh(gates[:, :LANES] / GATE_CAP)
    fpre = gates[:, LANES:]
    lf = jnp.minimum(fpre, 0.0) - jnp.log1p(jnp.exp(-jnp.abs(fpre)))
    if sample:
        ic = jnp.where(real, ic, NEG_BIG)
        lf = jnp.where(real, lf, 0.0)
    b_s[...] = _seg_cumsum(lf, L)
    ic_s[...] = ic

    if not sample:
        @pl.when(j == 0)
        def _():
            cst_ref[...] = jnp.zeros(cst_ref.shape, jnp.float32)
            nst_ref[...] = jnp.zeros(nst_ref.shape, jnp.float32)
            mst_ref[...] = jnp.zeros(mst_ref.shape, jnp.float32)

    ti = lax.broadcasted_iota(jnp.int32, (L, L), 0)
    si = lax.broadcasted_iota(jnp.int32, (L, L), 1)
    eye = ti == si
    causal = si <= ti

    def chunk(c, carry):
        i = c if sample else 0
        r0 = pl.multiple_of(c * L, L)
        rows = pl.ds(r0, L)
        bt = b_s[rows, :]
        ict = ic_s[rows, :]
        b_last = b_s[pl.ds(r0 + L - 1, 1), :]
        d_last = b_last - bt + ict
        m_loc2 = jnp.max(d_last, axis=0, keepdims=True)
        wk_loc = jnp.exp(d_last - m_loc2)
        m_prev = m0_ref[i]
        m_new = jnp.maximum(b_last + m_prev, m_loc2)
        a_st = jnp.exp(b_last + m_prev - m_new)
        sc = jnp.exp(m_loc2 - m_new)
        g = bt + m_prev
        for h in range(NH):
            qh = q_s[rows, h * DK:(h + 1) * DK]
            kh = k_s[rows, h * DK:(h + 1) * DK]
            vh = v_s[rows, h * DV:(h + 1) * DV].astype(jnp.bfloat16)
            qb = qh.astype(jnp.bfloat16)
            bh = bt[:, h:h + 1]
            rh = bh - ict[:, h:h + 1]
            r_row = jnp.sum(jnp.where(eye, rh, 0.0), axis=0, keepdims=True)
            dm = jnp.where(causal, bh - r_row, -jnp.inf)
            m_loc = jnp.max(dm, axis=1, keepdims=True)
            qk = lax.dot_general(qb, kh.astype(jnp.bfloat16), (((1,), (1,)), ((), ())),
                                 preferred_element_type=jnp.float32)
            p = qk * jnp.exp(dm - m_loc)
            rs = jnp.sum(p, axis=1, keepdims=True)
            pv = _bdot(p.astype(jnp.bfloat16), vh)
            gh = g[:, h:h + 1]
            mt = jnp.maximum(gh, m_loc)
            wi = jnp.exp(gh - mt)
            wl = jnp.exp(m_loc - mt)
            c_h = c0_ref[i, h]
            n_h = n0_ref[i, h:h + 1, :]
            qc = _bdot(qb, c_h.astype(jnp.bfloat16))
            qn = jnp.sum(qh * n_h, axis=1, keepdims=True)
            num = wi * qc + wl * pv
            den = wi * qn + wl * rs
            hh = num / jnp.maximum(jnp.abs(den), jnp.exp(-mt))
            hn = (hh * lax.rsqrt(jnp.mean(hh * hh, axis=1, keepdims=True) + EPS)
                  * gmh_ref[:, h * DV:(h + 1) * DV])
            mix_s[rows, h * DV:(h + 1) * DV] = jax.nn.sigmoid(og_s[rows, h * DV:(h + 1) * DV]) * hn
            kw = kh * wk_loc[:, h:h + 1]
            kv = lax.dot_general(kw.astype(jnp.bfloat16), vh, (((0,), (0,)), ((), ())),
                                 preferred_element_type=jnp.float32)
            ks = jnp.sum(kw, axis=0, keepdims=True)
            cst_ref[i, h] = a_st[:, h:h + 1] * c_h + sc[:, h:h + 1] * kv
            nst_ref[i, h:h + 1, :] = a_st[:, h:h + 1] * n_h + sc[:, h:h + 1] * ks
        mst_ref[i] = m_new
        return carry

    lax.fori_loop(0, nc, chunk, 0)

    out = x + _bdot(mix_s[...].astype(jnp.bfloat16), wout_ref[...])
    if sample:
        x1_ref[...] = out
    else:
        x1_ref[0] = out


def _const_spec(shape):
    nd = len(shape)
    return pl.BlockSpec(shape, lambda *_: (0,) * nd, pipeline_mode=pl.Buffered(1))


def _mixer_scratch(TS):
    f32 = jnp.float32
    return [
        pltpu.VMEM((TS, QK_W), f32),
        pltpu.VMEM((TS, QK_W), f32),
        pltpu.VMEM((TS, MLSTM_W), f32),
        pltpu.VMEM((TS, MLSTM_W), f32),
        pltpu.VMEM((TS, LANES), f32),
        pltpu.VMEM((TS, LANES), f32),
        pltpu.VMEM((TS, D_MODEL), f32),
        pltpu.VMEM((TS + SUBLANES, CONV_CH), f32),
    ]


def _weight_specs():
    return [
        _const_spec((1, D_MODEL)),
        _const_spec((D_MODEL, PROJ_COLS)),
        _const_spec((1, 2 * LANES)),
        _const_spec((1, MLSTM_W)),
        _const_spec((CONV_W, CONV_CH)),
        _const_spec((D_MODEL, D_MODEL)),
    ]


_TQ0, _TV0, _TOG0, _TG0 = 0, 256, 768, 1280
TPROJ_ROWS = _TG0 + 2 * SUBLANES
_NK0, _NBG0, _NCG0, _NHC0, NPROJ_COLS = 0, 256, 768, 1280, 1792
ST_ROWS = DV + 2 * SUBLANES


def _lane_scan(x, seg, op, fill):
    pos = lax.broadcasted_iota(jnp.int32, x.shape, 1) & (seg - 1)
    k = 1
    while k < seg:
        x = op(x, jnp.where(pos >= k, pltpu.roll(x, k, 1), fill))
        k *= 2
    return x


def _mixer_prompt_kernel(x_ref, gmix_ref, wn_ref, wt_ref, gbt_ref, gmhc_ref, cw_ref, wout_ref,
                         x1_ref, st_ref, m_ref, cvo_ref,
                         qt_s, vt_s, ogt_s, k_s, mixt_s, z_s, ucol_s, wi_s, c2_s, emt_s, wk_s, ast_s,
                         *, L, TS):
    nc = TS // L
    j = pl.program_id(1)
    bf16 = jnp.bfloat16

    @pl.when(j == 0)
    def _():
        z_s[0:SUBLANES, :] = jnp.zeros((SUBLANES, CONV_CH), jnp.float32)
        st_ref[...] = jnp.zeros(st_ref.shape, jnp.float32)
        m_ref[...] = jnp.zeros(m_ref.shape, jnp.float32)

    x = x_ref[0]
    a = _rms(x, gmix_ref[...]).astype(bf16)
    nt_dims = (((1,), (1,)), ((), ()))

    gt = lax.dot_general(wt_ref[_TG0:TPROJ_ROWS, :], a, nt_dims,
                         preferred_element_type=jnp.float32) + gbt_ref[...]
    ic = GATE_CAP * jnp.tanh(gt[0:SUBLANES] / GATE_CAP)
    fpre = gt[SUBLANES:]
    lf = jnp.minimum(fpre, 0.0) - jnp.log1p(jnp.exp(-jnp.abs(fpre)))
    b = _lane_scan(lf, L, jnp.add, 0.0)
    u = ic - b
    m_loc = b + _lane_scan(u, L, jnp.maximum, -jnp.inf)

    un = _bdot(a, wn_ref[...])
    k_s[...] = un[:, _NK0:_NBG0]
    z = un[:, _NCG0:_NHC0] * un[:, _NHC0:NPROJ_COLS]
    z_s[SUBLANES:SUBLANES + TS, :] = z
    yconv = (cw_ref[0:1, :] * z_s[SUBLANES - 2:SUBLANES - 2 + TS, :]
             + cw_ref[1:2, :] * z_s[SUBLANES - 1:SUBLANES - 1 + TS, :]
             + cw_ref[2:3, :] * z)
    yc = (un[:, _NBG0:_NCG0] * yconv).astype(bf16)
    z_s[0:SUBLANES, :] = z[TS - SUBLANES:, :]
    cvo_ref[0] = z[TS - SUBLANES:, :]

    ut = lax.dot_general(wt_ref[0:_TG0, :], a, nt_dims, preferred_element_type=jnp.float32)
    qt_s[...] = ut[_TQ0:_TV0] * (DK ** -0.5)
    vt_s[...] = ut[_TV0:_TOG0]
    ogt_s[...] = ut[_TOG0:_TG0]

    m_prev = m_ref[0]
    for c in range(nc):
        sl = slice(c * L, (c + 1) * L)
        bc, mlc = b[:, sl], m_loc[:, sl]
        b_last = jnp.broadcast_to(bc[:, L - 1:L], bc.shape)
        m_new = jnp.maximum(b_last + m_prev, jnp.broadcast_to(mlc[:, L - 1:L], bc.shape))
        g = bc + m_prev
        mt = jnp.maximum(g, mlc)
        wi_s[:, sl] = jnp.exp(g - mt)
        c2_s[:, sl] = mt - bc
        emt_s[:, sl] = jnp.exp(-mt)
        wk_s[:, sl] = jnp.exp(b_last - bc + ic[:, sl] - m_new)
        ast_s[c] = jnp.exp(b_last + m_prev - m_new)
        upad = jnp.concatenate([u[:, sl], jnp.zeros((L - SUBLANES, L), jnp.float32)], axis=0)
        ucol_s[c * L:(c + 1) * L, :] = upad.T
        m_prev = m_new
    m_ref[0] = m_prev

    s_i = lax.broadcasted_iota(jnp.int32, (L, L), 0)
    t_i = lax.broadcasted_iota(jnp.int32, (L, L), 1)
    causal = s_i <= t_i
    lane = lax.broadcasted_iota(jnp.int32, (L, LANES), 1)
    zeros_q = jnp.zeros((DK, L), jnp.float32)

    for c in range(nc):
        rows = slice(c * L, (c + 1) * L)
        for h in range(NH):
            pair, half = h // 2, h % 2
            hv = slice(h * DV, (h + 1) * DV)
            kp = k_s[rows, pair * LANES:(pair + 1) * LANES]
            qth = qt_s[h * DK:(h + 1) * DK, rows]
            qz = jnp.concatenate([qth, zeros_q] if half == 0 else [zeros_q, qth], axis=0).astype(bf16)
            st = _bdot(kp.astype(bf16), qz)
            pt = st * jnp.exp(jnp.where(causal, ucol_s[rows, h:h + 1] - c2_s[h:h + 1, rows], -jnp.inf))
            rs = jnp.sum(pt, axis=0, keepdims=True)
            vt = vt_s[hv, rows]
            state = st_ref[0, h]
            sq = _bdot(state.astype(bf16), qz) * wi_s[h:h + 1, rows]
            num = _bdot(vt.astype(bf16), pt.astype(bf16)) + sq[0:DV]
            den = sq[DV:DV + 1] + rs
            hh = num / jnp.maximum(jnp.abs(den), emt_s[h:h + 1, rows])
            hn = hh * lax.rsqrt(jnp.mean(hh * hh, axis=0, keepdims=True) + EPS) * gmhc_ref[hv, :]
            mixt_s[hv, rows] = jax.nn.sigmoid(ogt_s[hv, rows]) * hn
            wkr = wk_s[h:h + 1, rows]
            vw = jnp.concatenate([vt * wkr, jnp.broadcast_to(wkr, (2 * SUBLANES, L))], axis=0)
            km = jnp.where((lane >= half * DK) & (lane < (half + 1) * DK), kp, 0.0)
            st_ref[0, h] = ast_s[c][h:h + 1, :] * state + _bdot(vw.astype(bf16), km.astype(bf16))

    out = x + lax.dot_general(mixt_s[...].astype(bf16), wout_ref[0:MLSTM_W, :], (((0,), (0,)), ((), ())),
                              preferred_element_type=jnp.float32)
    x1_ref[0] = out + _bdot(yc, wout_ref[MLSTM_W:, :])


def _mixer_prompt(x, weights, TS, L):
    B, S, _ = x.shape
    nb = S // TS
    f32 = jnp.float32
    kern = functools.partial(_mixer_prompt_kernel, L=L, TS=TS)
    return pl.pallas_call(
        kern,
        grid=(B, nb),
        in_specs=[
            pl.BlockSpec((1, TS, D_MODEL), lambda b, j: (b, j, 0)),
            _const_spec((1, D_MODEL)),
            _const_spec((D_MODEL, NPROJ_COLS)),
            _const_spec((TPROJ_ROWS, D_MODEL)),
            _const_spec((2 * SUBLANES, TS)),
            _const_spec((MLSTM_W, LANES)),
            _const_spec((CONV_W, CONV_CH)),
            _const_spec((D_MODEL, D_MODEL)),
        ],
        out_specs=[
            pl.BlockSpec((1, TS, D_MODEL), lambda b, j: (b, j, 0)),
            pl.BlockSpec((1, NH, ST_ROWS, LANES), lambda b, j: (b, 0, 0, 0)),
            pl.BlockSpec((1, SUBLANES, LANES), lambda b, j: (b, 0, 0)),
            pl.BlockSpec((1, SUBLANES, CONV_CH), lambda b, j: (b, 0, 0)),
        ],
        out_shape=[
            jax.ShapeDtypeStruct((B, S, D_MODEL), f32),
            jax.ShapeDtypeStruct((B, NH, ST_ROWS, LANES), f32),
            jax.ShapeDtypeStruct((B, SUBLANES, LANES), f32),
            jax.ShapeDtypeStruct((B, SUBLANES, CONV_CH), f32),
        ],
        scratch_shapes=[
            pltpu.VMEM((QK_W, TS), f32),
            pltpu.VMEM((MLSTM_W, TS), f32),
            pltpu.VMEM((MLSTM_W, TS), f32),
            pltpu.VMEM((TS, QK_W), f32),
            pltpu.VMEM((MLSTM_W, TS), f32),
            pltpu.VMEM((TS + SUBLANES, CONV_CH), f32),
            pltpu.VMEM((TS, LANES), f32),
            pltpu.VMEM((SUBLANES, TS), f32),
            pltpu.VMEM((SUBLANES, TS), f32),
            pltpu.VMEM((SUBLANES, TS), f32),
            pltpu.VMEM((SUBLANES, TS), f32),
            pltpu.VMEM((TS // L, SUBLANES, LANES), f32),
        ],
        compiler_params=pltpu.CompilerParams(
            dimension_semantics=("arbitrary", "arbitrary"), vmem_limit_bytes=VMEM_LIMIT),
        name="mixer_prompt",
    )(x, *weights)


def _prep_prompt_weights(g_mix, w_in, b_ig, b_fg, g_mh, conv_w, w_out, TS):
    bf16 = jnp.bfloat16
    f32 = jnp.float32
    ig0 = 2 * QK_W + MLSTM_W
    fg0 = ig0 + NH
    og0 = fg0 + NH
    bg0 = og0 + MLSTM_W
    wn = jnp.concatenate([w_in[:, QK_W:2 * QK_W], w_in[:, bg0:]], axis=1).astype(bf16)
    gate_rows = jnp.zeros((2 * SUBLANES, D_MODEL), w_in.dtype)
    gate_rows = gate_rows.at[0:NH].set(w_in[:, ig0:fg0].T)
    gate_rows = gate_rows.at[SUBLANES:SUBLANES + NH].set(w_in[:, fg0:og0].T)
    wt = jnp.concatenate([w_in[:, 0:QK_W].T, w_in[:, 2 * QK_W:ig0].T, w_in[:, og0:bg0].T, gate_rows],
                         axis=0).astype(bf16)
    gb = jnp.zeros((2 * SUBLANES,), f32).at[0:NH].set(b_ig).at[SUBLANES:SUBLANES + NH].set(b_fg)
    gbt = jnp.broadcast_to(gb[:, None], (2 * SUBLANES, TS))
    gmhc = jnp.broadcast_to(g_mh[:, None], (MLSTM_W, LANES))
    return (g_mix.reshape(1, D_MODEL), wn, wt, gbt, gmhc, conv_w, w_out.astype(bf16))


def _unpack_prompt_state(st, m):
    B = st.shape[0]
    even = st[:, 0::2, :, 0:DK]
    odd = st[:, 1::2, :, DK:2 * DK]
    s = jnp.stack([even, odd], axis=2).reshape(B, NH, ST_ROWS, DK)
    return jnp.swapaxes(s[:, :, :DV, :], -1, -2), s[:, :, DV, :], m[:, :NH, 0]


def _mixer_sample(xpad, cpad, c0, n0, m0, weights, TS):
    T = xpad.shape[0]
    nseq = TS // SAMPLE_PAD
    B = T // SAMPLE_PAD
    f32 = jnp.float32
    kern = functools.partial(_mixer_kernel, L=SAMPLE_PAD, TS=TS, sample=True)
    c_spec = pl.BlockSpec((nseq, NH, DK, DV), lambda t: (t, 0, 0, 0))
    n_spec = pl.BlockSpec((nseq, NH, DK), lambda t: (t, 0, 0))
    m_spec = pl.BlockSpec((nseq, 1, LANES), lambda t: (t, 0, 0))
    return pl.pallas_call(
        kern,
        grid=(T // TS,),
        in_specs=[
            pl.BlockSpec((TS, D_MODEL), lambda t: (t, 0)),
            pl.BlockSpec((TS, CONV_CH), lambda t: (t, 0)),
            c_spec, n_spec, m_spec,
        ] + _weight_specs(),
        out_specs=[
            pl.BlockSpec((TS, D_MODEL), lambda t: (t, 0)),
            c_spec, n_spec, m_spec,
            pl.BlockSpec((TS, CONV_CH), lambda t: (t, 0)),
        ],
        out_shape=[
            jax.ShapeDtypeStruct((T, D_MODEL), f32),
            jax.ShapeDtypeStruct((B, NH, DK, DV), f32),
            jax.ShapeDtypeStruct((B, NH, DK), f32),
            jax.ShapeDtypeStruct((B, 1, LANES), f32),
            jax.ShapeDtypeStruct((T, CONV_CH), f32),
        ],
        scratch_shapes=_mixer_scratch(TS),
        compiler_params=pltpu.CompilerParams(
            dimension_semantics=("arbitrary",), vmem_limit_bytes=VMEM_LIMIT),
        name="mixer_sample",
    )(xpad, cpad, c0, n0, m0, *weights)


def _ffn_kernel(x_ref, pe_ref, gffn_ref, wg_ref, wu_ref, wd_ref, wple_ref, gple_ref, wpg_ref,
                gfin_ref, y_ref, *, final_norm, n_sub):
    sub = x_ref.shape[0] // n_sub
    for s in range(n_sub):
        rows = slice(s * sub, (s + 1) * sub)
        x = x_ref[rows, :]
        f = _rms(x, gffn_ref[...]).astype(jnp.bfloat16)
        gate = _bdot(f, wg_ref[...])
        up = _bdot(f, wu_ref[...])
        hmid = (gate * jax.nn.sigmoid(gate) * up).astype(jnp.bfloat16)
        x = x + _bdot(hmid, wd_ref[...])
        e = _rms(_bdot(pe_ref[rows, :].astype(jnp.bfloat16), wple_ref[...]), gple_ref[...])
        x = x + jax.nn.sigmoid(_bdot(x.astype(jnp.bfloat16), wpg_ref[...])) * e
        if final_norm:
            x = _rms(x, gfin_ref[...])
        y_ref[rows, :] = x


def _ffn(x, pe, weights, TM, final_norm, n_sub=1):
    T = x.shape[0]
    kern = functools.partial(_ffn_kernel, final_norm=final_norm, n_sub=n_sub)
    return pl.pallas_call(
        kern,
        grid=(T // TM,),
        in_specs=[
            pl.BlockSpec((TM, D_MODEL), lambda t: (t, 0)),
            pl.BlockSpec((TM, PLE_DIM), lambda t: (t, 0)),
            _const_spec((1, D_MODEL)),
            _const_spec((D_MODEL, D_FF)),
            _const_spec((D_MODEL, D_FF)),
            _const_spec((D_FF, D_MODEL)),
            _const_spec((PLE_DIM, D_MODEL)),
            _const_spec((1, D_MODEL)),
            _const_spec((D_MODEL, D_MODEL)),
            _const_spec((1, D_MODEL)),
        ],
        out_specs=pl.BlockSpec((TM, D_MODEL), lambda t: (t, 0)),
        out_shape=jax.ShapeDtypeStruct((T, D_MODEL), jnp.float32),
        compiler_params=pltpu.CompilerParams(
            dimension_semantics=("arbitrary",), vmem_limit_bytes=VMEM_LIMIT),
        name="ffn",
    )(x, pe, *weights)


def _prep_mixer_weights(g_mix, w_in, b_ig, b_fg, g_mh, conv_w, w_out):
    bf16 = jnp.bfloat16
    ig0 = 2 * QK_W + MLSTM_W
    fg0 = ig0 + NH
    og0 = fg0 + NH
    gate_cols = jnp.zeros((D_MODEL, 2 * LANES), w_in.dtype)
    gate_cols = gate_cols.at[:, 0:NH].set(w_in[:, ig0:fg0])
    gate_cols = gate_cols.at[:, LANES:LANES + NH].set(w_in[:, fg0:og0])
    w_all = jnp.concatenate([w_in[:, :ig0], w_in[:, og0:], gate_cols], axis=1).astype(bf16)
    gb = jnp.zeros((1, 2 * LANES), jnp.float32)
    gb = gb.at[0, 0:NH].set(b_ig).at[0, LANES:LANES + NH].set(b_fg)
    return (g_mix.reshape(1, D_MODEL), w_all, gb, g_mh.reshape(1, MLSTM_W), conv_w,
            w_out.astype(bf16))


def kernel(x_prompt, x_sample, p_prompt, p_sample, state_C, state_n, state_m, state_conv,
           g_mix, w_in, b_ig, b_fg, g_mh, conv_w, w_out, g_ffn, w_gate, w_up, w_down,
           w_ple, g_ple, w_pg, g_final):
    bf16 = jnp.bfloat16
    depth = g_mix.shape[0]
    B, S, _ = x_prompt.shape
    Bs, Ss, _ = x_sample.shape
    assert Ss == 4 and S % PROMPT_CHUNK == 0
    lead, tail = SAMPLE_LEAD, SAMPLE_PAD - SAMPLE_LEAD - Ss

    xp = x_prompt
    xs = x_sample
    outs = [[] for _ in range(8)]
    for i in range(depth):
        mw = _prep_mixer_weights(g_mix[i], w_in[i], b_ig[i], b_fg[i], g_mh[i], conv_w[i], w_out[i])
        last = i == depth - 1
        fw = (g_ffn[i].reshape(1, D_MODEL), w_gate[i].astype(bf16), w_up[i].astype(bf16),
              w_down[i].astype(bf16), w_ple[i].astype(bf16), g_ple[i].reshape(1, D_MODEL),
              w_pg[i].astype(bf16), g_final.reshape(1, D_MODEL))

        pw = _prep_prompt_weights(g_mix[i], w_in[i], b_ig[i], b_fg[i], g_mh[i], conv_w[i], w_out[i],
                                  PROMPT_TS)
        x1p, stp, mpr, cvp = _mixer_prompt(xp, pw, TS=PROMPT_TS, L=PROMPT_L)
        cp, n_p, mp = _unpack_prompt_state(stp, mpr)
        xp = _ffn(x1p.reshape(B * S, D_MODEL), p_prompt[i].reshape(B * S, PLE_DIM), fw, 512,
                  last, n_sub=2).reshape(B, S, D_MODEL)

        xs_pad = jnp.pad(xs, ((0, 0), (lead, tail), (0, 0))).reshape(Bs * SAMPLE_PAD, D_MODEL)
        cpad = jnp.pad(state_conv[i], ((0, 0), (0, SAMPLE_PAD - (CONV_W - 1)), (0, 0)))
        cpad = cpad.reshape(Bs * SAMPLE_PAD, CONV_CH)
        m0 = jnp.pad(state_m[i], ((0, 0), (0, LANES - NH))).reshape(Bs, 1, LANES)
        x1s, cs, n_s, ms, cvs = _mixer_sample(xs_pad, cpad, state_C[i], state_n[i], m0, mw, TS=256)
        x1s = x1s.reshape(Bs, SAMPLE_PAD, D_MODEL)[:, lead:lead + Ss].reshape(Bs * Ss, D_MODEL)
        xs = _ffn(x1s, p_sample[i].reshape(Bs * Ss, PLE_DIM), fw, 256, last).reshape(Bs, Ss, D_MODEL)

        new = (cp, n_p, mp, cvp[:, SUBLANES - (CONV_W - 1):],
               cs, n_s, ms[:, 0, :NH],
               cvs.reshape(Bs, SAMPLE_PAD, CONV_CH)[:, lead + Ss - (CONV_W - 1):lead + Ss])
        for lst, v in zip(outs, new):
            lst.append(v)

    return (xp, xs) + tuple(jnp.stack(l) for l in outs)
```

```python
import functools

import numpy as np
import jax
import jax.numpy as jnp
from jax import lax
from jax.experimental import pallas as pl
from jax.experimental.pallas import tpu as pltpu

D_MODEL = 1024
NH = 4
DK = 64
DV = 128
MLSTM_W = NH * DV
QK_W = NH * DK
CONV_CH = 512
CONV_W = 3
D_FF = 2816
PLE_DIM = 256
PROMPT_L = 128
PROMPT_TS = 512
SAMPLE_T = 4
SAMPLE_NSEQ = 32
GATE_CAP = 15.0
EPS = 1e-6

LANES = 128
SUBLANES = 8
NEG_BIG = -1e30

VMEM_LIMIT = 56 * 1024 * 1024


def _rms(x, g):
    ms = jnp.mean(x * x, axis=-1, keepdims=True)
    return x * lax.rsqrt(ms + EPS) * g


def _bdot(a, b):
    return jnp.dot(a, b, preferred_element_type=jnp.float32)


def _log_sigmoid(x):
    return jnp.minimum(x, 0.0) - jnp.log1p(jnp.exp(-jnp.abs(x)))


def _const_spec(shape):
    nd = len(shape)
    return pl.BlockSpec(shape, lambda *_: (0,) * nd, pipeline_mode=pl.Buffered(1))


_TQ0, _TV0, _TOG0, _TG0 = 0, 256, 768, 1280
TPROJ_ROWS = _TG0 + 2 * SUBLANES
_NK0, _NBG0, _NCG0, _NHC0, NPROJ_COLS = 0, 256, 768, 1280, 1792
ST_ROWS = DV + 2 * SUBLANES


def _lane_scan(x, seg, op, fill):
    pos = lax.broadcasted_iota(jnp.int32, x.shape, 1) & (seg - 1)
    k = 1
    while k < seg:
        x = op(x, jnp.where(pos >= k, pltpu.roll(x, k, 1), fill))
        k *= 2
    return x


def _mixer_prompt_kernel(x_ref, gmix_ref, wn_ref, wt_ref, gbt_ref, gmhc_ref, cw_ref, wout_ref,
                         x1_ref, st_ref, m_ref, cvo_ref,
                         qt_s, vt_s, ogt_s, k_s, mixt_s, z_s, ucol_s, wi_s, c2_s, emt_s, wk_s, ast_s,
                         *, L, TS):
    nc = TS // L
    j = pl.program_id(1)
    bf16 = jnp.bfloat16

    @pl.when(j == 0)
    def _():
        z_s[0:SUBLANES, :] = jnp.zeros((SUBLANES, CONV_CH), jnp.float32)
        st_ref[...] = jnp.zeros(st_ref.shape, jnp.float32)
        m_ref[...] = jnp.zeros(m_ref.shape, jnp.float32)

    x = x_ref[0]
    a = _rms(x, gmix_ref[...]).astype(bf16)
    nt_dims = (((1,), (1,)), ((), ()))

    gt = lax.dot_general(wt_ref[_TG0:TPROJ_ROWS, :], a, nt_dims,
                         preferred_element_type=jnp.float32) + gbt_ref[...]
    ic = GATE_CAP * jnp.tanh(gt[0:SUBLANES] / GATE_CAP)
    lf = _log_sigmoid(gt[SUBLANES:])
    b = _lane_scan(lf, L, jnp.add, 0.0)
    u = ic - b
    m_loc = b + _lane_scan(u, L, jnp.maximum, -jnp.inf)

    un = _bdot(a, wn_ref[...])
    k_s[...] = un[:, _NK0:_NBG0]
    z = un[:, _NCG0:_NHC0] * un[:, _NHC0:NPROJ_COLS]
    z_s[SUBLANES:SUBLANES + TS, :] = z
    yconv = (cw_ref[0:1, :] * z_s[SUBLANES - 2:SUBLANES - 2 + TS, :]
             + cw_ref[1:2, :] * z_s[SUBLANES - 1:SUBLANES - 1 + TS, :]
             + cw_ref[2:3, :] * z)
    yc = (un[:, _NBG0:_NCG0] * yconv).astype(bf16)
    z_s[0:SUBLANES, :] = z[TS - SUBLANES:, :]
    cvo_ref[0] = z[TS - SUBLANES:, :]

    ut = lax.dot_general(wt_ref[0:_TG0, :], a, nt_dims, preferred_element_type=jnp.float32)
    qt_s[...] = ut[_TQ0:_TV0] * (DK ** -0.5)
    vt_s[...] = ut[_TV0:_TOG0]
    ogt_s[...] = ut[_TOG0:_TG0]

    m_prev = m_ref[0]
    for c in range(nc):
        sl = slice(c * L, (c + 1) * L)
        bc, mlc = b[:, sl], m_loc[:, sl]
        b_last = jnp.broadcast_to(bc[:, L - 1:L], bc.shape)
        m_new = jnp.maximum(b_last + m_prev, jnp.broadcast_to(mlc[:, L - 1:L], bc.shape))
        g = bc + m_prev
        mt = jnp.maximum(g, mlc)
        wi_s[:, sl] = jnp.exp(g - mt)
        c2_s[:, sl] = mt - bc
        emt_s[:, sl] = jnp.exp(-mt)
        wk_s[:, sl] = jnp.exp(b_last - bc + ic[:, sl] - m_new)
        ast_s[c] = jnp.exp(b_last + m_prev - m_new)
        upad = jnp.concatenate([u[:, sl], jnp.zeros((L - SUBLANES, L), jnp.float32)], axis=0)
        ucol_s[c * L:(c + 1) * L, :] = upad.T
        m_prev = m_new
    m_ref[0] = m_prev

    s_i = lax.broadcasted_iota(jnp.int32, (L, L), 0)
    t_i = lax.broadcasted_iota(jnp.int32, (L, L), 1)
    causal = s_i <= t_i
    lane = lax.broadcasted_iota(jnp.int32, (L, LANES), 1)
    zeros_q = jnp.zeros((DK, L), jnp.float32)

    for c in range(nc):
        rows = slice(c * L, (c + 1) * L)
        for h in range(NH):
            pair, half = h // 2, h % 2
            hv = slice(h * DV, (h + 1) * DV)
            kp = k_s[rows, pair * LANES:(pair + 1) * LANES]
            qth = qt_s[h * DK:(h + 1) * DK, rows]
            qz = jnp.concatenate([qth, zeros_q] if half == 0 else [zeros_q, qth], axis=0).astype(bf16)
            st = _bdot(kp.astype(bf16), qz)
            pt = st * jnp.exp(jnp.where(causal, ucol_s[rows, h:h + 1] - c2_s[h:h + 1, rows], -jnp.inf))
            rs = jnp.sum(pt, axis=0, keepdims=True)
            vt = vt_s[hv, rows]
            state = st_ref[0, h]
            sq = _bdot(state.astype(bf16), qz) * wi_s[h:h + 1, rows]
            num = _bdot(vt.astype(bf16), pt.astype(bf16)) + sq[0:DV]
            den = sq[DV:DV + 1] + rs
            hh = num / jnp.maximum(jnp.abs(den), emt_s[h:h + 1, rows])
            hn = hh * lax.rsqrt(jnp.mean(hh * hh, axis=0, keepdims=True) + EPS) * gmhc_ref[hv, :]
            mixt_s[hv, rows] = jax.nn.sigmoid(ogt_s[hv, rows]) * hn
            wkr = wk_s[h:h + 1, rows]
            vw = jnp.concatenate([vt * wkr, jnp.broadcast_to(wkr, (2 * SUBLANES, L))], axis=0)
            km = jnp.where((lane >= half * DK) & (lane < (half + 1) * DK), kp, 0.0)
            st_ref[0, h] = ast_s[c][h:h + 1, :] * state + _bdot(vw.astype(bf16), km.astype(bf16))

    out = x + lax.dot_general(mixt_s[...].astype(bf16), wout_ref[0:MLSTM_W, :], (((0,), (0,)), ((), ())),
                              preferred_element_type=jnp.float32)
    x1_ref[0] = out + _bdot(yc, wout_ref[MLSTM_W:, :])


def _mixer_prompt(x, weights, TS, L):
    B, S, _ = x.shape
    nb = S // TS
    f32 = jnp.float32
    kern = functools.partial(_mixer_prompt_kernel, L=L, TS=TS)
    return pl.pallas_call(
        kern,
        grid=(B, nb),
        in_specs=[
            pl.BlockSpec((1, TS, D_MODEL), lambda b, j: (b, j, 0)),
            _const_spec((1, D_MODEL)),
            _const_spec((D_MODEL, NPROJ_COLS)),
            _const_spec((TPROJ_ROWS, D_MODEL)),
            _const_spec((2 * SUBLANES, TS)),
            _const_spec((MLSTM_W, LANES)),
            _const_spec((CONV_W, CONV_CH)),
            _const_spec((D_MODEL, D_MODEL)),
        ],
        out_specs=[
            pl.BlockSpec((1, TS, D_MODEL), lambda b, j: (b, j, 0)),
            pl.BlockSpec((1, NH, ST_ROWS, LANES), lambda b, j: (b, 0, 0, 0)),
            pl.BlockSpec((1, SUBLANES, LANES), lambda b, j: (b, 0, 0)),
            pl.BlockSpec((1, SUBLANES, CONV_CH), lambda b, j: (b, 0, 0)),
        ],
        out_shape=[
            jax.ShapeDtypeStruct((B, S, D_MODEL), f32),
            jax.ShapeDtypeStruct((B, NH, ST_ROWS, LANES), f32),
            jax.ShapeDtypeStruct((B, SUBLANES, LANES), f32),
            jax.ShapeDtypeStruct((B, SUBLANES, CONV_CH), f32),
        ],
        scratch_shapes=[
            pltpu.VMEM((QK_W, TS), f32),
            pltpu.VMEM((MLSTM_W, TS), f32),
            pltpu.VMEM((MLSTM_W, TS), f32),
            pltpu.VMEM((TS, QK_W), f32),
            pltpu.VMEM((MLSTM_W, TS), f32),
            pltpu.VMEM((TS + SUBLANES, CONV_CH), f32),
            pltpu.VMEM((TS, LANES), f32),
            pltpu.VMEM((SUBLANES, TS), f32),
            pltpu.VMEM((SUBLANES, TS), f32),
            pltpu.VMEM((SUBLANES, TS), f32),
            pltpu.VMEM((SUBLANES, TS), f32),
            pltpu.VMEM((TS // L, SUBLANES, LANES), f32),
        ],
        compiler_params=pltpu.CompilerParams(
            dimension_semantics=("arbitrary", "arbitrary"), vmem_limit_bytes=VMEM_LIMIT),
        name="mixer_prompt",
    )(x, *weights)


def _prep_prompt_weights(g_mix, w_in, b_ig, b_fg, g_mh, conv_w, w_out, TS):
    bf16 = jnp.bfloat16
    f32 = jnp.float32
    ig0 = 2 * QK_W + MLSTM_W
    fg0 = ig0 + NH
    og0 = fg0 + NH
    bg0 = og0 + MLSTM_W
    wn = jnp.concatenate([w_in[:, QK_W:2 * QK_W], w_in[:, bg0:]], axis=1).astype(bf16)
    gate_rows = jnp.zeros((2 * SUBLANES, D_MODEL), w_in.dtype)
    gate_rows = gate_rows.at[0:NH].set(w_in[:, ig0:fg0].T)
    gate_rows = gate_rows.at[SUBLANES:SUBLANES + NH].set(w_in[:, fg0:og0].T)
    wt = jnp.concatenate([w_in[:, 0:QK_W].T, w_in[:, 2 * QK_W:ig0].T, w_in[:, og0:bg0].T, gate_rows],
                         axis=0).astype(bf16)
    gb = jnp.zeros((2 * SUBLANES,), f32).at[0:NH].set(b_ig).at[SUBLANES:SUBLANES + NH].set(b_fg)
    gbt = jnp.broadcast_to(gb[:, None], (2 * SUBLANES, TS))
    gmhc = jnp.broadcast_to(g_mh[:, None], (MLSTM_W, LANES))
    return (g_mix.reshape(1, D_MODEL), wn, wt, gbt, gmhc, conv_w, w_out.astype(bf16))


def _unpack_prompt_state(st, m):
    B = st.shape[0]
    even = st[:, 0::2, :, 0:DK]
    odd = st[:, 1::2, :, DK:2 * DK]
    s = jnp.stack([even, odd], axis=2).reshape(B, NH, ST_ROWS, DK)
    return jnp.swapaxes(s[:, :, :DV, :], -1, -2), s[:, :, DV, :], m[:, :NH, 0]


_Q0, _K0, _V0, _OG0, _BG0, _CG0, _HC0, _GT0 = 0, 256, 512, 1024, 1536, 2048, 2560, 3072
PROJ_COLS = _GT0 + 2 * LANES


def _row_scan(x, seg, op, fill):
    pos = lax.broadcasted_iota(jnp.int32, x.shape, 0) & (seg - 1)
    k = 1
    while k < seg:
        x = op(x, jnp.where(pos >= k, pltpu.roll(x, k, 0), fill))
        k *= 2
    return x


def _split_dot(x, sel, parts):
    acc = None
    rem = x
    for p in range(parts):
        hi = rem.astype(jnp.bfloat16)
        d = _bdot(hi, sel)
        acc = d if acc is None else acc + d
        if p + 1 < parts:
            rem = rem - hi.astype(jnp.float32)
    return acc


def _selectors():
    h = np.arange(NH)
    seg_qk = np.zeros((QK_W, LANES), np.float32)
    seg_qk[np.arange(QK_W), np.arange(QK_W) // DK] = 1.0
    exp_v = np.zeros((LANES, MLSTM_W), np.float32)
    exp_k = np.zeros((LANES, QK_W), np.float32)
    for i in h:
        exp_v[i, i * DV:(i + 1) * DV] = 1.0
        exp_k[i, i * DK:(i + 1) * DK] = 1.0
    mean_v = np.kron(np.eye(NH, dtype=np.float32), np.full((DV, DV), 1.0 / DV, np.float32))
    return tuple(jnp.asarray(m, jnp.bfloat16) for m in (seg_qk, exp_v, exp_k, mean_v))


def _mixer_sample_kernel(x_ref, cv_ref, c0_ref, n0_ref, m0_ref,
                         gmix_ref, win_ref, gb_ref, gmh_ref, cw_ref, wout_ref,
                         segqk_ref, expv_ref, expk_ref, meanv_ref,
                         x1_ref, cst_ref, nst_ref, mst_ref, cvo_ref,
                         xp_s, z_s, mp_s, qc_s, *, NSEQ):
    R = SUBLANES
    TS = NSEQ * R
    bf16 = jnp.bfloat16
    f32 = jnp.float32

    xp_s[:, SAMPLE_T:, :] = jnp.zeros((NSEQ, R - SAMPLE_T, D_MODEL), f32)
    xp_s[:, 0:SAMPLE_T, :] = x_ref[...]
    x = xp_s[...].reshape(TS, D_MODEL)
    a = _rms(x, gmix_ref[...]).astype(bf16)
    u = _bdot(a, win_ref[...])

    pos = lax.broadcasted_iota(jnp.int32, (TS, 1), 0) & (R - 1)
    real = pos < SAMPLE_T

    z = u[:, _CG0:_HC0] * u[:, _HC0:_GT0]
    z3 = z.reshape(NSEQ, R, CONV_CH)
    z_s[...] = z3
    z_s[:, R - (CONV_W - 1):, :] = cv_ref[...]
    zf = z_s[...].reshape(TS, CONV_CH)
    zm1 = jnp.where(pos >= 1, pltpu.roll(z, 1, 0), pltpu.roll(zf, TS - (R - 1), 0))
    zm2 = jnp.where(pos >= 2, pltpu.roll(z, 2, 0), pltpu.roll(zf, TS - (R - 2), 0))
    yc = u[:, _BG0:_CG0] * (cw_ref[0:1, :] * zm2 + cw_ref[1:2, :] * zm1 + cw_ref[2:3, :] * z)
    cvo_ref[...] = z3[:, SAMPLE_T - (CONV_W - 1):SAMPLE_T, :]

    gates = u[:, _GT0:PROJ_COLS] + gb_ref[...]
    ic = jnp.where(real, GATE_CAP * jnp.tanh(gates[:, :LANES] / GATE_CAP), NEG_BIG)
    lf = jnp.where(real, _log_sigmoid(gates[:, LANES:]), 0.0)
    b = _row_scan(lf, R, jnp.add, 0.0)
    uu = ic - b
    m_loc = b + _row_scan(uu, R, jnp.maximum, -jnp.inf)

    mp_s[...] = jnp.zeros(mp_s.shape, f32)
    mp_s[:, :, 0:NH] = m0_ref[...]
    m_prev = mp_s[...]
    b3 = b.reshape(NSEQ, R, LANES)
    ml3 = m_loc.reshape(NSEQ, R, LANES)
    b_last = b3[:, R - 1:R, :]
    m_new = jnp.maximum(b_last + m_prev, ml3[:, R - 1:R, :])
    g3 = b3 + m_prev
    mt3 = jnp.maximum(g3, ml3)
    wi = jnp.exp(g3 - mt3).reshape(TS, LANES)
    c2 = (mt3 - b3).reshape(TS, LANES)
    emt = jnp.exp(-mt3).reshape(TS, LANES)
    wk = jnp.exp(b_last - b3 + ic.reshape(NSEQ, R, LANES) - m_new).reshape(TS, LANES)
    a_st = jnp.broadcast_to(jnp.exp(b_last + m_prev - m_new), (NSEQ, R, LANES)).reshape(TS, LANES)
    mst_ref[...] = m_new[:, :, 0:NH]

    q = u[:, _Q0:_K0] * (DK ** -0.5)
    k = u[:, _K0:_V0]
    v = u[:, _V0:_OG0]
    rs = jnp.zeros((TS, LANES), f32)
    num = jnp.zeros((TS, MLSTM_W), f32)
    for d in range(SAMPLE_T):
        kd = k if d == 0 else pltpu.roll(k, d, 0)
        vd = v if d == 0 else pltpu.roll(v, d, 0)
        ud = uu if d == 0 else pltpu.roll(uu, d, 0)
        p = _split_dot(q * kd, segqk_ref[...], 2) * jnp.exp(ud - c2)
        rs = rs + p
        num = num + _split_dot(p, expv_ref[...], 2) * vd

    n0x = jnp.broadcast_to(n0_ref[...], (NSEQ, R, QK_W)).reshape(TS, QK_W)
    qn = _split_dot(q * n0x, segqk_ref[...], 2)
    den = wi * qn + rs
    rden = 1.0 / jnp.maximum(jnp.abs(den), emt)
    kw = k * _split_dot(wk, expk_ref[...], 2)
    ax = _split_dot(a_st, expv_ref[...], 3)
    lane_q = lax.broadcasted_iota(jnp.int32, (R, QK_W), 1)
    for i in range(NSEQ):
        rows = slice(i * R, (i + 1) * R)
        qi = q[rows, :]
        lhs = jnp.concatenate(
            [jnp.where((lane_q >= h * DK) & (lane_q < (h + 1) * DK), qi, 0.0) for h in range(NH)], axis=0)
        c0 = c0_ref[i]
        qc = _bdot(lhs.astype(bf16), c0.reshape(QK_W, DV).astype(bf16))
        qc_s[rows, :] = jnp.concatenate([qc[h * R:(h + 1) * R, :] for h in range(NH)], axis=1)
        kwi = kw[rows, :].astype(bf16)
        vi = v[rows, :].astype(bf16)
        for h in range(NH):
            dc = lax.dot_general(kwi[:, h * DK:(h + 1) * DK], vi[:, h * DV:(h + 1) * DV],
                                 (((0,), (0,)), ((), ())), preferred_element_type=f32)
            cst_ref[i, h] = ax[i * R:i * R + 1, h * DV:(h + 1) * DV] * c0[h] + dc

    hh = (_split_dot(wi, expv_ref[...], 2) * qc_s[...] + num) * _split_dot(rden, expv_ref[...], 2)
    ms = _split_dot(hh * hh, meanv_ref[...], 2)
    hm = jax.nn.sigmoid(u[:, _OG0:_BG0]) * (hh * lax.rsqrt(ms + EPS) * gmh_ref[...])
    mix = jnp.concatenate([hm, yc], axis=1).astype(bf16)
    out = x + _bdot(mix, wout_ref[...])
    x1_ref[...] = out.reshape(NSEQ, R, D_MODEL)[:, 0:SAMPLE_T, :]

    a_k = _split_dot(a_st, expk_ref[...], 3).reshape(NSEQ, R, QK_W)[:, 0:1, :]
    nst_ref[...] = a_k * n0_ref[...] + jnp.sum(kw.reshape(NSEQ, R, QK_W), axis=1, keepdims=True)


def _mixer_sample(x, cv, c0, n0, m0, weights, NSEQ):
    Bs = x.shape[0]
    f32 = jnp.float32
    kern = functools.partial(_mixer_sample_kernel, NSEQ=NSEQ)
    x_spec = pl.BlockSpec((NSEQ, SAMPLE_T, D_MODEL), lambda t: (t, 0, 0))
    cv_spec = pl.BlockSpec((NSEQ, CONV_W - 1, CONV_CH), lambda t: (t, 0, 0))
    c_spec = pl.BlockSpec((NSEQ, NH, DK, DV), lambda t: (t, 0, 0, 0))
    n_spec = pl.BlockSpec((NSEQ, 1, QK_W), lambda t: (t, 0, 0))
    m_spec = pl.BlockSpec((NSEQ, 1, NH), lambda t: (t, 0, 0))
    return pl.pallas_call(
        kern,
        grid=(Bs // NSEQ,),
        in_specs=[
            x_spec, cv_spec, c_spec, n_spec, m_spec,
            _const_spec((1, D_MODEL)),
            _const_spec((D_MODEL, PROJ_COLS)),
            _const_spec((1, 2 * LANES)),
            _const_spec((1, MLSTM_W)),
            _const_spec((CONV_W, CONV_CH)),
            _const_spec((D_MODEL, D_MODEL)),
            _const_spec((QK_W, LANES)),
            _const_spec((LANES, MLSTM_W)),
            _const_spec((LANES, QK_W)),
            _const_spec((MLSTM_W, MLSTM_W)),
        ],
        out_specs=[x_spec, c_spec, n_spec, m_spec, cv_spec],
        out_shape=[
            jax.ShapeDtypeStruct((Bs, SAMPLE_T, D_MODEL), f32),
            jax.ShapeDtypeStruct((Bs, NH, DK, DV), f32),
            jax.ShapeDtypeStruct((Bs, 1, QK_W), f32),
            jax.ShapeDtypeStruct((Bs, 1, NH), f32),
            jax.ShapeDtypeStruct((Bs, CONV_W - 1, CONV_CH), f32),
        ],
        scratch_shapes=[
            pltpu.VMEM((NSEQ, SUBLANES, D_MODEL), f32),
            pltpu.VMEM((NSEQ, SUBLANES, CONV_CH), f32),
            pltpu.VMEM((NSEQ, 1, LANES), f32),
            pltpu.VMEM((NSEQ * SUBLANES, MLSTM_W), f32),
        ],
        compiler_params=pltpu.CompilerParams(
            dimension_semantics=("arbitrary",), vmem_limit_bytes=VMEM_LIMIT),
        name="mixer_sample",
    )(x, cv, c0, n0, m0, *weights)


def _prep_sample_weights(g_mix, w_in, b_ig, b_fg, g_mh, conv_w, w_out):
    bf16 = jnp.bfloat16
    ig0 = 2 * QK_W + MLSTM_W
    fg0 = ig0 + NH
    og0 = fg0 + NH
    gate_cols = jnp.zeros((D_MODEL, 2 * LANES), w_in.dtype)
    gate_cols = gate_cols.at[:, 0:NH].set(w_in[:, ig0:fg0])
    gate_cols = gate_cols.at[:, LANES:LANES + NH].set(w_in[:, fg0:og0])
    w_all = jnp.concatenate([w_in[:, :ig0], w_in[:, og0:], gate_cols], axis=1).astype(bf16)
    gb = jnp.zeros((1, 2 * LANES), jnp.float32)
    gb = gb.at[0, 0:NH].set(b_ig).at[0, LANES:LANES + NH].set(b_fg)
    return (g_mix.reshape(1, D_MODEL), w_all, gb, g_mh.reshape(1, MLSTM_W), conv_w,
            w_out.astype(bf16)) + _selectors()


def _ffn_kernel(x_ref, pe_ref, gffn_ref, wg_ref, wu_ref, wd_ref, wple_ref, gple_ref, wpg_ref,
                gfin_ref, y_ref, *, final_norm, n_sub):
    sub = x_ref.shape[0] // n_sub
    for s in range(n_sub):
        rows = slice(s * sub, (s + 1) * sub)
        x = x_ref[rows, :]
        f = _rms(x, gffn_ref[...]).astype(jnp.bfloat16)
        gate = _bdot(f, wg_ref[...])
        up = _bdot(f, wu_ref[...])
        hmid = (gate * jax.nn.sigmoid(gate) * up).astype(jnp.bfloat16)
        x = x + _bdot(hmid, wd_ref[...])
        e = _rms(_bdot(pe_ref[rows, :].astype(jnp.bfloat16), wple_ref[...]), gple_ref[...])
        x = x + jax.nn.sigmoid(_bdot(x.astype(jnp.bfloat16), wpg_ref[...])) * e
        if final_norm:
            x = _rms(x, gfin_ref[...])
        y_ref[rows, :] = x


def _ffn(x, pe, weights, TM, final_norm, n_sub=1):
    T = x.shape[0]
    kern = functools.partial(_ffn_kernel, final_norm=final_norm, n_sub=n_sub)
    return pl.pallas_call(
        kern,
        grid=(T // TM,),
        in_specs=[
            pl.BlockSpec((TM, D_MODEL), lambda t: (t, 0)),
            pl.BlockSpec((TM, PLE_DIM), lambda t: (t, 0)),
            _const_spec((1, D_MODEL)),
            _const_spec((D_MODEL, D_FF)),
            _const_spec((D_MODEL, D_FF)),
            _const_spec((D_FF, D_MODEL)),
            _const_spec((PLE_DIM, D_MODEL)),
            _const_spec((1, D_MODEL)),
            _const_spec((D_MODEL, D_MODEL)),
            _const_spec((1, D_MODEL)),
        ],
        out_specs=pl.BlockSpec((TM, D_MODEL), lambda t: (t, 0)),
        out_shape=jax.ShapeDtypeStruct((T, D_MODEL), jnp.float32),
        compiler_params=pltpu.CompilerParams(
            dimension_semantics=("arbitrary",), vmem_limit_bytes=VMEM_LIMIT),
        name="ffn",
    )(x, pe, *weights)


def kernel(x_prompt, x_sample, p_prompt, p_sample, state_C, state_n, state_m, state_conv,
           g_mix, w_in, b_ig, b_fg, g_mh, conv_w, w_out, g_ffn, w_gate, w_up, w_down,
           w_ple, g_ple, w_pg, g_final):
    bf16 = jnp.bfloat16
    depth = g_mix.shape[0]
    B, S, _ = x_prompt.shape
    Bs, Ss, _ = x_sample.shape
    assert Ss == SAMPLE_T and S % PROMPT_TS == 0 and Bs % SAMPLE_NSEQ == 0

    xp = x_prompt
    xs = x_sample
    outs = [[] for _ in range(8)]
    for i in range(depth):
        last = i == depth - 1
        fw = (g_ffn[i].reshape(1, D_MODEL), w_gate[i].astype(bf16), w_up[i].astype(bf16),
              w_down[i].astype(bf16), w_ple[i].astype(bf16), g_ple[i].reshape(1, D_MODEL),
              w_pg[i].astype(bf16), g_final.reshape(1, D_MODEL))

        pw = _prep_prompt_weights(g_mix[i], w_in[i], b_ig[i], b_fg[i], g_mh[i], conv_w[i], w_out[i],
                                  PROMPT_TS)
        x1p, stp, mpr, cvp = _mixer_prompt(xp, pw, TS=PROMPT_TS, L=PROMPT_L)
        cp, n_p, mp = _unpack_prompt_state(stp, mpr)
        xp = _ffn(x1p.reshape(B * S, D_MODEL), p_prompt[i].reshape(B * S, PLE_DIM), fw, 512,
                  last, n_sub=2).reshape(B, S, D_MODEL)

        sw = _prep_sample_weights(g_mix[i], w_in[i], b_ig[i], b_fg[i], g_mh[i], conv_w[i], w_out[i])
        x1s, cs, n_s, ms, cvs = _mixer_sample(
            xs, state_conv[i], state_C[i], state_n[i].reshape(Bs, 1, QK_W),
            state_m[i].reshape(Bs, 1, NH), sw, SAMPLE_NSEQ)
        xs = _ffn(x1s.reshape(Bs * Ss, D_MODEL), p_sample[i].reshape(Bs * Ss, PLE_DIM), fw, 256,
                  last).reshape(Bs, Ss, D_MODEL)

        new = (cp, n_p, mp, cvp[:, SUBLANES - (CONV_W - 1):],
               cs, n_s.reshape(Bs, NH, DK), ms.reshape(Bs, NH), cvs)
        for lst, v in zip(outs, new):
            lst.append(v)

    return (xp, xs) + tuple(jnp.stack(l) for l in outs)
```

```python
import functools

import numpy as np
import jax
import jax.numpy as jnp
from jax import lax
from jax.experimental import pallas as pl
from jax.experimental.pallas import tpu as pltpu

D_MODEL = 1024
NH = 4
DK = 64
DV = 128
MLSTM_W = NH * DV
QK_W = NH * DK
CONV_CH = 512
CONV_W = 3
D_FF = 2816
PLE_DIM = 256
PROMPT_L = 128
PROMPT_TS = 512
PROMPT_NSUB = 2
SAMPLE_T = 4
SAMPLE_NSEQ = 32
GATE_CAP = 15.0
EPS = 1e-6

LANES = 128
SUBLANES = 8
NEG_BIG = -1e30

VMEM_LIMIT = 56 * 1024 * 1024


def _rms(x, g):
    ms = jnp.mean(x * x, axis=-1, keepdims=True)
    return x * lax.rsqrt(ms + EPS) * g


def _bdot(a, b):
    return jnp.dot(a, b, preferred_element_type=jnp.float32)


def _log_sigmoid(x):
    return jnp.minimum(x, 0.0) - jnp.log1p(jnp.exp(-jnp.abs(x)))


def _const_spec(shape):
    nd = len(shape)
    return pl.BlockSpec(shape, lambda *_: (0,) * nd, pipeline_mode=pl.Buffered(1))


_TQ0, _TV0, _TOG0, _TG0 = 0, 256, 768, 1280
TPROJ_ROWS = _TG0 + 2 * SUBLANES
_NK0, _NBG0, _NCG0, _NHC0, NPROJ_COLS = 0, 256, 768, 1280, 1792
ST_ROWS = DV + 2 * SUBLANES


def _lane_scan(x, seg, op, fill):
    pos = lax.broadcasted_iota(jnp.int32, x.shape, 1) & (seg - 1)
    k = 1
    while k < seg:
        x = op(x, jnp.where(pos >= k, pltpu.roll(x, k, 1), fill))
        k *= 2
    return x


def _mixer_prompt_kernel(x_ref, gmix_ref, wn_ref, wt_ref, gbt_ref, gmhc_ref, cw_ref, wout_ref,
                         x1_ref, c_ref, n_ref, m_ref, cvo_ref,
                         qt_s, vt_s, ogt_s, k_s, mixt_s, z_s, ucol_s, wi_s, c2_s, emt_s, wk_s, ast_s, st_s,
                         *, L, TB, NSUB):
    j = pl.program_id(1)

    @pl.when(j == 0)
    def _():
        z_s[0, 0:SUBLANES, :] = jnp.zeros((SUBLANES, CONV_CH), jnp.float32)
        st_s[...] = jnp.zeros(st_s.shape, jnp.float32)
        m_ref[...] = jnp.zeros(m_ref.shape, jnp.float32)

    m_prev = m_ref[0]
    sub = [dict(sb=sb, x_ref=x_ref, gmix_ref=gmix_ref, wn_ref=wn_ref, wt_ref=wt_ref, gbt_ref=gbt_ref,
                gmhc_ref=gmhc_ref, cw_ref=cw_ref, wout_ref=wout_ref, x1_ref=x1_ref, qt_s=qt_s.at[sb],
                vt_s=vt_s.at[sb], ogt_s=ogt_s.at[sb], k_s=k_s.at[sb], mixt_s=mixt_s.at[sb], z_s=z_s,
                ucol_s=ucol_s.at[sb], wi_s=wi_s.at[sb], c2_s=c2_s.at[sb], emt_s=emt_s.at[sb],
                wk_s=wk_s.at[sb], ast_s=ast_s.at[sb], st_s=st_s, L=L, TS=TB) for sb in range(NSUB)]
    for s in sub:
        _prompt_project(s)
    for s in sub:
        m_prev = _prompt_chunks(s, m_prev)
    for s in sub:
        _prompt_output(s)
    m_ref[0] = m_prev
    cvo_ref[0] = z_s[NSUB - 1, TB:TB + SUBLANES, :]
    z_s[0, 0:SUBLANES, :] = z_s[NSUB - 1, TB:TB + SUBLANES, :]

    @pl.when(j == pl.num_programs(1) - 1)
    def _():
        for pr in range(NH // 2):
            state = st_s[pr]
            c_pair = state[0:DV, :].T
            c_ref[0, 2 * pr] = c_pair[0:DK, :]
            c_ref[0, 2 * pr + 1] = c_pair[DK:, :]
            n_ref[0, pr:pr + 1, :] = state[DV:DV + 1, :]


def _prompt_project(s):
    sb, L, TS = s["sb"], s["L"], s["TS"]
    x_ref, gmix_ref, wn_ref, wt_ref, gbt_ref, cw_ref = (
        s["x_ref"], s["gmix_ref"], s["wn_ref"], s["wt_ref"], s["gbt_ref"], s["cw_ref"])
    qt_s, vt_s, ogt_s, k_s, z_s = s["qt_s"], s["vt_s"], s["ogt_s"], s["k_s"], s["z_s"]
    bf16 = jnp.bfloat16

    x = x_ref[0, sb * TS:(sb + 1) * TS, :]
    a = _rms(x, gmix_ref[...]).astype(bf16)
    nt_dims = (((1,), (1,)), ((), ()))

    gt = lax.dot_general(wt_ref[_TG0:TPROJ_ROWS, :], a, nt_dims,
                         preferred_element_type=jnp.float32) + gbt_ref[...]
    ic = GATE_CAP * jnp.tanh(gt[0:SUBLANES] / GATE_CAP)
    lf = _log_sigmoid(gt[SUBLANES:])
    b = _lane_scan(lf, L, jnp.add, 0.0)
    u = ic - b
    m_loc = b + _lane_scan(u, L, jnp.maximum, -jnp.inf)

    un = _bdot(a, wn_ref[...])
    k_s[...] = un[:, _NK0:_NBG0]
    z = un[:, _NCG0:_NHC0] * un[:, _NHC0:NPROJ_COLS]
    if sb > 0:
        z_s[sb, 0:SUBLANES, :] = z_s[sb - 1, TS:TS + SUBLANES, :]
    z_s[sb, SUBLANES:SUBLANES + TS, :] = z
    yconv = (cw_ref[0:1, :] * z_s[sb, SUBLANES - 2:SUBLANES - 2 + TS, :]
             + cw_ref[1:2, :] * z_s[sb, SUBLANES - 1:SUBLANES - 1 + TS, :]
             + cw_ref[2:3, :] * z)
    yc = (un[:, _NBG0:_NCG0] * yconv).astype(bf16)

    ut = lax.dot_general(wt_ref[0:_TG0, :], a, nt_dims, preferred_element_type=jnp.float32)
    qt_s[...] = ut[_TQ0:_TV0] * (DK ** -0.5)
    vt_s[...] = ut[_TV0:_TOG0]
    ogt_s[...] = ut[_TOG0:_TG0]
    s.update(b=b, u=u, ic=ic, m_loc=m_loc, yc=yc)


def _prompt_chunks(s, m_prev):
    L, TS = s["L"], s["TS"]
    gmhc_ref, qt_s, vt_s, ogt_s, k_s, mixt_s, ucol_s, wi_s, c2_s, emt_s, wk_s, ast_s, st_s = (
        s["gmhc_ref"], s["qt_s"], s["vt_s"], s["ogt_s"], s["k_s"], s["mixt_s"], s["ucol_s"], s["wi_s"],
        s["c2_s"], s["emt_s"], s["wk_s"], s["ast_s"], s["st_s"])
    b, u, ic, m_loc = s["b"], s["u"], s["ic"], s["m_loc"]
    nc = TS // L
    bf16 = jnp.bfloat16

    for c in range(nc):
        sl = slice(c * L, (c + 1) * L)
        bc, mlc = b[:, sl], m_loc[:, sl]
        b_last = jnp.broadcast_to(bc[:, L - 1:L], bc.shape)
        m_new = jnp.maximum(b_last + m_prev, jnp.broadcast_to(mlc[:, L - 1:L], bc.shape))
        g = bc + m_prev
        mt = jnp.maximum(g, mlc)
        wi_s[:, sl] = jnp.exp(g - mt)
        c2_s[:, sl] = mt - bc
        emt_s[:, sl] = jnp.exp(-mt)
        wk_s[:, sl] = jnp.exp(b_last - bc + ic[:, sl] - m_new)
        ast_s[c] = jnp.exp(b_last + m_prev - m_new)
        upad = jnp.concatenate([u[:, sl], jnp.zeros((L - SUBLANES, L), jnp.float32)], axis=0)
        ucol_s[c * L:(c + 1) * L, :] = upad.T
        m_prev = m_new

    s_i = lax.broadcasted_iota(jnp.int32, (L, L), 0)
    t_i = lax.broadcasted_iota(jnp.int32, (L, L), 1)
    causal = s_i <= t_i
    low_half = lax.broadcasted_iota(jnp.int32, (L, LANES), 1) < DK
    zeros_q = jnp.zeros((DK, L), jnp.float32)
    zeros_p = jnp.zeros((L, L), bf16)

    def pair_row(ref, h0, rows):
        return jnp.concatenate([ref[h0:h0 + 1, rows], ref[h0 + 1:h0 + 2, rows]], axis=1)

    for c in range(nc):
        rows = slice(c * L, (c + 1) * L)
        for pr in range(NH // 2):
            h0 = 2 * pr
            hv0 = slice(h0 * DV, (h0 + 1) * DV)
            hv1 = slice((h0 + 1) * DV, (h0 + 2) * DV)
            kp = k_s[rows, pr * LANES:(pr + 1) * LANES]
            q0 = qt_s[h0 * DK:(h0 + 1) * DK, rows]
            q1 = qt_s[(h0 + 1) * DK:(h0 + 2) * DK, rows]
            qbd = jnp.concatenate([jnp.concatenate([q0, zeros_q], axis=1),
                                   jnp.concatenate([zeros_q, q1], axis=1)], axis=0).astype(bf16)
            st = _bdot(kp.astype(bf16), qbd)
            arg = jnp.concatenate(
                [jnp.where(causal, ucol_s[rows, h:h + 1] - c2_s[h:h + 1, rows], -jnp.inf)
                 for h in (h0, h0 + 1)], axis=1)
            pt = st * jnp.exp(arg)
            rs = jnp.sum(pt, axis=0, keepdims=True)
            ptb = pt.astype(bf16)
            pbd = jnp.concatenate([jnp.concatenate([ptb[:, 0:L], zeros_p], axis=1),
                                   jnp.concatenate([zeros_p, ptb[:, L:]], axis=1)], axis=0)
            vt = jnp.concatenate([vt_s[hv0, rows], vt_s[hv1, rows]], axis=1)
            state = st_s[pr]
            sq = _bdot(state.astype(bf16), qbd) * pair_row(wi_s, h0, rows)
            num = _bdot(vt.astype(bf16), pbd) + sq[0:DV]
            den = sq[DV:DV + 1] + rs
            hh = num / jnp.maximum(jnp.abs(den), pair_row(emt_s, h0, rows))
            hn = hh * lax.rsqrt(jnp.mean(hh * hh, axis=0, keepdims=True) + EPS)
            mixt_s[hv0, rows] = jax.nn.sigmoid(ogt_s[hv0, rows]) * (hn[:, 0:L] * gmhc_ref[hv0, :])
            mixt_s[hv1, rows] = jax.nn.sigmoid(ogt_s[hv1, rows]) * (hn[:, L:] * gmhc_ref[hv1, :])
            wkr = pair_row(wk_s, h0, rows)
            vw = jnp.concatenate([vt * wkr, jnp.broadcast_to(wkr, (2 * SUBLANES, 2 * L))], axis=0)
            km = jnp.concatenate([jnp.where(low_half, kp, 0.0), jnp.where(low_half, 0.0, kp)], axis=0)
            decay = jnp.where(low_half[0:1, :], ast_s[c][h0:h0 + 1, :], ast_s[c][h0 + 1:h0 + 2, :])
            st_s[pr] = decay * state + _bdot(vw.astype(bf16), km.astype(bf16))
    return m_prev


def _prompt_output(s):
    sb, TS = s["sb"], s["TS"]
    x_ref, wout_ref, x1_ref, mixt_s = s["x_ref"], s["wout_ref"], s["x1_ref"], s["mixt_s"]
    tok = slice(sb * TS, (sb + 1) * TS)
    out = x_ref[0, tok, :] + lax.dot_general(
        mixt_s[...].astype(jnp.bfloat16), wout_ref[0:MLSTM_W, :], (((0,), (0,)), ((), ())),
        preferred_element_type=jnp.float32)
    x1_ref[0, tok, :] = out + _bdot(s["yc"], wout_ref[MLSTM_W:, :])


def _mixer_prompt(x, weights, TB, NSUB, L):
    B, S, _ = x.shape
    TS = TB * NSUB
    nb = S // TS
    f32 = jnp.float32
    kern = functools.partial(_mixer_prompt_kernel, L=L, TB=TB, NSUB=NSUB)
    return pl.pallas_call(
        kern,
        grid=(B, nb),
        in_specs=[
            pl.BlockSpec((1, TS, D_MODEL), lambda b, j: (b, j, 0)),
            _const_spec((1, D_MODEL)),
            _const_spec((D_MODEL, NPROJ_COLS)),
            _const_spec((TPROJ_ROWS, D_MODEL)),
            _const_spec((2 * SUBLANES, TB)),
            _const_spec((MLSTM_W, LANES)),
            _const_spec((CONV_W, CONV_CH)),
            _const_spec((D_MODEL, D_MODEL)),
        ],
        out_specs=[
            pl.BlockSpec((1, TS, D_MODEL), lambda b, j: (b, j, 0)),
            pl.BlockSpec((1, NH, DK, DV), lambda b, j: (b, 0, 0, 0)),
            pl.BlockSpec((1, NH // 2, 2 * DK), lambda b, j: (b, 0, 0)),
            pl.BlockSpec((1, SUBLANES, LANES), lambda b, j: (b, 0, 0)),
            pl.BlockSpec((1, SUBLANES, CONV_CH), lambda b, j: (b, 0, 0)),
        ],
        out_shape=[
            jax.ShapeDtypeStruct((B, S, D_MODEL), f32),
            jax.ShapeDtypeStruct((B, NH, DK, DV), f32),
            jax.ShapeDtypeStruct((B, NH // 2, 2 * DK), f32),
            jax.ShapeDtypeStruct((B, SUBLANES, LANES), f32),
            jax.ShapeDtypeStruct((B, SUBLANES, CONV_CH), f32),
        ],
        scratch_shapes=[
            pltpu.VMEM((NSUB, QK_W, TB), f32),
            pltpu.VMEM((NSUB, MLSTM_W, TB), f32),
            pltpu.VMEM((NSUB, MLSTM_W, TB), f32),
            pltpu.VMEM((NSUB, TB, QK_W), f32),
            pltpu.VMEM((NSUB, MLSTM_W, TB), f32),
            pltpu.VMEM((NSUB, TB + SUBLANES, CONV_CH), f32),
            pltpu.VMEM((NSUB, TB, LANES), f32),
            pltpu.VMEM((NSUB, SUBLANES, TB), f32),
            pltpu.VMEM((NSUB, SUBLANES, TB), f32),
            pltpu.VMEM((NSUB, SUBLANES, TB), f32),
            pltpu.VMEM((NSUB, SUBLANES, TB), f32),
            pltpu.VMEM((NSUB, TB // L, SUBLANES, LANES), f32),
            pltpu.VMEM((NH // 2, ST_ROWS, LANES), f32),
        ],
        compiler_params=pltpu.CompilerParams(
            dimension_semantics=("arbitrary", "arbitrary"), vmem_limit_bytes=VMEM_LIMIT),
        name="mixer_prompt",
    )(x, *weights)


def _prep_prompt_weights(g_mix, w_in, b_ig, b_fg, g_mh, conv_w, w_out, TS):
    bf16 = jnp.bfloat16
    f32 = jnp.float32
    ig0 = 2 * QK_W + MLSTM_W
    fg0 = ig0 + NH
    og0 = fg0 + NH
    bg0 = og0 + MLSTM_W
    wn = jnp.concatenate([w_in[:, QK_W:2 * QK_W], w_in[:, bg0:]], axis=1).astype(bf16)
    gate_rows = jnp.zeros((2 * SUBLANES, D_MODEL), w_in.dtype)
    gate_rows = gate_rows.at[0:NH].set(w_in[:, ig0:fg0].T)
    gate_rows = gate_rows.at[SUBLANES:SUBLANES + NH].set(w_in[:, fg0:og0].T)
    wt = jnp.concatenate([w_in[:, 0:QK_W].T, w_in[:, 2 * QK_W:ig0].T, w_in[:, og0:bg0].T, gate_rows],
                         axis=0).astype(bf16)
    gb = jnp.zeros((2 * SUBLANES,), f32).at[0:NH].set(b_ig).at[SUBLANES:SUBLANES + NH].set(b_fg)
    gbt = jnp.broadcast_to(gb[:, None], (2 * SUBLANES, TS))
    gmhc = jnp.broadcast_to(g_mh[:, None], (MLSTM_W, LANES))
    return (g_mix.reshape(1, D_MODEL), wn, wt, gbt, gmhc, conv_w, w_out.astype(bf16))


_Q0, _K0, _V0, _OG0, _BG0, _CG0, _HC0, _GT0 = 0, 256, 512, 1024, 1536, 2048, 2560, 3072
PROJ_COLS = _GT0 + 2 * LANES


def _row_scan(x, seg, op, fill):
    pos = lax.broadcasted_iota(jnp.int32, x.shape, 0) & (seg - 1)
    k = 1
    while k < seg:
        x = op(x, jnp.where(pos >= k, pltpu.roll(x, k, 0), fill))
        k *= 2
    return x


def _split_dot(x, sel, parts):
    acc = None
    rem = x
    for p in range(parts):
        hi = rem.astype(jnp.bfloat16)
        d = _bdot(hi, sel)
        acc = d if acc is None else acc + d
        if p + 1 < parts:
            rem = rem - hi.astype(jnp.float32)
    return acc


def _selectors():
    h = np.arange(NH)
    seg_qk = np.zeros((QK_W, LANES), np.float32)
    seg_qk[np.arange(QK_W), np.arange(QK_W) // DK] = 1.0
    exp_v = np.zeros((LANES, MLSTM_W), np.float32)
    exp_k = np.zeros((LANES, QK_W), np.float32)
    for i in h:
        exp_v[i, i * DV:(i + 1) * DV] = 1.0
        exp_k[i, i * DK:(i + 1) * DK] = 1.0
    mean_v = np.kron(np.eye(NH, dtype=np.float32), np.full((DV, DV), 1.0 / DV, np.float32))
    return tuple(jnp.asarray(m, jnp.bfloat16) for m in (seg_qk, exp_v, exp_k, mean_v))


def _mixer_sample_kernel(x_ref, cv_ref, c0_ref, n0_ref, m0_ref,
                         gmix_ref, win_ref, gb_ref, gmh_ref, cw_ref, wout_ref,
                         segqk_ref, expv_ref, expk_ref, meanv_ref,
                         x1_ref, cst_ref, nst_ref, mst_ref, cvo_ref,
                         xp_s, z_s, mp_s, qc_s, *, NSEQ):
    R = SUBLANES
    TS = NSEQ * R
    bf16 = jnp.bfloat16
    f32 = jnp.float32

    xp_s[:, SAMPLE_T:, :] = jnp.zeros((NSEQ, R - SAMPLE_T, D_MODEL), f32)
    xp_s[:, 0:SAMPLE_T, :] = x_ref[...]
    x = xp_s[...].reshape(TS, D_MODEL)
    a = _rms(x, gmix_ref[...]).astype(bf16)
    u = _bdot(a, win_ref[...])

    pos = lax.broadcasted_iota(jnp.int32, (TS, 1), 0) & (R - 1)
    real = pos < SAMPLE_T

    z = u[:, _CG0:_HC0] * u[:, _HC0:_GT0]
    z3 = z.reshape(NSEQ, R, CONV_CH)
    z_s[...] = z3
    z_s[:, R - (CONV_W - 1):, :] = cv_ref[...]
    zf = z_s[...].reshape(TS, CONV_CH)
    zm1 = jnp.where(pos >= 1, pltpu.roll(z, 1, 0), pltpu.roll(zf, TS - (R - 1), 0))
    zm2 = jnp.where(pos >= 2, pltpu.roll(z, 2, 0), pltpu.roll(zf, TS - (R - 2), 0))
    yc = u[:, _BG0:_CG0] * (cw_ref[0:1, :] * zm2 + cw_ref[1:2, :] * zm1 + cw_ref[2:3, :] * z)
    cvo_ref[...] = z3[:, SAMPLE_T - (CONV_W - 1):SAMPLE_T, :]

    gates = u[:, _GT0:PROJ_COLS] + gb_ref[...]
    ic = jnp.where(real, GATE_CAP * jnp.tanh(gates[:, :LANES] / GATE_CAP), NEG_BIG)
    lf = jnp.where(real, _log_sigmoid(gates[:, LANES:]), 0.0)
    b = _row_scan(lf, R, jnp.add, 0.0)
    uu = ic - b
    m_loc = b + _row_scan(uu, R, jnp.maximum, -jnp.inf)

    mp_s[...] = jnp.zeros(mp_s.shape, f32)
    mp_s[:, :, 0:NH] = m0_ref[...]
    m_prev = mp_s[...]
    b3 = b.reshape(NSEQ, R, LANES)
    ml3 = m_loc.reshape(NSEQ, R, LANES)
    b_last = b3[:, R - 1:R, :]
    m_new = jnp.maximum(b_last + m_prev, ml3[:, R - 1:R, :])
    g3 = b3 + m_prev
    mt3 = jnp.maximum(g3, ml3)
    wi = jnp.exp(g3 - mt3).reshape(TS, LANES)
    c2 = (mt3 - b3).reshape(TS, LANES)
    emt = jnp.exp(-mt3).reshape(TS, LANES)
    wk = jnp.exp(b_last - b3 + ic.reshape(NSEQ, R, LANES) - m_new).reshape(TS, LANES)
    a_st = jnp.broadcast_to(jnp.exp(b_last + m_prev - m_new), (NSEQ, R, LANES)).reshape(TS, LANES)
    mst_ref[...] = m_new[:, :, 0:NH]

    q = u[:, _Q0:_K0] * (DK ** -0.5)
    k = u[:, _K0:_V0]
    v = u[:, _V0:_OG0]
    rs = jnp.zeros((TS, LANES), f32)
    num = jnp.zeros((TS, MLSTM_W), f32)
    for d in range(SAMPLE_T):
        kd = k if d == 0 else pltpu.roll(k, d, 0)
        vd = v if d == 0 else pltpu.roll(v, d, 0)
        ud = uu if d == 0 else pltpu.roll(uu, d, 0)
        p = _split_dot(q * kd, segqk_ref[...], 2) * jnp.exp(ud - c2)
        rs = rs + p
        num = num + _split_dot(p, expv_ref[...], 2) * vd

    n0x = jnp.broadcast_to(n0_ref[...], (NSEQ, R, QK_W)).reshape(TS, QK_W)
    qn = _split_dot(q * n0x, segqk_ref[...], 2)
    den = wi * qn + rs
    rden = 1.0 / jnp.maximum(jnp.abs(den), emt)
    kw = k * _split_dot(wk, expk_ref[...], 2)
    ax = _split_dot(a_st, expv_ref[...], 3)
    lane_q = lax.broadcasted_iota(jnp.int32, (R, QK_W), 1)
    for i in range(NSEQ):
        rows = slice(i * R, (i + 1) * R)
        qi = q[rows, :]
        lhs = jnp.concatenate(
            [jnp.where((lane_q >= h * DK) & (lane_q < (h + 1) * DK), qi, 0.0) for h in range(NH)], axis=0)
        c0 = c0_ref[i]
        qc = _bdot(lhs.astype(bf16), c0.reshape(QK_W, DV).astype(bf16))
        qc_s[rows, :] = jnp.concatenate([qc[h * R:(h + 1) * R, :] for h in range(NH)], axis=1)
        kwi = kw[rows, :].astype(bf16)
        vi = v[rows, :].astype(bf16)
        for h in range(NH):
            dc = lax.dot_general(kwi[:, h * DK:(h + 1) * DK], vi[:, h * DV:(h + 1) * DV],
                                 (((0,), (0,)), ((), ())), preferred_element_type=f32)
            cst_ref[i, h] = ax[i * R:i * R + 1, h * DV:(h + 1) * DV] * c0[h] + dc

    hh = (_split_dot(wi, expv_ref[...], 2) * qc_s[...] + num) * _split_dot(rden, expv_ref[...], 2)
    ms = _split_dot(hh * hh, meanv_ref[...], 2)
    hm = jax.nn.sigmoid(u[:, _OG0:_BG0]) * (hh * lax.rsqrt(ms + EPS) * gmh_ref[...])
    mix = jnp.concatenate([hm, yc], axis=1).astype(bf16)
    out = x + _bdot(mix, wout_ref[...])
    x1_ref[...] = out.reshape(NSEQ, R, D_MODEL)[:, 0:SAMPLE_T, :]

    a_k = _split_dot(a_st, expk_ref[...], 3).reshape(NSEQ, R, QK_W)[:, 0:1, :]
    nst_ref[...] = a_k * n0_ref[...] + jnp.sum(kw.reshape(NSEQ, R, QK_W), axis=1, keepdims=True)


def _mixer_sample(x, cv, c0, n0, m0, weights, NSEQ):
    Bs = x.shape[0]
    f32 = jnp.float32
    kern = functools.partial(_mixer_sample_kernel, NSEQ=NSEQ)
    x_spec = pl.BlockSpec((NSEQ, SAMPLE_T, D_MODEL), lambda t: (t, 0, 0))
    cv_spec = pl.BlockSpec((NSEQ, CONV_W - 1, CONV_CH), lambda t: (t, 0, 0))
    c_spec = pl.BlockSpec((NSEQ, NH, DK, DV), lambda t: (t, 0, 0, 0))
    n_spec = pl.BlockSpec((NSEQ, 1, QK_W), lambda t: (t, 0, 0))
    m_spec = pl.BlockSpec((NSEQ, 1, NH), lambda t: (t, 0, 0))
    return pl.pallas_call(
        kern,
        grid=(Bs // NSEQ,),
        in_specs=[
            x_spec, cv_spec, c_spec, n_spec, m_spec,
            _const_spec((1, D_MODEL)),
            _const_spec((D_MODEL, PROJ_COLS)),
            _const_spec((1, 2 * LANES)),
            _const_spec((1, MLSTM_W)),
            _const_spec((CONV_W, CONV_CH)),
            _const_spec((D_MODEL, D_MODEL)),
            _const_spec((QK_W, LANES)),
            _const_spec((LANES, MLSTM_W)),
            _const_spec((LANES, QK_W)),
            _const_spec((MLSTM_W, MLSTM_W)),
        ],
        out_specs=[x_spec, c_spec, n_spec, m_spec, cv_spec],
        out_shape=[
            jax.ShapeDtypeStruct((Bs, SAMPLE_T, D_MODEL), f32),
            jax.ShapeDtypeStruct((Bs, NH, DK, DV), f32),
            jax.ShapeDtypeStruct((Bs, 1, QK_W), f32),
            jax.ShapeDtypeStruct((Bs, 1, NH), f32),
            jax.ShapeDtypeStruct((Bs, CONV_W - 1, CONV_CH), f32),
        ],
        scratch_shapes=[
            pltpu.VMEM((NSEQ, SUBLANES, D_MODEL), f32),
            pltpu.VMEM((NSEQ, SUBLANES, CONV_CH), f32),
            pltpu.VMEM((NSEQ, 1, LANES), f32),
            pltpu.VMEM((NSEQ * SUBLANES, MLSTM_W), f32),
        ],
        compiler_params=pltpu.CompilerParams(
            dimension_semantics=("arbitrary",), vmem_limit_bytes=VMEM_LIMIT),
        name="mixer_sample",
    )(x, cv, c0, n0, m0, *weights)


def _prep_sample_weights(g_mix, w_in, b_ig, b_fg, g_mh, conv_w, w_out):
    bf16 = jnp.bfloat16
    ig0 = 2 * QK_W + MLSTM_W
    fg0 = ig0 + NH
    og0 = fg0 + NH
    gate_cols = jnp.zeros((D_MODEL, 2 * LANES), w_in.dtype)
    gate_cols = gate_cols.at[:, 0:NH].set(w_in[:, ig0:fg0])
    gate_cols = gate_cols.at[:, LANES:LANES + NH].set(w_in[:, fg0:og0])
    w_all = jnp.concatenate([w_in[:, :ig0], w_in[:, og0:], gate_cols], axis=1).astype(bf16)
    gb = jnp.zeros((1, 2 * LANES), jnp.float32)
    gb = gb.at[0, 0:NH].set(b_ig).at[0, LANES:LANES + NH].set(b_fg)
    return (g_mix.reshape(1, D_MODEL), w_all, gb, g_mh.reshape(1, MLSTM_W), conv_w,
            w_out.astype(bf16)) + _selectors()


def _ffn_kernel(x_ref, pe_ref, gffn_ref, wg_ref, wu_ref, wd_ref, wple_ref, gple_ref, wpg_ref,
                gfin_ref, y_ref, *, final_norm, n_sub):
    sub = x_ref.shape[0] // n_sub
    for s in range(n_sub):
        rows = slice(s * sub, (s + 1) * sub)
        x = x_ref[rows, :]
        f = _rms(x, gffn_ref[...]).astype(jnp.bfloat16)
        gate = _bdot(f, wg_ref[...])
        up = _bdot(f, wu_ref[...])
        hmid = (gate * jax.nn.sigmoid(gate) * up).astype(jnp.bfloat16)
        x = x + _bdot(hmid, wd_ref[...])
        e = _rms(_bdot(pe_ref[rows, :].astype(jnp.bfloat16), wple_ref[...]), gple_ref[...])
        x = x + jax.nn.sigmoid(_bdot(x.astype(jnp.bfloat16), wpg_ref[...])) * e
        if final_norm:
            x = _rms(x, gfin_ref[...])
        y_ref[rows, :] = x


def _ffn(x, pe, weights, TM, final_norm, n_sub=1):
    T = x.shape[0]
    kern = functools.partial(_ffn_kernel, final_norm=final_norm, n_sub=n_sub)
    return pl.pallas_call(
        kern,
        grid=(T // TM,),
        in_specs=[
            pl.BlockSpec((TM, D_MODEL), lambda t: (t, 0)),
            pl.BlockSpec((TM, PLE_DIM), lambda t: (t, 0)),
            _const_spec((1, D_MODEL)),
            _const_spec((D_MODEL, D_FF)),
            _const_spec((D_MODEL, D_FF)),
            _const_spec((D_FF, D_MODEL)),
            _const_spec((PLE_DIM, D_MODEL)),
            _const_spec((1, D_MODEL)),
            _const_spec((D_MODEL, D_MODEL)),
            _const_spec((1, D_MODEL)),
        ],
        out_specs=pl.BlockSpec((TM, D_MODEL), lambda t: (t, 0)),
        out_shape=jax.ShapeDtypeStruct((T, D_MODEL), jnp.float32),
        compiler_params=pltpu.CompilerParams(
            dimension_semantics=("arbitrary",), vmem_limit_bytes=VMEM_LIMIT),
        name="ffn",
    )(x, pe, *weights)


def kernel(x_prompt, x_sample, p_prompt, p_sample, state_C, state_n, state_m, state_conv,
           g_mix, w_in, b_ig, b_fg, g_mh, conv_w, w_out, g_ffn, w_gate, w_up, w_down,
           w_ple, g_ple, w_pg, g_final):
    bf16 = jnp.bfloat16
    depth = g_mix.shape[0]
    B, S, _ = x_prompt.shape
    Bs, Ss, _ = x_sample.shape
    assert Ss == SAMPLE_T and S % PROMPT_TS == 0 and Bs % SAMPLE_NSEQ == 0

    xp = x_prompt
    xs = x_sample
    outs = [[] for _ in range(8)]
    for i in range(depth):
        last = i == depth - 1
        fw = (g_ffn[i].reshape(1, D_MODEL), w_gate[i].astype(bf16), w_up[i].astype(bf16),
              w_down[i].astype(bf16), w_ple[i].astype(bf16), g_ple[i].reshape(1, D_MODEL),
              w_pg[i].astype(bf16), g_final.reshape(1, D_MODEL))

        pw = _prep_prompt_weights(g_mix[i], w_in[i], b_ig[i], b_fg[i], g_mh[i], conv_w[i], w_out[i],
                                  PROMPT_TS)
        x1p, cp, n_p, mp, cvp = _mixer_prompt(xp, pw, TB=PROMPT_TS, NSUB=PROMPT_NSUB, L=PROMPT_L)
        xp = _ffn(x1p.reshape(B * S, D_MODEL), p_prompt[i].reshape(B * S, PLE_DIM), fw, 1024,
                  last, n_sub=4).reshape(B, S, D_MODEL)

        sw = _prep_sample_weights(g_mix[i], w_in[i], b_ig[i], b_fg[i], g_mh[i], conv_w[i], w_out[i])
        x1s, cs, n_s, ms, cvs = _mixer_sample(
            xs, state_conv[i], state_C[i], state_n[i].reshape(Bs, 1, QK_W),
            state_m[i].reshape(Bs, 1, NH), sw, SAMPLE_NSEQ)
        xs = _ffn(x1s.reshape(Bs * Ss, D_MODEL), p_sample[i].reshape(Bs * Ss, PLE_DIM), fw, 256,
                  last).reshape(Bs, Ss, D_MODEL)

        new = (cp, n_p.reshape(B, NH, DK), mp[:, :NH, 0], cvp[:, SUBLANES - (CONV_W - 1):],
               cs, n_s.reshape(Bs, NH, DK), ms.reshape(Bs, NH), cvs)
        for lst, v in zip(outs, new):
            lst.append(v)

    return (xp, xs) + tuple(jnp.stack(l) for l in outs)
```

```python
import functools

import numpy as np
import jax
import jax.numpy as jnp
from jax import lax
from jax.experimental import pallas as pl
from jax.experimental.pallas import tpu as pltpu

D_MODEL = 1024
NH = 4
DK = 64
DV = 128
MLSTM_W = NH * DV
QK_W = NH * DK
CONV_CH = 512
CONV_W = 3
D_FF = 2816
PLE_DIM = 256
PROMPT_L = 128
PROMPT_TS = 512
PROMPT_NSUB = 2
SAMPLE_T = 4
SAMPLE_NSEQ = 32
GATE_CAP = 15.0
EPS = 1e-6

LANES = 128
SUBLANES = 8
NEG_BIG = -1e30

VMEM_LIMIT = 56 * 1024 * 1024


def _rms(x, g):
    ms = jnp.mean(x * x, axis=-1, keepdims=True)
    return x * lax.rsqrt(ms + EPS) * g


def _bdot(a, b):
    return jnp.dot(a, b, preferred_element_type=jnp.float32)


def _log_sigmoid(x):
    return jnp.minimum(x, 0.0) - jnp.log1p(jnp.exp(-jnp.abs(x)))


def _const_spec(shape):
    nd = len(shape)
    return pl.BlockSpec(shape, lambda *_: (0,) * nd, pipeline_mode=pl.Buffered(1))


_TQ0, _TV0, _TOG0, _TG0 = 0, 256, 768, 1280
TPROJ_ROWS = _TG0 + 2 * SUBLANES
_NK0, _NBG0, _NCG0, _NHC0, NPROJ_COLS = 0, 256, 768, 1280, 1792
ST_ROWS = DV + 2 * SUBLANES


def _lane_scan(x, seg, op, fill):
    pos = lax.broadcasted_iota(jnp.int32, x.shape, 1) & (seg - 1)
    k = 1
    while k < seg:
        x = op(x, jnp.where(pos >= k, pltpu.roll(x, k, 1), fill))
        k *= 2
    return x


def _mixer_prompt_kernel(x_ref, gmix_ref, wn_ref, wt_ref, gbt_ref, gmhc_ref, cw_ref, wout_ref,
                         x1_ref, c_ref, n_ref, m_ref, cvo_ref,
                         qt_s, vt_s, ogt_s, k_s, mixt_s, z_s, ucol_s, wi_s, c2_s, emt_s, wk_s, ast_s, st_s,
                         *, L, TB, NSUB):
    j = pl.program_id(1)

    @pl.when(j == 0)
    def _():
        z_s[0, 0:SUBLANES, :] = jnp.zeros((SUBLANES, CONV_CH), jnp.float32)
        st_s[...] = jnp.zeros(st_s.shape, jnp.float32)
        m_ref[...] = jnp.zeros(m_ref.shape, jnp.float32)

    m_prev = m_ref[0]
    sub = [dict(sb=sb, x_ref=x_ref, gmix_ref=gmix_ref, wn_ref=wn_ref, wt_ref=wt_ref, gbt_ref=gbt_ref,
                gmhc_ref=gmhc_ref, cw_ref=cw_ref, wout_ref=wout_ref, x1_ref=x1_ref, qt_s=qt_s.at[sb],
                vt_s=vt_s.at[sb], ogt_s=ogt_s.at[sb], k_s=k_s.at[sb], mixt_s=mixt_s.at[sb], z_s=z_s,
                ucol_s=ucol_s.at[sb], wi_s=wi_s.at[sb], c2_s=c2_s.at[sb], emt_s=emt_s.at[sb],
                wk_s=wk_s.at[sb], ast_s=ast_s.at[sb], st_s=st_s, L=L, TS=TB) for sb in range(NSUB)]
    for s in sub:
        _prompt_project(s)
    for s in sub:
        m_prev = _prompt_chunks(s, m_prev)
    for s in sub:
        _prompt_output(s)
    m_ref[0] = m_prev
    cvo_ref[0] = z_s[NSUB - 1, TB:TB + SUBLANES, :]
    z_s[0, 0:SUBLANES, :] = z_s[NSUB - 1, TB:TB + SUBLANES, :]

    @pl.when(j == pl.num_programs(1) - 1)
    def _():
        for pr in range(NH // 2):
            state = st_s[pr]
            c_pair = state[0:DV, :].T
            c_ref[0, 2 * pr] = c_pair[0:DK, :]
            c_ref[0, 2 * pr + 1] = c_pair[DK:, :]
            n_ref[0, pr:pr + 1, :] = state[DV:DV + 1, :]


def _prompt_project(s):
    sb, L, TS = s["sb"], s["L"], s["TS"]
    x_ref, gmix_ref, wn_ref, wt_ref, gbt_ref, cw_ref = (
        s["x_ref"], s["gmix_ref"], s["wn_ref"], s["wt_ref"], s["gbt_ref"], s["cw_ref"])
    qt_s, vt_s, ogt_s, k_s, z_s = s["qt_s"], s["vt_s"], s["ogt_s"], s["k_s"], s["z_s"]
    bf16 = jnp.bfloat16

    x = x_ref[0, sb * TS:(sb + 1) * TS, :]
    a = _rms(x, gmix_ref[...]).astype(bf16)
    nt_dims = (((1,), (1,)), ((), ()))

    gt = lax.dot_general(wt_ref[_TG0:TPROJ_ROWS, :], a, nt_dims,
                         preferred_element_type=jnp.float32) + gbt_ref[...]
    ic = GATE_CAP * jnp.tanh(gt[0:SUBLANES] / GATE_CAP)
    lf = _log_sigmoid(gt[SUBLANES:])
    b = _lane_scan(lf, L, jnp.add, 0.0)
    u = ic - b
    m_loc = b + _lane_scan(u, L, jnp.maximum, -jnp.inf)

    un = _bdot(a, wn_ref[...])
    k_s[...] = un[:, _NK0:_NBG0]
    z = un[:, _NCG0:_NHC0] * un[:, _NHC0:NPROJ_COLS]
    if sb > 0:
        z_s[sb, 0:SUBLANES, :] = z_s[sb - 1, TS:TS + SUBLANES, :]
    z_s[sb, SUBLANES:SUBLANES + TS, :] = z
    yconv = (cw_ref[0:1, :] * z_s[sb, SUBLANES - 2:SUBLANES - 2 + TS, :]
             + cw_ref[1:2, :] * z_s[sb, SUBLANES - 1:SUBLANES - 1 + TS, :]
             + cw_ref[2:3, :] * z)
    yc = (un[:, _NBG0:_NCG0] * yconv).astype(bf16)

    ut = lax.dot_general(wt_ref[0:_TG0, :], a, nt_dims, preferred_element_type=jnp.float32)
    qt_s[...] = ut[_TQ0:_TV0] * (DK ** -0.5)
    vt_s[...] = ut[_TV0:_TOG0]
    ogt_s[...] = ut[_TOG0:_TG0]
    s.update(b=b, u=u, ic=ic, m_loc=m_loc, yc=yc)


def _prompt_chunks(s, m_prev):
    L, TS = s["L"], s["TS"]
    gmhc_ref, qt_s, vt_s, ogt_s, k_s, mixt_s, ucol_s, wi_s, c2_s, emt_s, wk_s, ast_s, st_s = (
        s["gmhc_ref"], s["qt_s"], s["vt_s"], s["ogt_s"], s["k_s"], s["mixt_s"], s["ucol_s"], s["wi_s"],
        s["c2_s"], s["emt_s"], s["wk_s"], s["ast_s"], s["st_s"])
    b, u, ic, m_loc = s["b"], s["u"], s["ic"], s["m_loc"]
    nc = TS // L
    bf16 = jnp.bfloat16

    for c in range(nc):
        sl = slice(c * L, (c + 1) * L)
        bc, mlc = b[:, sl], m_loc[:, sl]
        b_last = jnp.broadcast_to(bc[:, L - 1:L], bc.shape)
        m_new = jnp.maximum(b_last + m_prev, jnp.broadcast_to(mlc[:, L - 1:L], bc.shape))
        g = bc + m_prev
        mt = jnp.maximum(g, mlc)
        wi_s[:, sl] = jnp.exp(g - mt)
        c2_s[:, sl] = mt - bc
        emt_s[:, sl] = jnp.exp(-mt)
        wk_s[:, sl] = jnp.exp(b_last - bc + ic[:, sl] - m_new)
        ast_s[c] = jnp.exp(b_last + m_prev - m_new)
        upad = jnp.concatenate([u[:, sl], jnp.zeros((L - SUBLANES, L), jnp.float32)], axis=0)
        ucol_s[c * L:(c + 1) * L, :] = upad.T
        m_prev = m_new

    s_i = lax.broadcasted_iota(jnp.int32, (L, L), 0)
    t_i = lax.broadcasted_iota(jnp.int32, (L, L), 1)
    causal = s_i <= t_i
    low_half = lax.broadcasted_iota(jnp.int32, (L, LANES), 1) < DK
    zeros_q = jnp.zeros((DK, L), jnp.float32)
    zeros_p = jnp.zeros((L, L), bf16)

    def pair_row(ref, h0, rows):
        return jnp.concatenate([ref[h0:h0 + 1, rows], ref[h0 + 1:h0 + 2, rows]], axis=1)

    for c in range(nc):
        rows = slice(c * L, (c + 1) * L)
        for pr in range(NH // 2):
            h0 = 2 * pr
            hv0 = slice(h0 * DV, (h0 + 1) * DV)
            hv1 = slice((h0 + 1) * DV, (h0 + 2) * DV)
            kp = k_s[rows, pr * LANES:(pr + 1) * LANES]
            q0 = qt_s[h0 * DK:(h0 + 1) * DK, rows]
            q1 = qt_s[(h0 + 1) * DK:(h0 + 2) * DK, rows]
            qbd = jnp.concatenate([jnp.concatenate([q0, zeros_q], axis=1),
                                   jnp.concatenate([zeros_q, q1], axis=1)], axis=0).astype(bf16)
            st = _bdot(kp.astype(bf16), qbd)
            arg = jnp.concatenate(
                [jnp.where(causal, ucol_s[rows, h:h + 1] - c2_s[h:h + 1, rows], -jnp.inf)
                 for h in (h0, h0 + 1)], axis=1)
            pt = st * jnp.exp(arg)
            rs = jnp.sum(pt, axis=0, keepdims=True)
            ptb = pt.astype(bf16)
            pbd = jnp.concatenate([jnp.concatenate([ptb[:, 0:L], zeros_p], axis=1),
                                   jnp.concatenate([zeros_p, ptb[:, L:]], axis=1)], axis=0)
            vt = jnp.concatenate([vt_s[hv0, rows], vt_s[hv1, rows]], axis=1)
            state = st_s[pr]
            sq = _bdot(state.astype(bf16), qbd) * pair_row(wi_s, h0, rows)
            num = _bdot(vt.astype(bf16), pbd) + sq[0:DV]
            den = sq[DV:DV + 1] + rs
            hh = num / jnp.maximum(jnp.abs(den), pair_row(emt_s, h0, rows))
            hn = hh * lax.rsqrt(jnp.mean(hh * hh, axis=0, keepdims=True) + EPS)
            mixt_s[hv0, rows] = jax.nn.sigmoid(ogt_s[hv0, rows]) * (hn[:, 0:L] * gmhc_ref[hv0, :])
            mixt_s[hv1, rows] = jax.nn.sigmoid(ogt_s[hv1, rows]) * (hn[:, L:] * gmhc_ref[hv1, :])
            wkr = pair_row(wk_s, h0, rows)
            vw = jnp.concatenate([vt * wkr, jnp.broadcast_to(wkr, (2 * SUBLANES, 2 * L))], axis=0)
            km = jnp.concatenate([jnp.where(low_half, kp, 0.0), jnp.where(low_half, 0.0, kp)], axis=0)
            decay = jnp.where(low_half[0:1, :], ast_s[c][h0:h0 + 1, :], ast_s[c][h0 + 1:h0 + 2, :])
            st_s[pr] = decay * state + _bdot(vw.astype(bf16), km.astype(bf16))
    return m_prev


def _prompt_output(s):
    sb, TS = s["sb"], s["TS"]
    x_ref, wout_ref, x1_ref, mixt_s = s["x_ref"], s["wout_ref"], s["x1_ref"], s["mixt_s"]
    tok = slice(sb * TS, (sb + 1) * TS)
    out = x_ref[0, tok, :] + lax.dot_general(
        mixt_s[...].astype(jnp.bfloat16), wout_ref[0:MLSTM_W, :], (((0,), (0,)), ((), ())),
        preferred_element_type=jnp.float32)
    x1_ref[0, tok, :] = out + _bdot(s["yc"], wout_ref[MLSTM_W:, :])


def _mixer_prompt(x, weights, TB, NSUB, L):
    B, S, _ = x.shape
    TS = TB * NSUB
    nb = S // TS
    f32 = jnp.float32
    kern = functools.partial(_mixer_prompt_kernel, L=L, TB=TB, NSUB=NSUB)
    return pl.pallas_call(
        kern,
        grid=(B, nb),
        in_specs=[
            pl.BlockSpec((1, TS, D_MODEL), lambda b, j: (b, j, 0)),
            _const_spec((1, D_MODEL)),
            _const_spec((D_MODEL, NPROJ_COLS)),
            _const_spec((TPROJ_ROWS, D_MODEL)),
            _const_spec((2 * SUBLANES, TB)),
            _const_spec((MLSTM_W, LANES)),
            _const_spec((CONV_W, CONV_CH)),
            _const_spec((D_MODEL, D_MODEL)),
        ],
        out_specs=[
            pl.BlockSpec((1, TS, D_MODEL), lambda b, j: (b, j, 0)),
            pl.BlockSpec((1, NH, DK, DV), lambda b, j: (b, 0, 0, 0)),
            pl.BlockSpec((1, NH // 2, 2 * DK), lambda b, j: (b, 0, 0)),
            pl.BlockSpec((1, SUBLANES, LANES), lambda b, j: (b, 0, 0)),
            pl.BlockSpec((1, SUBLANES, CONV_CH), lambda b, j: (b, 0, 0)),
        ],
        out_shape=[
            jax.ShapeDtypeStruct((B, S, D_MODEL), f32),
            jax.ShapeDtypeStruct((B, NH, DK, DV), f32),
            jax.ShapeDtypeStruct((B, NH // 2, 2 * DK), f32),
            jax.ShapeDtypeStruct((B, SUBLANES, LANES), f32),
            jax.ShapeDtypeStruct((B, SUBLANES, CONV_CH), f32),
        ],
        scratch_shapes=[
            pltpu.VMEM((NSUB, QK_W, TB), f32),
            pltpu.VMEM((NSUB, MLSTM_W, TB), f32),
            pltpu.VMEM((NSUB, MLSTM_W, TB), f32),
            pltpu.VMEM((NSUB, TB, QK_W), f32),
            pltpu.VMEM((NSUB, MLSTM_W, TB), f32),
            pltpu.VMEM((NSUB, TB + SUBLANES, CONV_CH), f32),
            pltpu.VMEM((NSUB, TB, LANES), f32),
            pltpu.VMEM((NSUB, SUBLANES, TB), f32),
            pltpu.VMEM((NSUB, SUBLANES, TB), f32),
            pltpu.VMEM((NSUB, SUBLANES, TB), f32),
            pltpu.VMEM((NSUB, SUBLANES, TB), f32),
            pltpu.VMEM((NSUB, TB // L, SUBLANES, LANES), f32),
            pltpu.VMEM((NH // 2, ST_ROWS, LANES), f32),
        ],
        compiler_params=pltpu.CompilerParams(
            dimension_semantics=("arbitrary", "arbitrary"), vmem_limit_bytes=VMEM_LIMIT),
        name="mixer_prompt",
    )(x, *weights)


def _prep_prompt_weights(g_mix, w_in, b_ig, b_fg, g_mh, conv_w, w_out, TS):
    bf16 = jnp.bfloat16
    f32 = jnp.float32
    ig0 = 2 * QK_W + MLSTM_W
    fg0 = ig0 + NH
    og0 = fg0 + NH
    bg0 = og0 + MLSTM_W
    wn = jnp.concatenate([w_in[:, QK_W:2 * QK_W], w_in[:, bg0:]], axis=1).astype(bf16)
    gate_rows = jnp.zeros((2 * SUBLANES, D_MODEL), w_in.dtype)
    gate_rows = gate_rows.at[0:NH].set(w_in[:, ig0:fg0].T)
    gate_rows = gate_rows.at[SUBLANES:SUBLANES + NH].set(w_in[:, fg0:og0].T)
    wt = jnp.concatenate([w_in[:, 0:QK_W].T, w_in[:, 2 * QK_W:ig0].T, w_in[:, og0:bg0].T, gate_rows],
                         axis=0).astype(bf16)
    gb = jnp.zeros((2 * SUBLANES,), f32).at[0:NH].set(b_ig).at[SUBLANES:SUBLANES + NH].set(b_fg)
    gbt = jnp.broadcast_to(gb[:, None], (2 * SUBLANES, TS))
    gmhc = jnp.broadcast_to(g_mh[:, None], (MLSTM_W, LANES))
    return (g_mix.reshape(1, D_MODEL), wn, wt, gbt, gmhc, conv_w, w_out.astype(bf16))


_Q0, _K0, _V0, _OG0, _BG0, _CG0, _HC0, _GT0 = 0, 256, 512, 1024, 1536, 2048, 2560, 3072
PROJ_COLS = _GT0 + 2 * LANES


def _row_scan(x, seg, op, fill):
    pos = lax.broadcasted_iota(jnp.int32, x.shape, 0) & (seg - 1)
    k = 1
    while k < seg:
        x = op(x, jnp.where(pos >= k, pltpu.roll(x, k, 0), fill))
        k *= 2
    return x


def _split_dot(x, sel, parts):
    acc = None
    rem = x
    for p in range(parts):
        hi = rem.astype(jnp.bfloat16)
        d = _bdot(hi, sel)
        acc = d if acc is None else acc + d
        if p + 1 < parts:
            rem = rem - hi.astype(jnp.float32)
    return acc


def _selectors():
    h = np.arange(NH)
    seg_qk = np.zeros((QK_W, LANES), np.float32)
    seg_qk[np.arange(QK_W), np.arange(QK_W) // DK] = 1.0
    exp_v = np.zeros((LANES, MLSTM_W), np.float32)
    exp_k = np.zeros((LANES, QK_W), np.float32)
    for i in h:
        exp_v[i, i * DV:(i + 1) * DV] = 1.0
        exp_k[i, i * DK:(i + 1) * DK] = 1.0
    mean_v = np.kron(np.eye(NH, dtype=np.float32), np.full((DV, DV), 1.0 / DV, np.float32))
    return tuple(jnp.asarray(m, jnp.bfloat16) for m in (seg_qk, exp_v, exp_k, mean_v))


def _mixer_sample_kernel(x_ref, cv_ref, c0_ref, n0_ref, m0_ref,
                         gmix_ref, win_ref, gb_ref, gmh_ref, cw_ref, wout_ref,
                         segqk_ref, expv_ref, expk_ref, meanv_ref,
                         x1_ref, cst_ref, nst_ref, mst_ref, cvo_ref,
                         xp_s, z_s, mp_s, qc_s, *, NSEQ):
    R = SUBLANES
    TS = NSEQ * R
    bf16 = jnp.bfloat16
    f32 = jnp.float32

    xp_s[:, SAMPLE_T:, :] = jnp.zeros((NSEQ, R - SAMPLE_T, D_MODEL), f32)
    xp_s[:, 0:SAMPLE_T, :] = x_ref[...]
    x = xp_s[...].reshape(TS, D_MODEL)
    a = _rms(x, gmix_ref[...]).astype(bf16)
    u = _bdot(a, win_ref[...])

    pos = lax.broadcasted_iota(jnp.int32, (TS, 1), 0) & (R - 1)
    real = pos < SAMPLE_T

    z = u[:, _CG0:_HC0] * u[:, _HC0:_GT0]
    z3 = z.reshape(NSEQ, R, CONV_CH)
    z_s[...] = z3
    z_s[:, R - (CONV_W - 1):, :] = cv_ref[...]
    zf = z_s[...].reshape(TS, CONV_CH)
    zm1 = jnp.where(pos >= 1, pltpu.roll(z, 1, 0), pltpu.roll(zf, TS - (R - 1), 0))
    zm2 = jnp.where(pos >= 2, pltpu.roll(z, 2, 0), pltpu.roll(zf, TS - (R - 2), 0))
    yc = u[:, _BG0:_CG0] * (cw_ref[0:1, :] * zm2 + cw_ref[1:2, :] * zm1 + cw_ref[2:3, :] * z)
    cvo_ref[...] = z3[:, SAMPLE_T - (CONV_W - 1):SAMPLE_T, :]

    gates = u[:, _GT0:PROJ_COLS] + gb_ref[...]
    ic = jnp.where(real, GATE_CAP * jnp.tanh(gates[:, :LANES] / GATE_CAP), NEG_BIG)
    lf = jnp.where(real, _log_sigmoid(gates[:, LANES:]), 0.0)
    b = _row_scan(lf, R, jnp.add, 0.0)
    uu = ic - b
    m_loc = b + _row_scan(uu, R, jnp.maximum, -jnp.inf)

    mp_s[...] = jnp.zeros(mp_s.shape, f32)
    mp_s[:, :, 0:NH] = m0_ref[...]
    m_prev = mp_s[...]
    b3 = b.reshape(NSEQ, R, LANES)
    ml3 = m_loc.reshape(NSEQ, R, LANES)
    b_last = b3[:, R - 1:R, :]
    m_new = jnp.maximum(b_last + m_prev, ml3[:, R - 1:R, :])
    g3 = b3 + m_prev
    mt3 = jnp.maximum(g3, ml3)
    wi = jnp.exp(g3 - mt3).reshape(TS, LANES)
    c2 = (mt3 - b3).reshape(TS, LANES)
    emt = jnp.exp(-mt3).reshape(TS, LANES)
    wk = jnp.exp(b_last - b3 + ic.reshape(NSEQ, R, LANES) - m_new).reshape(TS, LANES)
    a_st = jnp.broadcast_to(jnp.exp(b_last + m_prev - m_new), (NSEQ, R, LANES)).reshape(TS, LANES)
    mst_ref[...] = m_new[:, :, 0:NH]

    q = u[:, _Q0:_K0] * (DK ** -0.5)
    k = u[:, _K0:_V0]
    v = u[:, _V0:_OG0]
    rs = jnp.zeros((TS, LANES), f32)
    num = jnp.zeros((TS, MLSTM_W), f32)
    for d in range(SAMPLE_T):
        kd = k if d == 0 else pltpu.roll(k, d, 0)
        vd = v if d == 0 else pltpu.roll(v, d, 0)
        ud = uu if d == 0 else pltpu.roll(uu, d, 0)
        p = _split_dot(q * kd, segqk_ref[...], 2) * jnp.exp(ud - c2)
        rs = rs + p
        num = num + _split_dot(p, expv_ref[...], 2) * vd

    n0x = jnp.broadcast_to(n0_ref[...], (NSEQ, R, QK_W)).reshape(TS, QK_W)
    qn = _split_dot(q * n0x, segqk_ref[...], 2)
    den = wi * qn + rs
    rden = 1.0 / jnp.maximum(jnp.abs(den), emt)
    kw = k * _split_dot(wk, expk_ref[...], 2)
    ax = _split_dot(a_st, expv_ref[...], 3)
    lane_q = lax.broadcasted_iota(jnp.int32, (R, QK_W), 1)
    for i in range(NSEQ):
        rows = slice(i * R, (i + 1) * R)
        qi = q[rows, :]
        lhs = jnp.concatenate(
            [jnp.where((lane_q >= h * DK) & (lane_q < (h + 1) * DK), qi, 0.0) for h in range(NH)], axis=0)
        c0 = c0_ref[i]
        qc = _bdot(lhs.astype(bf16), c0.reshape(QK_W, DV).astype(bf16))
        qc_s[rows, :] = jnp.concatenate([qc[h * R:(h + 1) * R, :] for h in range(NH)], axis=1)
        kwi = kw[rows, :].astype(bf16)
        vi = v[rows, :].astype(bf16)
        for h in range(NH):
            dc = lax.dot_general(kwi[:, h * DK:(h + 1) * DK], vi[:, h * DV:(h + 1) * DV],
                                 (((0,), (0,)), ((), ())), preferred_element_type=f32)
            cst_ref[i, h] = ax[i * R:i * R + 1, h * DV:(h + 1) * DV] * c0[h] + dc

    hh = (_split_dot(wi, expv_ref[...], 2) * qc_s[...] + num) * _split_dot(rden, expv_ref[...], 2)
    ms = _split_dot(hh * hh, meanv_ref[...], 2)
    hm = jax.nn.sigmoid(u[:, _OG0:_BG0]) * (hh * lax.rsqrt(ms + EPS) * gmh_ref[...])
    mix = jnp.concatenate([hm, yc], axis=1).astype(bf16)
    out = x + _bdot(mix, wout_ref[...])
    x1_ref[...] = out.reshape(NSEQ, R, D_MODEL)[:, 0:SAMPLE_T, :]

    a_k = _split_dot(a_st, expk_ref[...], 3).reshape(NSEQ, R, QK_W)[:, 0:1, :]
    nst_ref[...] = a_k * n0_ref[...] + jnp.sum(kw.reshape(NSEQ, R, QK_W), axis=1, keepdims=True)


def _mixer_sample(x, cv, c0, n0, m0, weights, NSEQ):
    Bs = x.shape[0]
    f32 = jnp.float32
    kern = functools.partial(_mixer_sample_kernel, NSEQ=NSEQ)
    x_spec = pl.BlockSpec((NSEQ, SAMPLE_T, D_MODEL), lambda t: (t, 0, 0))
    cv_spec = pl.BlockSpec((NSEQ, CONV_W - 1, CONV_CH), lambda t: (t, 0, 0))
    c_spec = pl.BlockSpec((NSEQ, NH, DK, DV), lambda t: (t, 0, 0, 0))
    n_spec = pl.BlockSpec((NSEQ, 1, QK_W), lambda t: (t, 0, 0))
    m_spec = pl.BlockSpec((NSEQ, 1, NH), lambda t: (t, 0, 0))
    return pl.pallas_call(
        kern,
        grid=(Bs // NSEQ,),
        in_specs=[
            x_spec, cv_spec, c_spec, n_spec, m_spec,
            _const_spec((1, D_MODEL)),
            _const_spec((D_MODEL, PROJ_COLS)),
            _const_spec((1, 2 * LANES)),
            _const_spec((1, MLSTM_W)),
            _const_spec((CONV_W, CONV_CH)),
            _const_spec((D_MODEL, D_MODEL)),
            _const_spec((QK_W, LANES)),
            _const_spec((LANES, MLSTM_W)),
            _const_spec((LANES, QK_W)),
            _const_spec((MLSTM_W, MLSTM_W)),
        ],
        out_specs=[x_spec, c_spec, n_spec, m_spec, cv_spec],
        out_shape=[
            jax.ShapeDtypeStruct((Bs, SAMPLE_T, D_MODEL), f32),
            jax.ShapeDtypeStruct((Bs, NH, DK, DV), f32),
            jax.ShapeDtypeStruct((Bs, 1, QK_W), f32),
            jax.ShapeDtypeStruct((Bs, 1, NH), f32),
            jax.ShapeDtypeStruct((Bs, CONV_W - 1, CONV_CH), f32),
        ],
        scratch_shapes=[
            pltpu.VMEM((NSEQ, SUBLANES, D_MODEL), f32),
            pltpu.VMEM((NSEQ, SUBLANES, CONV_CH), f32),
            pltpu.VMEM((NSEQ, 1, LANES), f32),
            pltpu.VMEM((NSEQ * SUBLANES, MLSTM_W), f32),
        ],
        compiler_params=pltpu.CompilerParams(
            dimension_semantics=("arbitrary",), vmem_limit_bytes=VMEM_LIMIT),
        name="mixer_sample",
    )(x, cv, c0, n0, m0, *weights)


def _prep_sample_weights(g_mix, w_in, b_ig, b_fg, g_mh, conv_w, w_out):
    bf16 = jnp.bfloat16
    ig0 = 2 * QK_W + MLSTM_W
    fg0 = ig0 + NH
    og0 = fg0 + NH
    gate_cols = jnp.zeros((D_MODEL, 2 * LANES), w_in.dtype)
    gate_cols = gate_cols.at[:, 0:NH].set(w_in[:, ig0:fg0])
    gate_cols = gate_cols.at[:, LANES:LANES + NH].set(w_in[:, fg0:og0])
    w_all = jnp.concatenate([w_in[:, :ig0], w_in[:, og0:], gate_cols], axis=1).astype(bf16)
    gb = jnp.zeros((1, 2 * LANES), jnp.float32)
    gb = gb.at[0, 0:NH].set(b_ig).at[0, LANES:LANES + NH].set(b_fg)
    return (g_mix.reshape(1, D_MODEL), w_all, gb, g_mh.reshape(1, MLSTM_W), conv_w,
            w_out.astype(bf16)) + _selectors()


def _ffn_kernel(x_ref, pe_ref, gffn_ref, wg_ref, wu_ref, wd_ref, wple_ref, gple_ref, wpg_ref,
                gfin_ref, y_ref, *, final_norm, n_sub):
    sub = x_ref.shape[0] // n_sub
    for s in range(n_sub):
        rows = slice(s * sub, (s + 1) * sub)
        x = x_ref[rows, :]
        f = _rms(x, gffn_ref[...]).astype(jnp.bfloat16)
        gate = _bdot(f, wg_ref[...])
        up = _bdot(f, wu_ref[...])
        hmid = (gate * jax.nn.sigmoid(gate) * up).astype(jnp.bfloat16)
        x = x + _bdot(hmid, wd_ref[...])
        e = _rms(_bdot(pe_ref[rows, :].astype(jnp.bfloat16), wple_ref[...]), gple_ref[...])
        x = x + jax.nn.sigmoid(_bdot(x.astype(jnp.bfloat16), wpg_ref[...])) * e
        if final_norm:
            x = _rms(x, gfin_ref[...])
        y_ref[rows, :] = x


def _ffn(x, pe, weights, TM, final_norm, n_sub=1):
    T = x.shape[0]
    kern = functools.partial(_ffn_kernel, final_norm=final_norm, n_sub=n_sub)
    return pl.pallas_call(
        kern,
        grid=(T // TM,),
        in_specs=[
            pl.BlockSpec((TM, D_MODEL), lambda t: (t, 0)),
            pl.BlockSpec((TM, PLE_DIM), lambda t: (t, 0)),
            _const_spec((1, D_MODEL)),
            _const_spec((D_MODEL, D_FF)),
            _const_spec((D_MODEL, D_FF)),
            _const_spec((D_FF, D_MODEL)),
            _const_spec((PLE_DIM, D_MODEL)),
            _const_spec((1, D_MODEL)),
            _const_spec((D_MODEL, D_MODEL)),
            _const_spec((1, D_MODEL)),
        ],
        out_specs=pl.BlockSpec((TM, D_MODEL), lambda t: (t, 0)),
        out_shape=jax.ShapeDtypeStruct((T, D_MODEL), jnp.float32),
        compiler_params=pltpu.CompilerParams(
            dimension_semantics=("arbitrary",), vmem_limit_bytes=VMEM_LIMIT),
        name="ffn",
    )(x, pe, *weights)


def kernel(x_prompt, x_sample, p_prompt, p_sample, state_C, state_n, state_m, state_conv,
           g_mix, w_in, b_ig, b_fg, g_mh, conv_w, w_out, g_ffn, w_gate, w_up, w_down,
           w_ple, g_ple, w_pg, g_final):
    bf16 = jnp.bfloat16
    depth = g_mix.shape[0]
    B, S, _ = x_prompt.shape
    Bs, Ss, _ = x_sample.shape
    assert Ss == SAMPLE_T and S % PROMPT_TS == 0 and Bs % SAMPLE_NSEQ == 0

    xp = x_prompt
    xs = x_sample
    outs = [[] for _ in range(8)]
    for i in range(depth):
        last = i == depth - 1
        fw = (g_ffn[i].reshape(1, D_MODEL), w_gate[i].astype(bf16), w_up[i].astype(bf16),
              w_down[i].astype(bf16), w_ple[i].astype(bf16), g_ple[i].reshape(1, D_MODEL),
              w_pg[i].astype(bf16), g_final.reshape(1, D_MODEL))

        pw = _prep_prompt_weights(g_mix[i], w_in[i], b_ig[i], b_fg[i], g_mh[i], conv_w[i], w_out[i],
                                  PROMPT_TS)
        x1p, cp, n_p, mp, cvp = _mixer_prompt(xp, pw, TB=PROMPT_TS, NSUB=PROMPT_NSUB, L=PROMPT_L)
        xp = _ffn(x1p.reshape(B * S, D_MODEL), p_prompt[i].reshape(B * S, PLE_DIM), fw, 1024,
                  last, n_sub=2).reshape(B, S, D_MODEL)

        sw = _prep_sample_weights(g_mix[i], w_in[i], b_ig[i], b_fg[i], g_mh[i], conv_w[i], w_out[i])
        x1s, cs, n_s, ms, cvs = _mixer_sample(
            xs, state_conv[i], state_C[i], state_n[i].reshape(Bs, 1, QK_W),
            state_m[i].reshape(Bs, 1, NH), sw, SAMPLE_NSEQ)
        xs = _ffn(x1s.reshape(Bs * Ss, D_MODEL), p_sample[i].reshape(Bs * Ss, PLE_DIM), fw, 256,
                  last).reshape(Bs, Ss, D_MODEL)

        new = (cp, n_p.reshape(B, NH, DK), mp[:, :NH, 0], cvp[:, SUBLANES - (CONV_W - 1):],
               cs, n_s.reshape(Bs, NH, DK), ms.reshape(Bs, NH), cvs)
        for lst, v in zip(outs, new):
            lst.append(v)

    return (xp, xs) + tuple(jnp.stack(l) for l in outs)
```

```python
import functools

import numpy as np
import jax
import jax.numpy as jnp
from jax import lax
from jax.experimental import pallas as pl
from jax.experimental.pallas import tpu as pltpu

D_MODEL = 1024
NH = 4
DK = 64
DV = 128
MLSTM_W = NH * DV
QK_W = NH * DK
CONV_CH = 512
CONV_W = 3
D_FF = 2816
FF_CHUNK = 512
PLE_DIM = 256
PROMPT_L = 128
PROMPT_TS = 512
PROMPT_NSUB = 2
SAMPLE_T = 4
SAMPLE_NSEQ = 32
GATE_CAP = 15.0
EPS = 1e-6

LANES = 128
SUBLANES = 8
NEG_BIG = -1e30

VMEM_LIMIT = 56 * 1024 * 1024


def _rms(x, g):
    ms = jnp.mean(x * x, axis=-1, keepdims=True)
    return x * lax.rsqrt(ms + EPS) * g


def _bdot(a, b):
    return jnp.dot(a, b, preferred_element_type=jnp.float32)


def _log_sigmoid(x):
    return jnp.minimum(x, 0.0) - jnp.log1p(jnp.exp(-jnp.abs(x)))


def _const_spec(shape):
    nd = len(shape)
    return pl.BlockSpec(shape, lambda *_: (0,) * nd, pipeline_mode=pl.Buffered(1))


_TQ0, _TV0, _TOG0, _TG0 = 0, 256, 768, 1280
TPROJ_ROWS = _TG0 + 2 * SUBLANES
_NK0, _NBG0, _NCG0, _NHC0, NPROJ_COLS = 0, 256, 768, 1280, 1792
ST_ROWS = DV + 2 * SUBLANES


def _lane_scan(x, seg, op, fill):
    pos = lax.broadcasted_iota(jnp.int32, x.shape, 1) & (seg - 1)
    k = 1
    while k < seg:
        x = op(x, jnp.where(pos >= k, pltpu.roll(x, k, 1), fill))
        k *= 2
    return x


def _mixer_prompt_kernel(x_ref, gmix_ref, wn_ref, wt_ref, gbt_ref, gmhc_ref, cw_ref, wout_ref,
                         x1_ref, c_ref, n_ref, m_ref, cvo_ref,
                         qt_s, vt_s, ogt_s, k_s, mixt_s, z_s, ucol_s, wi_s, c2_s, emt_s, wk_s, ast_s, st_s,
                         *, L, TB, NSUB):
    j = pl.program_id(1)

    @pl.when(j == 0)
    def _():
        z_s[0, 0:SUBLANES, :] = jnp.zeros((SUBLANES, CONV_CH), jnp.float32)
        st_s[...] = jnp.zeros(st_s.shape, jnp.float32)
        m_ref[...] = jnp.zeros(m_ref.shape, jnp.float32)

    m_prev = m_ref[0]
    sub = [dict(sb=sb, x_ref=x_ref, gmix_ref=gmix_ref, wn_ref=wn_ref, wt_ref=wt_ref, gbt_ref=gbt_ref,
                gmhc_ref=gmhc_ref, cw_ref=cw_ref, wout_ref=wout_ref, x1_ref=x1_ref, qt_s=qt_s.at[sb],
                vt_s=vt_s.at[sb], ogt_s=ogt_s.at[sb], k_s=k_s.at[sb], mixt_s=mixt_s.at[sb], z_s=z_s,
                ucol_s=ucol_s.at[sb], wi_s=wi_s.at[sb], c2_s=c2_s.at[sb], emt_s=emt_s.at[sb],
                wk_s=wk_s.at[sb], ast_s=ast_s.at[sb], st_s=st_s, L=L, TS=TB) for sb in range(NSUB)]
    for s in sub:
        _prompt_project(s)
    for s in sub:
        m_prev = _prompt_chunks(s, m_prev)
    for s in sub:
        _prompt_output(s)
    m_ref[0] = m_prev
    cvo_ref[0] = z_s[NSUB - 1, TB:TB + SUBLANES, :]
    z_s[0, 0:SUBLANES, :] = z_s[NSUB - 1, TB:TB + SUBLANES, :]

    @pl.when(j == pl.num_programs(1) - 1)
    def _():
        for pr in range(NH // 2):
            state = st_s[pr]
            c_pair = state[0:DV, :].T
            c_ref[0, 2 * pr] = c_pair[0:DK, :]
            c_ref[0, 2 * pr + 1] = c_pair[DK:, :]
            n_ref[0, pr:pr + 1, :] = state[DV:DV + 1, :]


def _prompt_project(s):
    sb, L, TS = s["sb"], s["L"], s["TS"]
    x_ref, gmix_ref, wn_ref, wt_ref, gbt_ref, cw_ref = (
        s["x_ref"], s["gmix_ref"], s["wn_ref"], s["wt_ref"], s["gbt_ref"], s["cw_ref"])
    qt_s, vt_s, ogt_s, k_s, z_s = s["qt_s"], s["vt_s"], s["ogt_s"], s["k_s"], s["z_s"]
    bf16 = jnp.bfloat16

    x = x_ref[0, sb * TS:(sb + 1) * TS, :]
    a = _rms(x, gmix_ref[...]).astype(bf16)
    nt_dims = (((1,), (1,)), ((), ()))

    gt = lax.dot_general(wt_ref[_TG0:TPROJ_ROWS, :], a, nt_dims,
                         preferred_element_type=jnp.float32) + gbt_ref[...]
    ic = GATE_CAP * jnp.tanh(gt[0:SUBLANES] / GATE_CAP)
    lf = _log_sigmoid(gt[SUBLANES:])
    b = _lane_scan(lf, L, jnp.add, 0.0)
    u = ic - b
    m_loc = b + _lane_scan(u, L, jnp.maximum, -jnp.inf)

    un = _bdot(a, wn_ref[...])
    k_s[...] = un[:, _NK0:_NBG0]
    z = un[:, _NCG0:_NHC0] * un[:, _NHC0:NPROJ_COLS]
    if sb > 0:
        z_s[sb, 0:SUBLANES, :] = z_s[sb - 1, TS:TS + SUBLANES, :]
    z_s[sb, SUBLANES:SUBLANES + TS, :] = z
    yconv = (cw_ref[0:1, :] * z_s[sb, SUBLANES - 2:SUBLANES - 2 + TS, :]
             + cw_ref[1:2, :] * z_s[sb, SUBLANES - 1:SUBLANES - 1 + TS, :]
             + cw_ref[2:3, :] * z)
    yc = (un[:, _NBG0:_NCG0] * yconv).astype(bf16)

    ut = lax.dot_general(wt_ref[0:_TG0, :], a, nt_dims, preferred_element_type=jnp.float32)
    qt_s[...] = ut[_TQ0:_TV0] * (DK ** -0.5)
    vt_s[...] = ut[_TV0:_TOG0]
    ogt_s[...] = ut[_TOG0:_TG0]
    s.update(b=b, u=u, ic=ic, m_loc=m_loc, yc=yc)


def _prompt_chunks(s, m_prev):
    L, TS = s["L"], s["TS"]
    gmhc_ref, qt_s, vt_s, ogt_s, k_s, mixt_s, ucol_s, wi_s, c2_s, emt_s, wk_s, ast_s, st_s = (
        s["gmhc_ref"], s["qt_s"], s["vt_s"], s["ogt_s"], s["k_s"], s["mixt_s"], s["ucol_s"], s["wi_s"],
        s["c2_s"], s["emt_s"], s["wk_s"], s["ast_s"], s["st_s"])
    b, u, ic, m_loc = s["b"], s["u"], s["ic"], s["m_loc"]
    nc = TS // L
    bf16 = jnp.bfloat16

    for c in range(nc):
        sl = slice(c * L, (c + 1) * L)
        bc, mlc = b[:, sl], m_loc[:, sl]
        b_last = jnp.broadcast_to(bc[:, L - 1:L], bc.shape)
        m_new = jnp.maximum(b_last + m_prev, jnp.broadcast_to(mlc[:, L - 1:L], bc.shape))
        g = bc + m_prev
        mt = jnp.maximum(g, mlc)
        wi_s[:, sl] = jnp.exp(g - mt)
        c2_s[:, sl] = mt - bc
        emt_s[:, sl] = jnp.exp(-mt)
        wk_s[:, sl] = jnp.exp(b_last - bc + ic[:, sl] - m_new)
        ast_s[c] = jnp.exp(b_last + m_prev - m_new)
        upad = jnp.concatenate([u[:, sl], jnp.zeros((L - SUBLANES, L), jnp.float32)], axis=0)
        ucol_s[c * L:(c + 1) * L, :] = upad.T
        m_prev = m_new

    s_i = lax.broadcasted_iota(jnp.int32, (L, L), 0)
    t_i = lax.broadcasted_iota(jnp.int32, (L, L), 1)
    causal = s_i <= t_i
    low_half = lax.broadcasted_iota(jnp.int32, (L, LANES), 1) < DK
    zeros_q = jnp.zeros((DK, L), jnp.float32)
    zeros_p = jnp.zeros((L, L), bf16)

    def pair_row(ref, h0, rows):
        return jnp.concatenate([ref[h0:h0 + 1, rows], ref[h0 + 1:h0 + 2, rows]], axis=1)

    for c in range(nc):
        rows = slice(c * L, (c + 1) * L)
        for pr in range(NH // 2):
            h0 = 2 * pr
            hv0 = slice(h0 * DV, (h0 + 1) * DV)
            hv1 = slice((h0 + 1) * DV, (h0 + 2) * DV)
            kp = k_s[rows, pr * LANES:(pr + 1) * LANES]
            q0 = qt_s[h0 * DK:(h0 + 1) * DK, rows]
            q1 = qt_s[(h0 + 1) * DK:(h0 + 2) * DK, rows]
            qbd = jnp.concatenate([jnp.concatenate([q0, zeros_q], axis=1),
                                   jnp.concatenate([zeros_q, q1], axis=1)], axis=0).astype(bf16)
            st = _bdot(kp.astype(bf16), qbd)
            arg = jnp.concatenate(
                [jnp.where(causal, ucol_s[rows, h:h + 1] - c2_s[h:h + 1, rows], -jnp.inf)
                 for h in (h0, h0 + 1)], axis=1)
            pt = st * jnp.exp(arg)
            rs = jnp.sum(pt, axis=0, keepdims=True)
            ptb = pt.astype(bf16)
            pbd = jnp.concatenate([jnp.concatenate([ptb[:, 0:L], zeros_p], axis=1),
                                   jnp.concatenate([zeros_p, ptb[:, L:]], axis=1)], axis=0)
            vt = jnp.concatenate([vt_s[hv0, rows], vt_s[hv1, rows]], axis=1)
            state = st_s[pr]
            sq = _bdot(state.astype(bf16), qbd) * pair_row(wi_s, h0, rows)
            num = _bdot(vt.astype(bf16), pbd) + sq[0:DV]
            den = sq[DV:DV + 1] + rs
            hh = num / jnp.maximum(jnp.abs(den), pair_row(emt_s, h0, rows))
            hn = hh * lax.rsqrt(jnp.mean(hh * hh, axis=0, keepdims=True) + EPS)
            mixt_s[hv0, rows] = jax.nn.sigmoid(ogt_s[hv0, rows]) * (hn[:, 0:L] * gmhc_ref[hv0, :])
            mixt_s[hv1, rows] = jax.nn.sigmoid(ogt_s[hv1, rows]) * (hn[:, L:] * gmhc_ref[hv1, :])
            wkr = pair_row(wk_s, h0, rows)
            vw = jnp.concatenate([vt * wkr, jnp.broadcast_to(wkr, (2 * SUBLANES, 2 * L))], axis=0)
            km = jnp.concatenate([jnp.where(low_half, kp, 0.0), jnp.where(low_half, 0.0, kp)], axis=0)
            decay = jnp.where(low_half[0:1, :], ast_s[c][h0:h0 + 1, :], ast_s[c][h0 + 1:h0 + 2, :])
            st_s[pr] = decay * state + _bdot(vw.astype(bf16), km.astype(bf16))
    return m_prev


def _prompt_output(s):
    sb, TS = s["sb"], s["TS"]
    x_ref, wout_ref, x1_ref, mixt_s = s["x_ref"], s["wout_ref"], s["x1_ref"], s["mixt_s"]
    tok = slice(sb * TS, (sb + 1) * TS)
    out = x_ref[0, tok, :] + lax.dot_general(
        mixt_s[...].astype(jnp.bfloat16), wout_ref[0:MLSTM_W, :], (((0,), (0,)), ((), ())),
        preferred_element_type=jnp.float32)
    x1_ref[0, tok, :] = out + _bdot(s["yc"], wout_ref[MLSTM_W:, :])


def _mixer_prompt(x, weights, TB, NSUB, L):
    B, S, _ = x.shape
    TS = TB * NSUB
    nb = S // TS
    f32 = jnp.float32
    kern = functools.partial(_mixer_prompt_kernel, L=L, TB=TB, NSUB=NSUB)
    return pl.pallas_call(
        kern,
        grid=(B, nb),
        in_specs=[
            pl.BlockSpec((1, TS, D_MODEL), lambda b, j: (b, j, 0)),
            _const_spec((1, D_MODEL)),
            _const_spec((D_MODEL, NPROJ_COLS)),
            _const_spec((TPROJ_ROWS, D_MODEL)),
            _const_spec((2 * SUBLANES, TB)),
            _const_spec((MLSTM_W, LANES)),
            _const_spec((CONV_W, CONV_CH)),
            _const_spec((D_MODEL, D_MODEL)),
        ],
        out_specs=[
            pl.BlockSpec((1, TS, D_MODEL), lambda b, j: (b, j, 0)),
            pl.BlockSpec((1, NH, DK, DV), lambda b, j: (b, 0, 0, 0)),
            pl.BlockSpec((1, NH // 2, 2 * DK), lambda b, j: (b, 0, 0)),
            pl.BlockSpec((1, SUBLANES, LANES), lambda b, j: (b, 0, 0)),
            pl.BlockSpec((1, SUBLANES, CONV_CH), lambda b, j: (b, 0, 0)),
        ],
        out_shape=[
            jax.ShapeDtypeStruct((B, S, D_MODEL), f32),
            jax.ShapeDtypeStruct((B, NH, DK, DV), f32),
            jax.ShapeDtypeStruct((B, NH // 2, 2 * DK), f32),
            jax.ShapeDtypeStruct((B, SUBLANES, LANES), f32),
            jax.ShapeDtypeStruct((B, SUBLANES, CONV_CH), f32),
        ],
        scratch_shapes=[
            pltpu.VMEM((NSUB, QK_W, TB), f32),
            pltpu.VMEM((NSUB, MLSTM_W, TB), f32),
            pltpu.VMEM((NSUB, MLSTM_W, TB), f32),
            pltpu.VMEM((NSUB, TB, QK_W), f32),
            pltpu.VMEM((NSUB, MLSTM_W, TB), f32),
            pltpu.VMEM((NSUB, TB + SUBLANES, CONV_CH), f32),
            pltpu.VMEM((NSUB, TB, LANES), f32),
            pltpu.VMEM((NSUB, SUBLANES, TB), f32),
            pltpu.VMEM((NSUB, SUBLANES, TB), f32),
            pltpu.VMEM((NSUB, SUBLANES, TB), f32),
            pltpu.VMEM((NSUB, SUBLANES, TB), f32),
            pltpu.VMEM((NSUB, TB // L, SUBLANES, LANES), f32),
            pltpu.VMEM((NH // 2, ST_ROWS, LANES), f32),
        ],
        compiler_params=pltpu.CompilerParams(
            dimension_semantics=("arbitrary", "arbitrary"), vmem_limit_bytes=VMEM_LIMIT),
        name="mixer_prompt",
    )(x, *weights)


def _prep_prompt_weights(g_mix, w_in, b_ig, b_fg, g_mh, conv_w, w_out, TS):
    bf16 = jnp.bfloat16
    f32 = jnp.float32
    ig0 = 2 * QK_W + MLSTM_W
    fg0 = ig0 + NH
    og0 = fg0 + NH
    bg0 = og0 + MLSTM_W
    wn = jnp.concatenate([w_in[:, QK_W:2 * QK_W], w_in[:, bg0:]], axis=1).astype(bf16)
    gate_rows = jnp.zeros((2 * SUBLANES, D_MODEL), w_in.dtype)
    gate_rows = gate_rows.at[0:NH].set(w_in[:, ig0:fg0].T)
    gate_rows = gate_rows.at[SUBLANES:SUBLANES + NH].set(w_in[:, fg0:og0].T)
    wt = jnp.concatenate([w_in[:, 0:QK_W].T, w_in[:, 2 * QK_W:ig0].T, w_in[:, og0:bg0].T, gate_rows],
                         axis=0).astype(bf16)
    gb = jnp.zeros((2 * SUBLANES,), f32).at[0:NH].set(b_ig).at[SUBLANES:SUBLANES + NH].set(b_fg)
    gbt = jnp.broadcast_to(gb[:, None], (2 * SUBLANES, TS))
    gmhc = jnp.broadcast_to(g_mh[:, None], (MLSTM_W, LANES))
    return (g_mix.reshape(1, D_MODEL), wn, wt, gbt, gmhc, conv_w, w_out.astype(bf16))


_Q0, _K0, _V0, _OG0, _BG0, _CG0, _HC0, _GT0 = 0, 256, 512, 1024, 1536, 2048, 2560, 3072
PROJ_COLS = _GT0 + 2 * LANES


def _row_scan(x, seg, op, fill):
    pos = lax.broadcasted_iota(jnp.int32, x.shape, 0) & (seg - 1)
    k = 1
    while k < seg:
        x = op(x, jnp.where(pos >= k, pltpu.roll(x, k, 0), fill))
        k *= 2
    return x


def _split_dot(x, sel, parts):
    acc = None
    rem = x
    for p in range(parts):
        hi = rem.astype(jnp.bfloat16)
        d = _bdot(hi, sel)
        acc = d if acc is None else acc + d
        if p + 1 < parts:
            rem = rem - hi.astype(jnp.float32)
    return acc


def _selectors():
    h = np.arange(NH)
    seg_qk = np.zeros((QK_W, LANES), np.float32)
    seg_qk[np.arange(QK_W), np.arange(QK_W) // DK] = 1.0
    exp_v = np.zeros((LANES, MLSTM_W), np.float32)
    exp_k = np.zeros((LANES, QK_W), np.float32)
    for i in h:
        exp_v[i, i * DV:(i + 1) * DV] = 1.0
        exp_k[i, i * DK:(i + 1) * DK] = 1.0
    mean_v = np.kron(np.eye(NH, dtype=np.float32), np.full((DV, DV), 1.0 / DV, np.float32))
    return tuple(jnp.asarray(m, jnp.bfloat16) for m in (seg_qk, exp_v, exp_k, mean_v))


def _mixer_sample_kernel(x_ref, cv_ref, c0_ref, n0_ref, m0_ref,
                         gmix_ref, win_ref, gb_ref, gmh_ref, cw_ref, wout_ref,
                         segqk_ref, expv_ref, expk_ref, meanv_ref,
                         x1_ref, cst_ref, nst_ref, mst_ref, cvo_ref,
                         xp_s, z_s, mp_s, qc_s, *, NSEQ):
    R = SUBLANES
    TS = NSEQ * R
    bf16 = jnp.bfloat16
    f32 = jnp.float32

    xp_s[:, SAMPLE_T:, :] = jnp.zeros((NSEQ, R - SAMPLE_T, D_MODEL), f32)
    xp_s[:, 0:SAMPLE_T, :] = x_ref[...]
    x = xp_s[...].reshape(TS, D_MODEL)
    a = _rms(x, gmix_ref[...]).astype(bf16)
    u = _bdot(a, win_ref[...])

    pos = lax.broadcasted_iota(jnp.int32, (TS, 1), 0) & (R - 1)
    real = pos < SAMPLE_T

    z = u[:, _CG0:_HC0] * u[:, _HC0:_GT0]
    z3 = z.reshape(NSEQ, R, CONV_CH)
    z_s[...] = z3
    z_s[:, R - (CONV_W - 1):, :] = cv_ref[...]
    zf = z_s[...].reshape(TS, CONV_CH)
    zm1 = jnp.where(pos >= 1, pltpu.roll(z, 1, 0), pltpu.roll(zf, TS - (R - 1), 0))
    zm2 = jnp.where(pos >= 2, pltpu.roll(z, 2, 0), pltpu.roll(zf, TS - (R - 2), 0))
    yc = u[:, _BG0:_CG0] * (cw_ref[0:1, :] * zm2 + cw_ref[1:2, :] * zm1 + cw_ref[2:3, :] * z)
    cvo_ref[...] = z3[:, SAMPLE_T - (CONV_W - 1):SAMPLE_T, :]

    gates = u[:, _GT0:PROJ_COLS] + gb_ref[...]
    ic = jnp.where(real, GATE_CAP * jnp.tanh(gates[:, :LANES] / GATE_CAP), NEG_BIG)
    lf = jnp.where(real, _log_sigmoid(gates[:, LANES:]), 0.0)
    b = _row_scan(lf, R, jnp.add, 0.0)
    uu = ic - b
    m_loc = b + _row_scan(uu, R, jnp.maximum, -jnp.inf)

    mp_s[...] = jnp.zeros(mp_s.shape, f32)
    mp_s[:, :, 0:NH] = m0_ref[...]
    m_prev = mp_s[...]
    b3 = b.reshape(NSEQ, R, LANES)
    ml3 = m_loc.reshape(NSEQ, R, LANES)
    b_last = b3[:, R - 1:R, :]
    m_new = jnp.maximum(b_last + m_prev, ml3[:, R - 1:R, :])
    g3 = b3 + m_prev
    mt3 = jnp.maximum(g3, ml3)
    wi = jnp.exp(g3 - mt3).reshape(TS, LANES)
    c2 = (mt3 - b3).reshape(TS, LANES)
    emt = jnp.exp(-mt3).reshape(TS, LANES)
    wk = jnp.exp(b_last - b3 + ic.reshape(NSEQ, R, LANES) - m_new).reshape(TS, LANES)
    a_st = jnp.broadcast_to(jnp.exp(b_last + m_prev - m_new), (NSEQ, R, LANES)).reshape(TS, LANES)
    mst_ref[...] = m_new[:, :, 0:NH]

    q = u[:, _Q0:_K0] * (DK ** -0.5)
    k = u[:, _K0:_V0]
    v = u[:, _V0:_OG0]
    rs = jnp.zeros((TS, LANES), f32)
    num = jnp.zeros((TS, MLSTM_W), f32)
    for d in range(SAMPLE_T):
        kd = k if d == 0 else pltpu.roll(k, d, 0)
        vd = v if d == 0 else pltpu.roll(v, d, 0)
        ud = uu if d == 0 else pltpu.roll(uu, d, 0)
        p = _split_dot(q * kd, segqk_ref[...], 2) * jnp.exp(ud - c2)
        rs = rs + p
        num = num + _split_dot(p, expv_ref[...], 2) * vd

    n0x = jnp.broadcast_to(n0_ref[...], (NSEQ, R, QK_W)).reshape(TS, QK_W)
    qn = _split_dot(q * n0x, segqk_ref[...], 2)
    den = wi * qn + rs
    rden = 1.0 / jnp.maximum(jnp.abs(den), emt)
    kw = k * _split_dot(wk, expk_ref[...], 2)
    ax = _split_dot(a_st, expv_ref[...], 3)
    lane_q = lax.broadcasted_iota(jnp.int32, (R, QK_W), 1)
    for i in range(NSEQ):
        rows = slice(i * R, (i + 1) * R)
        qi = q[rows, :]
        lhs = jnp.concatenate(
            [jnp.where((lane_q >= h * DK) & (lane_q < (h + 1) * DK), qi, 0.0) for h in range(NH)], axis=0)
        c0 = c0_ref[i]
        qc = _bdot(lhs.astype(bf16), c0.reshape(QK_W, DV).astype(bf16))
        qc_s[rows, :] = jnp.concatenate([qc[h * R:(h + 1) * R, :] for h in range(NH)], axis=1)
        kwi = kw[rows, :].astype(bf16)
        vi = v[rows, :].astype(bf16)
        for h in range(NH):
            dc = lax.dot_general(kwi[:, h * DK:(h + 1) * DK], vi[:, h * DV:(h + 1) * DV],
                                 (((0,), (0,)), ((), ())), preferred_element_type=f32)
            cst_ref[i, h] = ax[i * R:i * R + 1, h * DV:(h + 1) * DV] * c0[h] + dc

    hh = (_split_dot(wi, expv_ref[...], 2) * qc_s[...] + num) * _split_dot(rden, expv_ref[...], 2)
    ms = _split_dot(hh * hh, meanv_ref[...], 2)
    hm = jax.nn.sigmoid(u[:, _OG0:_BG0]) * (hh * lax.rsqrt(ms + EPS) * gmh_ref[...])
    mix = jnp.concatenate([hm, yc], axis=1).astype(bf16)
    out = x + _bdot(mix, wout_ref[...])
    x1_ref[...] = out.reshape(NSEQ, R, D_MODEL)[:, 0:SAMPLE_T, :]

    a_k = _split_dot(a_st, expk_ref[...], 3).reshape(NSEQ, R, QK_W)[:, 0:1, :]
    nst_ref[...] = a_k * n0_ref[...] + jnp.sum(kw.reshape(NSEQ, R, QK_W), axis=1, keepdims=True)


def _mixer_sample(x, cv, c0, n0, m0, weights, NSEQ):
    Bs = x.shape[0]
    f32 = jnp.float32
    kern = functools.partial(_mixer_sample_kernel, NSEQ=NSEQ)
    x_spec = pl.BlockSpec((NSEQ, SAMPLE_T, D_MODEL), lambda t: (t, 0, 0))
    cv_spec = pl.BlockSpec((NSEQ, CONV_W - 1, CONV_CH), lambda t: (t, 0, 0))
    c_spec = pl.BlockSpec((NSEQ, NH, DK, DV), lambda t: (t, 0, 0, 0))
    n_spec = pl.BlockSpec((NSEQ, 1, QK_W), lambda t: (t, 0, 0))
    m_spec = pl.BlockSpec((NSEQ, 1, NH), lambda t: (t, 0, 0))
    return pl.pallas_call(
        kern,
        grid=(Bs // NSEQ,),
        in_specs=[
            x_spec, cv_spec, c_spec, n_spec, m_spec,
            _const_spec((1, D_MODEL)),
            _const_spec((D_MODEL, PROJ_COLS)),
            _const_spec((1, 2 * LANES)),
            _const_spec((1, MLSTM_W)),
            _const_spec((CONV_W, CONV_CH)),
            _const_spec((D_MODEL, D_MODEL)),
            _const_spec((QK_W, LANES)),
            _const_spec((LANES, MLSTM_W)),
            _const_spec((LANES, QK_W)),
            _const_spec((MLSTM_W, MLSTM_W)),
        ],
        out_specs=[x_spec, c_spec, n_spec, m_spec, cv_spec],
        out_shape=[
            jax.ShapeDtypeStruct((Bs, SAMPLE_T, D_MODEL), f32),
            jax.ShapeDtypeStruct((Bs, NH, DK, DV), f32),
            jax.ShapeDtypeStruct((Bs, 1, QK_W), f32),
            jax.ShapeDtypeStruct((Bs, 1, NH), f32),
            jax.ShapeDtypeStruct((Bs, CONV_W - 1, CONV_CH), f32),
        ],
        scratch_shapes=[
            pltpu.VMEM((NSEQ, SUBLANES, D_MODEL), f32),
            pltpu.VMEM((NSEQ, SUBLANES, CONV_CH), f32),
            pltpu.VMEM((NSEQ, 1, LANES), f32),
            pltpu.VMEM((NSEQ * SUBLANES, MLSTM_W), f32),
        ],
        compiler_params=pltpu.CompilerParams(
            dimension_semantics=("arbitrary",), vmem_limit_bytes=VMEM_LIMIT),
        name="mixer_sample",
    )(x, cv, c0, n0, m0, *weights)


def _prep_sample_weights(g_mix, w_in, b_ig, b_fg, g_mh, conv_w, w_out):
    bf16 = jnp.bfloat16
    ig0 = 2 * QK_W + MLSTM_W
    fg0 = ig0 + NH
    og0 = fg0 + NH
    gate_cols = jnp.zeros((D_MODEL, 2 * LANES), w_in.dtype)
    gate_cols = gate_cols.at[:, 0:NH].set(w_in[:, ig0:fg0])
    gate_cols = gate_cols.at[:, LANES:LANES + NH].set(w_in[:, fg0:og0])
    w_all = jnp.concatenate([w_in[:, :ig0], w_in[:, og0:], gate_cols], axis=1).astype(bf16)
    gb = jnp.zeros((1, 2 * LANES), jnp.float32)
    gb = gb.at[0, 0:NH].set(b_ig).at[0, LANES:LANES + NH].set(b_fg)
    return (g_mix.reshape(1, D_MODEL), w_all, gb, g_mh.reshape(1, MLSTM_W), conv_w,
            w_out.astype(bf16)) + _selectors()


def _ffn_kernel(x_ref, pe_ref, gffn_ref, wg_ref, wu_ref, wd_ref, wple_ref, gple_ref, wpg_ref,
                gfin_ref, y_ref, *, final_norm, n_sub):
    sub = x_ref.shape[0] // n_sub
    for s in range(n_sub):
        rows = slice(s * sub, (s + 1) * sub)
        x = x_ref[rows, :]
        f = _rms(x, gffn_ref[...]).astype(jnp.bfloat16)
        for c0 in range(0, D_FF, FF_CHUNK):
            c1 = min(c0 + FF_CHUNK, D_FF)
            gate = _bdot(f, wg_ref[:, c0:c1])
            up = _bdot(f, wu_ref[:, c0:c1])
            hmid = (gate * jax.nn.sigmoid(gate) * up).astype(jnp.bfloat16)
            x = x + _bdot(hmid, wd_ref[c0:c1, :])
        e = _rms(_bdot(pe_ref[rows, :].astype(jnp.bfloat16), wple_ref[...]), gple_ref[...])
        x = x + jax.nn.sigmoid(_bdot(x.astype(jnp.bfloat16), wpg_ref[...])) * e
        if final_norm:
            x = _rms(x, gfin_ref[...])
        y_ref[rows, :] = x


def _ffn(x, pe, weights, TM, final_norm, n_sub=1):
    T = x.shape[0]
    kern = functools.partial(_ffn_kernel, final_norm=final_norm, n_sub=n_sub)
    return pl.pallas_call(
        kern,
        grid=(T // TM,),
        in_specs=[
            pl.BlockSpec((TM, D_MODEL), lambda t: (t, 0)),
            pl.BlockSpec((TM, PLE_DIM), lambda t: (t, 0)),
            _const_spec((1, D_MODEL)),
            _const_spec((D_MODEL, D_FF)),
            _const_spec((D_MODEL, D_FF)),
            _const_spec((D_FF, D_MODEL)),
            _const_spec((PLE_DIM, D_MODEL)),
            _const_spec((1, D_MODEL)),
            _const_spec((D_MODEL, D_MODEL)),
            _const_spec((1, D_MODEL)),
        ],
        out_specs=pl.BlockSpec((TM, D_MODEL), lambda t: (t, 0)),
        out_shape=jax.ShapeDtypeStruct((T, D_MODEL), jnp.float32),
        compiler_params=pltpu.CompilerParams(
            dimension_semantics=("arbitrary",), vmem_limit_bytes=VMEM_LIMIT),
        name="ffn",
    )(x, pe, *weights)


def kernel(x_prompt, x_sample, p_prompt, p_sample, state_C, state_n, state_m, state_conv,
           g_mix, w_in, b_ig, b_fg, g_mh, conv_w, w_out, g_ffn, w_gate, w_up, w_down,
           w_ple, g_ple, w_pg, g_final):
    bf16 = jnp.bfloat16
    depth = g_mix.shape[0]
    B, S, _ = x_prompt.shape
    Bs, Ss, _ = x_sample.shape
    assert Ss == SAMPLE_T and S % PROMPT_TS == 0 and Bs % SAMPLE_NSEQ == 0

    xp = x_prompt
    xs = x_sample
    outs = [[] for _ in range(8)]
    for i in range(depth):
        last = i == depth - 1
        fw = (g_ffn[i].reshape(1, D_MODEL), w_gate[i].astype(bf16), w_up[i].astype(bf16),
              w_down[i].astype(bf16), w_ple[i].astype(bf16), g_ple[i].reshape(1, D_MODEL),
              w_pg[i].astype(bf16), g_final.reshape(1, D_MODEL))

        pw = _prep_prompt_weights(g_mix[i], w_in[i], b_ig[i], b_fg[i], g_mh[i], conv_w[i], w_out[i],
                                  PROMPT_TS)
        x1p, cp, n_p, mp, cvp = _mixer_prompt(xp, pw, TB=PROMPT_TS, NSUB=PROMPT_NSUB, L=PROMPT_L)
        xp = _ffn(x1p.reshape(B * S, D_MODEL), p_prompt[i].reshape(B * S, PLE_DIM), fw, 1024,
                  last, n_sub=1).reshape(B, S, D_MODEL)

        sw = _prep_sample_weights(g_mix[i], w_in[i], b_ig[i], b_fg[i], g_mh[i], conv_w[i], w_out[i])
        x1s, cs, n_s, ms, cvs = _mixer_sample(
            xs, state_conv[i], state_C[i], state_n[i].reshape(Bs, 1, QK_W),
            state_m[i].reshape(Bs, 1, NH), sw, SAMPLE_NSEQ)
        xs = _ffn(x1s.reshape(Bs * Ss, D_MODEL), p_sample[i].reshape(Bs * Ss, PLE_DIM), fw, 256,
                  last).reshape(Bs, Ss, D_MODEL)

        new = (cp, n_p.reshape(B, NH, DK), mp[:, :NH, 0], cvp[:, SUBLANES - (CONV_W - 1):],
               cs, n_s.reshape(Bs, NH, DK), ms.reshape(Bs, NH), cvs)
        for lst, v in zip(outs, new):
            lst.append(v)

    return (xp, xs) + tuple(jnp.stack(l) for l in outs)
```

```python
import functools

import numpy as np
import jax
import jax.numpy as jnp
from jax import lax
from jax.experimental import pallas as pl
from jax.experimental.pallas import tpu as pltpu

D_MODEL = 1024
NH = 4
DK = 64
DV = 128
MLSTM_W = NH * DV
QK_W = NH * DK
CONV_CH = 512
CONV_W = 3
D_FF = 2816
FF_CHUNK = 512
PLE_DIM = 256
PROMPT_L = 128
PROMPT_TS = 512
PROMPT_NSUB = 2
SAMPLE_T = 4
SAMPLE_NSEQ = 32
GATE_CAP = 15.0
EPS = 1e-6

LANES = 128
SUBLANES = 8
NEG_BIG = -1e30

VMEM_LIMIT = 56 * 1024 * 1024


def _rms(x, g):
    ms = jnp.mean(x * x, axis=-1, keepdims=True)
    return x * lax.rsqrt(ms + EPS) * g


def _bdot(a, b):
    return jnp.dot(a, b, preferred_element_type=jnp.float32)


def _log_sigmoid(x):
    return jnp.minimum(x, 0.0) - jnp.log1p(jnp.exp(-jnp.abs(x)))


def _const_spec(shape):
    nd = len(shape)
    return pl.BlockSpec(shape, lambda *_: (0,) * nd, pipeline_mode=pl.Buffered(1))


_IG0 = 2 * QK_W + MLSTM_W
_OG_IN = _IG0 + 2 * NH
PROJ_IN = _OG_IN + MLSTM_W + 3 * CONV_CH
_Q0, _K0, _V0, _OG0, _BG0, _CG0, _HC0, _GT0 = 0, 256, 512, 1024, 1536, 2048, 2560, 3072
PROJ_COLS = _GT0 + 2 * LANES
_TQ0, _TV0, _TOG0, _TG0 = 0, 256, 768, 1280
TPROJ_ROWS = _TG0 + 2 * SUBLANES
PREP_TK = 256


def _prep_weights_kernel(w_ref, wall_ref, wt_ref):
    bf16 = jnp.bfloat16
    w = w_ref[...]
    tk = w.shape[0]
    aligned = w[:, 0:_IG0]
    rest = w[:, _OG_IN:PROJ_IN]
    gt = w[:, _IG0:_IG0 + LANES]
    lane = lax.broadcasted_iota(jnp.int32, (tk, LANES), 1)
    g_in = jnp.where(lane < NH, gt, 0.0)
    g_fg = jnp.where(lane < NH, pltpu.roll(gt, LANES - NH, 1), 0.0)
    wall_ref[:, 0:_OG0] = aligned.astype(bf16)
    wall_ref[:, _OG0:_GT0] = rest.astype(bf16)
    wall_ref[:, _GT0:_GT0 + LANES] = g_in.astype(bf16)
    wall_ref[:, _GT0 + LANES:PROJ_COLS] = g_fg.astype(bf16)

    wt_ref[_TQ0:_TV0, :] = w[:, _Q0:_K0].T.astype(bf16)
    wt_ref[_TV0:_TOG0, :] = w[:, _V0:_OG0].T.astype(bf16)
    wt_ref[_TOG0:_TG0, :] = rest[:, 0:MLSTM_W].T.astype(bf16)
    gtt = gt.T[0:SUBLANES, :]
    row = lax.broadcasted_iota(jnp.int32, (SUBLANES, tk), 0)
    g_rows = jnp.concatenate([jnp.where(row < NH, gtt, 0.0),
                              jnp.where(row < NH, pltpu.roll(gtt, SUBLANES - NH, 0), 0.0)], axis=0)
    wt_ref[_TG0:TPROJ_ROWS, :] = g_rows.astype(bf16)


def _prep_weights(w_in):
    bf16 = jnp.bfloat16
    return pl.pallas_call(
        _prep_weights_kernel,
        grid=(D_MODEL // PREP_TK,),
        in_specs=[pl.BlockSpec((PREP_TK, PROJ_IN), lambda i: (i, 0))],
        out_specs=[pl.BlockSpec((PREP_TK, PROJ_COLS), lambda i: (i, 0)),
                   pl.BlockSpec((TPROJ_ROWS, PREP_TK), lambda i: (0, i))],
        out_shape=[jax.ShapeDtypeStruct((D_MODEL, PROJ_COLS), bf16),
                   jax.ShapeDtypeStruct((TPROJ_ROWS, D_MODEL), bf16)],
        compiler_params=pltpu.CompilerParams(
            dimension_semantics=("arbitrary",), vmem_limit_bytes=VMEM_LIMIT),
        name="prep_weights",
    )(w_in)


ST_ROWS = DV + 2 * SUBLANES


def _lane_scan(x, seg, op, fill):
    pos = lax.broadcasted_iota(jnp.int32, x.shape, 1) & (seg - 1)
    k = 1
    while k < seg:
        x = op(x, jnp.where(pos >= k, pltpu.roll(x, k, 1), fill))
        k *= 2
    return x


def _mixer_prompt_kernel(x_ref, gmix_ref, wn_ref, wt_ref, gbt_ref, gmhc_ref, cw_ref, wout_ref,
                         x1_ref, c_ref, n_ref, m_ref, cvo_ref,
                         qt_s, vt_s, ogt_s, k_s, mixt_s, z_s, ucol_s, wi_s, c2_s, emt_s, wk_s, ast_s, st_s,
                         *, L, TB, NSUB):
    j = pl.program_id(1)

    @pl.when(j == 0)
    def _():
        z_s[0, 0:SUBLANES, :] = jnp.zeros((SUBLANES, CONV_CH), jnp.float32)
        st_s[...] = jnp.zeros(st_s.shape, jnp.float32)
        m_ref[...] = jnp.zeros(m_ref.shape, jnp.float32)

    m_prev = m_ref[0]
    sub = [dict(sb=sb, x_ref=x_ref, gmix_ref=gmix_ref, wn_ref=wn_ref, wt_ref=wt_ref, gbt_ref=gbt_ref,
                gmhc_ref=gmhc_ref, cw_ref=cw_ref, wout_ref=wout_ref, x1_ref=x1_ref, qt_s=qt_s.at[sb],
                vt_s=vt_s.at[sb], ogt_s=ogt_s.at[sb], k_s=k_s.at[sb], mixt_s=mixt_s.at[sb], z_s=z_s,
                ucol_s=ucol_s.at[sb], wi_s=wi_s.at[sb], c2_s=c2_s.at[sb], emt_s=emt_s.at[sb],
                wk_s=wk_s.at[sb], ast_s=ast_s.at[sb], st_s=st_s, L=L, TS=TB) for sb in range(NSUB)]
    for s in sub:
        _prompt_project(s)
    for s in sub:
        m_prev = _prompt_chunks(s, m_prev)
    for s in sub:
        _prompt_output(s)
    m_ref[0] = m_prev
    cvo_ref[0] = z_s[NSUB - 1, TB:TB + SUBLANES, :]
    z_s[0, 0:SUBLANES, :] = z_s[NSUB - 1, TB:TB + SUBLANES, :]

    @pl.when(j == pl.num_programs(1) - 1)
    def _():
        for pr in range(NH // 2):
            state = st_s[pr]
            c_pair = state[0:DV, :].T
            c_ref[0, 2 * pr] = c_pair[0:DK, :]
            c_ref[0, 2 * pr + 1] = c_pair[DK:, :]
            n_ref[0, pr:pr + 1, :] = state[DV:DV + 1, :]


def _prompt_project(s):
    sb, L, TS = s["sb"], s["L"], s["TS"]
    x_ref, gmix_ref, wn_ref, wt_ref, gbt_ref, cw_ref = (
        s["x_ref"], s["gmix_ref"], s["wn_ref"], s["wt_ref"], s["gbt_ref"], s["cw_ref"])
    qt_s, vt_s, ogt_s, k_s, z_s = s["qt_s"], s["vt_s"], s["ogt_s"], s["k_s"], s["z_s"]
    bf16 = jnp.bfloat16

    x = x_ref[0, sb * TS:(sb + 1) * TS, :]
    a = _rms(x, gmix_ref[...]).astype(bf16)
    nt_dims = (((1,), (1,)), ((), ()))

    gt = lax.dot_general(wt_ref[_TG0:TPROJ_ROWS, :], a, nt_dims,
                         preferred_element_type=jnp.float32) + gbt_ref[...]
    ic = GATE_CAP * jnp.tanh(gt[0:SUBLANES] / GATE_CAP)
    lf = _log_sigmoid(gt[SUBLANES:])
    b = _lane_scan(lf, L, jnp.add, 0.0)
    u = ic - b
    m_loc = b + _lane_scan(u, L, jnp.maximum, -jnp.inf)

    k_s[...] = _bdot(a, wn_ref[:, _K0:_V0])
    un = _bdot(a, wn_ref[:, _BG0:_GT0])
    z = un[:, CONV_CH:2 * CONV_CH] * un[:, 2 * CONV_CH:]
    if sb > 0:
        z_s[sb, 0:SUBLANES, :] = z_s[sb - 1, TS:TS + SUBLANES, :]
    z_s[sb, SUBLANES:SUBLANES + TS, :] = z
    yconv = (cw_ref[0:1, :] * z_s[sb, SUBLANES - 2:SUBLANES - 2 + TS, :]
             + cw_ref[1:2, :] * z_s[sb, SUBLANES - 1:SUBLANES - 1 + TS, :]
             + cw_ref[2:3, :] * z)
    yc = (un[:, 0:CONV_CH] * yconv).astype(bf16)

    ut = lax.dot_general(wt_ref[0:_TG0, :], a, nt_dims, preferred_element_type=jnp.float32)
    qt_s[...] = ut[_TQ0:_TV0] * (DK ** -0.5)
    vt_s[...] = ut[_TV0:_TOG0]
    ogt_s[...] = ut[_TOG0:_TG0]
    s.update(b=b, u=u, ic=ic, m_loc=m_loc, yc=yc)


def _prompt_chunks(s, m_prev):
    L, TS = s["L"], s["TS"]
    gmhc_ref, qt_s, vt_s, ogt_s, k_s, mixt_s, ucol_s, wi_s, c2_s, emt_s, wk_s, ast_s, st_s = (
        s["gmhc_ref"], s["qt_s"], s["vt_s"], s["ogt_s"], s["k_s"], s["mixt_s"], s["ucol_s"], s["wi_s"],
        s["c2_s"], s["emt_s"], s["wk_s"], s["ast_s"], s["st_s"])
    b, u, ic, m_loc = s["b"], s["u"], s["ic"], s["m_loc"]
    nc = TS // L
    bf16 = jnp.bfloat16

    for c in range(nc):
        sl = slice(c * L, (c + 1) * L)
        bc, mlc = b[:, sl], m_loc[:, sl]
        b_last = jnp.broadcast_to(bc[:, L - 1:L], bc.shape)
        m_new = jnp.maximum(b_last + m_prev, jnp.broadcast_to(mlc[:, L - 1:L], bc.shape))
        g = bc + m_prev
        mt = jnp.maximum(g, mlc)
        wi_s[:, sl] = jnp.exp(g - mt)
        c2_s[:, sl] = mt - bc
        emt_s[:, sl] = jnp.exp(-mt)
        wk_s[:, sl] = jnp.exp(b_last - bc + ic[:, sl] - m_new)
        ast_s[c] = jnp.exp(b_last + m_prev - m_new)
        upad = jnp.concatenate([u[:, sl], jnp.zeros((L - SUBLANES, L), jnp.float32)], axis=0)
        ucol_s[c * L:(c + 1) * L, :] = upad.T
        m_prev = m_new

    s_i = lax.broadcasted_iota(jnp.int32, (L, L), 0)
    t_i = lax.broadcasted_iota(jnp.int32, (L, L), 1)
    causal = s_i <= t_i
    low_half = lax.broadcasted_iota(jnp.int32, (L, LANES), 1) < DK
    zeros_q = jnp.zeros((DK, L), jnp.float32)
    zeros_p = jnp.zeros((L, L), bf16)

    def pair_row(ref, h0, rows):
        return jnp.concatenate([ref[h0:h0 + 1, rows], ref[h0 + 1:h0 + 2, rows]], axis=1)

    for c in range(nc):
        rows = slice(c * L, (c + 1) * L)
        for pr in range(NH // 2):
            h0 = 2 * pr
            hv0 = slice(h0 * DV, (h0 + 1) * DV)
            hv1 = slice((h0 + 1) * DV, (h0 + 2) * DV)
            kp = k_s[rows, pr * LANES:(pr + 1) * LANES]
            q0 = qt_s[h0 * DK:(h0 + 1) * DK, rows]
            q1 = qt_s[(h0 + 1) * DK:(h0 + 2) * DK, rows]
            qbd = jnp.concatenate([jnp.concatenate([q0, zeros_q], axis=1),
                                   jnp.concatenate([zeros_q, q1], axis=1)], axis=0).astype(bf16)
            st = _bdot(kp.astype(bf16), qbd)
            arg = jnp.concatenate(
                [jnp.where(causal, ucol_s[rows, h:h + 1] - c2_s[h:h + 1, rows], -jnp.inf)
                 for h in (h0, h0 + 1)], axis=1)
            pt = st * jnp.exp(arg)
            rs = jnp.sum(pt, axis=0, keepdims=True)
            ptb = pt.astype(bf16)
            pbd = jnp.concatenate([jnp.concatenate([ptb[:, 0:L], zeros_p], axis=1),
                                   jnp.concatenate([zeros_p, ptb[:, L:]], axis=1)], axis=0)
            vt = jnp.concatenate([vt_s[hv0, rows], vt_s[hv1, rows]], axis=1)
            state = st_s[pr]
            sq = _bdot(state.astype(bf16), qbd) * pair_row(wi_s, h0, rows)
            num = _bdot(vt.astype(bf16), pbd) + sq[0:DV]
            den = sq[DV:DV + 1] + rs
            hh = num / jnp.maximum(jnp.abs(den), pair_row(emt_s, h0, rows))
            hn = hh * lax.rsqrt(jnp.mean(hh * hh, axis=0, keepdims=True) + EPS)
            mixt_s[hv0, rows] = jax.nn.sigmoid(ogt_s[hv0, rows]) * (hn[:, 0:L] * gmhc_ref[hv0, :])
            mixt_s[hv1, rows] = jax.nn.sigmoid(ogt_s[hv1, rows]) * (hn[:, L:] * gmhc_ref[hv1, :])
            wkr = pair_row(wk_s, h0, rows)
            vw = jnp.concatenate([vt * wkr, jnp.broadcast_to(wkr, (2 * SUBLANES, 2 * L))], axis=0)
            km = jnp.concatenate([jnp.where(low_half, kp, 0.0), jnp.where(low_half, 0.0, kp)], axis=0)
            decay = jnp.where(low_half[0:1, :], ast_s[c][h0:h0 + 1, :], ast_s[c][h0 + 1:h0 + 2, :])
            st_s[pr] = decay * state + _bdot(vw.astype(bf16), km.astype(bf16))
    return m_prev


def _prompt_output(s):
    sb, TS = s["sb"], s["TS"]
    x_ref, wout_ref, x1_ref, mixt_s = s["x_ref"], s["wout_ref"], s["x1_ref"], s["mixt_s"]
    tok = slice(sb * TS, (sb + 1) * TS)
    out = x_ref[0, tok, :] + lax.dot_general(
        mixt_s[...].astype(jnp.bfloat16), wout_ref[0:MLSTM_W, :], (((0,), (0,)), ((), ())),
        preferred_element_type=jnp.float32)
    x1_ref[0, tok, :] = out + _bdot(s["yc"], wout_ref[MLSTM_W:, :])


def _mixer_prompt(x, weights, TB, NSUB, L):
    B, S, _ = x.shape
    TS = TB * NSUB
    nb = S // TS
    f32 = jnp.float32
    kern = functools.partial(_mixer_prompt_kernel, L=L, TB=TB, NSUB=NSUB)
    return pl.pallas_call(
        kern,
        grid=(B, nb),
        in_specs=[
            pl.BlockSpec((1, TS, D_MODEL), lambda b, j: (b, j, 0)),
            _const_spec((1, D_MODEL)),
            _const_spec((D_MODEL, PROJ_COLS)),
            _const_spec((TPROJ_ROWS, D_MODEL)),
            _const_spec((2 * SUBLANES, 1)),
            _const_spec((MLSTM_W, LANES)),
            _const_spec((CONV_W, CONV_CH)),
            _const_spec((D_MODEL, D_MODEL)),
        ],
        out_specs=[
            pl.BlockSpec((1, TS, D_MODEL), lambda b, j: (b, j, 0)),
            pl.BlockSpec((1, NH, DK, DV), lambda b, j: (b, 0, 0, 0)),
            pl.BlockSpec((1, NH // 2, 2 * DK), lambda b, j: (b, 0, 0)),
            pl.BlockSpec((1, SUBLANES, LANES), lambda b, j: (b, 0, 0)),
            pl.BlockSpec((1, SUBLANES, CONV_CH), lambda b, j: (b, 0, 0)),
        ],
        out_shape=[
            jax.ShapeDtypeStruct((B, S, D_MODEL), f32),
            jax.ShapeDtypeStruct((B, NH, DK, DV), f32),
            jax.ShapeDtypeStruct((B, NH // 2, 2 * DK), f32),
            jax.ShapeDtypeStruct((B, SUBLANES, LANES), f32),
            jax.ShapeDtypeStruct((B, SUBLANES, CONV_CH), f32),
        ],
        scratch_shapes=[
            pltpu.VMEM((NSUB, QK_W, TB), f32),
            pltpu.VMEM((NSUB, MLSTM_W, TB), f32),
            pltpu.VMEM((NSUB, MLSTM_W, TB), f32),
            pltpu.VMEM((NSUB, TB, QK_W), f32),
            pltpu.VMEM((NSUB, MLSTM_W, TB), f32),
            pltpu.VMEM((NSUB, TB + SUBLANES, CONV_CH), f32),
            pltpu.VMEM((NSUB, TB, LANES), f32),
            pltpu.VMEM((NSUB, SUBLANES, TB), f32),
            pltpu.VMEM((NSUB, SUBLANES, TB), f32),
            pltpu.VMEM((NSUB, SUBLANES, TB), f32),
            pltpu.VMEM((NSUB, SUBLANES, TB), f32),
            pltpu.VMEM((NSUB, TB // L, SUBLANES, LANES), f32),
            pltpu.VMEM((NH // 2, ST_ROWS, LANES), f32),
        ],
        compiler_params=pltpu.CompilerParams(
            dimension_semantics=("arbitrary", "arbitrary"), vmem_limit_bytes=VMEM_LIMIT),
        name="mixer_prompt",
    )(x, *weights)


def _row_scan(x, seg, op, fill):
    pos = lax.broadcasted_iota(jnp.int32, x.shape, 0) & (seg - 1)
    k = 1
    while k < seg:
        x = op(x, jnp.where(pos >= k, pltpu.roll(x, k, 0), fill))
        k *= 2
    return x


def _split_dot(x, sel, parts):
    acc = None
    rem = x
    for p in range(parts):
        hi = rem.astype(jnp.bfloat16)
        d = _bdot(hi, sel)
        acc = d if acc is None else acc + d
        if p + 1 < parts:
            rem = rem - hi.astype(jnp.float32)
    return acc


def _selectors():
    h = np.arange(NH)
    seg_qk = np.zeros((QK_W, LANES), np.float32)
    seg_qk[np.arange(QK_W), np.arange(QK_W) // DK] = 1.0
    exp_v = np.zeros((LANES, MLSTM_W), np.float32)
    exp_k = np.zeros((LANES, QK_W), np.float32)
    for i in h:
        exp_v[i, i * DV:(i + 1) * DV] = 1.0
        exp_k[i, i * DK:(i + 1) * DK] = 1.0
    mean_v = np.kron(np.eye(NH, dtype=np.float32), np.full((DV, DV), 1.0 / DV, np.float32))
    return tuple(jnp.asarray(m, jnp.bfloat16) for m in (seg_qk, exp_v, exp_k, mean_v))


def _mixer_sample_kernel(x_ref, cv_ref, c0_ref, n0_ref, m0_ref,
                         gmix_ref, win_ref, gb_ref, gmh_ref, cw_ref, wout_ref,
                         segqk_ref, expv_ref, expk_ref, meanv_ref,
                         x1_ref, cst_ref, nst_ref, mst_ref, cvo_ref,
                         xp_s, z_s, mp_s, qc_s, *, NSEQ):
    R = SUBLANES
    TS = NSEQ * R
    bf16 = jnp.bfloat16
    f32 = jnp.float32

    xp_s[:, SAMPLE_T:, :] = jnp.zeros((NSEQ, R - SAMPLE_T, D_MODEL), f32)
    xp_s[:, 0:SAMPLE_T, :] = x_ref[...]
    x = xp_s[...].reshape(TS, D_MODEL)
    a = _rms(x, gmix_ref[...]).astype(bf16)
    u = _bdot(a, win_ref[...])

    pos = lax.broadcasted_iota(jnp.int32, (TS, 1), 0) & (R - 1)
    real = pos < SAMPLE_T

    z = u[:, _CG0:_HC0] * u[:, _HC0:_GT0]
    z3 = z.reshape(NSEQ, R, CONV_CH)
    z_s[...] = z3
    z_s[:, R - (CONV_W - 1):, :] = cv_ref[...]
    zf = z_s[...].reshape(TS, CONV_CH)
    zm1 = jnp.where(pos >= 1, pltpu.roll(z, 1, 0), pltpu.roll(zf, TS - (R - 1), 0))
    zm2 = jnp.where(pos >= 2, pltpu.roll(z, 2, 0), pltpu.roll(zf, TS - (R - 2), 0))
    yc = u[:, _BG0:_CG0] * (cw_ref[0:1, :] * zm2 + cw_ref[1:2, :] * zm1 + cw_ref[2:3, :] * z)
    cvo_ref[...] = z3[:, SAMPLE_T - (CONV_W - 1):SAMPLE_T, :]

    gates = u[:, _GT0:PROJ_COLS] + gb_ref[...]
    ic = jnp.where(real, GATE_CAP * jnp.tanh(gates[:, :LANES] / GATE_CAP), NEG_BIG)
    lf = jnp.where(real, _log_sigmoid(gates[:, LANES:]), 0.0)
    b = _row_scan(lf, R, jnp.add, 0.0)
    uu = ic - b
    m_loc = b + _row_scan(uu, R, jnp.maximum, -jnp.inf)

    mp_s[...] = jnp.zeros(mp_s.shape, f32)
    mp_s[:, :, 0:NH] = m0_ref[...]
    m_prev = mp_s[...]
    b3 = b.reshape(NSEQ, R, LANES)
    ml3 = m_loc.reshape(NSEQ, R, LANES)
    b_last = b3[:, R - 1:R, :]
    m_new = jnp.maximum(b_last + m_prev, ml3[:, R - 1:R, :])
    g3 = b3 + m_prev
    mt3 = jnp.maximum(g3, ml3)
    wi = jnp.exp(g3 - mt3).reshape(TS, LANES)
    c2 = (mt3 - b3).reshape(TS, LANES)
    emt = jnp.exp(-mt3).reshape(TS, LANES)
    wk = jnp.exp(b_last - b3 + ic.reshape(NSEQ, R, LANES) - m_new).reshape(TS, LANES)
    a_st = jnp.broadcast_to(jnp.exp(b_last + m_prev - m_new), (NSEQ, R, LANES)).reshape(TS, LANES)
    mst_ref[...] = m_new[:, :, 0:NH]

    q = u[:, _Q0:_K0] * (DK ** -0.5)
    k = u[:, _K0:_V0]
    v = u[:, _V0:_OG0]
    rs = jnp.zeros((TS, LANES), f32)
    num = jnp.zeros((TS, MLSTM_W), f32)
    for d in range(SAMPLE_T):
        kd = k if d == 0 else pltpu.roll(k, d, 0)
        vd = v if d == 0 else pltpu.roll(v, d, 0)
        ud = uu if d == 0 else pltpu.roll(uu, d, 0)
        p = _split_dot(q * kd, segqk_ref[...], 2) * jnp.exp(ud - c2)
        rs = rs + p
        num = num + _split_dot(p, expv_ref[...], 2) * vd

    n0x = jnp.broadcast_to(n0_ref[...], (NSEQ, R, QK_W)).reshape(TS, QK_W)
    qn = _split_dot(q * n0x, segqk_ref[...], 2)
    den = wi * qn + rs
    rden = 1.0 / jnp.maximum(jnp.abs(den), emt)
    kw = k * _split_dot(wk, expk_ref[...], 2)
    ax = _split_dot(a_st, expv_ref[...], 3)
    lane_q = lax.broadcasted_iota(jnp.int32, (R, QK_W), 1)
    for i in range(NSEQ):
        rows = slice(i * R, (i + 1) * R)
        qi = q[rows, :]
        lhs = jnp.concatenate(
            [jnp.where((lane_q >= h * DK) & (lane_q < (h + 1) * DK), qi, 0.0) for h in range(NH)], axis=0)
        c0 = c0_ref[i]
        qc = _bdot(lhs.astype(bf16), c0.reshape(QK_W, DV).astype(bf16))
        qc_s[rows, :] = jnp.concatenate([qc[h * R:(h + 1) * R, :] for h in range(NH)], axis=1)
        kwi = kw[rows, :].astype(bf16)
        vi = v[rows, :].astype(bf16)
        for h in range(NH):
            dc = lax.dot_general(kwi[:, h * DK:(h + 1) * DK], vi[:, h * DV:(h + 1) * DV],
                                 (((0,), (0,)), ((), ())), preferred_element_type=f32)
            cst_ref[i, h] = ax[i * R:i * R + 1, h * DV:(h + 1) * DV] * c0[h] + dc

    hh = (_split_dot(wi, expv_ref[...], 2) * qc_s[...] + num) * _split_dot(rden, expv_ref[...], 2)
    ms = _split_dot(hh * hh, meanv_ref[...], 2)
    hm = jax.nn.sigmoid(u[:, _OG0:_BG0]) * (hh * lax.rsqrt(ms + EPS) * gmh_ref[...])
    mix = jnp.concatenate([hm, yc], axis=1).astype(bf16)
    out = x + _bdot(mix, wout_ref[...])
    x1_ref[...] = out.reshape(NSEQ, R, D_MODEL)[:, 0:SAMPLE_T, :]

    a_k = _split_dot(a_st, expk_ref[...], 3).reshape(NSEQ, R, QK_W)[:, 0:1, :]
    nst_ref[...] = a_k * n0_ref[...] + jnp.sum(kw.reshape(NSEQ, R, QK_W), axis=1, keepdims=True)


def _mixer_sample(x, cv, c0, n0, m0, weights, NSEQ):
    Bs = x.shape[0]
    f32 = jnp.float32
    kern = functools.partial(_mixer_sample_kernel, NSEQ=NSEQ)
    x_spec = pl.BlockSpec((NSEQ, SAMPLE_T, D_MODEL), lambda t: (t, 0, 0))
    cv_spec = pl.BlockSpec((NSEQ, CONV_W - 1, CONV_CH), lambda t: (t, 0, 0))
    c_spec = pl.BlockSpec((NSEQ, NH, DK, DV), lambda t: (t, 0, 0, 0))
    n_spec = pl.BlockSpec((NSEQ, 1, QK_W), lambda t: (t, 0, 0))
    m_spec = pl.BlockSpec((NSEQ, 1, NH), lambda t: (t, 0, 0))
    return pl.pallas_call(
        kern,
        grid=(Bs // NSEQ,),
        in_specs=[
            x_spec, cv_spec, c_spec, n_spec, m_spec,
            _const_spec((1, D_MODEL)),
            _const_spec((D_MODEL, PROJ_COLS)),
            _const_spec((1, 2 * LANES)),
            _const_spec((1, MLSTM_W)),
            _const_spec((CONV_W, CONV_CH)),
            _const_spec((D_MODEL, D_MODEL)),
            _const_spec((QK_W, LANES)),
            _const_spec((LANES, MLSTM_W)),
            _const_spec((LANES, QK_W)),
            _const_spec((MLSTM_W, MLSTM_W)),
        ],
        out_specs=[x_spec, c_spec, n_spec, m_spec, cv_spec],
        out_shape=[
            jax.ShapeDtypeStruct((Bs, SAMPLE_T, D_MODEL), f32),
            jax.ShapeDtypeStruct((Bs, NH, DK, DV), f32),
            jax.ShapeDtypeStruct((Bs, 1, QK_W), f32),
            jax.ShapeDtypeStruct((Bs, 1, NH), f32),
            jax.ShapeDtypeStruct((Bs, CONV_W - 1, CONV_CH), f32),
        ],
        scratch_shapes=[
            pltpu.VMEM((NSEQ, SUBLANES, D_MODEL), f32),
            pltpu.VMEM((NSEQ, SUBLANES, CONV_CH), f32),
            pltpu.VMEM((NSEQ, 1, LANES), f32),
            pltpu.VMEM((NSEQ * SUBLANES, MLSTM_W), f32),
        ],
        compiler_params=pltpu.CompilerParams(
            dimension_semantics=("arbitrary",), vmem_limit_bytes=VMEM_LIMIT),
        name="mixer_sample",
    )(x, cv, c0, n0, m0, *weights)


def _mixer_weights(g_mix, w_in, b_ig, b_fg, g_mh, conv_w, w_out):
    f32 = jnp.float32
    w_all, wt = _prep_weights(w_in)
    wout = w_out.astype(jnp.bfloat16)
    gmix = g_mix.reshape(1, D_MODEL)
    zeros = jnp.zeros((SUBLANES - NH,), f32)
    gb_col = jnp.concatenate([b_ig, zeros, b_fg, zeros]).reshape(2 * SUBLANES, 1)
    zeros = jnp.zeros((LANES - NH,), f32)
    gb_row = jnp.concatenate([b_ig, zeros, b_fg, zeros]).reshape(1, 2 * LANES)
    gmhc = jnp.broadcast_to(g_mh[:, None], (MLSTM_W, LANES))
    prompt = (gmix, w_all, wt, gb_col, gmhc, conv_w, wout)
    sample = (gmix, w_all, gb_row, g_mh.reshape(1, MLSTM_W), conv_w, wout) + _selectors()
    return prompt, sample


def _ffn_kernel(x_ref, pe_ref, gffn_ref, wg_ref, wu_ref, wd_ref, wple_ref, gple_ref, wpg_ref,
                gfin_ref, y_ref, *, final_norm, n_sub):
    sub = x_ref.shape[0] // n_sub
    for s in range(n_sub):
        rows = slice(s * sub, (s + 1) * sub)
        x = x_ref[rows, :]
        f = _rms(x, gffn_ref[...]).astype(jnp.bfloat16)
        for c0 in range(0, D_FF, FF_CHUNK):
            c1 = min(c0 + FF_CHUNK, D_FF)
            gate = _bdot(f, wg_ref[:, c0:c1])
            up = _bdot(f, wu_ref[:, c0:c1])
            hmid = (gate * jax.nn.sigmoid(gate) * up).astype(jnp.bfloat16)
            x = x + _bdot(hmid, wd_ref[c0:c1, :])
        e = _rms(_bdot(pe_ref[rows, :].astype(jnp.bfloat16), wple_ref[...]), gple_ref[...])
        x = x + jax.nn.sigmoid(_bdot(x.astype(jnp.bfloat16), wpg_ref[...])) * e
        if final_norm:
            x = _rms(x, gfin_ref[...])
        y_ref[rows, :] = x


def _ffn(x, pe, weights, TM, final_norm, n_sub=1):
    T = x.shape[0]
    kern = functools.partial(_ffn_kernel, final_norm=final_norm, n_sub=n_sub)
    return pl.pallas_call(
        kern,
        grid=(T // TM,),
        in_specs=[
            pl.BlockSpec((TM, D_MODEL), lambda t: (t, 0)),
            pl.BlockSpec((TM, PLE_DIM), lambda t: (t, 0)),
            _const_spec((1, D_MODEL)),
            _const_spec((D_MODEL, D_FF)),
            _const_spec((D_MODEL, D_FF)),
            _const_spec((D_FF, D_MODEL)),
            _const_spec((PLE_DIM, D_MODEL)),
            _const_spec((1, D_MODEL)),
            _const_spec((D_MODEL, D_MODEL)),
            _const_spec((1, D_MODEL)),
        ],
        out_specs=pl.BlockSpec((TM, D_MODEL), lambda t: (t, 0)),
        out_shape=jax.ShapeDtypeStruct((T, D_MODEL), jnp.float32),
        compiler_params=pltpu.CompilerParams(
            dimension_semantics=("arbitrary",), vmem_limit_bytes=VMEM_LIMIT),
        name="ffn",
    )(x, pe, *weights)


def kernel(x_prompt, x_sample, p_prompt, p_sample, state_C, state_n, state_m, state_conv,
           g_mix, w_in, b_ig, b_fg, g_mh, conv_w, w_out, g_ffn, w_gate, w_up, w_down,
           w_ple, g_ple, w_pg, g_final):
    bf16 = jnp.bfloat16
    depth = g_mix.shape[0]
    B, S, _ = x_prompt.shape
    Bs, Ss, _ = x_sample.shape
    assert Ss == SAMPLE_T and S % PROMPT_TS == 0 and Bs % SAMPLE_NSEQ == 0

    xp = x_prompt
    xs = x_sample
    outs = [[] for _ in range(8)]
    for i in range(depth):
        last = i == depth - 1
        fw = (g_ffn[i].reshape(1, D_MODEL), w_gate[i].astype(bf16), w_up[i].astype(bf16),
              w_down[i].astype(bf16), w_ple[i].astype(bf16), g_ple[i].reshape(1, D_MODEL),
              w_pg[i].astype(bf16), g_final.reshape(1, D_MODEL))

        pw, sw = _mixer_weights(g_mix[i], w_in[i], b_ig[i], b_fg[i], g_mh[i], conv_w[i], w_out[i])
        x1p, cp, n_p, mp, cvp = _mixer_prompt(xp, pw, TB=PROMPT_TS, NSUB=PROMPT_NSUB, L=PROMPT_L)
        xp = _ffn(x1p.reshape(B * S, D_MODEL), p_prompt[i].reshape(B * S, PLE_DIM), fw, 1024,
                  last, n_sub=1).reshape(B, S, D_MODEL)

        x1s, cs, n_s, ms, cvs = _mixer_sample(
            xs, state_conv[i], state_C[i], state_n[i].reshape(Bs, 1, QK_W),
            state_m[i].reshape(Bs, 1, NH), sw, SAMPLE_NSEQ)
        xs = _ffn(x1s.reshape(Bs * Ss, D_MODEL), p_sample[i].reshape(Bs * Ss, PLE_DIM), fw, 256,
                  last).reshape(Bs, Ss, D_MODEL)

        new = (cp, n_p.reshape(B, NH, DK), mp[:, :NH, 0], cvp[:, SUBLANES - (CONV_W - 1):],
               cs, n_s.reshape(Bs, NH, DK), ms.reshape(Bs, NH), cvs)
        for lst, v in zip(outs, new):
            lst.append(v)

    return (xp, xs) + tuple(jnp.stack(l) for l in outs)
```

```python
import functools

import numpy as np
import jax
import jax.numpy as jnp
from jax import lax
from jax.experimental import pallas as pl
from jax.experimental.pallas import tpu as pltpu

D_MODEL = 1024
NH = 4
DK = 64
DV = 128
MLSTM_W = NH * DV
QK_W = NH * DK
CONV_CH = 512
CONV_W = 3
D_FF = 2816
FF_CHUNK = 512
PLE_DIM = 256
PROMPT_L = 128
PROMPT_TS = 512
PROMPT_NSUB = 2
SAMPLE_T = 4
SAMPLE_NSEQ = 32
GATE_CAP = 15.0
EPS = 1e-6

LANES = 128
SUBLANES = 8
NEG_BIG = -1e30

VMEM_LIMIT = 56 * 1024 * 1024


def _rms(x, g):
    ms = jnp.mean(x * x, axis=-1, keepdims=True)
    return x * lax.rsqrt(ms + EPS) * g


def _bdot(a, b):
    return jnp.dot(a, b, preferred_element_type=jnp.float32)


def _log_sigmoid(x):
    return jnp.minimum(x, 0.0) - jnp.log1p(jnp.exp(-jnp.abs(x)))


def _const_spec(shape):
    nd = len(shape)
    return pl.BlockSpec(shape, lambda *_: (0,) * nd, pipeline_mode=pl.Buffered(1))


_IG0 = 2 * QK_W + MLSTM_W
_OG_IN = _IG0 + 2 * NH
PROJ_IN = _OG_IN + MLSTM_W + 3 * CONV_CH
_Q0, _K0, _V0, _OG0, _BG0, _CG0, _HC0, _GT0 = 0, 256, 512, 1024, 1536, 2048, 2560, 3072
PROJ_COLS = _GT0 + 2 * LANES
_TQ0, _TV0, _TOG0, _TG0 = 0, 256, 768, 1280
TPROJ_ROWS = _TG0 + 2 * SUBLANES
PREP_TK = 256


def _prep_weights_kernel(w_ref, wall_ref, wt_ref):
    bf16 = jnp.bfloat16
    wf = w_ref[...]
    tk = wf.shape[1]
    g8 = wf[_IG0:_OG_IN]
    row = lax.broadcasted_iota(jnp.int32, (SUBLANES, tk), 0)
    g_in = jnp.where(row < NH, g8, 0.0)
    g_fg = jnp.where(row < NH, pltpu.roll(g8, SUBLANES - NH, 0), 0.0)

    wt_ref[_TQ0:_TV0, :] = wf[_Q0:_K0].astype(bf16)
    wt_ref[_TV0:_TOG0, :] = wf[_V0:_OG0].astype(bf16)
    wt_ref[_TOG0:_TG0, :] = wf[_OG_IN:_OG_IN + MLSTM_W].astype(bf16)
    wt_ref[_TG0:TPROJ_ROWS, :] = jnp.concatenate([g_in, g_fg], axis=0).astype(bf16)

    wall_ref[:, 0:_OG0] = wf[0:_IG0].T.astype(bf16)
    wall_ref[:, _OG0:_GT0] = wf[_OG_IN:PROJ_IN].T.astype(bf16)
    pad = jnp.zeros((LANES - SUBLANES, tk), jnp.float32)
    wall_ref[:, _GT0:_GT0 + LANES] = jnp.concatenate([g_in, pad], axis=0).T.astype(bf16)
    wall_ref[:, _GT0 + LANES:PROJ_COLS] = jnp.concatenate([g_fg, pad], axis=0).T.astype(bf16)


def _prep_weights(w_in):
    bf16 = jnp.bfloat16
    return pl.pallas_call(
        _prep_weights_kernel,
        grid=(D_MODEL // PREP_TK,),
        in_specs=[pl.BlockSpec((PROJ_IN, PREP_TK), lambda i: (0, i))],
        out_specs=[pl.BlockSpec((PREP_TK, PROJ_COLS), lambda i: (i, 0)),
                   pl.BlockSpec((TPROJ_ROWS, PREP_TK), lambda i: (0, i))],
        out_shape=[jax.ShapeDtypeStruct((D_MODEL, PROJ_COLS), bf16),
                   jax.ShapeDtypeStruct((TPROJ_ROWS, D_MODEL), bf16)],
        compiler_params=pltpu.CompilerParams(
            dimension_semantics=("arbitrary",), vmem_limit_bytes=VMEM_LIMIT),
        name="prep_weights",
    )(w_in.T)


ST_ROWS = DV + 2 * SUBLANES


def _lane_scan(x, seg, op, fill):
    pos = lax.broadcasted_iota(jnp.int32, x.shape, 1) & (seg - 1)
    k = 1
    while k < seg:
        x = op(x, jnp.where(pos >= k, pltpu.roll(x, k, 1), fill))
        k *= 2
    return x


def _mixer_prompt_kernel(x_ref, gmix_ref, wn_ref, wt_ref, gbt_ref, gmhc_ref, cw_ref, wout_ref,
                         x1_ref, c_ref, n_ref, m_ref, cvo_ref,
                         qt_s, vt_s, ogt_s, k_s, mixt_s, z_s, ucol_s, wi_s, c2_s, emt_s, wk_s, ast_s, st_s,
                         *, L, TB, NSUB):
    j = pl.program_id(1)

    @pl.when(j == 0)
    def _():
        z_s[0, 0:SUBLANES, :] = jnp.zeros((SUBLANES, CONV_CH), jnp.float32)
        st_s[...] = jnp.zeros(st_s.shape, jnp.float32)
        m_ref[...] = jnp.zeros(m_ref.shape, jnp.float32)

    m_prev = m_ref[0]
    sub = [dict(sb=sb, x_ref=x_ref, gmix_ref=gmix_ref, wn_ref=wn_ref, wt_ref=wt_ref, gbt_ref=gbt_ref,
                gmhc_ref=gmhc_ref, cw_ref=cw_ref, wout_ref=wout_ref, x1_ref=x1_ref, qt_s=qt_s.at[sb],
                vt_s=vt_s.at[sb], ogt_s=ogt_s.at[sb], k_s=k_s.at[sb], mixt_s=mixt_s.at[sb], z_s=z_s,
                ucol_s=ucol_s.at[sb], wi_s=wi_s.at[sb], c2_s=c2_s.at[sb], emt_s=emt_s.at[sb],
                wk_s=wk_s.at[sb], ast_s=ast_s.at[sb], st_s=st_s, L=L, TS=TB) for sb in range(NSUB)]
    for s in sub:
        _prompt_project(s)
    for s in sub:
        m_prev = _prompt_chunks(s, m_prev)
    for s in sub:
        _prompt_output(s)
    m_ref[0] = m_prev
    cvo_ref[0] = z_s[NSUB - 1, TB:TB + SUBLANES, :]
    z_s[0, 0:SUBLANES, :] = z_s[NSUB - 1, TB:TB + SUBLANES, :]

    @pl.when(j == pl.num_programs(1) - 1)
    def _():
        for pr in range(NH // 2):
            state = st_s[pr]
            c_pair = state[0:DV, :].T
            c_ref[0, 2 * pr] = c_pair[0:DK, :]
            c_ref[0, 2 * pr + 1] = c_pair[DK:, :]
            n_ref[0, pr:pr + 1, :] = state[DV:DV + 1, :]


def _prompt_project(s):
    sb, L, TS = s["sb"], s["L"], s["TS"]
    x_ref, gmix_ref, wn_ref, wt_ref, gbt_ref, cw_ref = (
        s["x_ref"], s["gmix_ref"], s["wn_ref"], s["wt_ref"], s["gbt_ref"], s["cw_ref"])
    qt_s, vt_s, ogt_s, k_s, z_s = s["qt_s"], s["vt_s"], s["ogt_s"], s["k_s"], s["z_s"]
    bf16 = jnp.bfloat16

    x = x_ref[0, sb * TS:(sb + 1) * TS, :]
    a = _rms(x, gmix_ref[...]).astype(bf16)
    nt_dims = (((1,), (1,)), ((), ()))

    gt = lax.dot_general(wt_ref[_TG0:TPROJ_ROWS, :], a, nt_dims,
                         preferred_element_type=jnp.float32) + gbt_ref[...]
    ic = GATE_CAP * jnp.tanh(gt[0:SUBLANES] / GATE_CAP)
    lf = _log_sigmoid(gt[SUBLANES:])
    b = _lane_scan(lf, L, jnp.add, 0.0)
    u = ic - b
    m_loc = b + _lane_scan(u, L, jnp.maximum, -jnp.inf)

    k_s[...] = _bdot(a, wn_ref[:, _K0:_V0])
    un = _bdot(a, wn_ref[:, _BG0:_GT0])
    z = un[:, CONV_CH:2 * CONV_CH] * un[:, 2 * CONV_CH:]
    if sb > 0:
        z_s[sb, 0:SUBLANES, :] = z_s[sb - 1, TS:TS + SUBLANES, :]
    z_s[sb, SUBLANES:SUBLANES + TS, :] = z
    yconv = (cw_ref[0:1, :] * z_s[sb, SUBLANES - 2:SUBLANES - 2 + TS, :]
             + cw_ref[1:2, :] * z_s[sb, SUBLANES - 1:SUBLANES - 1 + TS, :]
             + cw_ref[2:3, :] * z)
    yc = (un[:, 0:CONV_CH] * yconv).astype(bf16)

    ut = lax.dot_general(wt_ref[0:_TG0, :], a, nt_dims, preferred_element_type=jnp.float32)
    qt_s[...] = ut[_TQ0:_TV0] * (DK ** -0.5)
    vt_s[...] = ut[_TV0:_TOG0]
    ogt_s[...] = ut[_TOG0:_TG0]
    s.update(b=b, u=u, ic=ic, m_loc=m_loc, yc=yc)


def _prompt_chunks(s, m_prev):
    L, TS = s["L"], s["TS"]
    gmhc_ref, qt_s, vt_s, ogt_s, k_s, mixt_s, ucol_s, wi_s, c2_s, emt_s, wk_s, ast_s, st_s = (
        s["gmhc_ref"], s["qt_s"], s["vt_s"], s["ogt_s"], s["k_s"], s["mixt_s"], s["ucol_s"], s["wi_s"],
        s["c2_s"], s["emt_s"], s["wk_s"], s["ast_s"], s["st_s"])
    b, u, ic, m_loc = s["b"], s["u"], s["ic"], s["m_loc"]
    nc = TS // L
    bf16 = jnp.bfloat16

    for c in range(nc):
        sl = slice(c * L, (c + 1) * L)
        bc, mlc = b[:, sl], m_loc[:, sl]
        b_last = jnp.broadcast_to(bc[:, L - 1:L], bc.shape)
        m_new = jnp.maximum(b_last + m_prev, jnp.broadcast_to(mlc[:, L - 1:L], bc.shape))
        g = bc + m_prev
        mt = jnp.maximum(g, mlc)
        wi_s[:, sl] = jnp.exp(g - mt)
        c2_s[:, sl] = mt - bc
        emt_s[:, sl] = jnp.exp(-mt)
        wk_s[:, sl] = jnp.exp(b_last - bc + ic[:, sl] - m_new)
        ast_s[c] = jnp.exp(b_last + m_prev - m_new)
        upad = jnp.concatenate([u[:, sl], jnp.zeros((L - SUBLANES, L), jnp.float32)], axis=0)
        ucol_s[c * L:(c + 1) * L, :] = upad.T
        m_prev = m_new

    s_i = lax.broadcasted_iota(jnp.int32, (L, L), 0)
    t_i = lax.broadcasted_iota(jnp.int32, (L, L), 1)
    causal = s_i <= t_i
    low_half = lax.broadcasted_iota(jnp.int32, (L, LANES), 1) < DK
    zeros_q = jnp.zeros((DK, L), jnp.float32)
    zeros_p = jnp.zeros((L, L), bf16)

    def pair_row(ref, h0, rows):
        return jnp.concatenate([ref[h0:h0 + 1, rows], ref[h0 + 1:h0 + 2, rows]], axis=1)

    for c in range(nc):
        rows = slice(c * L, (c + 1) * L)
        for pr in range(NH // 2):
            h0 = 2 * pr
            hv0 = slice(h0 * DV, (h0 + 1) * DV)
            hv1 = slice((h0 + 1) * DV, (h0 + 2) * DV)
            kp = k_s[rows, pr * LANES:(pr + 1) * LANES]
            q0 = qt_s[h0 * DK:(h0 + 1) * DK, rows]
            q1 = qt_s[(h0 + 1) * DK:(h0 + 2) * DK, rows]
            qbd = jnp.concatenate([jnp.concatenate([q0, zeros_q], axis=1),
                                   jnp.concatenate([zeros_q, q1], axis=1)], axis=0).astype(bf16)
            st = _bdot(kp.astype(bf16), qbd)
            arg = jnp.concatenate(
                [jnp.where(causal, ucol_s[rows, h:h + 1] - c2_s[h:h + 1, rows], -jnp.inf)
                 for h in (h0, h0 + 1)], axis=1)
            pt = st * jnp.exp(arg)
            rs = jnp.sum(pt, axis=0, keepdims=True)
            ptb = pt.astype(bf16)
            pbd = jnp.concatenate([jnp.concatenate([ptb[:, 0:L], zeros_p], axis=1),
                                   jnp.concatenate([zeros_p, ptb[:, L:]], axis=1)], axis=0)
            vt = jnp.concatenate([vt_s[hv0, rows], vt_s[hv1, rows]], axis=1)
            state = st_s[pr]
            sq = _bdot(state.astype(bf16), qbd) * pair_row(wi_s, h0, rows)
            num = _bdot(vt.astype(bf16), pbd) + sq[0:DV]
            den = sq[DV:DV + 1] + rs
            hh = num / jnp.maximum(jnp.abs(den), pair_row(emt_s, h0, rows))
            hn = hh * lax.rsqrt(jnp.mean(hh * hh, axis=0, keepdims=True) + EPS)
            mixt_s[hv0, rows] = jax.nn.sigmoid(ogt_s[hv0, rows]) * (hn[:, 0:L] * gmhc_ref[hv0, :])
            mixt_s[hv1, rows] = jax.nn.sigmoid(ogt_s[hv1, rows]) * (hn[:, L:] * gmhc_ref[hv1, :])
            wkr = pair_row(wk_s, h0, rows)
            vw = jnp.concatenate([vt * wkr, jnp.broadcast_to(wkr, (2 * SUBLANES, 2 * L))], axis=0)
            km = jnp.concatenate([jnp.where(low_half, kp, 0.0), jnp.where(low_half, 0.0, kp)], axis=0)
            decay = jnp.where(low_half[0:1, :], ast_s[c][h0:h0 + 1, :], ast_s[c][h0 + 1:h0 + 2, :])
            st_s[pr] = decay * state + _bdot(vw.astype(bf16), km.astype(bf16))
    return m_prev


def _prompt_output(s):
    sb, TS = s["sb"], s["TS"]
    x_ref, wout_ref, x1_ref, mixt_s = s["x_ref"], s["wout_ref"], s["x1_ref"], s["mixt_s"]
    tok = slice(sb * TS, (sb + 1) * TS)
    out = x_ref[0, tok, :] + lax.dot_general(
        mixt_s[...].astype(jnp.bfloat16), wout_ref[0:MLSTM_W, :], (((0,), (0,)), ((), ())),
        preferred_element_type=jnp.float32)
    x1_ref[0, tok, :] = out + _bdot(s["yc"], wout_ref[MLSTM_W:, :])


def _mixer_prompt(x, weights, TB, NSUB, L):
    B, S, _ = x.shape
    TS = TB * NSUB
    nb = S // TS
    f32 = jnp.float32
    kern = functools.partial(_mixer_prompt_kernel, L=L, TB=TB, NSUB=NSUB)
    return pl.pallas_call(
        kern,
        grid=(B, nb),
        in_specs=[
            pl.BlockSpec((1, TS, D_MODEL), lambda b, j: (b, j, 0)),
            _const_spec((1, D_MODEL)),
            _const_spec((D_MODEL, PROJ_COLS)),
            _const_spec((TPROJ_ROWS, D_MODEL)),
            _const_spec((2 * SUBLANES, 1)),
            _const_spec((MLSTM_W, LANES)),
            _const_spec((CONV_W, CONV_CH)),
            _const_spec((D_MODEL, D_MODEL)),
        ],
        out_specs=[
            pl.BlockSpec((1, TS, D_MODEL), lambda b, j: (b, j, 0)),
            pl.BlockSpec((1, NH, DK, DV), lambda b, j: (b, 0, 0, 0)),
            pl.BlockSpec((1, NH // 2, 2 * DK), lambda b, j: (b, 0, 0)),
            pl.BlockSpec((1, SUBLANES, LANES), lambda b, j: (b, 0, 0)),
            pl.BlockSpec((1, SUBLANES, CONV_CH), lambda b, j: (b, 0, 0)),
        ],
        out_shape=[
            jax.ShapeDtypeStruct((B, S, D_MODEL), f32),
            jax.ShapeDtypeStruct((B, NH, DK, DV), f32),
            jax.ShapeDtypeStruct((B, NH // 2, 2 * DK), f32),
            jax.ShapeDtypeStruct((B, SUBLANES, LANES), f32),
            jax.ShapeDtypeStruct((B, SUBLANES, CONV_CH), f32),
        ],
        scratch_shapes=[
            pltpu.VMEM((NSUB, QK_W, TB), f32),
            pltpu.VMEM((NSUB, MLSTM_W, TB), f32),
            pltpu.VMEM((NSUB, MLSTM_W, TB), f32),
            pltpu.VMEM((NSUB, TB, QK_W), f32),
            pltpu.VMEM((NSUB, MLSTM_W, TB), f32),
            pltpu.VMEM((NSUB, TB + SUBLANES, CONV_CH), f32),
            pltpu.VMEM((NSUB, TB, LANES), f32),
            pltpu.VMEM((NSUB, SUBLANES, TB), f32),
            pltpu.VMEM((NSUB, SUBLANES, TB), f32),
            pltpu.VMEM((NSUB, SUBLANES, TB), f32),
            pltpu.VMEM((NSUB, SUBLANES, TB), f32),
            pltpu.VMEM((NSUB, TB // L, SUBLANES, LANES), f32),
            pltpu.VMEM((NH // 2, ST_ROWS, LANES), f32),
        ],
        compiler_params=pltpu.CompilerParams(
            dimension_semantics=("arbitrary", "arbitrary"), vmem_limit_bytes=VMEM_LIMIT),
        name="mixer_prompt",
    )(x, *weights)


def _row_scan(x, seg, op, fill):
    pos = lax.broadcasted_iota(jnp.int32, x.shape, 0) & (seg - 1)
    k = 1
    while k < seg:
        x = op(x, jnp.where(pos >= k, pltpu.roll(x, k, 0), fill))
        k *= 2
    return x


def _split_dot(x, sel, parts):
    acc = None
    rem = x
    for p in range(parts):
        hi = rem.astype(jnp.bfloat16)
        d = _bdot(hi, sel)
        acc = d if acc is None else acc + d
        if p + 1 < parts:
            rem = rem - hi.astype(jnp.float32)
    return acc


def _selectors():
    h = np.arange(NH)
    seg_qk = np.zeros((QK_W, LANES), np.float32)
    seg_qk[np.arange(QK_W), np.arange(QK_W) // DK] = 1.0
    exp_v = np.zeros((LANES, MLSTM_W), np.float32)
    exp_k = np.zeros((LANES, QK_W), np.float32)
    for i in h:
        exp_v[i, i * DV:(i + 1) * DV] = 1.0
        exp_k[i, i * DK:(i + 1) * DK] = 1.0
    mean_v = np.kron(np.eye(NH, dtype=np.float32), np.full((DV, DV), 1.0 / DV, np.float32))
    return tuple(jnp.asarray(m, jnp.bfloat16) for m in (seg_qk, exp_v, exp_k, mean_v))


def _mixer_sample_kernel(x_ref, cv_ref, c0_ref, n0_ref, m0_ref,
                         gmix_ref, win_ref, gb_ref, gmh_ref, cw_ref, wout_ref,
                         segqk_ref, expv_ref, expk_ref, meanv_ref,
                         x1_ref, cst_ref, nst_ref, mst_ref, cvo_ref,
                         xp_s, z_s, mp_s, qc_s, *, NSEQ):
    R = SUBLANES
    TS = NSEQ * R
    bf16 = jnp.bfloat16
    f32 = jnp.float32

    xp_s[:, SAMPLE_T:, :] = jnp.zeros((NSEQ, R - SAMPLE_T, D_MODEL), f32)
    xp_s[:, 0:SAMPLE_T, :] = x_ref[...]
    x = xp_s[...].reshape(TS, D_MODEL)
    a = _rms(x, gmix_ref[...]).astype(bf16)
    u = _bdot(a, win_ref[...])

    pos = lax.broadcasted_iota(jnp.int32, (TS, 1), 0) & (R - 1)
    real = pos < SAMPLE_T

    z = u[:, _CG0:_HC0] * u[:, _HC0:_GT0]
    z3 = z.reshape(NSEQ, R, CONV_CH)
    z_s[...] = z3
    z_s[:, R - (CONV_W - 1):, :] = cv_ref[...]
    zf = z_s[...].reshape(TS, CONV_CH)
    zm1 = jnp.where(pos >= 1, pltpu.roll(z, 1, 0), pltpu.roll(zf, TS - (R - 1), 0))
    zm2 = jnp.where(pos >= 2, pltpu.roll(z, 2, 0), pltpu.roll(zf, TS - (R - 2), 0))
    yc = u[:, _BG0:_CG0] * (cw_ref[0:1, :] * zm2 + cw_ref[1:2, :] * zm1 + cw_ref[2:3, :] * z)
    cvo_ref[...] = z3[:, SAMPLE_T - (CONV_W - 1):SAMPLE_T, :]

    gates = u[:, _GT0:PROJ_COLS] + gb_ref[...]
    ic = jnp.where(real, GATE_CAP * jnp.tanh(gates[:, :LANES] / GATE_CAP), NEG_BIG)
    lf = jnp.where(real, _log_sigmoid(gates[:, LANES:]), 0.0)
    b = _row_scan(lf, R, jnp.add, 0.0)
    uu = ic - b
    m_loc = b + _row_scan(uu, R, jnp.maximum, -jnp.inf)

    mp_s[...] = jnp.zeros(mp_s.shape, f32)
    mp_s[:, :, 0:NH] = m0_ref[...]
    m_prev = mp_s[...]
    b3 = b.reshape(NSEQ, R, LANES)
    ml3 = m_loc.reshape(NSEQ, R, LANES)
    b_last = b3[:, R - 1:R, :]
    m_new = jnp.maximum(b_last + m_prev, ml3[:, R - 1:R, :])
    g3 = b3 + m_prev
    mt3 = jnp.maximum(g3, ml3)
    wi = jnp.exp(g3 - mt3).reshape(TS, LANES)
    c2 = (mt3 - b3).reshape(TS, LANES)
    emt = jnp.exp(-mt3).reshape(TS, LANES)
    wk = jnp.exp(b_last - b3 + ic.reshape(NSEQ, R, LANES) - m_new).reshape(TS, LANES)
    a_st = jnp.broadcast_to(jnp.exp(b_last + m_prev - m_new), (NSEQ, R, LANES)).reshape(TS, LANES)
    mst_ref[...] = m_new[:, :, 0:NH]

    q = u[:, _Q0:_K0] * (DK ** -0.5)
    k = u[:, _K0:_V0]
    v = u[:, _V0:_OG0]
    rs = jnp.zeros((TS, LANES), f32)
    num = jnp.zeros((TS, MLSTM_W), f32)
    for d in range(SAMPLE_T):
        kd = k if d == 0 else pltpu.roll(k, d, 0)
        vd = v if d == 0 else pltpu.roll(v, d, 0)
        ud = uu if d == 0 else pltpu.roll(uu, d, 0)
        p = _split_dot(q * kd, segqk_ref[...], 2) * jnp.exp(ud - c2)
        rs = rs + p
        num = num + _split_dot(p, expv_ref[...], 2) * vd

    n0x = jnp.broadcast_to(n0_ref[...], (NSEQ, R, QK_W)).reshape(TS, QK_W)
    qn = _split_dot(q * n0x, segqk_ref[...], 2)
    den = wi * qn + rs
    rden = 1.0 / jnp.maximum(jnp.abs(den), emt)
    kw = k * _split_dot(wk, expk_ref[...], 2)
    ax = _split_dot(a_st, expv_ref[...], 3)
    lane_q = lax.broadcasted_iota(jnp.int32, (R, QK_W), 1)
    for i in range(NSEQ):
        rows = slice(i * R, (i + 1) * R)
        qi = q[rows, :]
        lhs = jnp.concatenate(
            [jnp.where((lane_q >= h * DK) & (lane_q < (h + 1) * DK), qi, 0.0) for h in range(NH)], axis=0)
        c0 = c0_ref[i]
        qc = _bdot(lhs.astype(bf16), c0.reshape(QK_W, DV).astype(bf16))
        qc_s[rows, :] = jnp.concatenate([qc[h * R:(h + 1) * R, :] for h in range(NH)], axis=1)
        kwi = kw[rows, :].astype(bf16)
        vi = v[rows, :].astype(bf16)
        for h in range(NH):
            dc = lax.dot_general(kwi[:, h * DK:(h + 1) * DK], vi[:, h * DV:(h + 1) * DV],
                                 (((0,), (0,)), ((), ())), preferred_element_type=f32)
            cst_ref[i, h] = ax[i * R:i * R + 1, h * DV:(h + 1) * DV] * c0[h] + dc

    hh = (_split_dot(wi, expv_ref[...], 2) * qc_s[...] + num) * _split_dot(rden, expv_ref[...], 2)
    ms = _split_dot(hh * hh, meanv_ref[...], 2)
    hm = jax.nn.sigmoid(u[:, _OG0:_BG0]) * (hh * lax.rsqrt(ms + EPS) * gmh_ref[...])
    mix = jnp.concatenate([hm, yc], axis=1).astype(bf16)
    out = x + _bdot(mix, wout_ref[...])
    x1_ref[...] = out.reshape(NSEQ, R, D_MODEL)[:, 0:SAMPLE_T, :]

    a_k = _split_dot(a_st, expk_ref[...], 3).reshape(NSEQ, R, QK_W)[:, 0:1, :]
    nst_ref[...] = a_k * n0_ref[...] + jnp.sum(kw.reshape(NSEQ, R, QK_W), axis=1, keepdims=True)


def _mixer_sample(x, cv, c0, n0, m0, weights, NSEQ):
    Bs = x.shape[0]
    f32 = jnp.float32
    kern = functools.partial(_mixer_sample_kernel, NSEQ=NSEQ)
    x_spec = pl.BlockSpec((NSEQ, SAMPLE_T, D_MODEL), lambda t: (t, 0, 0))
    cv_spec = pl.BlockSpec((NSEQ, CONV_W - 1, CONV_CH), lambda t: (t, 0, 0))
    c_spec = pl.BlockSpec((NSEQ, NH, DK, DV), lambda t: (t, 0, 0, 0))
    n_spec = pl.BlockSpec((NSEQ, 1, QK_W), lambda t: (t, 0, 0))
    m_spec = pl.BlockSpec((NSEQ, 1, NH), lambda t: (t, 0, 0))
    return pl.pallas_call(
        kern,
        grid=(Bs // NSEQ,),
        in_specs=[
            x_spec, cv_spec, c_spec, n_spec, m_spec,
            _const_spec((1, D_MODEL)),
            _const_spec((D_MODEL, PROJ_COLS)),
            _const_spec((1, 2 * LANES)),
            _const_spec((1, MLSTM_W)),
            _const_spec((CONV_W, CONV_CH)),
            _const_spec((D_MODEL, D_MODEL)),
            _const_spec((QK_W, LANES)),
            _const_spec((LANES, MLSTM_W)),
            _const_spec((LANES, QK_W)),
            _const_spec((MLSTM_W, MLSTM_W)),
        ],
        out_specs=[x_spec, c_spec, n_spec, m_spec, cv_spec],
        out_shape=[
            jax.ShapeDtypeStruct((Bs, SAMPLE_T, D_MODEL), f32),
            jax.ShapeDtypeStruct((Bs, NH, DK, DV), f32),
            jax.ShapeDtypeStruct((Bs, 1, QK_W), f32),
            jax.ShapeDtypeStruct((Bs, 1, NH), f32),
            jax.ShapeDtypeStruct((Bs, CONV_W - 1, CONV_CH), f32),
        ],
        scratch_shapes=[
            pltpu.VMEM((NSEQ, SUBLANES, D_MODEL), f32),
            pltpu.VMEM((NSEQ, SUBLANES, CONV_CH), f32),
            pltpu.VMEM((NSEQ, 1, LANES), f32),
            pltpu.VMEM((NSEQ * SUBLANES, MLSTM_W), f32),
        ],
        compiler_params=pltpu.CompilerParams(
            dimension_semantics=("arbitrary",), vmem_limit_bytes=VMEM_LIMIT),
        name="mixer_sample",
    )(x, cv, c0, n0, m0, *weights)


def _mixer_weights(g_mix, w_in, b_ig, b_fg, g_mh, conv_w, w_out):
    f32 = jnp.float32
    w_all, wt = _prep_weights(w_in)
    wout = w_out.astype(jnp.bfloat16)
    gmix = g_mix.reshape(1, D_MODEL)
    zeros = jnp.zeros((SUBLANES - NH,), f32)
    gb_col = jnp.concatenate([b_ig, zeros, b_fg, zeros]).reshape(2 * SUBLANES, 1)
    zeros = jnp.zeros((LANES - NH,), f32)
    gb_row = jnp.concatenate([b_ig, zeros, b_fg, zeros]).reshape(1, 2 * LANES)
    gmhc = jnp.broadcast_to(g_mh[:, None], (MLSTM_W, LANES))
    prompt = (gmix, w_all, wt, gb_col, gmhc, conv_w, wout)
    sample = (gmix, w_all, gb_row, g_mh.reshape(1, MLSTM_W), conv_w, wout) + _selectors()
    return prompt, sample


def _ffn_kernel(x_ref, pe_ref, gffn_ref, wg_ref, wu_ref, wd_ref, wple_ref, gple_ref, wpg_ref,
                gfin_ref, y_ref, *, final_norm, n_sub):
    seq_blocks = len(x_ref.shape) == 3
    sub = x_ref.shape[0] // n_sub
    for s in range(n_sub):
        rows = slice(s * sub, (s + 1) * sub)
        if seq_blocks:
            n_tok = x_ref.shape[0] * x_ref.shape[1]
            x = x_ref[...].reshape(n_tok, D_MODEL)
            pe = pe_ref[...].reshape(n_tok, PLE_DIM)
        else:
            x = x_ref[rows, :]
            pe = pe_ref[rows, :]
        f = _rms(x, gffn_ref[...]).astype(jnp.bfloat16)
        for c0 in range(0, D_FF, FF_CHUNK):
            c1 = min(c0 + FF_CHUNK, D_FF)
            gate = _bdot(f, wg_ref[:, c0:c1])
            up = _bdot(f, wu_ref[:, c0:c1])
            hmid = (gate * jax.nn.sigmoid(gate) * up).astype(jnp.bfloat16)
            x = x + _bdot(hmid, wd_ref[c0:c1, :])
        e = _rms(_bdot(pe.astype(jnp.bfloat16), wple_ref[...]), gple_ref[...])
        x = x + jax.nn.sigmoid(_bdot(x.astype(jnp.bfloat16), wpg_ref[...])) * e
        if final_norm:
            x = _rms(x, gfin_ref[...])
        if seq_blocks:
            y_ref[...] = x.reshape(y_ref.shape)
        else:
            y_ref[rows, :] = x


def _ffn(x, pe, weights, TM, final_norm, n_sub=1):
    T = x.shape[0]
    kern = functools.partial(_ffn_kernel, final_norm=final_norm, n_sub=n_sub)
    if x.ndim == 3:
        assert n_sub == 1
        x_spec = pl.BlockSpec((TM,) + x.shape[1:], lambda t: (t, 0, 0))
        pe_spec = pl.BlockSpec((TM,) + pe.shape[1:], lambda t: (t, 0, 0))
    else:
        x_spec = pl.BlockSpec((TM, D_MODEL), lambda t: (t, 0))
        pe_spec = pl.BlockSpec((TM, PLE_DIM), lambda t: (t, 0))
    return pl.pallas_call(
        kern,
        grid=(T // TM,),
        in_specs=[
            x_spec,
            pe_spec,
            _const_spec((1, D_MODEL)),
            _const_spec((D_MODEL, D_FF)),
            _const_spec((D_MODEL, D_FF)),
            _const_spec((D_FF, D_MODEL)),
            _const_spec((PLE_DIM, D_MODEL)),
            _const_spec((1, D_MODEL)),
            _const_spec((D_MODEL, D_MODEL)),
            _const_spec((1, D_MODEL)),
        ],
        out_specs=x_spec,
        out_shape=jax.ShapeDtypeStruct(x.shape, jnp.float32),
        compiler_params=pltpu.CompilerParams(
            dimension_semantics=("arbitrary",), vmem_limit_bytes=VMEM_LIMIT),
        name="ffn",
    )(x, pe, *weights)


def kernel(x_prompt, x_sample, p_prompt, p_sample, state_C, state_n, state_m, state_conv,
           g_mix, w_in, b_ig, b_fg, g_mh, conv_w, w_out, g_ffn, w_gate, w_up, w_down,
           w_ple, g_ple, w_pg, g_final):
    bf16 = jnp.bfloat16
    depth = g_mix.shape[0]
    B, S, _ = x_prompt.shape
    Bs, Ss, _ = x_sample.shape
    assert Ss == SAMPLE_T and S % PROMPT_TS == 0 and Bs % SAMPLE_NSEQ == 0

    xp = x_prompt
    xs = x_sample
    outs = [[] for _ in range(8)]
    for i in range(depth):
        last = i == depth - 1
        fw = (g_ffn[i].reshape(1, D_MODEL), w_gate[i].astype(bf16), w_up[i].astype(bf16),
              w_down[i].astype(bf16), w_ple[i].astype(bf16), g_ple[i].reshape(1, D_MODEL),
              w_pg[i].astype(bf16), g_final.reshape(1, D_MODEL))

        pw, sw = _mixer_weights(g_mix[i], w_in[i], b_ig[i], b_fg[i], g_mh[i], conv_w[i], w_out[i])
        x1p, cp, n_p, mp, cvp = _mixer_prompt(xp, pw, TB=PROMPT_TS, NSUB=PROMPT_NSUB, L=PROMPT_L)
        xp = _ffn(x1p.reshape(B * S, D_MODEL), p_prompt[i].reshape(B * S, PLE_DIM), fw, 1024,
                  last, n_sub=1).reshape(B, S, D_MODEL)

        x1s, cs, n_s, ms, cvs = _mixer_sample(
            xs, state_conv[i], state_C[i], state_n[i].reshape(Bs, 1, QK_W),
            state_m[i].reshape(Bs, 1, NH), sw, SAMPLE_NSEQ)
        xs = _ffn(x1s, p_sample[i], fw, 64, last)

        new = (cp, n_p.reshape(B, NH, DK), mp[:, :NH, 0], cvp[:, SUBLANES - (CONV_W - 1):],
               cs, n_s.reshape(Bs, NH, DK), ms.reshape(Bs, NH), cvs)
        for lst, v in zip(outs, new):
            lst.append(v)

    return (xp, xs) + tuple(jnp.stack(l) for l in outs)
```

```python
import functools

import numpy as np
import jax
import jax.numpy as jnp
from jax import lax
from jax.experimental import pallas as pl
from jax.experimental.pallas import tpu as pltpu

D_MODEL = 1024
NH = 4
DK = 64
DV = 128
MLSTM_W = NH * DV
QK_W = NH * DK
CONV_CH = 512
CONV_W = 3
D_FF = 2816
FF_CHUNK = 512
FFN_TM = 512
PLE_DIM = 256
PROMPT_L = 128
PROMPT_TS = 512
PROMPT_NSUB = 2
SAMPLE_T = 4
SAMPLE_NSEQ = 32
GATE_CAP = 15.0
EPS = 1e-6

LANES = 128
SUBLANES = 8
NEG_BIG = -1e30

VMEM_LIMIT = 56 * 1024 * 1024


def _rms(x, g):
    ms = jnp.mean(x * x, axis=-1, keepdims=True)
    return x * lax.rsqrt(ms + EPS) * g


def _bdot(a, b):
    return jnp.dot(a, b, preferred_element_type=jnp.float32)


def _log_sigmoid(x):
    return jnp.minimum(x, 0.0) - jnp.log1p(jnp.exp(-jnp.abs(x)))


def _const_spec(shape):
    nd = len(shape)
    return pl.BlockSpec(shape, lambda *_: (0,) * nd, pipeline_mode=pl.Buffered(1))


_IG0 = 2 * QK_W + MLSTM_W
_OG_IN = _IG0 + 2 * NH
PROJ_IN = _OG_IN + MLSTM_W + 3 * CONV_CH
_Q0, _K0, _V0, _OG0, _BG0, _CG0, _HC0, _GT0 = 0, 256, 512, 1024, 1536, 2048, 2560, 3072
PROJ_COLS = _GT0 + 2 * LANES
_TQ0, _TV0, _TOG0, _TG0 = 0, 256, 768, 1280
TPROJ_ROWS = _TG0 + 2 * SUBLANES
PREP_TK = 256


def _prep_weights_kernel(w_ref, wall_ref, wt_ref):
    bf16 = jnp.bfloat16
    wf = w_ref[...]
    tk = wf.shape[1]
    g8 = wf[_IG0:_OG_IN]
    row = lax.broadcasted_iota(jnp.int32, (SUBLANES, tk), 0)
    g_in = jnp.where(row < NH, g8, 0.0)
    g_fg = jnp.where(row < NH, pltpu.roll(g8, SUBLANES - NH, 0), 0.0)

    wt_ref[_TQ0:_TV0, :] = wf[_Q0:_K0].astype(bf16)
    wt_ref[_TV0:_TOG0, :] = wf[_V0:_OG0].astype(bf16)
    wt_ref[_TOG0:_TG0, :] = wf[_OG_IN:_OG_IN + MLSTM_W].astype(bf16)
    wt_ref[_TG0:TPROJ_ROWS, :] = jnp.concatenate([g_in, g_fg], axis=0).astype(bf16)

    wall_ref[:, 0:_OG0] = wf[0:_IG0].T.astype(bf16)
    wall_ref[:, _OG0:_GT0] = wf[_OG_IN:PROJ_IN].T.astype(bf16)
    pad = jnp.zeros((LANES - SUBLANES, tk), jnp.float32)
    wall_ref[:, _GT0:_GT0 + LANES] = jnp.concatenate([g_in, pad], axis=0).T.astype(bf16)
    wall_ref[:, _GT0 + LANES:PROJ_COLS] = jnp.concatenate([g_fg, pad], axis=0).T.astype(bf16)


def _prep_weights(w_in):
    bf16 = jnp.bfloat16
    return pl.pallas_call(
        _prep_weights_kernel,
        grid=(D_MODEL // PREP_TK,),
        in_specs=[pl.BlockSpec((PROJ_IN, PREP_TK), lambda i: (0, i))],
        out_specs=[pl.BlockSpec((PREP_TK, PROJ_COLS), lambda i: (i, 0)),
                   pl.BlockSpec((TPROJ_ROWS, PREP_TK), lambda i: (0, i))],
        out_shape=[jax.ShapeDtypeStruct((D_MODEL, PROJ_COLS), bf16),
                   jax.ShapeDtypeStruct((TPROJ_ROWS, D_MODEL), bf16)],
        compiler_params=pltpu.CompilerParams(
            dimension_semantics=("arbitrary",), vmem_limit_bytes=VMEM_LIMIT),
        name="prep_weights",
    )(w_in.T)


ST_ROWS = DV + 2 * SUBLANES


def _lane_scan(x, seg, op, fill):
    pos = lax.broadcasted_iota(jnp.int32, x.shape, 1) & (seg - 1)
    k = 1
    while k < seg:
        x = op(x, jnp.where(pos >= k, pltpu.roll(x, k, 1), fill))
        k *= 2
    return x


def _mixer_prompt_kernel(x_ref, gmix_ref, wn_ref, wt_ref, gbt_ref, gmhc_ref, cw_ref, wout_ref,
                         x1_ref, c_ref, n_ref, m_ref, cvo_ref,
                         qt_s, vt_s, ogt_s, k_s, mixt_s, z_s, ucol_s, wi_s, c2_s, emt_s, wk_s, ast_s, st_s,
                         *, L, TB, NSUB):
    j = pl.program_id(1)

    @pl.when(j == 0)
    def _():
        z_s[0, 0:SUBLANES, :] = jnp.zeros((SUBLANES, CONV_CH), jnp.float32)
        st_s[...] = jnp.zeros(st_s.shape, jnp.float32)
        m_ref[...] = jnp.zeros(m_ref.shape, jnp.float32)

    m_prev = m_ref[0]
    sub = [dict(sb=sb, x_ref=x_ref, gmix_ref=gmix_ref, wn_ref=wn_ref, wt_ref=wt_ref, gbt_ref=gbt_ref,
                gmhc_ref=gmhc_ref, cw_ref=cw_ref, wout_ref=wout_ref, x1_ref=x1_ref, qt_s=qt_s.at[sb],
                vt_s=vt_s.at[sb], ogt_s=ogt_s.at[sb], k_s=k_s.at[sb], mixt_s=mixt_s.at[sb], z_s=z_s,
                ucol_s=ucol_s.at[sb], wi_s=wi_s.at[sb], c2_s=c2_s.at[sb], emt_s=emt_s.at[sb],
                wk_s=wk_s.at[sb], ast_s=ast_s.at[sb], st_s=st_s, L=L, TS=TB) for sb in range(NSUB)]
    for s in sub:
        _prompt_project(s)
    for s in sub:
        m_prev = _prompt_chunks(s, m_prev)
    for s in sub:
        _prompt_output(s)
    m_ref[0] = m_prev
    cvo_ref[0] = z_s[NSUB - 1, TB:TB + SUBLANES, :]
    z_s[0, 0:SUBLANES, :] = z_s[NSUB - 1, TB:TB + SUBLANES, :]

    @pl.when(j == pl.num_programs(1) - 1)
    def _():
        for pr in range(NH // 2):
            state = st_s[pr]
            c_pair = state[0:DV, :].T
            c_ref[0, 2 * pr] = c_pair[0:DK, :]
            c_ref[0, 2 * pr + 1] = c_pair[DK:, :]
            n_ref[0, pr:pr + 1, :] = state[DV:DV + 1, :]


def _prompt_project(s):
    sb, L, TS = s["sb"], s["L"], s["TS"]
    x_ref, gmix_ref, wn_ref, wt_ref, gbt_ref, cw_ref = (
        s["x_ref"], s["gmix_ref"], s["wn_ref"], s["wt_ref"], s["gbt_ref"], s["cw_ref"])
    qt_s, vt_s, ogt_s, k_s, z_s = s["qt_s"], s["vt_s"], s["ogt_s"], s["k_s"], s["z_s"]
    bf16 = jnp.bfloat16

    x = x_ref[0, sb * TS:(sb + 1) * TS, :]
    a = _rms(x, gmix_ref[...]).astype(bf16)
    nt_dims = (((1,), (1,)), ((), ()))

    gt = lax.dot_general(wt_ref[_TG0:TPROJ_ROWS, :], a, nt_dims,
                         preferred_element_type=jnp.float32) + gbt_ref[...]
    ic = GATE_CAP * jnp.tanh(gt[0:SUBLANES] / GATE_CAP)
    lf = _log_sigmoid(gt[SUBLANES:])
    b = _lane_scan(lf, L, jnp.add, 0.0)
    u = ic - b
    m_loc = b + _lane_scan(u, L, jnp.maximum, -jnp.inf)

    k_s[...] = _bdot(a, wn_ref[:, _K0:_V0])
    un = _bdot(a, wn_ref[:, _BG0:_GT0])
    z = un[:, CONV_CH:2 * CONV_CH] * un[:, 2 * CONV_CH:]
    if sb > 0:
        z_s[sb, 0:SUBLANES, :] = z_s[sb - 1, TS:TS + SUBLANES, :]
    z_s[sb, SUBLANES:SUBLANES + TS, :] = z
    yconv = (cw_ref[0:1, :] * z_s[sb, SUBLANES - 2:SUBLANES - 2 + TS, :]
             + cw_ref[1:2, :] * z_s[sb, SUBLANES - 1:SUBLANES - 1 + TS, :]
             + cw_ref[2:3, :] * z)
    yc = (un[:, 0:CONV_CH] * yconv).astype(bf16)

    ut = lax.dot_general(wt_ref[0:_TG0, :], a, nt_dims, preferred_element_type=jnp.float32)
    qt_s[...] = ut[_TQ0:_TV0] * (DK ** -0.5)
    vt_s[...] = ut[_TV0:_TOG0]
    ogt_s[...] = ut[_TOG0:_TG0]
    s.update(b=b, u=u, ic=ic, m_loc=m_loc, yc=yc)


def _prompt_chunks(s, m_prev):
    L, TS = s["L"], s["TS"]
    gmhc_ref, qt_s, vt_s, ogt_s, k_s, mixt_s, ucol_s, wi_s, c2_s, emt_s, wk_s, ast_s, st_s = (
        s["gmhc_ref"], s["qt_s"], s["vt_s"], s["ogt_s"], s["k_s"], s["mixt_s"], s["ucol_s"], s["wi_s"],
        s["c2_s"], s["emt_s"], s["wk_s"], s["ast_s"], s["st_s"])
    b, u, ic, m_loc = s["b"], s["u"], s["ic"], s["m_loc"]
    nc = TS // L
    bf16 = jnp.bfloat16

    for c in range(nc):
        sl = slice(c * L, (c + 1) * L)
        bc, mlc = b[:, sl], m_loc[:, sl]
        b_last = jnp.broadcast_to(bc[:, L - 1:L], bc.shape)
        m_new = jnp.maximum(b_last + m_prev, jnp.broadcast_to(mlc[:, L - 1:L], bc.shape))
        g = bc + m_prev
        mt = jnp.maximum(g, mlc)
        wi_s[:, sl] = jnp.exp(g - mt)
        c2_s[:, sl] = mt - bc
        emt_s[:, sl] = jnp.exp(-mt)
        wk_s[:, sl] = jnp.exp(b_last - bc + ic[:, sl] - m_new)
        ast_s[c] = jnp.exp(b_last + m_prev - m_new)
        upad = jnp.concatenate([u[:, sl], jnp.zeros((L - SUBLANES, L), jnp.float32)], axis=0)
        ucol_s[c * L:(c + 1) * L, :] = upad.T
        m_prev = m_new

    s_i = lax.broadcasted_iota(jnp.int32, (L, L), 0)
    t_i = lax.broadcasted_iota(jnp.int32, (L, L), 1)
    causal = s_i <= t_i
    low_half = lax.broadcasted_iota(jnp.int32, (L, LANES), 1) < DK
    zeros_q = jnp.zeros((DK, L), jnp.float32)
    zeros_p = jnp.zeros((L, L), bf16)

    def pair_row(ref, h0, rows):
        return jnp.concatenate([ref[h0:h0 + 1, rows], ref[h0 + 1:h0 + 2, rows]], axis=1)

    for c in range(nc):
        rows = slice(c * L, (c + 1) * L)
        for pr in range(NH // 2):
            h0 = 2 * pr
            hv0 = slice(h0 * DV, (h0 + 1) * DV)
            hv1 = slice((h0 + 1) * DV, (h0 + 2) * DV)
            kp = k_s[rows, pr * LANES:(pr + 1) * LANES]
            q0 = qt_s[h0 * DK:(h0 + 1) * DK, rows]
            q1 = qt_s[(h0 + 1) * DK:(h0 + 2) * DK, rows]
            qbd = jnp.concatenate([jnp.concatenate([q0, zeros_q], axis=1),
                                   jnp.concatenate([zeros_q, q1], axis=1)], axis=0).astype(bf16)
            st = _bdot(kp.astype(bf16), qbd)
            arg = jnp.concatenate(
                [jnp.where(causal, ucol_s[rows, h:h + 1] - c2_s[h:h + 1, rows], -jnp.inf)
                 for h in (h0, h0 + 1)], axis=1)
            pt = st * jnp.exp(arg)
            rs = jnp.sum(pt, axis=0, keepdims=True)
            ptb = pt.astype(bf16)
            pbd = jnp.concatenate([jnp.concatenate([ptb[:, 0:L], zeros_p], axis=1),
                                   jnp.concatenate([zeros_p, ptb[:, L:]], axis=1)], axis=0)
            vt = jnp.concatenate([vt_s[hv0, rows], vt_s[hv1, rows]], axis=1)
            state = st_s[pr]
            sq = _bdot(state.astype(bf16), qbd) * pair_row(wi_s, h0, rows)
            num = _bdot(vt.astype(bf16), pbd) + sq[0:DV]
            den = sq[DV:DV + 1] + rs
            hh = num / jnp.maximum(jnp.abs(den), pair_row(emt_s, h0, rows))
            hn = hh * lax.rsqrt(jnp.mean(hh * hh, axis=0, keepdims=True) + EPS)
            mixt_s[hv0, rows] = jax.nn.sigmoid(ogt_s[hv0, rows]) * (hn[:, 0:L] * gmhc_ref[hv0, :])
            mixt_s[hv1, rows] = jax.nn.sigmoid(ogt_s[hv1, rows]) * (hn[:, L:] * gmhc_ref[hv1, :])
            wkr = pair_row(wk_s, h0, rows)
            vw = jnp.concatenate([vt * wkr, jnp.broadcast_to(wkr, (2 * SUBLANES, 2 * L))], axis=0)
            km = jnp.concatenate([jnp.where(low_half, kp, 0.0), jnp.where(low_half, 0.0, kp)], axis=0)
            decay = jnp.where(low_half[0:1, :], ast_s[c][h0:h0 + 1, :], ast_s[c][h0 + 1:h0 + 2, :])
            st_s[pr] = decay * state + _bdot(vw.astype(bf16), km.astype(bf16))
    return m_prev


def _prompt_output(s):
    sb, TS = s["sb"], s["TS"]
    x_ref, wout_ref, x1_ref, mixt_s = s["x_ref"], s["wout_ref"], s["x1_ref"], s["mixt_s"]
    tok = slice(sb * TS, (sb + 1) * TS)
    out = x_ref[0, tok, :] + lax.dot_general(
        mixt_s[...].astype(jnp.bfloat16), wout_ref[0:MLSTM_W, :], (((0,), (0,)), ((), ())),
        preferred_element_type=jnp.float32)
    x1_ref[0, tok, :] = out + _bdot(s["yc"], wout_ref[MLSTM_W:, :])


def _mixer_prompt(x, weights, TB, NSUB, L):
    B, S, _ = x.shape
    TS = TB * NSUB
    nb = S // TS
    f32 = jnp.float32
    kern = functools.partial(_mixer_prompt_kernel, L=L, TB=TB, NSUB=NSUB)
    return pl.pallas_call(
        kern,
        grid=(B, nb),
        in_specs=[
            pl.BlockSpec((1, TS, D_MODEL), lambda b, j: (b, j, 0)),
            _const_spec((1, D_MODEL)),
            _const_spec((D_MODEL, PROJ_COLS)),
            _const_spec((TPROJ_ROWS, D_MODEL)),
            _const_spec((2 * SUBLANES, 1)),
            _const_spec((MLSTM_W, LANES)),
            _const_spec((CONV_W, CONV_CH)),
            _const_spec((D_MODEL, D_MODEL)),
        ],
        out_specs=[
            pl.BlockSpec((1, TS, D_MODEL), lambda b, j: (b, j, 0)),
            pl.BlockSpec((1, NH, DK, DV), lambda b, j: (b, 0, 0, 0)),
            pl.BlockSpec((1, NH // 2, 2 * DK), lambda b, j: (b, 0, 0)),
            pl.BlockSpec((1, SUBLANES, LANES), lambda b, j: (b, 0, 0)),
            pl.BlockSpec((1, SUBLANES, CONV_CH), lambda b, j: (b, 0, 0)),
        ],
        out_shape=[
            jax.ShapeDtypeStruct((B, S, D_MODEL), f32),
            jax.ShapeDtypeStruct((B, NH, DK, DV), f32),
            jax.ShapeDtypeStruct((B, NH // 2, 2 * DK), f32),
            jax.ShapeDtypeStruct((B, SUBLANES, LANES), f32),
            jax.ShapeDtypeStruct((B, SUBLANES, CONV_CH), f32),
        ],
        scratch_shapes=[
            pltpu.VMEM((NSUB, QK_W, TB), f32),
            pltpu.VMEM((NSUB, MLSTM_W, TB), f32),
            pltpu.VMEM((NSUB, MLSTM_W, TB), f32),
            pltpu.VMEM((NSUB, TB, QK_W), f32),
            pltpu.VMEM((NSUB, MLSTM_W, TB), f32),
            pltpu.VMEM((NSUB, TB + SUBLANES, CONV_CH), f32),
            pltpu.VMEM((NSUB, TB, LANES), f32),
            pltpu.VMEM((NSUB, SUBLANES, TB), f32),
            pltpu.VMEM((NSUB, SUBLANES, TB), f32),
            pltpu.VMEM((NSUB, SUBLANES, TB), f32),
            pltpu.VMEM((NSUB, SUBLANES, TB), f32),
            pltpu.VMEM((NSUB, TB // L, SUBLANES, LANES), f32),
            pltpu.VMEM((NH // 2, ST_ROWS, LANES), f32),
        ],
        compiler_params=pltpu.CompilerParams(
            dimension_semantics=("arbitrary", "arbitrary"), vmem_limit_bytes=VMEM_LIMIT),
        name="mixer_prompt",
    )(x, *weights)


def _row_scan(x, seg, op, fill):
    pos = lax.broadcasted_iota(jnp.int32, x.shape, 0) & (seg - 1)
    k = 1
    while k < seg:
        x = op(x, jnp.where(pos >= k, pltpu.roll(x, k, 0), fill))
        k *= 2
    return x


def _split_dot(x, sel, parts):
    acc = None
    rem = x
    for p in range(parts):
        hi = rem.astype(jnp.bfloat16)
        d = _bdot(hi, sel)
        acc = d if acc is None else acc + d
        if p + 1 < parts:
            rem = rem - hi.astype(jnp.float32)
    return acc


def _selectors():
    h = np.arange(NH)
    seg_qk = np.zeros((QK_W, LANES), np.float32)
    seg_qk[np.arange(QK_W), np.arange(QK_W) // DK] = 1.0
    exp_v = np.zeros((LANES, MLSTM_W), np.float32)
    exp_k = np.zeros((LANES, QK_W), np.float32)
    for i in h:
        exp_v[i, i * DV:(i + 1) * DV] = 1.0
        exp_k[i, i * DK:(i + 1) * DK] = 1.0
    mean_v = np.kron(np.eye(NH, dtype=np.float32), np.full((DV, DV), 1.0 / DV, np.float32))
    return tuple(jnp.asarray(m, jnp.bfloat16) for m in (seg_qk, exp_v, exp_k, mean_v))


def _mixer_sample_kernel(x_ref, cv_ref, c0_ref, n0_ref, m0_ref,
                         gmix_ref, win_ref, gb_ref, gmh_ref, cw_ref, wout_ref,
                         segqk_ref, expv_ref, expk_ref, meanv_ref,
                         x1_ref, cst_ref, nst_ref, mst_ref, cvo_ref,
                         xp_s, z_s, mp_s, qc_s, *, NSEQ):
    R = SUBLANES
    TS = NSEQ * R
    bf16 = jnp.bfloat16
    f32 = jnp.float32

    xp_s[:, SAMPLE_T:, :] = jnp.zeros((NSEQ, R - SAMPLE_T, D_MODEL), f32)
    xp_s[:, 0:SAMPLE_T, :] = x_ref[...]
    x = xp_s[...].reshape(TS, D_MODEL)
    a = _rms(x, gmix_ref[...]).astype(bf16)
    u = _bdot(a, win_ref[...])

    pos = lax.broadcasted_iota(jnp.int32, (TS, 1), 0) & (R - 1)
    real = pos < SAMPLE_T

    z = u[:, _CG0:_HC0] * u[:, _HC0:_GT0]
    z3 = z.reshape(NSEQ, R, CONV_CH)
    z_s[...] = z3
    z_s[:, R - (CONV_W - 1):, :] = cv_ref[...]
    zf = z_s[...].reshape(TS, CONV_CH)
    zm1 = jnp.where(pos >= 1, pltpu.roll(z, 1, 0), pltpu.roll(zf, TS - (R - 1), 0))
    zm2 = jnp.where(pos >= 2, pltpu.roll(z, 2, 0), pltpu.roll(zf, TS - (R - 2), 0))
    yc = u[:, _BG0:_CG0] * (cw_ref[0:1, :] * zm2 + cw_ref[1:2, :] * zm1 + cw_ref[2:3, :] * z)
    cvo_ref[...] = z3[:, SAMPLE_T - (CONV_W - 1):SAMPLE_T, :]

    gates = u[:, _GT0:PROJ_COLS] + gb_ref[...]
    ic = jnp.where(real, GATE_CAP * jnp.tanh(gates[:, :LANES] / GATE_CAP), NEG_BIG)
    lf = jnp.where(real, _log_sigmoid(gates[:, LANES:]), 0.0)
    b = _row_scan(lf, R, jnp.add, 0.0)
    uu = ic - b
    m_loc = b + _row_scan(uu, R, jnp.maximum, -jnp.inf)

    mp_s[...] = jnp.zeros(mp_s.shape, f32)
    mp_s[:, :, 0:NH] = m0_ref[...]
    m_prev = mp_s[...]
    b3 = b.reshape(NSEQ, R, LANES)
    ml3 = m_loc.reshape(NSEQ, R, LANES)
    b_last = b3[:, R - 1:R, :]
    m_new = jnp.maximum(b_last + m_prev, ml3[:, R - 1:R, :])
    g3 = b3 + m_prev
    mt3 = jnp.maximum(g3, ml3)
    wi = jnp.exp(g3 - mt3).reshape(TS, LANES)
    c2 = (mt3 - b3).reshape(TS, LANES)
    emt = jnp.exp(-mt3).reshape(TS, LANES)
    wk = jnp.exp(b_last - b3 + ic.reshape(NSEQ, R, LANES) - m_new).reshape(TS, LANES)
    a_st = jnp.broadcast_to(jnp.exp(b_last + m_prev - m_new), (NSEQ, R, LANES)).reshape(TS, LANES)
    mst_ref[...] = m_new[:, :, 0:NH]

    q = u[:, _Q0:_K0] * (DK ** -0.5)
    k = u[:, _K0:_V0]
    v = u[:, _V0:_OG0]
    rs = jnp.zeros((TS, LANES), f32)
    num = jnp.zeros((TS, MLSTM_W), f32)
    for d in range(SAMPLE_T):
        kd = k if d == 0 else pltpu.roll(k, d, 0)
        vd = v if d == 0 else pltpu.roll(v, d, 0)
        ud = uu if d == 0 else pltpu.roll(uu, d, 0)
        p = _split_dot(q * kd, segqk_ref[...], 2) * jnp.exp(ud - c2)
        rs = rs + p
        num = num + _split_dot(p, expv_ref[...], 2) * vd

    n0x = jnp.broadcast_to(n0_ref[...], (NSEQ, R, QK_W)).reshape(TS, QK_W)
    qn = _split_dot(q * n0x, segqk_ref[...], 2)
    den = wi * qn + rs
    rden = 1.0 / jnp.maximum(jnp.abs(den), emt)
    kw = k * _split_dot(wk, expk_ref[...], 2)
    ax = _split_dot(a_st, expv_ref[...], 3)
    lane_q = lax.broadcasted_iota(jnp.int32, (R, QK_W), 1)
    for i in range(NSEQ):
        rows = slice(i * R, (i + 1) * R)
        qi = q[rows, :]
        lhs = jnp.concatenate(
            [jnp.where((lane_q >= h * DK) & (lane_q < (h + 1) * DK), qi, 0.0) for h in range(NH)], axis=0)
        c0 = c0_ref[i]
        qc = _bdot(lhs.astype(bf16), c0.reshape(QK_W, DV).astype(bf16))
        qc_s[rows, :] = jnp.concatenate([qc[h * R:(h + 1) * R, :] for h in range(NH)], axis=1)
        kwi = kw[rows, :].astype(bf16)
        vi = v[rows, :].astype(bf16)
        for h in range(NH):
            dc = lax.dot_general(kwi[:, h * DK:(h + 1) * DK], vi[:, h * DV:(h + 1) * DV],
                                 (((0,), (0,)), ((), ())), preferred_element_type=f32)
            cst_ref[i, h] = ax[i * R:i * R + 1, h * DV:(h + 1) * DV] * c0[h] + dc

    hh = (_split_dot(wi, expv_ref[...], 2) * qc_s[...] + num) * _split_dot(rden, expv_ref[...], 2)
    ms = _split_dot(hh * hh, meanv_ref[...], 2)
    hm = jax.nn.sigmoid(u[:, _OG0:_BG0]) * (hh * lax.rsqrt(ms + EPS) * gmh_ref[...])
    mix = jnp.concatenate([hm, yc], axis=1).astype(bf16)
    out = x + _bdot(mix, wout_ref[...])
    x1_ref[...] = out.reshape(NSEQ, R, D_MODEL)[:, 0:SAMPLE_T, :]

    a_k = _split_dot(a_st, expk_ref[...], 3).reshape(NSEQ, R, QK_W)[:, 0:1, :]
    nst_ref[...] = a_k * n0_ref[...] + jnp.sum(kw.reshape(NSEQ, R, QK_W), axis=1, keepdims=True)


def _mixer_sample(x, cv, c0, n0, m0, weights, NSEQ):
    Bs = x.shape[0]
    f32 = jnp.float32
    kern = functools.partial(_mixer_sample_kernel, NSEQ=NSEQ)
    x_spec = pl.BlockSpec((NSEQ, SAMPLE_T, D_MODEL), lambda t: (t, 0, 0))
    cv_spec = pl.BlockSpec((NSEQ, CONV_W - 1, CONV_CH), lambda t: (t, 0, 0))
    c_spec = pl.BlockSpec((NSEQ, NH, DK, DV), lambda t: (t, 0, 0, 0))
    n_spec = pl.BlockSpec((NSEQ, 1, QK_W), lambda t: (t, 0, 0))
    m_spec = pl.BlockSpec((NSEQ, 1, NH), lambda t: (t, 0, 0))
    return pl.pallas_call(
        kern,
        grid=(Bs // NSEQ,),
        in_specs=[
            x_spec, cv_spec, c_spec, n_spec, m_spec,
            _const_spec((1, D_MODEL)),
            _const_spec((D_MODEL, PROJ_COLS)),
            _const_spec((1, 2 * LANES)),
            _const_spec((1, MLSTM_W)),
            _const_spec((CONV_W, CONV_CH)),
            _const_spec((D_MODEL, D_MODEL)),
            _const_spec((QK_W, LANES)),
            _const_spec((LANES, MLSTM_W)),
            _const_spec((LANES, QK_W)),
            _const_spec((MLSTM_W, MLSTM_W)),
        ],
        out_specs=[x_spec, c_spec, n_spec, m_spec, cv_spec],
        out_shape=[
            jax.ShapeDtypeStruct((Bs, SAMPLE_T, D_MODEL), f32),
            jax.ShapeDtypeStruct((Bs, NH, DK, DV), f32),
            jax.ShapeDtypeStruct((Bs, 1, QK_W), f32),
            jax.ShapeDtypeStruct((Bs, 1, NH), f32),
            jax.ShapeDtypeStruct((Bs, CONV_W - 1, CONV_CH), f32),
        ],
        scratch_shapes=[
            pltpu.VMEM((NSEQ, SUBLANES, D_MODEL), f32),
            pltpu.VMEM((NSEQ, SUBLANES, CONV_CH), f32),
            pltpu.VMEM((NSEQ, 1, LANES), f32),
            pltpu.VMEM((NSEQ * SUBLANES, MLSTM_W), f32),
        ],
        compiler_params=pltpu.CompilerParams(
            dimension_semantics=("arbitrary",), vmem_limit_bytes=VMEM_LIMIT),
        name="mixer_sample",
    )(x, cv, c0, n0, m0, *weights)


def _mixer_weights(g_mix, w_in, b_ig, b_fg, g_mh, conv_w, w_out):
    f32 = jnp.float32
    w_all, wt = _prep_weights(w_in)
    wout = w_out.astype(jnp.bfloat16)
    gmix = g_mix.reshape(1, D_MODEL)
    zeros = jnp.zeros((SUBLANES - NH,), f32)
    gb_col = jnp.concatenate([b_ig, zeros, b_fg, zeros]).reshape(2 * SUBLANES, 1)
    zeros = jnp.zeros((LANES - NH,), f32)
    gb_row = jnp.concatenate([b_ig, zeros, b_fg, zeros]).reshape(1, 2 * LANES)
    gmhc = jnp.broadcast_to(g_mh[:, None], (MLSTM_W, LANES))
    prompt = (gmix, w_all, wt, gb_col, gmhc, conv_w, wout)
    sample = (gmix, w_all, gb_row, g_mh.reshape(1, MLSTM_W), conv_w, wout) + _selectors()
    return prompt, sample


FF_STREAM = 256
_N_FF_PIECES = D_FF // FF_STREAM
_N_PIECES = _N_FF_PIECES + PLE_DIM // FF_STREAM + D_MODEL // FF_STREAM


def _ffn_rows(x, pe, gffn_ref, gple_ref, gfin_ref, wg_s, wu_s, wd_s, wple_s, wpg_s, *, chunk, final_norm,
              before_chunk=None, before_tail=None):
    bf16 = jnp.bfloat16
    f = _rms(x, gffn_ref[...]).astype(bf16)
    for c0 in range(0, D_FF, chunk):
        c1 = min(c0 + chunk, D_FF)
        if before_chunk is not None:
            before_chunk(c0 // chunk)
        gate = _bdot(f, wg_s[:, c0:c1])
        up = _bdot(f, wu_s[:, c0:c1])
        hmid = (gate * jax.nn.sigmoid(gate) * up).astype(bf16)
        x = x + _bdot(hmid, wd_s[c0:c1, :])
    if before_tail is not None:
        before_tail()
    e = _rms(_bdot(pe.astype(bf16), wple_s[...]), gple_ref[...])
    x = x + jax.nn.sigmoid(_bdot(x.astype(bf16), wpg_s[...])) * e
    if final_norm:
        x = _rms(x, gfin_ref[...])
    return x


def _ffn_stream_kernel(xp_ref, pp_ref, xs_ref, ps_ref, gffn_ref, gple_ref, gfin_ref,
                       wg_hbm, wu_hbm, wd_hbm, wple_hbm, wpg_hbm,
                       yp_ref, ys_ref,
                       wg_s, wu_s, wd_s, wple_s, wpg_s, stg_g, stg_u, stg_d, sem, *, final_norm):
    t = pl.program_id(0)
    bf16 = jnp.bfloat16
    weights = (wg_s, wu_s, wd_s, wple_s, wpg_s)

    def copies(k):
        slot = k % 2
        if k < _N_FF_PIECES:
            cols = pl.ds(k * FF_STREAM, FF_STREAM)
            return [pltpu.make_async_copy(wg_hbm.at[:, cols], stg_g.at[slot], sem.at[0, slot]),
                    pltpu.make_async_copy(wu_hbm.at[:, cols], stg_u.at[slot], sem.at[1, slot]),
                    pltpu.make_async_copy(wd_hbm.at[cols, :], stg_d.at[slot], sem.at[2, slot])]
        if k == _N_FF_PIECES:
            return [pltpu.make_async_copy(wple_hbm, stg_d.at[slot], sem.at[2, slot])]
        rows = pl.ds((k - _N_FF_PIECES - 1) * FF_STREAM, FF_STREAM)
        return [pltpu.make_async_copy(wpg_hbm.at[rows, :], stg_d.at[slot], sem.at[2, slot])]

    def land(k):
        if k + 1 < _N_PIECES:
            for cp in copies(k + 1):
                cp.start()
        for cp in copies(k):
            cp.wait()
        slot = k % 2
        if k < _N_FF_PIECES:
            cols = slice(k * FF_STREAM, (k + 1) * FF_STREAM)
            wg_s[:, cols] = stg_g[slot].astype(bf16)
            wu_s[:, cols] = stg_u[slot].astype(bf16)
            wd_s[cols, :] = stg_d[slot].astype(bf16)
        elif k == _N_FF_PIECES:
            wple_s[...] = stg_d[slot].astype(bf16)
        else:
            r0 = (k - _N_FF_PIECES - 1) * FF_STREAM
            wpg_s[r0:r0 + FF_STREAM, :] = stg_d[slot].astype(bf16)

    @pl.when(t == 0)
    def _():
        for cp in copies(0):
            cp.start()
        n_tok = xs_ref.shape[0] * xs_ref.shape[1]

        def tail():
            for k in range(_N_FF_PIECES, _N_PIECES):
                land(k)

        y = _ffn_rows(xs_ref[...].reshape(n_tok, D_MODEL), ps_ref[...].reshape(n_tok, PLE_DIM),
                      gffn_ref, gple_ref, gfin_ref, *weights, chunk=FF_STREAM, final_norm=final_norm,
                      before_chunk=land, before_tail=tail)
        ys_ref[...] = y.reshape(ys_ref.shape)

    @pl.when(t > 0)
    def _():
        yp_ref[...] = _ffn_rows(xp_ref[...], pp_ref[...], gffn_ref, gple_ref, gfin_ref, *weights,
                                chunk=FF_CHUNK, final_norm=final_norm)


def _ffn_stream(xp, pp, xs, ps, g_ffn, g_ple, g_final, w_gate, w_up, w_down, w_ple, w_pg, TM, final_norm):
    T = xp.shape[0]
    f32, bf16 = jnp.float32, jnp.bfloat16
    kern = functools.partial(_ffn_stream_kernel, final_norm=final_norm)
    row_map = lambda t: (jnp.maximum(t - 1, 0), 0)
    hbm = pl.BlockSpec(memory_space=pl.ANY)
    return pl.pallas_call(
        kern,
        grid=(T // TM + 1,),
        in_specs=[
            pl.BlockSpec((TM, D_MODEL), row_map),
            pl.BlockSpec((TM, PLE_DIM), row_map),
            _const_spec(xs.shape),
            _const_spec(ps.shape),
            _const_spec((1, D_MODEL)),
            _const_spec((1, D_MODEL)),
            _const_spec((1, D_MODEL)),
            hbm, hbm, hbm, hbm, hbm,
        ],
        out_specs=[pl.BlockSpec((TM, D_MODEL), row_map),
                   pl.BlockSpec(xs.shape, lambda t: (0, 0, 0))],
        out_shape=[jax.ShapeDtypeStruct(xp.shape, f32), jax.ShapeDtypeStruct(xs.shape, f32)],
        scratch_shapes=[
            pltpu.VMEM((D_MODEL, D_FF), bf16),
            pltpu.VMEM((D_MODEL, D_FF), bf16),
            pltpu.VMEM((D_FF, D_MODEL), bf16),
            pltpu.VMEM((PLE_DIM, D_MODEL), bf16),
            pltpu.VMEM((D_MODEL, D_MODEL), bf16),
            pltpu.VMEM((2, D_MODEL, FF_STREAM), f32),
            pltpu.VMEM((2, D_MODEL, FF_STREAM), f32),
            pltpu.VMEM((2, FF_STREAM, D_MODEL), f32),
            pltpu.SemaphoreType.DMA((3, 2)),
        ],
        compiler_params=pltpu.CompilerParams(
            dimension_semantics=("arbitrary",), vmem_limit_bytes=VMEM_LIMIT),
        name="ffn",
    )(xp, pp, xs, ps, g_ffn.reshape(1, D_MODEL), g_ple.reshape(1, D_MODEL), g_final.reshape(1, D_MODEL),
      w_gate, w_up, w_down, w_ple, w_pg)


def kernel(x_prompt, x_sample, p_prompt, p_sample, state_C, state_n, state_m, state_conv,
           g_mix, w_in, b_ig, b_fg, g_mh, conv_w, w_out, g_ffn, w_gate, w_up, w_down,
           w_ple, g_ple, w_pg, g_final):
    bf16 = jnp.bfloat16
    depth = g_mix.shape[0]
    B, S, _ = x_prompt.shape
    Bs, Ss, _ = x_sample.shape
    assert Ss == SAMPLE_T and S % PROMPT_TS == 0 and Bs % SAMPLE_NSEQ == 0

    xp = x_prompt
    xs = x_sample
    outs = [[] for _ in range(8)]
    for i in range(depth):
        last = i == depth - 1
        pw, sw = _mixer_weights(g_mix[i], w_in[i], b_ig[i], b_fg[i], g_mh[i], conv_w[i], w_out[i])
        x1p, cp, n_p, mp, cvp = _mixer_prompt(xp, pw, TB=PROMPT_TS, NSUB=PROMPT_NSUB, L=PROMPT_L)
        x1s, cs, n_s, ms, cvs = _mixer_sample(
            xs, state_conv[i], state_C[i], state_n[i].reshape(Bs, 1, QK_W),
            state_m[i].reshape(Bs, 1, NH), sw, SAMPLE_NSEQ)
        xp, xs = _ffn_stream(x1p.reshape(B * S, D_MODEL), p_prompt[i].reshape(B * S, PLE_DIM), x1s,
                             p_sample[i], g_ffn[i], g_ple[i], g_final, w_gate[i], w_up[i], w_down[i],
                             w_ple[i], w_pg[i], FFN_TM, last)
        xp = xp.reshape(B, S, D_MODEL)

        new = (cp, n_p.reshape(B, NH, DK), mp[:, :NH, 0], cvp[:, SUBLANES - (CONV_W - 1):],
               cs, n_s.reshape(Bs, NH, DK), ms.reshape(Bs, NH), cvs)
        for lst, v in zip(outs, new):
            lst.append(v)

    return (xp, xs) + tuple(jnp.stack(l) for l in outs)
```

```python
import functools

import numpy as np
import jax
import jax.numpy as jnp
from jax import lax
from jax.experimental import pallas as pl
from jax.experimental.pallas import tpu as pltpu

D_MODEL = 1024
NH = 4
DK = 64
DV = 128
MLSTM_W = NH * DV
QK_W = NH * DK
CONV_CH = 512
CONV_W = 3
D_FF = 2816
FF_CHUNK = 512
FFN_TM = 1024
PLE_DIM = 256
PROMPT_L = 128
PROMPT_TS = 512
PROMPT_NSUB = 2
SAMPLE_T = 4
SAMPLE_NSEQ = 32
GATE_CAP = 15.0
EPS = 1e-6

LANES = 128
SUBLANES = 8
NEG_BIG = -1e30

VMEM_LIMIT = 56 * 1024 * 1024
FFN_VMEM_LIMIT = 61 * 1024 * 1024


def _rms(x, g):
    ms = jnp.mean(x * x, axis=-1, keepdims=True)
    return x * lax.rsqrt(ms + EPS) * g


def _bdot(a, b):
    return jnp.dot(a, b, preferred_element_type=jnp.float32)


def _log_sigmoid(x):
    return jnp.minimum(x, 0.0) - jnp.log1p(jnp.exp(-jnp.abs(x)))


def _const_spec(shape):
    nd = len(shape)
    return pl.BlockSpec(shape, lambda *_: (0,) * nd, pipeline_mode=pl.Buffered(1))


_IG0 = 2 * QK_W + MLSTM_W
_OG_IN = _IG0 + 2 * NH
PROJ_IN = _OG_IN + MLSTM_W + 3 * CONV_CH
_Q0, _K0, _V0, _OG0, _BG0, _CG0, _HC0, _GT0 = 0, 256, 512, 1024, 1536, 2048, 2560, 3072
PROJ_COLS = _GT0 + 2 * LANES
_TQ0, _TV0, _TOG0, _TG0 = 0, 256, 768, 1280
TPROJ_ROWS = _TG0 + 2 * SUBLANES
PREP_TK = 256


def _prep_weights_kernel(w_ref, wall_ref, wt_ref):
    bf16 = jnp.bfloat16
    wf = w_ref[...]
    tk = wf.shape[1]
    g8 = wf[_IG0:_OG_IN]
    row = lax.broadcasted_iota(jnp.int32, (SUBLANES, tk), 0)
    g_in = jnp.where(row < NH, g8, 0.0)
    g_fg = jnp.where(row < NH, pltpu.roll(g8, SUBLANES - NH, 0), 0.0)

    wt_ref[_TQ0:_TV0, :] = wf[_Q0:_K0].astype(bf16)
    wt_ref[_TV0:_TOG0, :] = wf[_V0:_OG0].astype(bf16)
    wt_ref[_TOG0:_TG0, :] = wf[_OG_IN:_OG_IN + MLSTM_W].astype(bf16)
    wt_ref[_TG0:TPROJ_ROWS, :] = jnp.concatenate([g_in, g_fg], axis=0).astype(bf16)

    wall_ref[:, 0:_OG0] = wf[0:_IG0].T.astype(bf16)
    wall_ref[:, _OG0:_GT0] = wf[_OG_IN:PROJ_IN].T.astype(bf16)
    pad = jnp.zeros((LANES - SUBLANES, tk), jnp.float32)
    wall_ref[:, _GT0:_GT0 + LANES] = jnp.concatenate([g_in, pad], axis=0).T.astype(bf16)
    wall_ref[:, _GT0 + LANES:PROJ_COLS] = jnp.concatenate([g_fg, pad], axis=0).T.astype(bf16)


def _prep_weights(w_in):
    bf16 = jnp.bfloat16
    return pl.pallas_call(
        _prep_weights_kernel,
        grid=(D_MODEL // PREP_TK,),
        in_specs=[pl.BlockSpec((PROJ_IN, PREP_TK), lambda i: (0, i))],
        out_specs=[pl.BlockSpec((PREP_TK, PROJ_COLS), lambda i: (i, 0)),
                   pl.BlockSpec((TPROJ_ROWS, PREP_TK), lambda i: (0, i))],
        out_shape=[jax.ShapeDtypeStruct((D_MODEL, PROJ_COLS), bf16),
                   jax.ShapeDtypeStruct((TPROJ_ROWS, D_MODEL), bf16)],
        compiler_params=pltpu.CompilerParams(
            dimension_semantics=("arbitrary",), vmem_limit_bytes=VMEM_LIMIT),
        name="prep_weights",
    )(w_in.T)


ST_ROWS = DV + 2 * SUBLANES


def _lane_scan(x, seg, op, fill):
    pos = lax.broadcasted_iota(jnp.int32, x.shape, 1) & (seg - 1)
    k = 1
    while k < seg:
        x = op(x, jnp.where(pos >= k, pltpu.roll(x, k, 1), fill))
        k *= 2
    return x


def _mixer_prompt_kernel(x_ref, gmix_ref, wn_ref, wt_ref, gbt_ref, gmhc_ref, cw_ref, wout_ref,
                         x1_ref, c_ref, n_ref, m_ref, cvo_ref,
                         qt_s, vt_s, ogt_s, k_s, mixt_s, z_s, ucol_s, wi_s, c2_s, emt_s, wk_s, ast_s, st_s,
                         *, L, TB, NSUB):
    j = pl.program_id(1)

    @pl.when(j == 0)
    def _():
        z_s[0, 0:SUBLANES, :] = jnp.zeros((SUBLANES, CONV_CH), jnp.float32)
        st_s[...] = jnp.zeros(st_s.shape, jnp.float32)
        m_ref[...] = jnp.zeros(m_ref.shape, jnp.float32)

    m_prev = m_ref[0]
    sub = [dict(sb=sb, x_ref=x_ref, gmix_ref=gmix_ref, wn_ref=wn_ref, wt_ref=wt_ref, gbt_ref=gbt_ref,
                gmhc_ref=gmhc_ref, cw_ref=cw_ref, wout_ref=wout_ref, x1_ref=x1_ref, qt_s=qt_s.at[sb],
                vt_s=vt_s.at[sb], ogt_s=ogt_s.at[sb], k_s=k_s.at[sb], mixt_s=mixt_s.at[sb], z_s=z_s,
                ucol_s=ucol_s.at[sb], wi_s=wi_s.at[sb], c2_s=c2_s.at[sb], emt_s=emt_s.at[sb],
                wk_s=wk_s.at[sb], ast_s=ast_s.at[sb], st_s=st_s, L=L, TS=TB) for sb in range(NSUB)]
    for s in sub:
        _prompt_project(s)
    for s in sub:
        m_prev = _prompt_chunks(s, m_prev)
    for s in sub:
        _prompt_output(s)
    m_ref[0] = m_prev
    cvo_ref[0] = z_s[NSUB - 1, TB:TB + SUBLANES, :]
    z_s[0, 0:SUBLANES, :] = z_s[NSUB - 1, TB:TB + SUBLANES, :]

    @pl.when(j == pl.num_programs(1) - 1)
    def _():
        for pr in range(NH // 2):
            state = st_s[pr]
            c_pair = state[0:DV, :].T
            c_ref[0, 2 * pr] = c_pair[0:DK, :]
            c_ref[0, 2 * pr + 1] = c_pair[DK:, :]
            n_ref[0, pr:pr + 1, :] = state[DV:DV + 1, :]


def _prompt_project(s):
    sb, L, TS = s["sb"], s["L"], s["TS"]
    x_ref, gmix_ref, wn_ref, wt_ref, gbt_ref, cw_ref = (
        s["x_ref"], s["gmix_ref"], s["wn_ref"], s["wt_ref"], s["gbt_ref"], s["cw_ref"])
    qt_s, vt_s, ogt_s, k_s, z_s = s["qt_s"], s["vt_s"], s["ogt_s"], s["k_s"], s["z_s"]
    bf16 = jnp.bfloat16

    x = x_ref[0, sb * TS:(sb + 1) * TS, :]
    a = _rms(x, gmix_ref[...]).astype(bf16)
    nt_dims = (((1,), (1,)), ((), ()))

    gt = lax.dot_general(wt_ref[_TG0:TPROJ_ROWS, :], a, nt_dims,
                         preferred_element_type=jnp.float32) + gbt_ref[...]
    ic = GATE_CAP * jnp.tanh(gt[0:SUBLANES] / GATE_CAP)
    lf = _log_sigmoid(gt[SUBLANES:])
    b = _lane_scan(lf, L, jnp.add, 0.0)
    u = ic - b
    m_loc = b + _lane_scan(u, L, jnp.maximum, -jnp.inf)

    k_s[...] = _bdot(a, wn_ref[:, _K0:_V0])
    un = _bdot(a, wn_ref[:, _BG0:_GT0])
    z = un[:, CONV_CH:2 * CONV_CH] * un[:, 2 * CONV_CH:]
    if sb > 0:
        z_s[sb, 0:SUBLANES, :] = z_s[sb - 1, TS:TS + SUBLANES, :]
    z_s[sb, SUBLANES:SUBLANES + TS, :] = z
    yconv = (cw_ref[0:1, :] * z_s[sb, SUBLANES - 2:SUBLANES - 2 + TS, :]
             + cw_ref[1:2, :] * z_s[sb, SUBLANES - 1:SUBLANES - 1 + TS, :]
             + cw_ref[2:3, :] * z)
    yc = (un[:, 0:CONV_CH] * yconv).astype(bf16)

    ut = lax.dot_general(wt_ref[0:_TG0, :], a, nt_dims, preferred_element_type=jnp.float32)
    qt_s[...] = ut[_TQ0:_TV0] * (DK ** -0.5)
    vt_s[...] = ut[_TV0:_TOG0]
    ogt_s[...] = ut[_TOG0:_TG0]
    s.update(b=b, u=u, ic=ic, m_loc=m_loc, yc=yc)


def _prompt_chunks(s, m_prev):
    L, TS = s["L"], s["TS"]
    gmhc_ref, qt_s, vt_s, ogt_s, k_s, mixt_s, ucol_s, wi_s, c2_s, emt_s, wk_s, ast_s, st_s = (
        s["gmhc_ref"], s["qt_s"], s["vt_s"], s["ogt_s"], s["k_s"], s["mixt_s"], s["ucol_s"], s["wi_s"],
        s["c2_s"], s["emt_s"], s["wk_s"], s["ast_s"], s["st_s"])
    b, u, ic, m_loc = s["b"], s["u"], s["ic"], s["m_loc"]
    nc = TS // L
    bf16 = jnp.bfloat16

    for c in range(nc):
        sl = slice(c * L, (c + 1) * L)
        bc, mlc = b[:, sl], m_loc[:, sl]
        b_last = jnp.broadcast_to(bc[:, L - 1:L], bc.shape)
        m_new = jnp.maximum(b_last + m_prev, jnp.broadcast_to(mlc[:, L - 1:L], bc.shape))
        g = bc + m_prev
        mt = jnp.maximum(g, mlc)
        wi_s[:, sl] = jnp.exp(g - mt)
        c2_s[:, sl] = mt - bc
        emt_s[:, sl] = jnp.exp(-mt)
        wk_s[:, sl] = jnp.exp(b_last - bc + ic[:, sl] - m_new)
        ast_s[c] = jnp.exp(b_last + m_prev - m_new)
        upad = jnp.concatenate([u[:, sl], jnp.zeros((L - SUBLANES, L), jnp.float32)], axis=0)
        ucol_s[c * L:(c + 1) * L, :] = upad.T
        m_prev = m_new

    s_i = lax.broadcasted_iota(jnp.int32, (L, L), 0)
    t_i = lax.broadcasted_iota(jnp.int32, (L, L), 1)
    causal = s_i <= t_i
    low_half = lax.broadcasted_iota(jnp.int32, (L, LANES), 1) < DK
    zeros_q = jnp.zeros((DK, L), jnp.float32)
    zeros_p = jnp.zeros((L, L), bf16)

    def pair_row(ref, h0, rows):
        return jnp.concatenate([ref[h0:h0 + 1, rows], ref[h0 + 1:h0 + 2, rows]], axis=1)

    for c in range(nc):
        rows = slice(c * L, (c + 1) * L)
        for pr in range(NH // 2):
            h0 = 2 * pr
            hv0 = slice(h0 * DV, (h0 + 1) * DV)
            hv1 = slice((h0 + 1) * DV, (h0 + 2) * DV)
            kp = k_s[rows, pr * LANES:(pr + 1) * LANES]
            q0 = qt_s[h0 * DK:(h0 + 1) * DK, rows]
            q1 = qt_s[(h0 + 1) * DK:(h0 + 2) * DK, rows]
            qbd = jnp.concatenate([jnp.concatenate([q0, zeros_q], axis=1),
                                   jnp.concatenate([zeros_q, q1], axis=1)], axis=0).astype(bf16)
            st = _bdot(kp.astype(bf16), qbd)
            arg = jnp.concatenate(
                [jnp.where(causal, ucol_s[rows, h:h + 1] - c2_s[h:h + 1, rows], -jnp.inf)
                 for h in (h0, h0 + 1)], axis=1)
            pt = st * jnp.exp(arg)
            rs = jnp.sum(pt, axis=0, keepdims=True)
            ptb = pt.astype(bf16)
            pbd = jnp.concatenate([jnp.concatenate([ptb[:, 0:L], zeros_p], axis=1),
                                   jnp.concatenate([zeros_p, ptb[:, L:]], axis=1)], axis=0)
            vt = jnp.concatenate([vt_s[hv0, rows], vt_s[hv1, rows]], axis=1)
            state = st_s[pr]
            sq = _bdot(state.astype(bf16), qbd) * pair_row(wi_s, h0, rows)
            num = _bdot(vt.astype(bf16), pbd) + sq[0:DV]
            den = sq[DV:DV + 1] + rs
            hh = num / jnp.maximum(jnp.abs(den), pair_row(emt_s, h0, rows))
            hn = hh * lax.rsqrt(jnp.mean(hh * hh, axis=0, keepdims=True) + EPS)
            mixt_s[hv0, rows] = jax.nn.sigmoid(ogt_s[hv0, rows]) * (hn[:, 0:L] * gmhc_ref[hv0, :])
            mixt_s[hv1, rows] = jax.nn.sigmoid(ogt_s[hv1, rows]) * (hn[:, L:] * gmhc_ref[hv1, :])
            wkr = pair_row(wk_s, h0, rows)
            vw = jnp.concatenate([vt * wkr, jnp.broadcast_to(wkr, (2 * SUBLANES, 2 * L))], axis=0)
            km = jnp.concatenate([jnp.where(low_half, kp, 0.0), jnp.where(low_half, 0.0, kp)], axis=0)
            decay = jnp.where(low_half[0:1, :], ast_s[c][h0:h0 + 1, :], ast_s[c][h0 + 1:h0 + 2, :])
            st_s[pr] = decay * state + _bdot(vw.astype(bf16), km.astype(bf16))
    return m_prev


def _prompt_output(s):
    sb, TS = s["sb"], s["TS"]
    x_ref, wout_ref, x1_ref, mixt_s = s["x_ref"], s["wout_ref"], s["x1_ref"], s["mixt_s"]
    tok = slice(sb * TS, (sb + 1) * TS)
    out = x_ref[0, tok, :] + lax.dot_general(
        mixt_s[...].astype(jnp.bfloat16), wout_ref[0:MLSTM_W, :], (((0,), (0,)), ((), ())),
        preferred_element_type=jnp.float32)
    x1_ref[0, tok, :] = out + _bdot(s["yc"], wout_ref[MLSTM_W:, :])


def _mixer_prompt(x, weights, TB, NSUB, L):
    B, S, _ = x.shape
    TS = TB * NSUB
    nb = S // TS
    f32 = jnp.float32
    kern = functools.partial(_mixer_prompt_kernel, L=L, TB=TB, NSUB=NSUB)
    return pl.pallas_call(
        kern,
        grid=(B, nb),
        in_specs=[
            pl.BlockSpec((1, TS, D_MODEL), lambda b, j: (b, j, 0)),
            _const_spec((1, D_MODEL)),
            _const_spec((D_MODEL, PROJ_COLS)),
            _const_spec((TPROJ_ROWS, D_MODEL)),
            _const_spec((2 * SUBLANES, 1)),
            _const_spec((MLSTM_W, LANES)),
            _const_spec((CONV_W, CONV_CH)),
            _const_spec((D_MODEL, D_MODEL)),
        ],
        out_specs=[
            pl.BlockSpec((1, TS, D_MODEL), lambda b, j: (b, j, 0)),
            pl.BlockSpec((1, NH, DK, DV), lambda b, j: (b, 0, 0, 0)),
            pl.BlockSpec((1, NH // 2, 2 * DK), lambda b, j: (b, 0, 0)),
            pl.BlockSpec((1, SUBLANES, LANES), lambda b, j: (b, 0, 0)),
            pl.BlockSpec((1, SUBLANES, CONV_CH), lambda b, j: (b, 0, 0)),
        ],
        out_shape=[
            jax.ShapeDtypeStruct((B, S, D_MODEL), f32),
            jax.ShapeDtypeStruct((B, NH, DK, DV), f32),
            jax.ShapeDtypeStruct((B, NH // 2, 2 * DK), f32),
            jax.ShapeDtypeStruct((B, SUBLANES, LANES), f32),
            jax.ShapeDtypeStruct((B, SUBLANES, CONV_CH), f32),
        ],
        scratch_shapes=[
            pltpu.VMEM((NSUB, QK_W, TB), f32),
            pltpu.VMEM((NSUB, MLSTM_W, TB), f32),
            pltpu.VMEM((NSUB, MLSTM_W, TB), f32),
            pltpu.VMEM((NSUB, TB, QK_W), f32),
            pltpu.VMEM((NSUB, MLSTM_W, TB), f32),
            pltpu.VMEM((NSUB, TB + SUBLANES, CONV_CH), f32),
            pltpu.VMEM((NSUB, TB, LANES), f32),
            pltpu.VMEM((NSUB, SUBLANES, TB), f32),
            pltpu.VMEM((NSUB, SUBLANES, TB), f32),
            pltpu.VMEM((NSUB, SUBLANES, TB), f32),
            pltpu.VMEM((NSUB, SUBLANES, TB), f32),
            pltpu.VMEM((NSUB, TB // L, SUBLANES, LANES), f32),
            pltpu.VMEM((NH // 2, ST_ROWS, LANES), f32),
        ],
        compiler_params=pltpu.CompilerParams(
            dimension_semantics=("arbitrary", "arbitrary"), vmem_limit_bytes=VMEM_LIMIT),
        name="mixer_prompt",
    )(x, *weights)


def _row_scan(x, seg, op, fill):
    pos = lax.broadcasted_iota(jnp.int32, x.shape, 0) & (seg - 1)
    k = 1
    while k < seg:
        x = op(x, jnp.where(pos >= k, pltpu.roll(x, k, 0), fill))
        k *= 2
    return x


def _split_dot(x, sel, parts):
    acc = None
    rem = x
    for p in range(parts):
        hi = rem.astype(jnp.bfloat16)
        d = _bdot(hi, sel)
        acc = d if acc is None else acc + d
        if p + 1 < parts:
            rem = rem - hi.astype(jnp.float32)
    return acc


def _selectors():
    h = np.arange(NH)
    seg_qk = np.zeros((QK_W, LANES), np.float32)
    seg_qk[np.arange(QK_W), np.arange(QK_W) // DK] = 1.0
    exp_v = np.zeros((LANES, MLSTM_W), np.float32)
    exp_k = np.zeros((LANES, QK_W), np.float32)
    for i in h:
        exp_v[i, i * DV:(i + 1) * DV] = 1.0
        exp_k[i, i * DK:(i + 1) * DK] = 1.0
    mean_v = np.kron(np.eye(NH, dtype=np.float32), np.full((DV, DV), 1.0 / DV, np.float32))
    return tuple(jnp.asarray(m, jnp.bfloat16) for m in (seg_qk, exp_v, exp_k, mean_v))


def _mixer_sample_kernel(x_ref, cv_ref, c0_ref, n0_ref, m0_ref,
                         gmix_ref, win_ref, gb_ref, gmh_ref, cw_ref, wout_ref,
                         segqk_ref, expv_ref, expk_ref, meanv_ref,
                         x1_ref, cst_ref, nst_ref, mst_ref, cvo_ref,
                         xp_s, z_s, mp_s, qc_s, *, NSEQ):
    R = SUBLANES
    TS = NSEQ * R
    bf16 = jnp.bfloat16
    f32 = jnp.float32

    xp_s[:, SAMPLE_T:, :] = jnp.zeros((NSEQ, R - SAMPLE_T, D_MODEL), f32)
    xp_s[:, 0:SAMPLE_T, :] = x_ref[...]
    x = xp_s[...].reshape(TS, D_MODEL)
    a = _rms(x, gmix_ref[...]).astype(bf16)
    u = _bdot(a, win_ref[...])

    pos = lax.broadcasted_iota(jnp.int32, (TS, 1), 0) & (R - 1)
    real = pos < SAMPLE_T

    z = u[:, _CG0:_HC0] * u[:, _HC0:_GT0]
    z3 = z.reshape(NSEQ, R, CONV_CH)
    z_s[...] = z3
    z_s[:, R - (CONV_W - 1):, :] = cv_ref[...]
    zf = z_s[...].reshape(TS, CONV_CH)
    zm1 = jnp.where(pos >= 1, pltpu.roll(z, 1, 0), pltpu.roll(zf, TS - (R - 1), 0))
    zm2 = jnp.where(pos >= 2, pltpu.roll(z, 2, 0), pltpu.roll(zf, TS - (R - 2), 0))
    yc = u[:, _BG0:_CG0] * (cw_ref[0:1, :] * zm2 + cw_ref[1:2, :] * zm1 + cw_ref[2:3, :] * z)
    cvo_ref[...] = z3[:, SAMPLE_T - (CONV_W - 1):SAMPLE_T, :]

    gates = u[:, _GT0:PROJ_COLS] + gb_ref[...]
    ic = jnp.where(real, GATE_CAP * jnp.tanh(gates[:, :LANES] / GATE_CAP), NEG_BIG)
    lf = jnp.where(real, _log_sigmoid(gates[:, LANES:]), 0.0)
    b = _row_scan(lf, R, jnp.add, 0.0)
    uu = ic - b
    m_loc = b + _row_scan(uu, R, jnp.maximum, -jnp.inf)

    mp_s[...] = jnp.zeros(mp_s.shape, f32)
    mp_s[:, :, 0:NH] = m0_ref[...]
    m_prev = mp_s[...]
    b3 = b.reshape(NSEQ, R, LANES)
    ml3 = m_loc.reshape(NSEQ, R, LANES)
    b_last = b3[:, R - 1:R, :]
    m_new = jnp.maximum(b_last + m_prev, ml3[:, R - 1:R, :])
    g3 = b3 + m_prev
    mt3 = jnp.maximum(g3, ml3)
    wi = jnp.exp(g3 - mt3).reshape(TS, LANES)
    c2 = (mt3 - b3).reshape(TS, LANES)
    emt = jnp.exp(-mt3).reshape(TS, LANES)
    wk = jnp.exp(b_last - b3 + ic.reshape(NSEQ, R, LANES) - m_new).reshape(TS, LANES)
    a_st = jnp.broadcast_to(jnp.exp(b_last + m_prev - m_new), (NSEQ, R, LANES)).reshape(TS, LANES)
    mst_ref[...] = m_new[:, :, 0:NH]

    q = u[:, _Q0:_K0] * (DK ** -0.5)
    k = u[:, _K0:_V0]
    v = u[:, _V0:_OG0]
    rs = jnp.zeros((TS, LANES), f32)
    num = jnp.zeros((TS, MLSTM_W), f32)
    for d in range(SAMPLE_T):
        kd = k if d == 0 else pltpu.roll(k, d, 0)
        vd = v if d == 0 else pltpu.roll(v, d, 0)
        ud = uu if d == 0 else pltpu.roll(uu, d, 0)
        p = _split_dot(q * kd, segqk_ref[...], 2) * jnp.exp(ud - c2)
        rs = rs + p
        num = num + _split_dot(p, expv_ref[...], 2) * vd

    n0x = jnp.broadcast_to(n0_ref[...], (NSEQ, R, QK_W)).reshape(TS, QK_W)
    qn = _split_dot(q * n0x, segqk_ref[...], 2)
    den = wi * qn + rs
    rden = 1.0 / jnp.maximum(jnp.abs(den), emt)
    kw = k * _split_dot(wk, expk_ref[...], 2)
    ax = _split_dot(a_st, expv_ref[...], 3)
    lane_q = lax.broadcasted_iota(jnp.int32, (R, QK_W), 1)
    for i in range(NSEQ):
        rows = slice(i * R, (i + 1) * R)
        qi = q[rows, :]
        lhs = jnp.concatenate(
            [jnp.where((lane_q >= h * DK) & (lane_q < (h + 1) * DK), qi, 0.0) for h in range(NH)], axis=0)
        c0 = c0_ref[i]
        qc = _bdot(lhs.astype(bf16), c0.reshape(QK_W, DV).astype(bf16))
        qc_s[rows, :] = jnp.concatenate([qc[h * R:(h + 1) * R, :] for h in range(NH)], axis=1)
        kwi = kw[rows, :].astype(bf16)
        vi = v[rows, :].astype(bf16)
        for h in range(NH):
            dc = lax.dot_general(kwi[:, h * DK:(h + 1) * DK], vi[:, h * DV:(h + 1) * DV],
                                 (((0,), (0,)), ((), ())), preferred_element_type=f32)
            cst_ref[i, h] = ax[i * R:i * R + 1, h * DV:(h + 1) * DV] * c0[h] + dc

    hh = (_split_dot(wi, expv_ref[...], 2) * qc_s[...] + num) * _split_dot(rden, expv_ref[...], 2)
    ms = _split_dot(hh * hh, meanv_ref[...], 2)
    hm = jax.nn.sigmoid(u[:, _OG0:_BG0]) * (hh * lax.rsqrt(ms + EPS) * gmh_ref[...])
    mix = jnp.concatenate([hm, yc], axis=1).astype(bf16)
    out = x + _bdot(mix, wout_ref[...])
    x1_ref[...] = out.reshape(NSEQ, R, D_MODEL)[:, 0:SAMPLE_T, :]

    a_k = _split_dot(a_st, expk_ref[...], 3).reshape(NSEQ, R, QK_W)[:, 0:1, :]
    nst_ref[...] = a_k * n0_ref[...] + jnp.sum(kw.reshape(NSEQ, R, QK_W), axis=1, keepdims=True)


def _mixer_sample(x, cv, c0, n0, m0, weights, NSEQ):
    Bs = x.shape[0]
    f32 = jnp.float32
    kern = functools.partial(_mixer_sample_kernel, NSEQ=NSEQ)
    x_spec = pl.BlockSpec((NSEQ, SAMPLE_T, D_MODEL), lambda t: (t, 0, 0))
    cv_spec = pl.BlockSpec((NSEQ, CONV_W - 1, CONV_CH), lambda t: (t, 0, 0))
    c_spec = pl.BlockSpec((NSEQ, NH, DK, DV), lambda t: (t, 0, 0, 0))
    n_spec = pl.BlockSpec((NSEQ, 1, QK_W), lambda t: (t, 0, 0))
    m_spec = pl.BlockSpec((NSEQ, 1, NH), lambda t: (t, 0, 0))
    return pl.pallas_call(
        kern,
        grid=(Bs // NSEQ,),
        in_specs=[
            x_spec, cv_spec, c_spec, n_spec, m_spec,
            _const_spec((1, D_MODEL)),
            _const_spec((D_MODEL, PROJ_COLS)),
            _const_spec((1, 2 * LANES)),
            _const_spec((1, MLSTM_W)),
            _const_spec((CONV_W, CONV_CH)),
            _const_spec((D_MODEL, D_MODEL)),
            _const_spec((QK_W, LANES)),
            _const_spec((LANES, MLSTM_W)),
            _const_spec((LANES, QK_W)),
            _const_spec((MLSTM_W, MLSTM_W)),
        ],
        out_specs=[x_spec, c_spec, n_spec, m_spec, cv_spec],
        out_shape=[
            jax.ShapeDtypeStruct((Bs, SAMPLE_T, D_MODEL), f32),
            jax.ShapeDtypeStruct((Bs, NH, DK, DV), f32),
            jax.ShapeDtypeStruct((Bs, 1, QK_W), f32),
            jax.ShapeDtypeStruct((Bs, 1, NH), f32),
            jax.ShapeDtypeStruct((Bs, CONV_W - 1, CONV_CH), f32),
        ],
        scratch_shapes=[
            pltpu.VMEM((NSEQ, SUBLANES, D_MODEL), f32),
            pltpu.VMEM((NSEQ, SUBLANES, CONV_CH), f32),
            pltpu.VMEM((NSEQ, 1, LANES), f32),
            pltpu.VMEM((NSEQ * SUBLANES, MLSTM_W), f32),
        ],
        compiler_params=pltpu.CompilerParams(
            dimension_semantics=("arbitrary",), vmem_limit_bytes=VMEM_LIMIT),
        name="mixer_sample",
    )(x, cv, c0, n0, m0, *weights)


def _mixer_weights(g_mix, w_in, b_ig, b_fg, g_mh, conv_w, w_out):
    f32 = jnp.float32
    w_all, wt = _prep_weights(w_in)
    wout = w_out.astype(jnp.bfloat16)
    gmix = g_mix.reshape(1, D_MODEL)
    zeros = jnp.zeros((SUBLANES - NH,), f32)
    gb_col = jnp.concatenate([b_ig, zeros, b_fg, zeros]).reshape(2 * SUBLANES, 1)
    zeros = jnp.zeros((LANES - NH,), f32)
    gb_row = jnp.concatenate([b_ig, zeros, b_fg, zeros]).reshape(1, 2 * LANES)
    gmhc = jnp.broadcast_to(g_mh[:, None], (MLSTM_W, LANES))
    prompt = (gmix, w_all, wt, gb_col, gmhc, conv_w, wout)
    sample = (gmix, w_all, gb_row, g_mh.reshape(1, MLSTM_W), conv_w, wout) + _selectors()
    return prompt, sample


FF_STREAM = 256
_N_FF_PIECES = D_FF // FF_STREAM
_N_PIECES = _N_FF_PIECES + PLE_DIM // FF_STREAM + D_MODEL // FF_STREAM


def _ffn_rows(x, pe, gffn_ref, gple_ref, gfin_ref, wg_s, wu_s, wd_s, wple_s, wpg_s, *, chunk, final_norm,
              before_chunk=None, before_tail=None):
    bf16 = jnp.bfloat16
    f = _rms(x, gffn_ref[...]).astype(bf16)
    for c0 in range(0, D_FF, chunk):
        c1 = min(c0 + chunk, D_FF)
        if before_chunk is not None:
            before_chunk(c0 // chunk)
        gate = _bdot(f, wg_s[:, c0:c1])
        up = _bdot(f, wu_s[:, c0:c1])
        hmid = (gate * jax.nn.sigmoid(gate) * up).astype(bf16)
        x = x + _bdot(hmid, wd_s[c0:c1, :])
    if before_tail is not None:
        before_tail()
    e = _rms(_bdot(pe.astype(bf16), wple_s[...]), gple_ref[...])
    x = x + jax.nn.sigmoid(_bdot(x.astype(bf16), wpg_s[...])) * e
    if final_norm:
        x = _rms(x, gfin_ref[...])
    return x


def _ffn_stream_kernel(xp_ref, pp_ref, xs_ref, ps_ref, gffn_ref, gple_ref, gfin_ref,
                       wg_hbm, wu_hbm, wd_hbm, wple_hbm, wpg_hbm,
                       yp_ref, ys_ref,
                       wg_s, wu_s, wd_s, wple_s, wpg_s, stg_g, stg_u, stg_d, sem, *, final_norm):
    t = pl.program_id(0)
    bf16 = jnp.bfloat16
    weights = (wg_s, wu_s, wd_s, wple_s, wpg_s)

    def copies(k):
        slot = k % 2
        if k < _N_FF_PIECES:
            cols = pl.ds(k * FF_STREAM, FF_STREAM)
            return [pltpu.make_async_copy(wg_hbm.at[:, cols], stg_g.at[slot], sem.at[0, slot]),
                    pltpu.make_async_copy(wu_hbm.at[:, cols], stg_u.at[slot], sem.at[1, slot]),
                    pltpu.make_async_copy(wd_hbm.at[cols, :], stg_d.at[slot], sem.at[2, slot])]
        if k == _N_FF_PIECES:
            return [pltpu.make_async_copy(wple_hbm, stg_d.at[slot], sem.at[2, slot])]
        rows = pl.ds((k - _N_FF_PIECES - 1) * FF_STREAM, FF_STREAM)
        return [pltpu.make_async_copy(wpg_hbm.at[rows, :], stg_d.at[slot], sem.at[2, slot])]

    def land(k):
        if k + 1 < _N_PIECES:
            for cp in copies(k + 1):
                cp.start()
        for cp in copies(k):
            cp.wait()
        slot = k % 2
        if k < _N_FF_PIECES:
            cols = slice(k * FF_STREAM, (k + 1) * FF_STREAM)
            wg_s[:, cols] = stg_g[slot].astype(bf16)
            wu_s[:, cols] = stg_u[slot].astype(bf16)
            wd_s[cols, :] = stg_d[slot].astype(bf16)
        elif k == _N_FF_PIECES:
            wple_s[...] = stg_d[slot].astype(bf16)
        else:
            r0 = (k - _N_FF_PIECES - 1) * FF_STREAM
            wpg_s[r0:r0 + FF_STREAM, :] = stg_d[slot].astype(bf16)

    @pl.when(t == 0)
    def _():
        for cp in copies(0):
            cp.start()
        n_tok = xs_ref.shape[0] * xs_ref.shape[1]

        def tail():
            for k in range(_N_FF_PIECES, _N_PIECES):
                land(k)

        y = _ffn_rows(xs_ref[...].reshape(n_tok, D_MODEL), ps_ref[...].reshape(n_tok, PLE_DIM),
                      gffn_ref, gple_ref, gfin_ref, *weights, chunk=FF_STREAM, final_norm=final_norm,
                      before_chunk=land, before_tail=tail)
        ys_ref[...] = y.reshape(ys_ref.shape)

    @pl.when(t > 0)
    def _():
        yp_ref[...] = _ffn_rows(xp_ref[...], pp_ref[...], gffn_ref, gple_ref, gfin_ref, *weights,
                                chunk=FF_CHUNK, final_norm=final_norm)


def _ffn_stream(xp, pp, xs, ps, g_ffn, g_ple, g_final, w_gate, w_up, w_down, w_ple, w_pg, TM, final_norm):
    T = xp.shape[0]
    f32, bf16 = jnp.float32, jnp.bfloat16
    kern = functools.partial(_ffn_stream_kernel, final_norm=final_norm)
    row_map = lambda t: (jnp.maximum(t - 1, 0), 0)
    hbm = pl.BlockSpec(memory_space=pl.ANY)
    return pl.pallas_call(
        kern,
        grid=(T // TM + 1,),
        in_specs=[
            pl.BlockSpec((TM, D_MODEL), row_map),
            pl.BlockSpec((TM, PLE_DIM), row_map),
            _const_spec(xs.shape),
            _const_spec(ps.shape),
            _const_spec((1, D_MODEL)),
            _const_spec((1, D_MODEL)),
            _const_spec((1, D_MODEL)),
            hbm, hbm, hbm, hbm, hbm,
        ],
        out_specs=[pl.BlockSpec((TM, D_MODEL), row_map),
                   pl.BlockSpec(xs.shape, lambda t: (0, 0, 0))],
        out_shape=[jax.ShapeDtypeStruct(xp.shape, f32), jax.ShapeDtypeStruct(xs.shape, f32)],
        scratch_shapes=[
            pltpu.VMEM((D_MODEL, D_FF), bf16),
            pltpu.VMEM((D_MODEL, D_FF), bf16),
            pltpu.VMEM((D_FF, D_MODEL), bf16),
            pltpu.VMEM((PLE_DIM, D_MODEL), bf16),
            pltpu.VMEM((D_MODEL, D_MODEL), bf16),
            pltpu.VMEM((2, D_MODEL, FF_STREAM), f32),
            pltpu.VMEM((2, D_MODEL, FF_STREAM), f32),
            pltpu.VMEM((2, FF_STREAM, D_MODEL), f32),
            pltpu.SemaphoreType.DMA((3, 2)),
        ],
        compiler_params=pltpu.CompilerParams(
            dimension_semantics=("arbitrary",), vmem_limit_bytes=FFN_VMEM_LIMIT),
        name="ffn",
    )(xp, pp, xs, ps, g_ffn.reshape(1, D_MODEL), g_ple.reshape(1, D_MODEL), g_final.reshape(1, D_MODEL),
      w_gate, w_up, w_down, w_ple, w_pg)


def kernel(x_prompt, x_sample, p_prompt, p_sample, state_C, state_n, state_m, state_conv,
           g_mix, w_in, b_ig, b_fg, g_mh, conv_w, w_out, g_ffn, w_gate, w_up, w_down,
           w_ple, g_ple, w_pg, g_final):
    bf16 = jnp.bfloat16
    depth = g_mix.shape[0]
    B, S, _ = x_prompt.shape
    Bs, Ss, _ = x_sample.shape
    assert Ss == SAMPLE_T and S % PROMPT_TS == 0 and Bs % SAMPLE_NSEQ == 0

    xp = x_prompt
    xs = x_sample
    outs = [[] for _ in range(8)]
    for i in range(depth):
        last = i == depth - 1
        pw, sw = _mixer_weights(g_mix[i], w_in[i], b_ig[i], b_fg[i], g_mh[i], conv_w[i], w_out[i])
        x1p, cp, n_p, mp, cvp = _mixer_prompt(xp, pw, TB=PROMPT_TS, NSUB=PROMPT_NSUB, L=PROMPT_L)
        x1s, cs, n_s, ms, cvs = _mixer_sample(
            xs, state_conv[i], state_C[i], state_n[i].reshape(Bs, 1, QK_W),
            state_m[i].reshape(Bs, 1, NH), sw, SAMPLE_NSEQ)
        xp, xs = _ffn_stream(x1p.reshape(B * S, D_MODEL), p_prompt[i].reshape(B * S, PLE_DIM), x1s,
                             p_sample[i], g_ffn[i], g_ple[i], g_final, w_gate[i], w_up[i], w_down[i],
                             w_ple[i], w_pg[i], FFN_TM, last)
        xp = xp.reshape(B, S, D_MODEL)

        new = (cp, n_p.reshape(B, NH, DK), mp[:, :NH, 0], cvp[:, SUBLANES - (CONV_W - 1):],
               cs, n_s.reshape(Bs, NH, DK), ms.reshape(Bs, NH), cvs)
        for lst, v in zip(outs, new):
            lst.append(v)

    return (xp, xs) + tuple(jnp.stack(l) for l in outs)
```

```python
import functools

import numpy as np
import jax
import jax.numpy as jnp
from jax import lax
from jax.experimental import pallas as pl
from jax.experimental.pallas import tpu as pltpu

D_MODEL = 1024
NH = 4
DK = 64
DV = 128
MLSTM_W = NH * DV
QK_W = NH * DK
CONV_CH = 512
CONV_W = 3
D_FF = 2816
FF_CHUNK = 512
FFN_TM = 512
PLE_DIM = 256
PROMPT_L = 128
PROMPT_TS = 512
PROMPT_NSUB = 2
SAMPLE_T = 4
SAMPLE_NSEQ = 32
GATE_CAP = 15.0
EPS = 1e-6

LANES = 128
SUBLANES = 8
NEG_BIG = -1e30

VMEM_LIMIT = 56 * 1024 * 1024


def _rms(x, g):
    ms = jnp.mean(x * x, axis=-1, keepdims=True)
    return x * lax.rsqrt(ms + EPS) * g


def _bdot(a, b):
    return jnp.dot(a, b, preferred_element_type=jnp.float32)


def _log_sigmoid(x):
    return jnp.minimum(x, 0.0) - jnp.log1p(jnp.exp(-jnp.abs(x)))


def _const_spec(shape):
    nd = len(shape)
    return pl.BlockSpec(shape, lambda *_: (0,) * nd, pipeline_mode=pl.Buffered(1))


_IG0 = 2 * QK_W + MLSTM_W
_OG_IN = _IG0 + 2 * NH
PROJ_IN = _OG_IN + MLSTM_W + 3 * CONV_CH
_Q0, _K0, _V0, _OG0, _BG0, _CG0, _HC0, _GT0 = 0, 256, 512, 1024, 1536, 2048, 2560, 3072
PROJ_COLS = _GT0 + 2 * LANES
_TQ0, _TV0, _TOG0, _TG0 = 0, 256, 768, 1280
TPROJ_ROWS = _TG0 + 2 * SUBLANES
PREP_TK = 256


def _prep_weights_kernel(w_ref, wall_ref, wt_ref):
    bf16 = jnp.bfloat16
    wf = w_ref[...]
    tk = wf.shape[1]
    g8 = wf[_IG0:_OG_IN]
    row = lax.broadcasted_iota(jnp.int32, (SUBLANES, tk), 0)
    g_in = jnp.where(row < NH, g8, 0.0)
    g_fg = jnp.where(row < NH, pltpu.roll(g8, SUBLANES - NH, 0), 0.0)

    wt_ref[_TQ0:_TV0, :] = wf[_Q0:_K0].astype(bf16)
    wt_ref[_TV0:_TOG0, :] = wf[_V0:_OG0].astype(bf16)
    wt_ref[_TOG0:_TG0, :] = wf[_OG_IN:_OG_IN + MLSTM_W].astype(bf16)
    wt_ref[_TG0:TPROJ_ROWS, :] = jnp.concatenate([g_in, g_fg], axis=0).astype(bf16)

    wall_ref[:, 0:_OG0] = wf[0:_IG0].T.astype(bf16)
    wall_ref[:, _OG0:_GT0] = wf[_OG_IN:PROJ_IN].T.astype(bf16)
    pad = jnp.zeros((LANES - SUBLANES, tk), jnp.float32)
    wall_ref[:, _GT0:_GT0 + LANES] = jnp.concatenate([g_in, pad], axis=0).T.astype(bf16)
    wall_ref[:, _GT0 + LANES:PROJ_COLS] = jnp.concatenate([g_fg, pad], axis=0).T.astype(bf16)


def _prep_weights(w_in):
    bf16 = jnp.bfloat16
    return pl.pallas_call(
        _prep_weights_kernel,
        grid=(D_MODEL // PREP_TK,),
        in_specs=[pl.BlockSpec((PROJ_IN, PREP_TK), lambda i: (0, i))],
        out_specs=[pl.BlockSpec((PREP_TK, PROJ_COLS), lambda i: (i, 0)),
                   pl.BlockSpec((TPROJ_ROWS, PREP_TK), lambda i: (0, i))],
        out_shape=[jax.ShapeDtypeStruct((D_MODEL, PROJ_COLS), bf16),
                   jax.ShapeDtypeStruct((TPROJ_ROWS, D_MODEL), bf16)],
        compiler_params=pltpu.CompilerParams(
            dimension_semantics=("arbitrary",), vmem_limit_bytes=VMEM_LIMIT),
        name="prep_weights",
    )(w_in.T)


ST_ROWS = DV + 2 * SUBLANES


def _lane_scan(x, seg, op, fill):
    pos = lax.broadcasted_iota(jnp.int32, x.shape, 1) & (seg - 1)
    k = 1
    while k < seg:
        x = op(x, jnp.where(pos >= k, pltpu.roll(x, k, 1), fill))
        k *= 2
    return x


PROJ_PIECE = 256


def _run(*phases):
    live = list(phases)
    while live:
        for g in list(live):
            try:
                next(g)
            except StopIteration:
                live.remove(g)


def _mixer_prompt_kernel(x_ref, gmix_ref, wn_ref, wt_ref, gbt_ref, gmhc_ref, cw_ref, wout_ref,
                         x1_ref, c_ref, n_ref, m_ref, cvo_ref,
                         qt_s, vt_s, ogt_s, k_s, mixt_s, z_s, ucol_s, wi_s, c2_s, emt_s, wk_s, ast_s, st_s,
                         *, L, TB, NSUB):
    j = pl.program_id(1)

    @pl.when(j == 0)
    def _():
        z_s[0, 0:SUBLANES, :] = jnp.zeros((SUBLANES, CONV_CH), jnp.float32)
        st_s[...] = jnp.zeros(st_s.shape, jnp.float32)
        m_ref[...] = jnp.zeros(m_ref.shape, jnp.float32)

    m_prev = m_ref[0]
    sub = [dict(sb=sb, x_ref=x_ref, gmix_ref=gmix_ref, wn_ref=wn_ref, wt_ref=wt_ref, gbt_ref=gbt_ref,
                gmhc_ref=gmhc_ref, cw_ref=cw_ref, wout_ref=wout_ref, x1_ref=x1_ref, qt_s=qt_s.at[sb],
                vt_s=vt_s.at[sb], ogt_s=ogt_s.at[sb], k_s=k_s.at[sb], mixt_s=mixt_s.at[sb], z_s=z_s,
                ucol_s=ucol_s.at[sb], wi_s=wi_s.at[sb], c2_s=c2_s.at[sb], emt_s=emt_s.at[sb],
                wk_s=wk_s.at[sb], ast_s=ast_s.at[sb], st_s=st_s, L=L, TS=TB) for sb in range(NSUB)]
    carry = {"m": m_prev}
    _run(_prompt_project(sub[0]))
    for a, b in zip(sub[:-1], sub[1:]):
        _run(_prompt_chunks(a, carry), _prompt_project(b))
    _run(_prompt_chunks(sub[-1], carry), *[_prompt_output(s) for s in sub[:-1]])
    _run(_prompt_output(sub[-1]))
    m_ref[0] = carry["m"]
    cvo_ref[0] = z_s[NSUB - 1, TB:TB + SUBLANES, :]
    z_s[0, 0:SUBLANES, :] = z_s[NSUB - 1, TB:TB + SUBLANES, :]

    @pl.when(j == pl.num_programs(1) - 1)
    def _():
        for pr in range(NH // 2):
            state = st_s[pr]
            c_pair = state[0:DV, :].T
            c_ref[0, 2 * pr] = c_pair[0:DK, :]
            c_ref[0, 2 * pr + 1] = c_pair[DK:, :]
            n_ref[0, pr:pr + 1, :] = state[DV:DV + 1, :]


def _prompt_project(s):
    sb, L, TS = s["sb"], s["L"], s["TS"]
    x_ref, gmix_ref, wn_ref, wt_ref, gbt_ref, cw_ref = (
        s["x_ref"], s["gmix_ref"], s["wn_ref"], s["wt_ref"], s["gbt_ref"], s["cw_ref"])
    qt_s, vt_s, ogt_s, k_s, z_s = s["qt_s"], s["vt_s"], s["ogt_s"], s["k_s"], s["z_s"]
    bf16 = jnp.bfloat16

    x = x_ref[0, sb * TS:(sb + 1) * TS, :]
    a = _rms(x, gmix_ref[...]).astype(bf16)
    nt_dims = (((1,), (1,)), ((), ()))

    gt = lax.dot_general(wt_ref[_TG0:TPROJ_ROWS, :], a, nt_dims,
                         preferred_element_type=jnp.float32) + gbt_ref[...]
    ic = GATE_CAP * jnp.tanh(gt[0:SUBLANES] / GATE_CAP)
    lf = _log_sigmoid(gt[SUBLANES:])
    b = _lane_scan(lf, L, jnp.add, 0.0)
    u = ic - b
    m_loc = b + _lane_scan(u, L, jnp.maximum, -jnp.inf)
    s.update(b=b, u=u, ic=ic, m_loc=m_loc)
    yield

    k_s[...] = _bdot(a, wn_ref[:, _K0:_V0])
    yield
    hc = _bdot(a, wn_ref[:, _HC0:_GT0])
    yield
    z = _bdot(a, wn_ref[:, _CG0:_HC0]) * hc
    if sb > 0:
        z_s[sb, 0:SUBLANES, :] = z_s[sb - 1, TS:TS + SUBLANES, :]
    z_s[sb, SUBLANES:SUBLANES + TS, :] = z
    yield
    yconv = (cw_ref[0:1, :] * z_s[sb, SUBLANES - 2:SUBLANES - 2 + TS, :]
             + cw_ref[1:2, :] * z_s[sb, SUBLANES - 1:SUBLANES - 1 + TS, :]
             + cw_ref[2:3, :] * z)
    s["yc"] = (_bdot(a, wn_ref[:, _BG0:_CG0]) * yconv).astype(bf16)
    yield

    for r0 in range(0, _TG0, PROJ_PIECE):
        ut = lax.dot_general(wt_ref[r0:r0 + PROJ_PIECE, :], a, nt_dims, preferred_element_type=jnp.float32)
        if r0 < _TV0:
            qt_s[r0:r0 + PROJ_PIECE, :] = ut * (DK ** -0.5)
        elif r0 < _TOG0:
            vt_s[r0 - _TV0:r0 - _TV0 + PROJ_PIECE, :] = ut
        else:
            ogt_s[r0 - _TOG0:r0 - _TOG0 + PROJ_PIECE, :] = ut
        yield


def _prompt_chunks(s, carry):
    L, TS = s["L"], s["TS"]
    gmhc_ref, qt_s, vt_s, ogt_s, k_s, mixt_s, ucol_s, wi_s, c2_s, emt_s, wk_s, ast_s, st_s = (
        s["gmhc_ref"], s["qt_s"], s["vt_s"], s["ogt_s"], s["k_s"], s["mixt_s"], s["ucol_s"], s["wi_s"],
        s["c2_s"], s["emt_s"], s["wk_s"], s["ast_s"], s["st_s"])
    b, u, ic, m_loc = s["b"], s["u"], s["ic"], s["m_loc"]
    nc = TS // L
    bf16 = jnp.bfloat16
    m_prev = carry["m"]

    for c in range(nc):
        sl = slice(c * L, (c + 1) * L)
        bc, mlc = b[:, sl], m_loc[:, sl]
        b_last = jnp.broadcast_to(bc[:, L - 1:L], bc.shape)
        m_new = jnp.maximum(b_last + m_prev, jnp.broadcast_to(mlc[:, L - 1:L], bc.shape))
        g = bc + m_prev
        mt = jnp.maximum(g, mlc)
        wi_s[:, sl] = jnp.exp(g - mt)
        c2_s[:, sl] = mt - bc
        emt_s[:, sl] = jnp.exp(-mt)
        wk_s[:, sl] = jnp.exp(b_last - bc + ic[:, sl] - m_new)
        ast_s[c] = jnp.exp(b_last + m_prev - m_new)
        upad = jnp.concatenate([u[:, sl], jnp.zeros((L - SUBLANES, L), jnp.float32)], axis=0)
        ucol_s[c * L:(c + 1) * L, :] = upad.T
        m_prev = m_new
    carry["m"] = m_prev
    yield

    s_i = lax.broadcasted_iota(jnp.int32, (L, L), 0)
    t_i = lax.broadcasted_iota(jnp.int32, (L, L), 1)
    causal = s_i <= t_i
    low_half = lax.broadcasted_iota(jnp.int32, (L, LANES), 1) < DK
    zeros_q = jnp.zeros((DK, L), jnp.float32)
    zeros_p = jnp.zeros((L, L), bf16)

    def pair_row(ref, h0, rows):
        return jnp.concatenate([ref[h0:h0 + 1, rows], ref[h0 + 1:h0 + 2, rows]], axis=1)

    for c in range(nc):
        rows = slice(c * L, (c + 1) * L)
        for pr in range(NH // 2):
            h0 = 2 * pr
            hv0 = slice(h0 * DV, (h0 + 1) * DV)
            hv1 = slice((h0 + 1) * DV, (h0 + 2) * DV)
            kp = k_s[rows, pr * LANES:(pr + 1) * LANES]
            q0 = qt_s[h0 * DK:(h0 + 1) * DK, rows]
            q1 = qt_s[(h0 + 1) * DK:(h0 + 2) * DK, rows]
            qbd = jnp.concatenate([jnp.concatenate([q0, zeros_q], axis=1),
                                   jnp.concatenate([zeros_q, q1], axis=1)], axis=0).astype(bf16)
            st = _bdot(kp.astype(bf16), qbd)
            arg = jnp.concatenate(
                [jnp.where(causal, ucol_s[rows, h:h + 1] - c2_s[h:h + 1, rows], -jnp.inf)
                 for h in (h0, h0 + 1)], axis=1)
            pt = st * jnp.exp(arg)
            rs = jnp.sum(pt, axis=0, keepdims=True)
            ptb = pt.astype(bf16)
            pbd = jnp.concatenate([jnp.concatenate([ptb[:, 0:L], zeros_p], axis=1),
                                   jnp.concatenate([zeros_p, ptb[:, L:]], axis=1)], axis=0)
            vt = jnp.concatenate([vt_s[hv0, rows], vt_s[hv1, rows]], axis=1)
            state = st_s[pr]
            sq = _bdot(state.astype(bf16), qbd) * pair_row(wi_s, h0, rows)
            num = _bdot(vt.astype(bf16), pbd) + sq[0:DV]
            den = sq[DV:DV + 1] + rs
            hh = num / jnp.maximum(jnp.abs(den), pair_row(emt_s, h0, rows))
            hn = hh * lax.rsqrt(jnp.mean(hh * hh, axis=0, keepdims=True) + EPS)
            mixt_s[hv0, rows] = jax.nn.sigmoid(ogt_s[hv0, rows]) * (hn[:, 0:L] * gmhc_ref[hv0, :])
            mixt_s[hv1, rows] = jax.nn.sigmoid(ogt_s[hv1, rows]) * (hn[:, L:] * gmhc_ref[hv1, :])
            wkr = pair_row(wk_s, h0, rows)
            vw = jnp.concatenate([vt * wkr, jnp.broadcast_to(wkr, (2 * SUBLANES, 2 * L))], axis=0)
            km = jnp.concatenate([jnp.where(low_half, kp, 0.0), jnp.where(low_half, 0.0, kp)], axis=0)
            decay = jnp.where(low_half[0:1, :], ast_s[c][h0:h0 + 1, :], ast_s[c][h0 + 1:h0 + 2, :])
            st_s[pr] = decay * state + _bdot(vw.astype(bf16), km.astype(bf16))
            yield


def _prompt_output(s):
    sb, TS = s["sb"], s["TS"]
    x_ref, wout_ref, x1_ref, mixt_s = s["x_ref"], s["wout_ref"], s["x1_ref"], s["mixt_s"]
    tok = slice(sb * TS, (sb + 1) * TS)
    mix = jnp.concatenate([mixt_s[...].T.astype(jnp.bfloat16), s["yc"]], axis=1)
    yield
    for c0 in range(0, D_MODEL, PROJ_PIECE):
        cols = slice(c0, c0 + PROJ_PIECE)
        x1_ref[0, tok, cols] = x_ref[0, tok, cols] + _bdot(mix, wout_ref[:, cols])
        yield


def _mixer_prompt(x, weights, TB, NSUB, L):
    B, S, _ = x.shape
    TS = TB * NSUB
    nb = S // TS
    f32 = jnp.float32
    kern = functools.partial(_mixer_prompt_kernel, L=L, TB=TB, NSUB=NSUB)
    return pl.pallas_call(
        kern,
        grid=(B, nb),
        in_specs=[
            pl.BlockSpec((1, TS, D_MODEL), lambda b, j: (b, j, 0)),
            _const_spec((1, D_MODEL)),
            _const_spec((D_MODEL, PROJ_COLS)),
            _const_spec((TPROJ_ROWS, D_MODEL)),
            _const_spec((2 * SUBLANES, 1)),
            _const_spec((MLSTM_W, LANES)),
            _const_spec((CONV_W, CONV_CH)),
            _const_spec((D_MODEL, D_MODEL)),
        ],
        out_specs=[
            pl.BlockSpec((1, TS, D_MODEL), lambda b, j: (b, j, 0)),
            pl.BlockSpec((1, NH, DK, DV), lambda b, j: (b, 0, 0, 0)),
            pl.BlockSpec((1, NH // 2, 2 * DK), lambda b, j: (b, 0, 0)),
            pl.BlockSpec((1, SUBLANES, LANES), lambda b, j: (b, 0, 0)),
            pl.BlockSpec((1, SUBLANES, CONV_CH), lambda b, j: (b, 0, 0)),
        ],
        out_shape=[
            jax.ShapeDtypeStruct((B, S, D_MODEL), f32),
            jax.ShapeDtypeStruct((B, NH, DK, DV), f32),
            jax.ShapeDtypeStruct((B, NH // 2, 2 * DK), f32),
            jax.ShapeDtypeStruct((B, SUBLANES, LANES), f32),
            jax.ShapeDtypeStruct((B, SUBLANES, CONV_CH), f32),
        ],
        scratch_shapes=[
            pltpu.VMEM((NSUB, QK_W, TB), f32),
            pltpu.VMEM((NSUB, MLSTM_W, TB), f32),
            pltpu.VMEM((NSUB, MLSTM_W, TB), f32),
            pltpu.VMEM((NSUB, TB, QK_W), f32),
            pltpu.VMEM((NSUB, MLSTM_W, TB), f32),
            pltpu.VMEM((NSUB, TB + SUBLANES, CONV_CH), f32),
            pltpu.VMEM((NSUB, TB, LANES), f32),
            pltpu.VMEM((NSUB, SUBLANES, TB), f32),
            pltpu.VMEM((NSUB, SUBLANES, TB), f32),
            pltpu.VMEM((NSUB, SUBLANES, TB), f32),
            pltpu.VMEM((NSUB, SUBLANES, TB), f32),
            pltpu.VMEM((NSUB, TB // L, SUBLANES, LANES), f32),
            pltpu.VMEM((NH // 2, ST_ROWS, LANES), f32),
        ],
        compiler_params=pltpu.CompilerParams(
            dimension_semantics=("arbitrary", "arbitrary"), vmem_limit_bytes=VMEM_LIMIT),
        name="mixer_prompt",
    )(x, *weights)


def _row_scan(x, seg, op, fill):
    pos = lax.broadcasted_iota(jnp.int32, x.shape, 0) & (seg - 1)
    k = 1
    while k < seg:
        x = op(x, jnp.where(pos >= k, pltpu.roll(x, k, 0), fill))
        k *= 2
    return x


def _split_dot(x, sel, parts):
    acc = None
    rem = x
    for p in range(parts):
        hi = rem.astype(jnp.bfloat16)
        d = _bdot(hi, sel)
        acc = d if acc is None else acc + d
        if p + 1 < parts:
            rem = rem - hi.astype(jnp.float32)
    return acc


def _selectors():
    h = np.arange(NH)
    seg_qk = np.zeros((QK_W, LANES), np.float32)
    seg_qk[np.arange(QK_W), np.arange(QK_W) // DK] = 1.0
    exp_v = np.zeros((LANES, MLSTM_W), np.float32)
    exp_k = np.zeros((LANES, QK_W), np.float32)
    for i in h:
        exp_v[i, i * DV:(i + 1) * DV] = 1.0
        exp_k[i, i * DK:(i + 1) * DK] = 1.0
    mean_v = np.kron(np.eye(NH, dtype=np.float32), np.full((DV, DV), 1.0 / DV, np.float32))
    return tuple(jnp.asarray(m, jnp.bfloat16) for m in (seg_qk, exp_v, exp_k, mean_v))


def _mixer_sample_kernel(x_ref, cv_ref, c0_ref, n0_ref, m0_ref,
                         gmix_ref, win_ref, gb_ref, gmh_ref, cw_ref, wout_ref,
                         segqk_ref, expv_ref, expk_ref, meanv_ref,
                         x1_ref, cst_ref, nst_ref, mst_ref, cvo_ref,
                         xp_s, z_s, mp_s, qc_s, *, NSEQ):
    R = SUBLANES
    TS = NSEQ * R
    bf16 = jnp.bfloat16
    f32 = jnp.float32

    xp_s[:, SAMPLE_T:, :] = jnp.zeros((NSEQ, R - SAMPLE_T, D_MODEL), f32)
    xp_s[:, 0:SAMPLE_T, :] = x_ref[...]
    x = xp_s[...].reshape(TS, D_MODEL)
    a = _rms(x, gmix_ref[...]).astype(bf16)
    u = _bdot(a, win_ref[...])

    pos = lax.broadcasted_iota(jnp.int32, (TS, 1), 0) & (R - 1)
    real = pos < SAMPLE_T

    z = u[:, _CG0:_HC0] * u[:, _HC0:_GT0]
    z3 = z.reshape(NSEQ, R, CONV_CH)
    z_s[...] = z3
    z_s[:, R - (CONV_W - 1):, :] = cv_ref[...]
    zf = z_s[...].reshape(TS, CONV_CH)
    zm1 = jnp.where(pos >= 1, pltpu.roll(z, 1, 0), pltpu.roll(zf, TS - (R - 1), 0))
    zm2 = jnp.where(pos >= 2, pltpu.roll(z, 2, 0), pltpu.roll(zf, TS - (R - 2), 0))
    yc = u[:, _BG0:_CG0] * (cw_ref[0:1, :] * zm2 + cw_ref[1:2, :] * zm1 + cw_ref[2:3, :] * z)
    cvo_ref[...] = z3[:, SAMPLE_T - (CONV_W - 1):SAMPLE_T, :]

    gates = u[:, _GT0:PROJ_COLS] + gb_ref[...]
    ic = jnp.where(real, GATE_CAP * jnp.tanh(gates[:, :LANES] / GATE_CAP), NEG_BIG)
    lf = jnp.where(real, _log_sigmoid(gates[:, LANES:]), 0.0)
    b = _row_scan(lf, R, jnp.add, 0.0)
    uu = ic - b
    m_loc = b + _row_scan(uu, R, jnp.maximum, -jnp.inf)

    mp_s[...] = jnp.zeros(mp_s.shape, f32)
    mp_s[:, :, 0:NH] = m0_ref[...]
    m_prev = mp_s[...]
    b3 = b.reshape(NSEQ, R, LANES)
    ml3 = m_loc.reshape(NSEQ, R, LANES)
    b_last = b3[:, R - 1:R, :]
    m_new = jnp.maximum(b_last + m_prev, ml3[:, R - 1:R, :])
    g3 = b3 + m_prev
    mt3 = jnp.maximum(g3, ml3)
    wi = jnp.exp(g3 - mt3).reshape(TS, LANES)
    c2 = (mt3 - b3).reshape(TS, LANES)
    emt = jnp.exp(-mt3).reshape(TS, LANES)
    wk = jnp.exp(b_last - b3 + ic.reshape(NSEQ, R, LANES) - m_new).reshape(TS, LANES)
    a_st = jnp.broadcast_to(jnp.exp(b_last + m_prev - m_new), (NSEQ, R, LANES)).reshape(TS, LANES)
    mst_ref[...] = m_new[:, :, 0:NH]

    q = u[:, _Q0:_K0] * (DK ** -0.5)
    k = u[:, _K0:_V0]
    v = u[:, _V0:_OG0]
    rs = jnp.zeros((TS, LANES), f32)
    num = jnp.zeros((TS, MLSTM_W), f32)
    for d in range(SAMPLE_T):
        kd = k if d == 0 else pltpu.roll(k, d, 0)
        vd = v if d == 0 else pltpu.roll(v, d, 0)
        ud = uu if d == 0 else pltpu.roll(uu, d, 0)
        p = _split_dot(q * kd, segqk_ref[...], 2) * jnp.exp(ud - c2)
        rs = rs + p
        num = num + _split_dot(p, expv_ref[...], 2) * vd

    n0x = jnp.broadcast_to(n0_ref[...], (NSEQ, R, QK_W)).reshape(TS, QK_W)
    qn = _split_dot(q * n0x, segqk_ref[...], 2)
    den = wi * qn + rs
    rden = 1.0 / jnp.maximum(jnp.abs(den), emt)
    kw = k * _split_dot(wk, expk_ref[...], 2)
    ax = _split_dot(a_st, expv_ref[...], 3)
    lane_q = lax.broadcasted_iota(jnp.int32, (R, QK_W), 1)
    for i in range(NSEQ):
        rows = slice(i * R, (i + 1) * R)
        qi = q[rows, :]
        lhs = jnp.concatenate(
            [jnp.where((lane_q >= h * DK) & (lane_q < (h + 1) * DK), qi, 0.0) for h in range(NH)], axis=0)
        c0 = c0_ref[i]
        qc = _bdot(lhs.astype(bf16), c0.reshape(QK_W, DV).astype(bf16))
        qc_s[rows, :] = jnp.concatenate([qc[h * R:(h + 1) * R, :] for h in range(NH)], axis=1)
        kwi = kw[rows, :].astype(bf16)
        vi = v[rows, :].astype(bf16)
        for h in range(NH):
            dc = lax.dot_general(kwi[:, h * DK:(h + 1) * DK], vi[:, h * DV:(h + 1) * DV],
                                 (((0,), (0,)), ((), ())), preferred_element_type=f32)
            cst_ref[i, h] = ax[i * R:i * R + 1, h * DV:(h + 1) * DV] * c0[h] + dc

    hh = (_split_dot(wi, expv_ref[...], 2) * qc_s[...] + num) * _split_dot(rden, expv_ref[...], 2)
    ms = _split_dot(hh * hh, meanv_ref[...], 2)
    hm = jax.nn.sigmoid(u[:, _OG0:_BG0]) * (hh * lax.rsqrt(ms + EPS) * gmh_ref[...])
    mix = jnp.concatenate([hm, yc], axis=1).astype(bf16)
    out = x + _bdot(mix, wout_ref[...])
    x1_ref[...] = out.reshape(NSEQ, R, D_MODEL)[:, 0:SAMPLE_T, :]

    a_k = _split_dot(a_st, expk_ref[...], 3).reshape(NSEQ, R, QK_W)[:, 0:1, :]
    nst_ref[...] = a_k * n0_ref[...] + jnp.sum(kw.reshape(NSEQ, R, QK_W), axis=1, keepdims=True)


def _mixer_sample(x, cv, c0, n0, m0, weights, NSEQ):
    Bs = x.shape[0]
    f32 = jnp.float32
    kern = functools.partial(_mixer_sample_kernel, NSEQ=NSEQ)
    x_spec = pl.BlockSpec((NSEQ, SAMPLE_T, D_MODEL), lambda t: (t, 0, 0))
    cv_spec = pl.BlockSpec((NSEQ, CONV_W - 1, CONV_CH), lambda t: (t, 0, 0))
    c_spec = pl.BlockSpec((NSEQ, NH, DK, DV), lambda t: (t, 0, 0, 0))
    n_spec = pl.BlockSpec((NSEQ, 1, QK_W), lambda t: (t, 0, 0))
    m_spec = pl.BlockSpec((NSEQ, 1, NH), lambda t: (t, 0, 0))
    return pl.pallas_call(
        kern,
        grid=(Bs // NSEQ,),
        in_specs=[
            x_spec, cv_spec, c_spec, n_spec, m_spec,
            _const_spec((1, D_MODEL)),
            _const_spec((D_MODEL, PROJ_COLS)),
            _const_spec((1, 2 * LANES)),
            _const_spec((1, MLSTM_W)),
            _const_spec((CONV_W, CONV_CH)),
            _const_spec((D_MODEL, D_MODEL)),
            _const_spec((QK_W, LANES)),
            _const_spec((LANES, MLSTM_W)),
            _const_spec((LANES, QK_W)),
            _const_spec((MLSTM_W, MLSTM_W)),
        ],
        out_specs=[x_spec, c_spec, n_spec, m_spec, cv_spec],
        out_shape=[
            jax.ShapeDtypeStruct((Bs, SAMPLE_T, D_MODEL), f32),
            jax.ShapeDtypeStruct((Bs, NH, DK, DV), f32),
            jax.ShapeDtypeStruct((Bs, 1, QK_W), f32),
            jax.ShapeDtypeStruct((Bs, 1, NH), f32),
            jax.ShapeDtypeStruct((Bs, CONV_W - 1, CONV_CH), f32),
        ],
        scratch_shapes=[
            pltpu.VMEM((NSEQ, SUBLANES, D_MODEL), f32),
            pltpu.VMEM((NSEQ, SUBLANES, CONV_CH), f32),
            pltpu.VMEM((NSEQ, 1, LANES), f32),
            pltpu.VMEM((NSEQ * SUBLANES, MLSTM_W), f32),
        ],
        compiler_params=pltpu.CompilerParams(
            dimension_semantics=("arbitrary",), vmem_limit_bytes=VMEM_LIMIT),
        name="mixer_sample",
    )(x, cv, c0, n0, m0, *weights)


def _mixer_weights(g_mix, w_in, b_ig, b_fg, g_mh, conv_w, w_out):
    f32 = jnp.float32
    w_all, wt = _prep_weights(w_in)
    wout = w_out.astype(jnp.bfloat16)
    gmix = g_mix.reshape(1, D_MODEL)
    zeros = jnp.zeros((SUBLANES - NH,), f32)
    gb_col = jnp.concatenate([b_ig, zeros, b_fg, zeros]).reshape(2 * SUBLANES, 1)
    zeros = jnp.zeros((LANES - NH,), f32)
    gb_row = jnp.concatenate([b_ig, zeros, b_fg, zeros]).reshape(1, 2 * LANES)
    gmhc = jnp.broadcast_to(g_mh[:, None], (MLSTM_W, LANES))
    prompt = (gmix, w_all, wt, gb_col, gmhc, conv_w, wout)
    sample = (gmix, w_all, gb_row, g_mh.reshape(1, MLSTM_W), conv_w, wout) + _selectors()
    return prompt, sample


FF_STREAM = 256
_N_FF_PIECES = D_FF // FF_STREAM
_N_PIECES = _N_FF_PIECES + PLE_DIM // FF_STREAM + D_MODEL // FF_STREAM


def _ffn_rows(x, pe, gffn_ref, gple_ref, gfin_ref, wg_s, wu_s, wd_s, wple_s, wpg_s, *, chunk, final_norm,
              before_chunk=None, before_tail=None):
    bf16 = jnp.bfloat16
    f = _rms(x, gffn_ref[...]).astype(bf16)
    for c0 in range(0, D_FF, chunk):
        c1 = min(c0 + chunk, D_FF)
        if before_chunk is not None:
            before_chunk(c0 // chunk)
        gate = _bdot(f, wg_s[:, c0:c1])
        up = _bdot(f, wu_s[:, c0:c1])
        hmid = (gate * jax.nn.sigmoid(gate) * up).astype(bf16)
        x = x + _bdot(hmid, wd_s[c0:c1, :])
    if before_tail is not None:
        before_tail()
    e = _rms(_bdot(pe.astype(bf16), wple_s[...]), gple_ref[...])
    x = x + jax.nn.sigmoid(_bdot(x.astype(bf16), wpg_s[...])) * e
    if final_norm:
        x = _rms(x, gfin_ref[...])
    return x


def _ffn_stream_kernel(xp_ref, pp_ref, xs_ref, ps_ref, gffn_ref, gple_ref, gfin_ref,
                       wg_hbm, wu_hbm, wd_hbm, wple_hbm, wpg_hbm,
                       yp_ref, ys_ref,
                       wg_s, wu_s, wd_s, wple_s, wpg_s, stg_g, stg_u, stg_d, sem, *, final_norm):
    t = pl.program_id(0)
    bf16 = jnp.bfloat16
    weights = (wg_s, wu_s, wd_s, wple_s, wpg_s)

    def copies(k):
        slot = k % 2
        if k < _N_FF_PIECES:
            cols = pl.ds(k * FF_STREAM, FF_STREAM)
            return [pltpu.make_async_copy(wg_hbm.at[:, cols], stg_g.at[slot], sem.at[0, slot]),
                    pltpu.make_async_copy(wu_hbm.at[:, cols], stg_u.at[slot], sem.at[1, slot]),
                    pltpu.make_async_copy(wd_hbm.at[cols, :], stg_d.at[slot], sem.at[2, slot])]
        if k == _N_FF_PIECES:
            return [pltpu.make_async_copy(wple_hbm, stg_d.at[slot], sem.at[2, slot])]
        rows = pl.ds((k - _N_FF_PIECES - 1) * FF_STREAM, FF_STREAM)
        return [pltpu.make_async_copy(wpg_hbm.at[rows, :], stg_d.at[slot], sem.at[2, slot])]

    def land(k):
        if k + 1 < _N_PIECES:
            for cp in copies(k + 1):
                cp.start()
        for cp in copies(k):
            cp.wait()
        slot = k % 2
        if k < _N_FF_PIECES:
            cols = slice(k * FF_STREAM, (k + 1) * FF_STREAM)
            wg_s[:, cols] = stg_g[slot].astype(bf16)
            wu_s[:, cols] = stg_u[slot].astype(bf16)
            wd_s[cols, :] = stg_d[slot].astype(bf16)
        elif k == _N_FF_PIECES:
            wple_s[...] = stg_d[slot].astype(bf16)
        else:
            r0 = (k - _N_FF_PIECES - 1) * FF_STREAM
            wpg_s[r0:r0 + FF_STREAM, :] = stg_d[slot].astype(bf16)

    @pl.when(t == 0)
    def _():
        for cp in copies(0):
            cp.start()
        n_tok = xs_ref.shape[0] * xs_ref.shape[1]

        def tail():
            for k in range(_N_FF_PIECES, _N_PIECES):
                land(k)

        y = _ffn_rows(xs_ref[...].reshape(n_tok, D_MODEL), ps_ref[...].reshape(n_tok, PLE_DIM),
                      gffn_ref, gple_ref, gfin_ref, *weights, chunk=FF_STREAM, final_norm=final_norm,
                      before_chunk=land, before_tail=tail)
        ys_ref[...] = y.reshape(ys_ref.shape)

    @pl.when(t > 0)
    def _():
        yp_ref[...] = _ffn_rows(xp_ref[...], pp_ref[...], gffn_ref, gple_ref, gfin_ref, *weights,
                                chunk=FF_CHUNK, final_norm=final_norm)


def _ffn_stream(xp, pp, xs, ps, g_ffn, g_ple, g_final, w_gate, w_up, w_down, w_ple, w_pg, TM, final_norm):
    T = xp.shape[0]
    f32, bf16 = jnp.float32, jnp.bfloat16
    kern = functools.partial(_ffn_stream_kernel, final_norm=final_norm)
    row_map = lambda t: (jnp.maximum(t - 1, 0), 0)
    hbm = pl.BlockSpec(memory_space=pl.ANY)
    return pl.pallas_call(
        kern,
        grid=(T // TM + 1,),
        in_specs=[
            pl.BlockSpec((TM, D_MODEL), row_map),
            pl.BlockSpec((TM, PLE_DIM), row_map),
            _const_spec(xs.shape),
            _const_spec(ps.shape),
            _const_spec((1, D_MODEL)),
            _const_spec((1, D_MODEL)),
            _const_spec((1, D_MODEL)),
            hbm, hbm, hbm, hbm, hbm,
        ],
        out_specs=[pl.BlockSpec((TM, D_MODEL), row_map),
                   pl.BlockSpec(xs.shape, lambda t: (0, 0, 0))],
        out_shape=[jax.ShapeDtypeStruct(xp.shape, f32), jax.ShapeDtypeStruct(xs.shape, f32)],
        scratch_shapes=[
            pltpu.VMEM((D_MODEL, D_FF), bf16),
            pltpu.VMEM((D_MODEL, D_FF), bf16),
            pltpu.VMEM((D_FF, D_MODEL), bf16),
            pltpu.VMEM((PLE_DIM, D_MODEL), bf16),
            pltpu.VMEM((D_MODEL, D_MODEL), bf16),
            pltpu.VMEM((2, D_MODEL, FF_STREAM), f32),
            pltpu.VMEM((2, D_MODEL, FF_STREAM), f32),
            pltpu.VMEM((2, FF_STREAM, D_MODEL), f32),
            pltpu.SemaphoreType.DMA((3, 2)),
        ],
        compiler_params=pltpu.CompilerParams(
            dimension_semantics=("arbitrary",), vmem_limit_bytes=VMEM_LIMIT),
        name="ffn",
    )(xp, pp, xs, ps, g_ffn.reshape(1, D_MODEL), g_ple.reshape(1, D_MODEL), g_final.reshape(1, D_MODEL),
      w_gate, w_up, w_down, w_ple, w_pg)


def kernel(x_prompt, x_sample, p_prompt, p_sample, state_C, state_n, state_m, state_conv,
           g_mix, w_in, b_ig, b_fg, g_mh, conv_w, w_out, g_ffn, w_gate, w_up, w_down,
           w_ple, g_ple, w_pg, g_final):
    bf16 = jnp.bfloat16
    depth = g_mix.shape[0]
    B, S, _ = x_prompt.shape
    Bs, Ss, _ = x_sample.shape
    assert Ss == SAMPLE_T and S % PROMPT_TS == 0 and Bs % SAMPLE_NSEQ == 0

    xp = x_prompt
    xs = x_sample
    outs = [[] for _ in range(8)]
    for i in range(depth):
        last = i == depth - 1
        pw, sw = _mixer_weights(g_mix[i], w_in[i], b_ig[i], b_fg[i], g_mh[i], conv_w[i], w_out[i])
        x1p, cp, n_p, mp, cvp = _mixer_prompt(xp, pw, TB=PROMPT_TS, NSUB=PROMPT_NSUB, L=PROMPT_L)
        x1s, cs, n_s, ms, cvs = _mixer_sample(
            xs, state_conv[i], state_C[i], state_n[i].reshape(Bs, 1, QK_W),
            state_m[i].reshape(Bs, 1, NH), sw, SAMPLE_NSEQ)
        xp, xs = _ffn_stream(x1p.reshape(B * S, D_MODEL), p_prompt[i].reshape(B * S, PLE_DIM), x1s,
                             p_sample[i], g_ffn[i], g_ple[i], g_final, w_gate[i], w_up[i], w_down[i],
                             w_ple[i], w_pg[i], FFN_TM, last)
        xp = xp.reshape(B, S, D_MODEL)

        new = (cp, n_p.reshape(B, NH, DK), mp[:, :NH, 0], cvp[:, SUBLANES - (CONV_W - 1):],
               cs, n_s.reshape(Bs, NH, DK), ms.reshape(Bs, NH), cvs)
        for lst, v in zip(outs, new):
            lst.append(v)

    return (xp, xs) + tuple(jnp.stack(l) for l in outs)
```

```python
import functools

import numpy as np
import jax
import jax.numpy as jnp
from jax import lax
from jax.experimental import pallas as pl
from jax.experimental.pallas import tpu as pltpu

D_MODEL = 1024
NH = 4
DK = 64
DV = 128
MLSTM_W = NH * DV
QK_W = NH * DK
CONV_CH = 512
CONV_W = 3
D_FF = 2816
FF_CHUNK = 512
FFN_TM = 512
PLE_DIM = 256
PROMPT_L = 128
PROMPT_TS = 512
PROMPT_NSUB = 2
SAMPLE_T = 4
SAMPLE_NSEQ = 32
GATE_CAP = 15.0
EPS = 1e-6

LANES = 128
SUBLANES = 8
NEG_BIG = -1e30

VMEM_LIMIT = 56 * 1024 * 1024


def _rms(x, g):
    ms = jnp.mean(x * x, axis=-1, keepdims=True)
    return x * lax.rsqrt(ms + EPS) * g


def _bdot(a, b):
    return jnp.dot(a, b, preferred_element_type=jnp.float32)


def _log_sigmoid(x):
    return jnp.minimum(x, 0.0) - jnp.log1p(jnp.exp(-jnp.abs(x)))


def _const_spec(shape):
    nd = len(shape)
    return pl.BlockSpec(shape, lambda *_: (0,) * nd, pipeline_mode=pl.Buffered(1))


_IG0 = 2 * QK_W + MLSTM_W
_OG_IN = _IG0 + 2 * NH
PROJ_IN = _OG_IN + MLSTM_W + 3 * CONV_CH
_Q0, _K0, _V0, _OG0, _BG0, _CG0, _HC0, _GT0 = 0, 256, 512, 1024, 1536, 2048, 2560, 3072
PROJ_COLS = _GT0 + 2 * LANES
_TQ0, _TV0, _TOG0, _TG0 = 0, 256, 768, 1280
TPROJ_ROWS = _TG0 + 2 * SUBLANES
PREP_TK = 256


def _prep_weights_kernel(w_ref, wo_ref, wall_ref, wt_ref, wout_ref):
    bf16 = jnp.bfloat16
    wout_ref[...] = wo_ref[...].astype(bf16)
    wf = w_ref[...]
    tk = wf.shape[1]
    g8 = wf[_IG0:_OG_IN]
    row = lax.broadcasted_iota(jnp.int32, (SUBLANES, tk), 0)
    g_in = jnp.where(row < NH, g8, 0.0)
    g_fg = jnp.where(row < NH, pltpu.roll(g8, SUBLANES - NH, 0), 0.0)

    wt_ref[_TQ0:_TV0, :] = wf[_Q0:_K0].astype(bf16)
    wt_ref[_TV0:_TOG0, :] = wf[_V0:_OG0].astype(bf16)
    wt_ref[_TOG0:_TG0, :] = wf[_OG_IN:_OG_IN + MLSTM_W].astype(bf16)
    wt_ref[_TG0:TPROJ_ROWS, :] = jnp.concatenate([g_in, g_fg], axis=0).astype(bf16)

    wall_ref[:, 0:_OG0] = wf[0:_IG0].T.astype(bf16)
    wall_ref[:, _OG0:_GT0] = wf[_OG_IN:PROJ_IN].T.astype(bf16)
    pad = jnp.zeros((LANES - SUBLANES, tk), jnp.float32)
    wall_ref[:, _GT0:_GT0 + LANES] = jnp.concatenate([g_in, pad], axis=0).T.astype(bf16)
    wall_ref[:, _GT0 + LANES:PROJ_COLS] = jnp.concatenate([g_fg, pad], axis=0).T.astype(bf16)


def _prep_weights(w_in, w_out):
    bf16 = jnp.bfloat16
    return pl.pallas_call(
        _prep_weights_kernel,
        grid=(D_MODEL // PREP_TK,),
        in_specs=[pl.BlockSpec((PROJ_IN, PREP_TK), lambda i: (0, i)),
                  pl.BlockSpec((PREP_TK, D_MODEL), lambda i: (i, 0))],
        out_specs=[pl.BlockSpec((PREP_TK, PROJ_COLS), lambda i: (i, 0)),
                   pl.BlockSpec((TPROJ_ROWS, PREP_TK), lambda i: (0, i)),
                   pl.BlockSpec((PREP_TK, D_MODEL), lambda i: (i, 0))],
        out_shape=[jax.ShapeDtypeStruct((D_MODEL, PROJ_COLS), bf16),
                   jax.ShapeDtypeStruct((TPROJ_ROWS, D_MODEL), bf16),
                   jax.ShapeDtypeStruct((D_MODEL, D_MODEL), bf16)],
        compiler_params=pltpu.CompilerParams(
            dimension_semantics=("arbitrary",), vmem_limit_bytes=VMEM_LIMIT),
        name="prep_weights",
    )(w_in.T, w_out)


ST_ROWS = DV + 2 * SUBLANES


def _lane_scan(x, seg, op, fill):
    pos = lax.broadcasted_iota(jnp.int32, x.shape, 1) & (seg - 1)
    k = 1
    while k < seg:
        x = op(x, jnp.where(pos >= k, pltpu.roll(x, k, 1), fill))
        k *= 2
    return x


PROJ_PIECE = 256


def _run(*phases):
    live = list(phases)
    while live:
        for g in list(live):
            try:
                next(g)
            except StopIteration:
                live.remove(g)


def _mixer_prompt_kernel(x_ref, gmix_ref, wn_ref, wt_ref, gbt_ref, gmhc_ref, cw_ref, wout_ref,
                         x1_ref, c_ref, n_ref, m_ref, cvo_ref,
                         qt_s, vt_s, ogt_s, k_s, mixt_s, z_s, ucol_s, wi_s, c2_s, emt_s, wk_s, ast_s, st_s,
                         *, L, TB, NSUB):
    j = pl.program_id(1)

    @pl.when(j == 0)
    def _():
        z_s[0, 0:SUBLANES, :] = jnp.zeros((SUBLANES, CONV_CH), jnp.float32)
        st_s[...] = jnp.zeros(st_s.shape, jnp.float32)
        m_ref[...] = jnp.zeros(m_ref.shape, jnp.float32)

    m_prev = m_ref[0]
    sub = [dict(sb=sb, x_ref=x_ref, gmix_ref=gmix_ref, wn_ref=wn_ref, wt_ref=wt_ref, gbt_ref=gbt_ref,
                gmhc_ref=gmhc_ref, cw_ref=cw_ref, wout_ref=wout_ref, x1_ref=x1_ref, qt_s=qt_s.at[sb],
                vt_s=vt_s.at[sb], ogt_s=ogt_s.at[sb], k_s=k_s.at[sb], mixt_s=mixt_s.at[sb], z_s=z_s,
                ucol_s=ucol_s.at[sb], wi_s=wi_s.at[sb], c2_s=c2_s.at[sb], emt_s=emt_s.at[sb],
                wk_s=wk_s.at[sb], ast_s=ast_s.at[sb], st_s=st_s, L=L, TS=TB) for sb in range(NSUB)]
    carry = {"m": m_prev}
    _run(_prompt_project(sub[0]))
    for a, b in zip(sub[:-1], sub[1:]):
        _run(_prompt_chunks(a, carry), _prompt_project(b))
    _run(_prompt_chunks(sub[-1], carry), *[_prompt_output(s) for s in sub[:-1]])
    _run(_prompt_output(sub[-1]))
    m_ref[0] = carry["m"]
    cvo_ref[0] = z_s[NSUB - 1, TB:TB + SUBLANES, :]
    z_s[0, 0:SUBLANES, :] = z_s[NSUB - 1, TB:TB + SUBLANES, :]

    @pl.when(j == pl.num_programs(1) - 1)
    def _():
        for pr in range(NH // 2):
            state = st_s[pr]
            c_pair = state[0:DV, :].T
            c_ref[0, 2 * pr] = c_pair[0:DK, :]
            c_ref[0, 2 * pr + 1] = c_pair[DK:, :]
            n_ref[0, pr:pr + 1, :] = state[DV:DV + 1, :]


def _prompt_project(s):
    sb, L, TS = s["sb"], s["L"], s["TS"]
    x_ref, gmix_ref, wn_ref, wt_ref, gbt_ref, cw_ref = (
        s["x_ref"], s["gmix_ref"], s["wn_ref"], s["wt_ref"], s["gbt_ref"], s["cw_ref"])
    qt_s, vt_s, ogt_s, k_s, z_s = s["qt_s"], s["vt_s"], s["ogt_s"], s["k_s"], s["z_s"]
    bf16 = jnp.bfloat16

    x = x_ref[0, sb * TS:(sb + 1) * TS, :]
    a = _rms(x, gmix_ref[...]).astype(bf16)
    nt_dims = (((1,), (1,)), ((), ()))

    gt = lax.dot_general(wt_ref[_TG0:TPROJ_ROWS, :], a, nt_dims,
                         preferred_element_type=jnp.float32) + gbt_ref[...]
    ic = GATE_CAP * jnp.tanh(gt[0:SUBLANES] / GATE_CAP)
    lf = _log_sigmoid(gt[SUBLANES:])
    b = _lane_scan(lf, L, jnp.add, 0.0)
    u = ic - b
    m_loc = b + _lane_scan(u, L, jnp.maximum, -jnp.inf)
    s.update(b=b, u=u, ic=ic, m_loc=m_loc)
    yield

    k_s[...] = _bdot(a, wn_ref[:, _K0:_V0])
    yield
    hc = _bdot(a, wn_ref[:, _HC0:_GT0])
    yield
    z = _bdot(a, wn_ref[:, _CG0:_HC0]) * hc
    if sb > 0:
        z_s[sb, 0:SUBLANES, :] = z_s[sb - 1, TS:TS + SUBLANES, :]
    z_s[sb, SUBLANES:SUBLANES + TS, :] = z
    yield
    yconv = (cw_ref[0:1, :] * z_s[sb, SUBLANES - 2:SUBLANES - 2 + TS, :]
             + cw_ref[1:2, :] * z_s[sb, SUBLANES - 1:SUBLANES - 1 + TS, :]
             + cw_ref[2:3, :] * z)
    s["yc"] = (_bdot(a, wn_ref[:, _BG0:_CG0]) * yconv).astype(bf16)
    yield

    for r0 in range(0, _TG0, PROJ_PIECE):
        ut = lax.dot_general(wt_ref[r0:r0 + PROJ_PIECE, :], a, nt_dims, preferred_element_type=jnp.float32)
        if r0 < _TV0:
            qt_s[r0:r0 + PROJ_PIECE, :] = ut * (DK ** -0.5)
        elif r0 < _TOG0:
            vt_s[r0 - _TV0:r0 - _TV0 + PROJ_PIECE, :] = ut
        else:
            ogt_s[r0 - _TOG0:r0 - _TOG0 + PROJ_PIECE, :] = ut
        yield


def _prompt_chunks(s, carry):
    L, TS = s["L"], s["TS"]
    qt_s, vt_s, ogt_s, k_s, mixt_s, ucol_s, wi_s, c2_s, emt_s, wk_s, ast_s, st_s = (
        s["qt_s"], s["vt_s"], s["ogt_s"], s["k_s"], s["mixt_s"], s["ucol_s"], s["wi_s"],
        s["c2_s"], s["emt_s"], s["wk_s"], s["ast_s"], s["st_s"])
    b, u, ic, m_loc = s["b"], s["u"], s["ic"], s["m_loc"]
    nc = TS // L
    bf16 = jnp.bfloat16
    m_prev = carry["m"]

    for c in range(nc):
        sl = slice(c * L, (c + 1) * L)
        bc, mlc = b[:, sl], m_loc[:, sl]
        b_last = jnp.broadcast_to(bc[:, L - 1:L], bc.shape)
        m_new = jnp.maximum(b_last + m_prev, jnp.broadcast_to(mlc[:, L - 1:L], bc.shape))
        g = bc + m_prev
        mt = jnp.maximum(g, mlc)
        wi_s[:, sl] = jnp.exp(g - mt)
        c2_s[:, sl] = mt - bc
        emt_s[:, sl] = jnp.exp(-mt)
        wk_s[:, sl] = jnp.exp(b_last - bc + ic[:, sl] - m_new)
        ast_s[c] = jnp.exp(b_last + m_prev - m_new)
        upad = jnp.concatenate([u[:, sl], jnp.zeros((L - SUBLANES, L), jnp.float32)], axis=0)
        ucol_s[c * L:(c + 1) * L, :] = upad.T
        m_prev = m_new
    carry["m"] = m_prev
    yield

    s_i = lax.broadcasted_iota(jnp.int32, (L, L), 0)
    t_i = lax.broadcasted_iota(jnp.int32, (L, L), 1)
    causal = s_i <= t_i
    low_half = lax.broadcasted_iota(jnp.int32, (L, LANES), 1) < DK
    zeros_q = jnp.zeros((DK, L), jnp.float32)
    zeros_p = jnp.zeros((L, L), bf16)

    def pair_row(ref, h0, rows):
        return jnp.concatenate([ref[h0:h0 + 1, rows], ref[h0 + 1:h0 + 2, rows]], axis=1)

    for c in range(nc):
        rows = slice(c * L, (c + 1) * L)
        for pr in range(NH // 2):
            h0 = 2 * pr
            hv0 = slice(h0 * DV, (h0 + 1) * DV)
            hv1 = slice((h0 + 1) * DV, (h0 + 2) * DV)
            kp = k_s[rows, pr * LANES:(pr + 1) * LANES]
            q0 = qt_s[h0 * DK:(h0 + 1) * DK, rows]
            q1 = qt_s[(h0 + 1) * DK:(h0 + 2) * DK, rows]
            qbd = jnp.concatenate([jnp.concatenate([q0, zeros_q], axis=1),
                                   jnp.concatenate([zeros_q, q1], axis=1)], axis=0).astype(bf16)
            st = _bdot(kp.astype(bf16), qbd)
            arg = jnp.concatenate(
                [jnp.where(causal, ucol_s[rows, h:h + 1] - c2_s[h:h + 1, rows], -jnp.inf)
                 for h in (h0, h0 + 1)], axis=1)
            pt = st * jnp.exp(arg)
            rs = jnp.sum(pt, axis=0, keepdims=True)
            ptb = pt.astype(bf16)
            pbd = jnp.concatenate([jnp.concatenate([ptb[:, 0:L], zeros_p], axis=1),
                                   jnp.concatenate([zeros_p, ptb[:, L:]], axis=1)], axis=0)
            vt = jnp.concatenate([vt_s[hv0, rows], vt_s[hv1, rows]], axis=1)
            state = st_s[pr]
            sq = _bdot(state.astype(bf16), qbd) * pair_row(wi_s, h0, rows)
            num = _bdot(vt.astype(bf16), pbd) + sq[0:DV]
            den = sq[DV:DV + 1] + rs
            hh = num * (1.0 / jnp.maximum(jnp.abs(den), pair_row(emt_s, h0, rows)))
            hn = hh * lax.rsqrt(jnp.mean(hh * hh, axis=0, keepdims=True) + EPS)
            mixt_s[hv0, rows] = jax.nn.sigmoid(ogt_s[hv0, rows]) * hn[:, 0:L]
            mixt_s[hv1, rows] = jax.nn.sigmoid(ogt_s[hv1, rows]) * hn[:, L:]
            wkr = pair_row(wk_s, h0, rows)
            vw = jnp.concatenate([vt * wkr, jnp.broadcast_to(wkr, (2 * SUBLANES, 2 * L))], axis=0)
            km = jnp.concatenate([jnp.where(low_half, kp, 0.0), jnp.where(low_half, 0.0, kp)], axis=0)
            decay = jnp.where(low_half[0:1, :], ast_s[c][h0:h0 + 1, :], ast_s[c][h0 + 1:h0 + 2, :])
            st_s[pr] = decay * state + _bdot(vw.astype(bf16), km.astype(bf16))
            yield


def _prompt_output(s):
    sb, TS = s["sb"], s["TS"]
    x_ref, wout_ref, x1_ref, mixt_s = s["x_ref"], s["wout_ref"], s["x1_ref"], s["mixt_s"]
    tok = slice(sb * TS, (sb + 1) * TS)
    hm = mixt_s[...].T * s["gmhc_ref"][...]
    mix = jnp.concatenate([hm.astype(jnp.bfloat16), s["yc"]], axis=1)
    yield
    for c0 in range(0, D_MODEL, PROJ_PIECE):
        cols = slice(c0, c0 + PROJ_PIECE)
        x1_ref[0, tok, cols] = x_ref[0, tok, cols] + _bdot(mix, wout_ref[:, cols])
        yield


def _mixer_prompt(x, weights, TB, NSUB, L):
    B, S, _ = x.shape
    TS = TB * NSUB
    nb = S // TS
    f32 = jnp.float32
    kern = functools.partial(_mixer_prompt_kernel, L=L, TB=TB, NSUB=NSUB)
    return pl.pallas_call(
        kern,
        grid=(B, nb),
        in_specs=[
            pl.BlockSpec((1, TS, D_MODEL), lambda b, j: (b, j, 0)),
            _const_spec((1, D_MODEL)),
            _const_spec((D_MODEL, PROJ_COLS)),
            _const_spec((TPROJ_ROWS, D_MODEL)),
            _const_spec((2 * SUBLANES, 1)),
            _const_spec((1, MLSTM_W)),
            _const_spec((CONV_W, CONV_CH)),
            _const_spec((D_MODEL, D_MODEL)),
        ],
        out_specs=[
            pl.BlockSpec((1, TS, D_MODEL), lambda b, j: (b, j, 0)),
            pl.BlockSpec((1, NH, DK, DV), lambda b, j: (b, 0, 0, 0)),
            pl.BlockSpec((1, NH // 2, 2 * DK), lambda b, j: (b, 0, 0)),
            pl.BlockSpec((1, SUBLANES, LANES), lambda b, j: (b, 0, 0)),
            pl.BlockSpec((1, SUBLANES, CONV_CH), lambda b, j: (b, 0, 0)),
        ],
        out_shape=[
            jax.ShapeDtypeStruct((B, S, D_MODEL), f32),
            jax.ShapeDtypeStruct((B, NH, DK, DV), f32),
            jax.ShapeDtypeStruct((B, NH // 2, 2 * DK), f32),
            jax.ShapeDtypeStruct((B, SUBLANES, LANES), f32),
            jax.ShapeDtypeStruct((B, SUBLANES, CONV_CH), f32),
        ],
        scratch_shapes=[
            pltpu.VMEM((NSUB, QK_W, TB), f32),
            pltpu.VMEM((NSUB, MLSTM_W, TB), f32),
            pltpu.VMEM((NSUB, MLSTM_W, TB), f32),
            pltpu.VMEM((NSUB, TB, QK_W), f32),
            pltpu.VMEM((NSUB, MLSTM_W, TB), f32),
            pltpu.VMEM((NSUB, TB + SUBLANES, CONV_CH), f32),
            pltpu.VMEM((NSUB, TB, LANES), f32),
            pltpu.VMEM((NSUB, SUBLANES, TB), f32),
            pltpu.VMEM((NSUB, SUBLANES, TB), f32),
            pltpu.VMEM((NSUB, SUBLANES, TB), f32),
            pltpu.VMEM((NSUB, SUBLANES, TB), f32),
            pltpu.VMEM((NSUB, TB // L, SUBLANES, LANES), f32),
            pltpu.VMEM((NH // 2, ST_ROWS, LANES), f32),
        ],
        compiler_params=pltpu.CompilerParams(
            dimension_semantics=("arbitrary", "arbitrary"), vmem_limit_bytes=VMEM_LIMIT),
        name="mixer_prompt",
    )(x, *weights)


def _row_scan(x, seg, op, fill):
    pos = lax.broadcasted_iota(jnp.int32, x.shape, 0) & (seg - 1)
    k = 1
    while k < seg:
        x = op(x, jnp.where(pos >= k, pltpu.roll(x, k, 0), fill))
        k *= 2
    return x


def _split_dot(x, sel, parts):
    acc = None
    rem = x
    for p in range(parts):
        hi = rem.astype(jnp.bfloat16)
        d = _bdot(hi, sel)
        acc = d if acc is None else acc + d
        if p + 1 < parts:
            rem = rem - hi.astype(jnp.float32)
    return acc


def _selectors():
    h = np.arange(NH)
    seg_qk = np.zeros((QK_W, LANES), np.float32)
    seg_qk[np.arange(QK_W), np.arange(QK_W) // DK] = 1.0
    exp_v = np.zeros((LANES, MLSTM_W), np.float32)
    exp_k = np.zeros((LANES, QK_W), np.float32)
    for i in h:
        exp_v[i, i * DV:(i + 1) * DV] = 1.0
        exp_k[i, i * DK:(i + 1) * DK] = 1.0
    mean_v = np.kron(np.eye(NH, dtype=np.float32), np.full((DV, DV), 1.0 / DV, np.float32))
    return tuple(jnp.asarray(m, jnp.bfloat16) for m in (seg_qk, exp_v, exp_k, mean_v))


def _mixer_sample_kernel(x_ref, cv_ref, c0_ref, n0_ref, m0_ref,
                         gmix_ref, win_ref, gb_ref, gmh_ref, cw_ref, wout_ref,
                         segqk_ref, expv_ref, expk_ref, meanv_ref,
                         x1_ref, cst_ref, nst_ref, mst_ref, cvo_ref,
                         xp_s, z_s, mp_s, qc_s, *, NSEQ):
    R = SUBLANES
    TS = NSEQ * R
    bf16 = jnp.bfloat16
    f32 = jnp.float32

    xp_s[:, SAMPLE_T:, :] = jnp.zeros((NSEQ, R - SAMPLE_T, D_MODEL), f32)
    xp_s[:, 0:SAMPLE_T, :] = x_ref[...]
    x = xp_s[...].reshape(TS, D_MODEL)
    a = _rms(x, gmix_ref[...]).astype(bf16)
    u = _bdot(a, win_ref[...])

    pos = lax.broadcasted_iota(jnp.int32, (TS, 1), 0) & (R - 1)
    real = pos < SAMPLE_T

    z = u[:, _CG0:_HC0] * u[:, _HC0:_GT0]
    z3 = z.reshape(NSEQ, R, CONV_CH)
    z_s[...] = z3
    z_s[:, R - (CONV_W - 1):, :] = cv_ref[...]
    zf = z_s[...].reshape(TS, CONV_CH)
    zm1 = jnp.where(pos >= 1, pltpu.roll(z, 1, 0), pltpu.roll(zf, TS - (R - 1), 0))
    zm2 = jnp.where(pos >= 2, pltpu.roll(z, 2, 0), pltpu.roll(zf, TS - (R - 2), 0))
    yc = u[:, _BG0:_CG0] * (cw_ref[0:1, :] * zm2 + cw_ref[1:2, :] * zm1 + cw_ref[2:3, :] * z)
    cvo_ref[...] = z3[:, SAMPLE_T - (CONV_W - 1):SAMPLE_T, :]

    gates = u[:, _GT0:PROJ_COLS] + gb_ref[...]
    ic = jnp.where(real, GATE_CAP * jnp.tanh(gates[:, :LANES] / GATE_CAP), NEG_BIG)
    lf = jnp.where(real, _log_sigmoid(gates[:, LANES:]), 0.0)
    b = _row_scan(lf, R, jnp.add, 0.0)
    uu = ic - b
    m_loc = b + _row_scan(uu, R, jnp.maximum, -jnp.inf)

    mp_s[...] = jnp.zeros(mp_s.shape, f32)
    mp_s[:, :, 0:NH] = m0_ref[...]
    m_prev = mp_s[...]
    b3 = b.reshape(NSEQ, R, LANES)
    ml3 = m_loc.reshape(NSEQ, R, LANES)
    b_last = b3[:, R - 1:R, :]
    m_new = jnp.maximum(b_last + m_prev, ml3[:, R - 1:R, :])
    g3 = b3 + m_prev
    mt3 = jnp.maximum(g3, ml3)
    wi = jnp.exp(g3 - mt3).reshape(TS, LANES)
    c2 = (mt3 - b3).reshape(TS, LANES)
    emt = jnp.exp(-mt3).reshape(TS, LANES)
    wk = jnp.exp(b_last - b3 + ic.reshape(NSEQ, R, LANES) - m_new).reshape(TS, LANES)
    a_st = jnp.broadcast_to(jnp.exp(b_last + m_prev - m_new), (NSEQ, R, LANES)).reshape(TS, LANES)
    mst_ref[...] = m_new[:, :, 0:NH]

    q = u[:, _Q0:_K0] * (DK ** -0.5)
    k = u[:, _K0:_V0]
    v = u[:, _V0:_OG0]
    rs = jnp.zeros((TS, LANES), f32)
    num = jnp.zeros((TS, MLSTM_W), f32)
    for d in range(SAMPLE_T):
        kd = k if d == 0 else pltpu.roll(k, d, 0)
        vd = v if d == 0 else pltpu.roll(v, d, 0)
        ud = uu if d == 0 else pltpu.roll(uu, d, 0)
        p = _split_dot(q * kd, segqk_ref[...], 2) * jnp.exp(ud - c2)
        rs = rs + p
        num = num + _split_dot(p, expv_ref[...], 2) * vd

    n0x = jnp.broadcast_to(n0_ref[...], (NSEQ, R, QK_W)).reshape(TS, QK_W)
    qn = _split_dot(q * n0x, segqk_ref[...], 2)
    den = wi * qn + rs
    rden = 1.0 / jnp.maximum(jnp.abs(den), emt)
    kw = k * _split_dot(wk, expk_ref[...], 2)
    ax = _split_dot(a_st, expv_ref[...], 3)
    lane_q = lax.broadcasted_iota(jnp.int32, (R, QK_W), 1)
    for i in range(NSEQ):
        rows = slice(i * R, (i + 1) * R)
        qi = q[rows, :]
        lhs = jnp.concatenate(
            [jnp.where((lane_q >= h * DK) & (lane_q < (h + 1) * DK), qi, 0.0) for h in range(NH)], axis=0)
        c0 = c0_ref[i]
        qc = _bdot(lhs.astype(bf16), c0.reshape(QK_W, DV).astype(bf16))
        qc_s[rows, :] = jnp.concatenate([qc[h * R:(h + 1) * R, :] for h in range(NH)], axis=1)
        kwi = kw[rows, :].astype(bf16)
        vi = v[rows, :].astype(bf16)
        for h in range(NH):
            dc = lax.dot_general(kwi[:, h * DK:(h + 1) * DK], vi[:, h * DV:(h + 1) * DV],
                                 (((0,), (0,)), ((), ())), preferred_element_type=f32)
            cst_ref[i, h] = ax[i * R:i * R + 1, h * DV:(h + 1) * DV] * c0[h] + dc

    hh = (_split_dot(wi, expv_ref[...], 2) * qc_s[...] + num) * _split_dot(rden, expv_ref[...], 2)
    ms = _split_dot(hh * hh, meanv_ref[...], 2)
    hm = jax.nn.sigmoid(u[:, _OG0:_BG0]) * (hh * lax.rsqrt(ms + EPS) * gmh_ref[...])
    mix = jnp.concatenate([hm, yc], axis=1).astype(bf16)
    out = x + _bdot(mix, wout_ref[...])
    x1_ref[...] = out.reshape(NSEQ, R, D_MODEL)[:, 0:SAMPLE_T, :]

    a_k = _split_dot(a_st, expk_ref[...], 3).reshape(NSEQ, R, QK_W)[:, 0:1, :]
    nst_ref[...] = a_k * n0_ref[...] + jnp.sum(kw.reshape(NSEQ, R, QK_W), axis=1, keepdims=True)


def _mixer_sample(x, cv, c0, n0, m0, weights, NSEQ):
    Bs = x.shape[0]
    f32 = jnp.float32
    kern = functools.partial(_mixer_sample_kernel, NSEQ=NSEQ)
    x_spec = pl.BlockSpec((NSEQ, SAMPLE_T, D_MODEL), lambda t: (t, 0, 0))
    cv_spec = pl.BlockSpec((NSEQ, CONV_W - 1, CONV_CH), lambda t: (t, 0, 0))
    c_spec = pl.BlockSpec((NSEQ, NH, DK, DV), lambda t: (t, 0, 0, 0))
    n_spec = pl.BlockSpec((NSEQ, 1, QK_W), lambda t: (t, 0, 0))
    m_spec = pl.BlockSpec((NSEQ, 1, NH), lambda t: (t, 0, 0))
    return pl.pallas_call(
        kern,
        grid=(Bs // NSEQ,),
        in_specs=[
            x_spec, cv_spec, c_spec, n_spec, m_spec,
            _const_spec((1, D_MODEL)),
            _const_spec((D_MODEL, PROJ_COLS)),
            _const_spec((1, 2 * LANES)),
            _const_spec((1, MLSTM_W)),
            _const_spec((CONV_W, CONV_CH)),
            _const_spec((D_MODEL, D_MODEL)),
            _const_spec((QK_W, LANES)),
            _const_spec((LANES, MLSTM_W)),
            _const_spec((LANES, QK_W)),
            _const_spec((MLSTM_W, MLSTM_W)),
        ],
        out_specs=[x_spec, c_spec, n_spec, m_spec, cv_spec],
        out_shape=[
            jax.ShapeDtypeStruct((Bs, SAMPLE_T, D_MODEL), f32),
            jax.ShapeDtypeStruct((Bs, NH, DK, DV), f32),
            jax.ShapeDtypeStruct((Bs, 1, QK_W), f32),
            jax.ShapeDtypeStruct((Bs, 1, NH), f32),
            jax.ShapeDtypeStruct((Bs, CONV_W - 1, CONV_CH), f32),
        ],
        scratch_shapes=[
            pltpu.VMEM((NSEQ, SUBLANES, D_MODEL), f32),
            pltpu.VMEM((NSEQ, SUBLANES, CONV_CH), f32),
            pltpu.VMEM((NSEQ, 1, LANES), f32),
            pltpu.VMEM((NSEQ * SUBLANES, MLSTM_W), f32),
        ],
        compiler_params=pltpu.CompilerParams(
            dimension_semantics=("arbitrary",), vmem_limit_bytes=VMEM_LIMIT),
        name="mixer_sample",
    )(x, cv, c0, n0, m0, *weights)


def _mixer_weights(g_mix, w_in, b_ig, b_fg, g_mh, conv_w, w_out):
    f32 = jnp.float32
    w_all, wt, wout = _prep_weights(w_in, w_out)
    gmix = g_mix.reshape(1, D_MODEL)
    zeros = jnp.zeros((SUBLANES - NH,), f32)
    gb_col = jnp.concatenate([b_ig, zeros, b_fg, zeros]).reshape(2 * SUBLANES, 1)
    zeros = jnp.zeros((LANES - NH,), f32)
    gb_row = jnp.concatenate([b_ig, zeros, b_fg, zeros]).reshape(1, 2 * LANES)
    gmh = g_mh.reshape(1, MLSTM_W)
    prompt = (gmix, w_all, wt, gb_col, gmh, conv_w, wout)
    sample = (gmix, w_all, gb_row, gmh, conv_w, wout) + _selectors()
    return prompt, sample


FF_STREAM = 256
_N_FF_PIECES = D_FF // FF_STREAM
_N_PIECES = _N_FF_PIECES + PLE_DIM // FF_STREAM + D_MODEL // FF_STREAM


def _ffn_rows(x, pe, gffn_ref, gple_ref, gfin_ref, wg_s, wu_s, wd_s, wple_s, wpg_s, *, chunk, final_norm,
              before_chunk=None, before_tail=None):
    bf16 = jnp.bfloat16
    f = _rms(x, gffn_ref[...]).astype(bf16)
    e = None
    if before_tail is None:
        e = _rms(_bdot(pe.astype(bf16), wple_s[...]), gple_ref[...])
    for c0 in range(0, D_FF, chunk):
        c1 = min(c0 + chunk, D_FF)
        if before_chunk is not None:
            before_chunk(c0 // chunk)
        gate = _bdot(f, wg_s[:, c0:c1])
        up = _bdot(f, wu_s[:, c0:c1])
        hmid = (gate * jax.nn.sigmoid(gate) * up).astype(bf16)
        x = x + _bdot(hmid, wd_s[c0:c1, :])
    if before_tail is not None:
        before_tail()
        e = _rms(_bdot(pe.astype(bf16), wple_s[...]), gple_ref[...])
    x = x + jax.nn.sigmoid(_bdot(x.astype(bf16), wpg_s[...])) * e
    if final_norm:
        x = _rms(x, gfin_ref[...])
    return x


def _ffn_stream_kernel(xp_ref, pp_ref, xs_ref, ps_ref, gffn_ref, gple_ref, gfin_ref,
                       wg_hbm, wu_hbm, wd_hbm, wple_hbm, wpg_hbm,
                       yp_ref, ys_ref,
                       wg_s, wu_s, wd_s, wple_s, wpg_s, stg_g, stg_u, stg_d, sem, *, final_norm):
    t = pl.program_id(0)
    bf16 = jnp.bfloat16
    weights = (wg_s, wu_s, wd_s, wple_s, wpg_s)

    def copies(k):
        slot = k % 2
        if k < _N_FF_PIECES:
            cols = pl.ds(k * FF_STREAM, FF_STREAM)
            return [pltpu.make_async_copy(wg_hbm.at[:, cols], stg_g.at[slot], sem.at[0, slot]),
                    pltpu.make_async_copy(wu_hbm.at[:, cols], stg_u.at[slot], sem.at[1, slot]),
                    pltpu.make_async_copy(wd_hbm.at[cols, :], stg_d.at[slot], sem.at[2, slot])]
        if k == _N_FF_PIECES:
            return [pltpu.make_async_copy(wple_hbm, stg_d.at[slot], sem.at[2, slot])]
        rows = pl.ds((k - _N_FF_PIECES - 1) * FF_STREAM, FF_STREAM)
        return [pltpu.make_async_copy(wpg_hbm.at[rows, :], stg_d.at[slot], sem.at[2, slot])]

    def land(k):
        if k + 1 < _N_PIECES:
            for cp in copies(k + 1):
                cp.start()
        for cp in copies(k):
            cp.wait()
        slot = k % 2
        if k < _N_FF_PIECES:
            cols = slice(k * FF_STREAM, (k + 1) * FF_STREAM)
            wg_s[:, cols] = stg_g[slot].astype(bf16)
            wu_s[:, cols] = stg_u[slot].astype(bf16)
            wd_s[cols, :] = stg_d[slot].astype(bf16)
        elif k == _N_FF_PIECES:
            wple_s[...] = stg_d[slot].astype(bf16)
        else:
            r0 = (k - _N_FF_PIECES - 1) * FF_STREAM
            wpg_s[r0:r0 + FF_STREAM, :] = stg_d[slot].astype(bf16)

    @pl.when(t == 0)
    def _():
        for cp in copies(0):
            cp.start()
        n_tok = xs_ref.shape[0] * xs_ref.shape[1]

        def tail():
            for k in range(_N_FF_PIECES, _N_PIECES):
                land(k)

        y = _ffn_rows(xs_ref[...].reshape(n_tok, D_MODEL), ps_ref[...].reshape(n_tok, PLE_DIM),
                      gffn_ref, gple_ref, gfin_ref, *weights, chunk=FF_STREAM, final_norm=final_norm,
                      before_chunk=land, before_tail=tail)
        ys_ref[...] = y.reshape(ys_ref.shape)

    @pl.when(t > 0)
    def _():
        yp_ref[...] = _ffn_rows(xp_ref[...], pp_ref[...], gffn_ref, gple_ref, gfin_ref, *weights,
                                chunk=FF_CHUNK, final_norm=final_norm)


def _ffn_stream(xp, pp, xs, ps, g_ffn, g_ple, g_final, w_gate, w_up, w_down, w_ple, w_pg, TM, final_norm):
    T = xp.shape[0]
    f32, bf16 = jnp.float32, jnp.bfloat16
    kern = functools.partial(_ffn_stream_kernel, final_norm=final_norm)
    row_map = lambda t: (jnp.maximum(t - 1, 0), 0)
    hbm = pl.BlockSpec(memory_space=pl.ANY)
    return pl.pallas_call(
        kern,
        grid=(T // TM + 1,),
        in_specs=[
            pl.BlockSpec((TM, D_MODEL), row_map),
            pl.BlockSpec((TM, PLE_DIM), row_map),
            _const_spec(xs.shape),
            _const_spec(ps.shape),
            _const_spec((1, D_MODEL)),
            _const_spec((1, D_MODEL)),
            _const_spec((1, D_MODEL)),
            hbm, hbm, hbm, hbm, hbm,
        ],
        out_specs=[pl.BlockSpec((TM, D_MODEL), row_map),
                   pl.BlockSpec(xs.shape, lambda t: (0, 0, 0))],
        out_shape=[jax.ShapeDtypeStruct(xp.shape, f32), jax.ShapeDtypeStruct(xs.shape, f32)],
        scratch_shapes=[
            pltpu.VMEM((D_MODEL, D_FF), bf16),
            pltpu.VMEM((D_MODEL, D_FF), bf16),
            pltpu.VMEM((D_FF, D_MODEL), bf16),
            pltpu.VMEM((PLE_DIM, D_MODEL), bf16),
            pltpu.VMEM((D_MODEL, D_MODEL), bf16),
            pltpu.VMEM((2, D_MODEL, FF_STREAM), f32),
            pltpu.VMEM((2, D_MODEL, FF_STREAM), f32),
            pltpu.VMEM((2, FF_STREAM, D_MODEL), f32),
            pltpu.SemaphoreType.DMA((3, 2)),
        ],
        compiler_params=pltpu.CompilerParams(
            dimension_semantics=("arbitrary",), vmem_limit_bytes=VMEM_LIMIT),
        name="ffn",
    )(xp, pp, xs, ps, g_ffn.reshape(1, D_MODEL), g_ple.reshape(1, D_MODEL), g_final.reshape(1, D_MODEL),
      w_gate, w_up, w_down, w_ple, w_pg)


def kernel(x_prompt, x_sample, p_prompt, p_sample, state_C, state_n, state_m, state_conv,
           g_mix, w_in, b_ig, b_fg, g_mh, conv_w, w_out, g_ffn, w_gate, w_up, w_down,
           w_ple, g_ple, w_pg, g_final):
    bf16 = jnp.bfloat16
    depth = g_mix.shape[0]
    B, S, _ = x_prompt.shape
    Bs, Ss, _ = x_sample.shape
    assert Ss == SAMPLE_T and S % PROMPT_TS == 0 and Bs % SAMPLE_NSEQ == 0

    xp = x_prompt
    xs = x_sample
    outs = [[] for _ in range(8)]
    for i in range(depth):
        last = i == depth - 1
        pw, sw = _mixer_weights(g_mix[i], w_in[i], b_ig[i], b_fg[i], g_mh[i], conv_w[i], w_out[i])
        x1p, cp, n_p, mp, cvp = _mixer_prompt(xp, pw, TB=PROMPT_TS, NSUB=PROMPT_NSUB, L=PROMPT_L)
        x1s, cs, n_s, ms, cvs = _mixer_sample(
            xs, state_conv[i], state_C[i], state_n[i].reshape(Bs, 1, QK_W),
            state_m[i].reshape(Bs, 1, NH), sw, SAMPLE_NSEQ)
        xp, xs = _ffn_stream(x1p.reshape(B * S, D_MODEL), p_prompt[i].reshape(B * S, PLE_DIM), x1s,
                             p_sample[i], g_ffn[i], g_ple[i], g_final, w_gate[i], w_up[i], w_down[i],
                             w_ple[i], w_pg[i], FFN_TM, last)
        xp = xp.reshape(B, S, D_MODEL)

        new = (cp, n_p.reshape(B, NH, DK), mp[:, :NH, 0], cvp[:, SUBLANES - (CONV_W - 1):],
               cs, n_s.reshape(Bs, NH, DK), ms.reshape(Bs, NH), cvs)
        for lst, v in zip(outs, new):
            lst.append(v)

    return (xp, xs) + tuple(jnp.stack(l) for l in outs)
```

```python
import functools

import numpy as np
import jax
import jax.numpy as jnp
from jax import lax
from jax.experimental import pallas as pl
from jax.experimental.pallas import tpu as pltpu

D_MODEL = 1024
NH = 4
DK = 64
DV = 128
MLSTM_W = NH * DV
QK_W = NH * DK
CONV_CH = 512
CONV_W = 3
D_FF = 2816
FF_CHUNK = 512
FFN_TM = 512
PLE_DIM = 256
PROMPT_L = 128
PROMPT_TS = 512
PROMPT_NSUB = 2
SAMPLE_T = 4
SAMPLE_NSEQ = 32
GATE_CAP = 15.0
EPS = 1e-6

LANES = 128
SUBLANES = 8
NEG_BIG = -1e30

VMEM_LIMIT = 56 * 1024 * 1024


def _rms(x, g):
    ms = jnp.mean(x * x, axis=-1, keepdims=True)
    return x * lax.rsqrt(ms + EPS) * g


def _bdot(a, b):
    return jnp.dot(a, b, preferred_element_type=jnp.float32)


def _log_sigmoid(x):
    return jnp.minimum(x, 0.0) - jnp.log1p(jnp.exp(-jnp.abs(x)))


def _const_spec(shape):
    nd = len(shape)
    return pl.BlockSpec(shape, lambda *_: (0,) * nd, pipeline_mode=pl.Buffered(1))


_IG0 = 2 * QK_W + MLSTM_W
_OG_IN = _IG0 + 2 * NH
PROJ_IN = _OG_IN + MLSTM_W + 3 * CONV_CH
_Q0, _K0, _V0, _OG0, _BG0, _CG0, _HC0, _GT0 = 0, 256, 512, 1024, 1536, 2048, 2560, 3072
PROJ_COLS = _GT0 + 2 * LANES
_TQ0, _TV0, _TOG0, _TG0 = 0, 256, 768, 1280
TPROJ_ROWS = _TG0 + 2 * SUBLANES
PREP_TK = 256


def _prep_weights_kernel(w_ref, wo_ref, wall_ref, wt_ref, wout_ref):
    bf16 = jnp.bfloat16
    wout_ref[...] = wo_ref[...].astype(bf16)
    wf = w_ref[...]
    tk = wf.shape[1]
    g8 = wf[_IG0:_OG_IN]
    row = lax.broadcasted_iota(jnp.int32, (SUBLANES, tk), 0)
    g_in = jnp.where(row < NH, g8, 0.0)
    g_fg = jnp.where(row < NH, pltpu.roll(g8, SUBLANES - NH, 0), 0.0)

    wt_ref[_TQ0:_TV0, :] = wf[_Q0:_K0].astype(bf16)
    wt_ref[_TV0:_TOG0, :] = wf[_V0:_OG0].astype(bf16)
    wt_ref[_TOG0:_TG0, :] = wf[_OG_IN:_OG_IN + MLSTM_W].astype(bf16)
    wt_ref[_TG0:TPROJ_ROWS, :] = jnp.concatenate([g_in, g_fg], axis=0).astype(bf16)

    wall_ref[:, 0:_OG0] = wf[0:_IG0].T.astype(bf16)
    wall_ref[:, _OG0:_GT0] = wf[_OG_IN:PROJ_IN].T.astype(bf16)
    pad = jnp.zeros((LANES - SUBLANES, tk), jnp.float32)
    wall_ref[:, _GT0:_GT0 + LANES] = jnp.concatenate([g_in, pad], axis=0).T.astype(bf16)
    wall_ref[:, _GT0 + LANES:PROJ_COLS] = jnp.concatenate([g_fg, pad], axis=0).T.astype(bf16)


def _prep_weights(w_in, w_out):
    bf16 = jnp.bfloat16
    return pl.pallas_call(
        _prep_weights_kernel,
        grid=(D_MODEL // PREP_TK,),
        in_specs=[pl.BlockSpec((PROJ_IN, PREP_TK), lambda i: (0, i)),
                  pl.BlockSpec((PREP_TK, D_MODEL), lambda i: (i, 0))],
        out_specs=[pl.BlockSpec((PREP_TK, PROJ_COLS), lambda i: (i, 0)),
                   pl.BlockSpec((TPROJ_ROWS, PREP_TK), lambda i: (0, i)),
                   pl.BlockSpec((PREP_TK, D_MODEL), lambda i: (i, 0))],
        out_shape=[jax.ShapeDtypeStruct((D_MODEL, PROJ_COLS), bf16),
                   jax.ShapeDtypeStruct((TPROJ_ROWS, D_MODEL), bf16),
                   jax.ShapeDtypeStruct((D_MODEL, D_MODEL), bf16)],
        compiler_params=pltpu.CompilerParams(
            dimension_semantics=("arbitrary",), vmem_limit_bytes=VMEM_LIMIT),
        name="prep_weights",
    )(w_in.T, w_out)


ST_ROWS = DV + 2 * SUBLANES


def _lane_scan(x, seg, op, fill):
    pos = lax.broadcasted_iota(jnp.int32, x.shape, 1) & (seg - 1)
    k = 1
    while k < seg:
        x = op(x, jnp.where(pos >= k, pltpu.roll(x, k, 1), fill))
        k *= 2
    return x


PROJ_PIECE = 256


def _run(*phases):
    live = list(phases)
    while live:
        for g in list(live):
            try:
                next(g)
            except StopIteration:
                live.remove(g)


def _mixer_prompt_kernel(x_ref, gmix_ref, wn_ref, wt_ref, gbt_ref, gmhc_ref, cw_ref, wout_ref,
                         x1_ref, c_ref, n_ref, m_ref, cvo_ref,
                         qt_s, vt_s, ogt_s, k_s, mixt_s, z_s, ucol_s, wi_s, c2_s, emt_s, wk_s, ast_s, st_s,
                         *, L, TB, NSUB):
    j = pl.program_id(1)

    @pl.when(j == 0)
    def _():
        z_s[0, 0:SUBLANES, :] = jnp.zeros((SUBLANES, CONV_CH), jnp.float32)
        st_s[...] = jnp.zeros(st_s.shape, jnp.float32)
        m_ref[...] = jnp.zeros(m_ref.shape, jnp.float32)

    m_prev = m_ref[0]
    sub = [dict(sb=sb, x_ref=x_ref, gmix_ref=gmix_ref, wn_ref=wn_ref, wt_ref=wt_ref, gbt_ref=gbt_ref,
                gmhc_ref=gmhc_ref, cw_ref=cw_ref, wout_ref=wout_ref, x1_ref=x1_ref, qt_s=qt_s.at[sb],
                vt_s=vt_s.at[sb], ogt_s=ogt_s.at[sb], k_s=k_s.at[sb], mixt_s=mixt_s.at[sb], z_s=z_s,
                ucol_s=ucol_s.at[sb], wi_s=wi_s.at[sb], c2_s=c2_s.at[sb], emt_s=emt_s.at[sb],
                wk_s=wk_s.at[sb], ast_s=ast_s.at[sb], st_s=st_s, L=L, TS=TB) for sb in range(NSUB)]
    carry = {"m": m_prev}
    _run(_prompt_project(sub[0]))
    for a, b in zip(sub[:-1], sub[1:]):
        _run(_prompt_chunks(a, carry), _prompt_project(b))
    _run(_prompt_chunks(sub[-1], carry), *[_prompt_output(s) for s in sub[:-1]])
    _run(_prompt_output(sub[-1]))
    m_ref[0] = carry["m"]
    cvo_ref[0] = z_s[NSUB - 1, TB:TB + SUBLANES, :]
    z_s[0, 0:SUBLANES, :] = z_s[NSUB - 1, TB:TB + SUBLANES, :]

    @pl.when(j == pl.num_programs(1) - 1)
    def _():
        for pr in range(NH // 2):
            state = st_s[pr]
            c_pair = state[0:DV, :].T
            c_ref[0, 2 * pr] = c_pair[0:DK, :]
            c_ref[0, 2 * pr + 1] = c_pair[DK:, :]
            n_ref[0, pr:pr + 1, :] = state[DV:DV + 1, :]


def _prompt_project(s):
    sb, L, TS = s["sb"], s["L"], s["TS"]
    x_ref, gmix_ref, wn_ref, wt_ref, gbt_ref, cw_ref = (
        s["x_ref"], s["gmix_ref"], s["wn_ref"], s["wt_ref"], s["gbt_ref"], s["cw_ref"])
    qt_s, vt_s, ogt_s, k_s, z_s = s["qt_s"], s["vt_s"], s["ogt_s"], s["k_s"], s["z_s"]
    bf16 = jnp.bfloat16

    x = x_ref[0, sb * TS:(sb + 1) * TS, :]
    a = _rms(x, gmix_ref[...]).astype(bf16)

    g2 = _bdot(a, wn_ref[:, _GT0:PROJ_COLS])
    gt = jnp.concatenate([g2[:, 0:LANES].T[0:SUBLANES], g2[:, LANES:].T[0:SUBLANES]], axis=0) + gbt_ref[...]
    ic = GATE_CAP * jnp.tanh(gt[0:SUBLANES] / GATE_CAP)
    lf = _log_sigmoid(gt[SUBLANES:])
    b = _lane_scan(lf, L, jnp.add, 0.0)
    u = ic - b
    m_loc = b + _lane_scan(u, L, jnp.maximum, -jnp.inf)
    s.update(b=b, u=u, ic=ic, m_loc=m_loc)
    yield

    k_s[...] = _bdot(a, wn_ref[:, _K0:_V0])
    yield
    hc = _bdot(a, wn_ref[:, _HC0:_GT0])
    yield
    z = _bdot(a, wn_ref[:, _CG0:_HC0]) * hc
    if sb > 0:
        z_s[sb, 0:SUBLANES, :] = z_s[sb - 1, TS:TS + SUBLANES, :]
    z_s[sb, SUBLANES:SUBLANES + TS, :] = z
    yield
    yconv = (cw_ref[0:1, :] * z_s[sb, SUBLANES - 2:SUBLANES - 2 + TS, :]
             + cw_ref[1:2, :] * z_s[sb, SUBLANES - 1:SUBLANES - 1 + TS, :]
             + cw_ref[2:3, :] * z)
    s["yc"] = (_bdot(a, wn_ref[:, _BG0:_CG0]) * yconv).astype(bf16)
    yield

    qt_s[...] = (_bdot(a, wn_ref[:, _Q0:_K0]) * (DK ** -0.5)).T
    yield
    for c0 in range(0, MLSTM_W, PROJ_PIECE):
        vt_s[c0:c0 + PROJ_PIECE, :] = _bdot(a, wn_ref[:, _V0 + c0:_V0 + c0 + PROJ_PIECE]).T
        yield
    for c0 in range(0, MLSTM_W, PROJ_PIECE):
        ogt_s[c0:c0 + PROJ_PIECE, :] = _bdot(a, wn_ref[:, _OG0 + c0:_OG0 + c0 + PROJ_PIECE]).T
        yield


def _prompt_chunks(s, carry):
    L, TS = s["L"], s["TS"]
    qt_s, vt_s, ogt_s, k_s, mixt_s, ucol_s, wi_s, c2_s, emt_s, wk_s, ast_s, st_s = (
        s["qt_s"], s["vt_s"], s["ogt_s"], s["k_s"], s["mixt_s"], s["ucol_s"], s["wi_s"],
        s["c2_s"], s["emt_s"], s["wk_s"], s["ast_s"], s["st_s"])
    b, u, ic, m_loc = s["b"], s["u"], s["ic"], s["m_loc"]
    nc = TS // L
    bf16 = jnp.bfloat16
    m_prev = carry["m"]

    for c in range(nc):
        sl = slice(c * L, (c + 1) * L)
        bc, mlc = b[:, sl], m_loc[:, sl]
        b_last = jnp.broadcast_to(bc[:, L - 1:L], bc.shape)
        m_new = jnp.maximum(b_last + m_prev, jnp.broadcast_to(mlc[:, L - 1:L], bc.shape))
        g = bc + m_prev
        mt = jnp.maximum(g, mlc)
        wi_s[:, sl] = jnp.exp(g - mt)
        c2_s[:, sl] = mt - bc
        emt_s[:, sl] = jnp.exp(-mt)
        wk_s[:, sl] = jnp.exp(b_last - bc + ic[:, sl] - m_new)
        ast_s[c] = jnp.exp(b_last + m_prev - m_new)
        upad = jnp.concatenate([u[:, sl], jnp.zeros((L - SUBLANES, L), jnp.float32)], axis=0)
        ucol_s[c * L:(c + 1) * L, :] = upad.T
        m_prev = m_new
    carry["m"] = m_prev
    yield

    s_i = lax.broadcasted_iota(jnp.int32, (L, L), 0)
    t_i = lax.broadcasted_iota(jnp.int32, (L, L), 1)
    causal = s_i <= t_i
    low_half = lax.broadcasted_iota(jnp.int32, (L, LANES), 1) < DK
    zeros_q = jnp.zeros((DK, L), jnp.float32)
    zeros_p = jnp.zeros((L, L), bf16)

    def pair_row(ref, h0, rows):
        return jnp.concatenate([ref[h0:h0 + 1, rows], ref[h0 + 1:h0 + 2, rows]], axis=1)

    for c in range(nc):
        rows = slice(c * L, (c + 1) * L)
        for pr in range(NH // 2):
            h0 = 2 * pr
            hv0 = slice(h0 * DV, (h0 + 1) * DV)
            hv1 = slice((h0 + 1) * DV, (h0 + 2) * DV)
            kp = k_s[rows, pr * LANES:(pr + 1) * LANES]
            q0 = qt_s[h0 * DK:(h0 + 1) * DK, rows]
            q1 = qt_s[(h0 + 1) * DK:(h0 + 2) * DK, rows]
            qbd = jnp.concatenate([jnp.concatenate([q0, zeros_q], axis=1),
                                   jnp.concatenate([zeros_q, q1], axis=1)], axis=0).astype(bf16)
            st = _bdot(kp.astype(bf16), qbd)
            arg = jnp.concatenate(
                [jnp.where(causal, ucol_s[rows, h:h + 1] - c2_s[h:h + 1, rows], -jnp.inf)
                 for h in (h0, h0 + 1)], axis=1)
            pt = st * jnp.exp(arg)
            rs = jnp.sum(pt, axis=0, keepdims=True)
            ptb = pt.astype(bf16)
            pbd = jnp.concatenate([jnp.concatenate([ptb[:, 0:L], zeros_p], axis=1),
                                   jnp.concatenate([zeros_p, ptb[:, L:]], axis=1)], axis=0)
            vt = jnp.concatenate([vt_s[hv0, rows], vt_s[hv1, rows]], axis=1)
            state = st_s[pr]
            sq = _bdot(state.astype(bf16), qbd) * pair_row(wi_s, h0, rows)
            num = _bdot(vt.astype(bf16), pbd) + sq[0:DV]
            den = sq[DV:DV + 1] + rs
            hh = num * (1.0 / jnp.maximum(jnp.abs(den), pair_row(emt_s, h0, rows)))
            hn = hh * lax.rsqrt(jnp.mean(hh * hh, axis=0, keepdims=True) + EPS)
            mixt_s[hv0, rows] = jax.nn.sigmoid(ogt_s[hv0, rows]) * hn[:, 0:L]
            mixt_s[hv1, rows] = jax.nn.sigmoid(ogt_s[hv1, rows]) * hn[:, L:]
            wkr = pair_row(wk_s, h0, rows)
            vw = jnp.concatenate([vt * wkr, jnp.broadcast_to(wkr, (2 * SUBLANES, 2 * L))], axis=0)
            km = jnp.concatenate([jnp.where(low_half, kp, 0.0), jnp.where(low_half, 0.0, kp)], axis=0)
            decay = jnp.where(low_half[0:1, :], ast_s[c][h0:h0 + 1, :], ast_s[c][h0 + 1:h0 + 2, :])
            st_s[pr] = decay * state + _bdot(vw.astype(bf16), km.astype(bf16))
            yield


def _prompt_output(s):
    sb, TS = s["sb"], s["TS"]
    x_ref, wout_ref, x1_ref, mixt_s = s["x_ref"], s["wout_ref"], s["x1_ref"], s["mixt_s"]
    tok = slice(sb * TS, (sb + 1) * TS)
    hm = mixt_s[...].T * s["gmhc_ref"][...]
    mix = jnp.concatenate([hm.astype(jnp.bfloat16), s["yc"]], axis=1)
    yield
    for c0 in range(0, D_MODEL, PROJ_PIECE):
        cols = slice(c0, c0 + PROJ_PIECE)
        x1_ref[0, tok, cols] = x_ref[0, tok, cols] + _bdot(mix, wout_ref[:, cols])
        yield


def _mixer_prompt(x, weights, TB, NSUB, L):
    B, S, _ = x.shape
    TS = TB * NSUB
    nb = S // TS
    f32 = jnp.float32
    kern = functools.partial(_mixer_prompt_kernel, L=L, TB=TB, NSUB=NSUB)
    return pl.pallas_call(
        kern,
        grid=(B, nb),
        in_specs=[
            pl.BlockSpec((1, TS, D_MODEL), lambda b, j: (b, j, 0)),
            _const_spec((1, D_MODEL)),
            _const_spec((D_MODEL, PROJ_COLS)),
            _const_spec((TPROJ_ROWS, D_MODEL)),
            _const_spec((2 * SUBLANES, 1)),
            _const_spec((1, MLSTM_W)),
            _const_spec((CONV_W, CONV_CH)),
            _const_spec((D_MODEL, D_MODEL)),
        ],
        out_specs=[
            pl.BlockSpec((1, TS, D_MODEL), lambda b, j: (b, j, 0)),
            pl.BlockSpec((1, NH, DK, DV), lambda b, j: (b, 0, 0, 0)),
            pl.BlockSpec((1, NH // 2, 2 * DK), lambda b, j: (b, 0, 0)),
            pl.BlockSpec((1, SUBLANES, LANES), lambda b, j: (b, 0, 0)),
            pl.BlockSpec((1, SUBLANES, CONV_CH), lambda b, j: (b, 0, 0)),
        ],
        out_shape=[
            jax.ShapeDtypeStruct((B, S, D_MODEL), f32),
            jax.ShapeDtypeStruct((B, NH, DK, DV), f32),
            jax.ShapeDtypeStruct((B, NH // 2, 2 * DK), f32),
            jax.ShapeDtypeStruct((B, SUBLANES, LANES), f32),
            jax.ShapeDtypeStruct((B, SUBLANES, CONV_CH), f32),
        ],
        scratch_shapes=[
            pltpu.VMEM((NSUB, QK_W, TB), f32),
            pltpu.VMEM((NSUB, MLSTM_W, TB), f32),
            pltpu.VMEM((NSUB, MLSTM_W, TB), f32),
            pltpu.VMEM((NSUB, TB, QK_W), f32),
            pltpu.VMEM((NSUB, MLSTM_W, TB), f32),
            pltpu.VMEM((NSUB, TB + SUBLANES, CONV_CH), f32),
            pltpu.VMEM((NSUB, TB, LANES), f32),
            pltpu.VMEM((NSUB, SUBLANES, TB), f32),
            pltpu.VMEM((NSUB, SUBLANES, TB), f32),
            pltpu.VMEM((NSUB, SUBLANES, TB), f32),
            pltpu.VMEM((NSUB, SUBLANES, TB), f32),
            pltpu.VMEM((NSUB, TB // L, SUBLANES, LANES), f32),
            pltpu.VMEM((NH // 2, ST_ROWS, LANES), f32),
        ],
        compiler_params=pltpu.CompilerParams(
            dimension_semantics=("arbitrary", "arbitrary"), vmem_limit_bytes=VMEM_LIMIT),
        name="mixer_prompt",
    )(x, *weights)


def _row_scan(x, seg, op, fill):
    pos = lax.broadcasted_iota(jnp.int32, x.shape, 0) & (seg - 1)
    k = 1
    while k < seg:
        x = op(x, jnp.where(pos >= k, pltpu.roll(x, k, 0), fill))
        k *= 2
    return x


def _split_dot(x, sel, parts):
    acc = None
    rem = x
    for p in range(parts):
        hi = rem.astype(jnp.bfloat16)
        d = _bdot(hi, sel)
        acc = d if acc is None else acc + d
        if p + 1 < parts:
            rem = rem - hi.astype(jnp.float32)
    return acc


def _selectors():
    h = np.arange(NH)
    seg_qk = np.zeros((QK_W, LANES), np.float32)
    seg_qk[np.arange(QK_W), np.arange(QK_W) // DK] = 1.0
    exp_v = np.zeros((LANES, MLSTM_W), np.float32)
    exp_k = np.zeros((LANES, QK_W), np.float32)
    for i in h:
        exp_v[i, i * DV:(i + 1) * DV] = 1.0
        exp_k[i, i * DK:(i + 1) * DK] = 1.0
    mean_v = np.kron(np.eye(NH, dtype=np.float32), np.full((DV, DV), 1.0 / DV, np.float32))
    return tuple(jnp.asarray(m, jnp.bfloat16) for m in (seg_qk, exp_v, exp_k, mean_v))


def _mixer_sample_kernel(x_ref, cv_ref, c0_ref, n0_ref, m0_ref,
                         gmix_ref, win_ref, gb_ref, gmh_ref, cw_ref, wout_ref,
                         segqk_ref, expv_ref, expk_ref, meanv_ref,
                         x1_ref, cst_ref, nst_ref, mst_ref, cvo_ref,
                         xp_s, z_s, mp_s, qc_s, *, NSEQ):
    R = SUBLANES
    TS = NSEQ * R
    bf16 = jnp.bfloat16
    f32 = jnp.float32

    xp_s[:, SAMPLE_T:, :] = jnp.zeros((NSEQ, R - SAMPLE_T, D_MODEL), f32)
    xp_s[:, 0:SAMPLE_T, :] = x_ref[...]
    x = xp_s[...].reshape(TS, D_MODEL)
    a = _rms(x, gmix_ref[...]).astype(bf16)
    u = _bdot(a, win_ref[...])

    pos = lax.broadcasted_iota(jnp.int32, (TS, 1), 0) & (R - 1)
    real = pos < SAMPLE_T

    z = u[:, _CG0:_HC0] * u[:, _HC0:_GT0]
    z3 = z.reshape(NSEQ, R, CONV_CH)
    z_s[...] = z3
    z_s[:, R - (CONV_W - 1):, :] = cv_ref[...]
    zf = z_s[...].reshape(TS, CONV_CH)
    zm1 = jnp.where(pos >= 1, pltpu.roll(z, 1, 0), pltpu.roll(zf, TS - (R - 1), 0))
    zm2 = jnp.where(pos >= 2, pltpu.roll(z, 2, 0), pltpu.roll(zf, TS - (R - 2), 0))
    yc = u[:, _BG0:_CG0] * (cw_ref[0:1, :] * zm2 + cw_ref[1:2, :] * zm1 + cw_ref[2:3, :] * z)
    cvo_ref[...] = z3[:, SAMPLE_T - (CONV_W - 1):SAMPLE_T, :]

    gates = u[:, _GT0:PROJ_COLS] + gb_ref[...]
    ic = jnp.where(real, GATE_CAP * jnp.tanh(gates[:, :LANES] / GATE_CAP), NEG_BIG)
    lf = jnp.where(real, _log_sigmoid(gates[:, LANES:]), 0.0)
    b = _row_scan(lf, R, jnp.add, 0.0)
    uu = ic - b
    m_loc = b + _row_scan(uu, R, jnp.maximum, -jnp.inf)

    mp_s[...] = jnp.zeros(mp_s.shape, f32)
    mp_s[:, :, 0:NH] = m0_ref[...]
    m_prev = mp_s[...]
    b3 = b.reshape(NSEQ, R, LANES)
    ml3 = m_loc.reshape(NSEQ, R, LANES)
    b_last = b3[:, R - 1:R, :]
    m_new = jnp.maximum(b_last + m_prev, ml3[:, R - 1:R, :])
    g3 = b3 + m_prev
    mt3 = jnp.maximum(g3, ml3)
    wi = jnp.exp(g3 - mt3).reshape(TS, LANES)
    c2 = (mt3 - b3).reshape(TS, LANES)
    emt = jnp.exp(-mt3).reshape(TS, LANES)
    wk = jnp.exp(b_last - b3 + ic.reshape(NSEQ, R, LANES) - m_new).reshape(TS, LANES)
    a_st = jnp.broadcast_to(jnp.exp(b_last + m_prev - m_new), (NSEQ, R, LANES)).reshape(TS, LANES)
    mst_ref[...] = m_new[:, :, 0:NH]

    q = u[:, _Q0:_K0] * (DK ** -0.5)
    k = u[:, _K0:_V0]
    v = u[:, _V0:_OG0]
    rs = jnp.zeros((TS, LANES), f32)
    num = jnp.zeros((TS, MLSTM_W), f32)
    for d in range(SAMPLE_T):
        kd = k if d == 0 else pltpu.roll(k, d, 0)
        vd = v if d == 0 else pltpu.roll(v, d, 0)
        ud = uu if d == 0 else pltpu.roll(uu, d, 0)
        p = _split_dot(q * kd, segqk_ref[...], 2) * jnp.exp(ud - c2)
        rs = rs + p
        num = num + _split_dot(p, expv_ref[...], 2) * vd

    n0x = jnp.broadcast_to(n0_ref[...], (NSEQ, R, QK_W)).reshape(TS, QK_W)
    qn = _split_dot(q * n0x, segqk_ref[...], 2)
    den = wi * qn + rs
    rden = 1.0 / jnp.maximum(jnp.abs(den), emt)
    kw = k * _split_dot(wk, expk_ref[...], 2)
    ax = _split_dot(a_st, expv_ref[...], 3)
    lane_q = lax.broadcasted_iota(jnp.int32, (R, QK_W), 1)
    for i in range(NSEQ):
        rows = slice(i * R, (i + 1) * R)
        qi = q[rows, :]
        lhs = jnp.concatenate(
            [jnp.where((lane_q >= h * DK) & (lane_q < (h + 1) * DK), qi, 0.0) for h in range(NH)], axis=0)
        c0 = c0_ref[i]
        qc = _bdot(lhs.astype(bf16), c0.reshape(QK_W, DV).astype(bf16))
        qc_s[rows, :] = jnp.concatenate([qc[h * R:(h + 1) * R, :] for h in range(NH)], axis=1)
        kwi = kw[rows, :].astype(bf16)
        vi = v[rows, :].astype(bf16)
        for h in range(NH):
            dc = lax.dot_general(kwi[:, h * DK:(h + 1) * DK], vi[:, h * DV:(h + 1) * DV],
                                 (((0,), (0,)), ((), ())), preferred_element_type=f32)
            cst_ref[i, h] = ax[i * R:i * R + 1, h * DV:(h + 1) * DV] * c0[h] + dc

    hh = (_split_dot(wi, expv_ref[...], 2) * qc_s[...] + num) * _split_dot(rden, expv_ref[...], 2)
    ms = _split_dot(hh * hh, meanv_ref[...], 2)
    hm = jax.nn.sigmoid(u[:, _OG0:_BG0]) * (hh * lax.rsqrt(ms + EPS) * gmh_ref[...])
    mix = jnp.concatenate([hm, yc], axis=1).astype(bf16)
    out = x + _bdot(mix, wout_ref[...])
    x1_ref[...] = out.reshape(NSEQ, R, D_MODEL)[:, 0:SAMPLE_T, :]

    a_k = _split_dot(a_st, expk_ref[...], 3).reshape(NSEQ, R, QK_W)[:, 0:1, :]
    nst_ref[...] = a_k * n0_ref[...] + jnp.sum(kw.reshape(NSEQ, R, QK_W), axis=1, keepdims=True)


def _mixer_sample(x, cv, c0, n0, m0, weights, NSEQ):
    Bs = x.shape[0]
    f32 = jnp.float32
    kern = functools.partial(_mixer_sample_kernel, NSEQ=NSEQ)
    x_spec = pl.BlockSpec((NSEQ, SAMPLE_T, D_MODEL), lambda t: (t, 0, 0))
    cv_spec = pl.BlockSpec((NSEQ, CONV_W - 1, CONV_CH), lambda t: (t, 0, 0))
    c_spec = pl.BlockSpec((NSEQ, NH, DK, DV), lambda t: (t, 0, 0, 0))
    n_spec = pl.BlockSpec((NSEQ, 1, QK_W), lambda t: (t, 0, 0))
    m_spec = pl.BlockSpec((NSEQ, 1, NH), lambda t: (t, 0, 0))
    return pl.pallas_call(
        kern,
        grid=(Bs // NSEQ,),
        in_specs=[
            x_spec, cv_spec, c_spec, n_spec, m_spec,
            _const_spec((1, D_MODEL)),
            _const_spec((D_MODEL, PROJ_COLS)),
            _const_spec((1, 2 * LANES)),
            _const_spec((1, MLSTM_W)),
            _const_spec((CONV_W, CONV_CH)),
            _const_spec((D_MODEL, D_MODEL)),
            _const_spec((QK_W, LANES)),
            _const_spec((LANES, MLSTM_W)),
            _const_spec((LANES, QK_W)),
            _const_spec((MLSTM_W, MLSTM_W)),
        ],
        out_specs=[x_spec, c_spec, n_spec, m_spec, cv_spec],
        out_shape=[
            jax.ShapeDtypeStruct((Bs, SAMPLE_T, D_MODEL), f32),
            jax.ShapeDtypeStruct((Bs, NH, DK, DV), f32),
            jax.ShapeDtypeStruct((Bs, 1, QK_W), f32),
            jax.ShapeDtypeStruct((Bs, 1, NH), f32),
            jax.ShapeDtypeStruct((Bs, CONV_W - 1, CONV_CH), f32),
        ],
        scratch_shapes=[
            pltpu.VMEM((NSEQ, SUBLANES, D_MODEL), f32),
            pltpu.VMEM((NSEQ, SUBLANES, CONV_CH), f32),
            pltpu.VMEM((NSEQ, 1, LANES), f32),
            pltpu.VMEM((NSEQ * SUBLANES, MLSTM_W), f32),
        ],
        compiler_params=pltpu.CompilerParams(
            dimension_semantics=("arbitrary",), vmem_limit_bytes=VMEM_LIMIT),
        name="mixer_sample",
    )(x, cv, c0, n0, m0, *weights)


def _mixer_weights(g_mix, w_in, b_ig, b_fg, g_mh, conv_w, w_out):
    f32 = jnp.float32
    w_all, wt, wout = _prep_weights(w_in, w_out)
    gmix = g_mix.reshape(1, D_MODEL)
    zeros = jnp.zeros((SUBLANES - NH,), f32)
    gb_col = jnp.concatenate([b_ig, zeros, b_fg, zeros]).reshape(2 * SUBLANES, 1)
    zeros = jnp.zeros((LANES - NH,), f32)
    gb_row = jnp.concatenate([b_ig, zeros, b_fg, zeros]).reshape(1, 2 * LANES)
    gmh = g_mh.reshape(1, MLSTM_W)
    prompt = (gmix, w_all, wt, gb_col, gmh, conv_w, wout)
    sample = (gmix, w_all, gb_row, gmh, conv_w, wout) + _selectors()
    return prompt, sample


FF_STREAM = 256
_N_FF_PIECES = D_FF // FF_STREAM
_N_PIECES = _N_FF_PIECES + PLE_DIM // FF_STREAM + D_MODEL // FF_STREAM


def _ffn_rows(x, pe, gffn_ref, gple_ref, gfin_ref, wg_s, wu_s, wd_s, wple_s, wpg_s, *, chunk, final_norm,
              before_chunk=None, before_tail=None):
    bf16 = jnp.bfloat16
    f = _rms(x, gffn_ref[...]).astype(bf16)
    e = None
    if before_tail is None:
        e = _rms(_bdot(pe.astype(bf16), wple_s[...]), gple_ref[...])
    for c0 in range(0, D_FF, chunk):
        c1 = min(c0 + chunk, D_FF)
        if before_chunk is not None:
            before_chunk(c0 // chunk)
        gate = _bdot(f, wg_s[:, c0:c1])
        up = _bdot(f, wu_s[:, c0:c1])
        hmid = (gate * jax.nn.sigmoid(gate) * up).astype(bf16)
        x = x + _bdot(hmid, wd_s[c0:c1, :])
    if before_tail is not None:
        before_tail()
        e = _rms(_bdot(pe.astype(bf16), wple_s[...]), gple_ref[...])
    x = x + jax.nn.sigmoid(_bdot(x.astype(bf16), wpg_s[...])) * e
    if final_norm:
        x = _rms(x, gfin_ref[...])
    return x


def _ffn_stream_kernel(xp_ref, pp_ref, xs_ref, ps_ref, gffn_ref, gple_ref, gfin_ref,
                       wg_hbm, wu_hbm, wd_hbm, wple_hbm, wpg_hbm,
                       yp_ref, ys_ref,
                       wg_s, wu_s, wd_s, wple_s, wpg_s, stg_g, stg_u, stg_d, sem, *, final_norm):
    t = pl.program_id(0)
    bf16 = jnp.bfloat16
    weights = (wg_s, wu_s, wd_s, wple_s, wpg_s)

    def copies(k):
        slot = k % 2
        if k < _N_FF_PIECES:
            cols = pl.ds(k * FF_STREAM, FF_STREAM)
            return [pltpu.make_async_copy(wg_hbm.at[:, cols], stg_g.at[slot], sem.at[0, slot]),
                    pltpu.make_async_copy(wu_hbm.at[:, cols], stg_u.at[slot], sem.at[1, slot]),
                    pltpu.make_async_copy(wd_hbm.at[cols, :], stg_d.at[slot], sem.at[2, slot])]
        if k == _N_FF_PIECES:
            return [pltpu.make_async_copy(wple_hbm, stg_d.at[slot], sem.at[2, slot])]
        rows = pl.ds((k - _N_FF_PIECES - 1) * FF_STREAM, FF_STREAM)
        return [pltpu.make_async_copy(wpg_hbm.at[rows, :], stg_d.at[slot], sem.at[2, slot])]

    def land(k):
        if k + 1 < _N_PIECES:
            for cp in copies(k + 1):
                cp.start()
        for cp in copies(k):
            cp.wait()
        slot = k % 2
        if k < _N_FF_PIECES:
            cols = slice(k * FF_STREAM, (k + 1) * FF_STREAM)
            wg_s[:, cols] = stg_g[slot].astype(bf16)
            wu_s[:, cols] = stg_u[slot].astype(bf16)
            wd_s[cols, :] = stg_d[slot].astype(bf16)
        elif k == _N_FF_PIECES:
            wple_s[...] = stg_d[slot].astype(bf16)
        else:
            r0 = (k - _N_FF_PIECES - 1) * FF_STREAM
            wpg_s[r0:r0 + FF_STREAM, :] = stg_d[slot].astype(bf16)

    @pl.when(t == 0)
    def _():
        for cp in copies(0):
            cp.start()
        n_tok = xs_ref.shape[0] * xs_ref.shape[1]

        def tail():
            for k in range(_N_FF_PIECES, _N_PIECES):
                land(k)

        y = _ffn_rows(xs_ref[...].reshape(n_tok, D_MODEL), ps_ref[...].reshape(n_tok, PLE_DIM),
                      gffn_ref, gple_ref, gfin_ref, *weights, chunk=FF_STREAM, final_norm=final_norm,
                      before_chunk=land, before_tail=tail)
        ys_ref[...] = y.reshape(ys_ref.shape)

    @pl.when(t > 0)
    def _():
        yp_ref[...] = _ffn_rows(xp_ref[...], pp_ref[...], gffn_ref, gple_ref, gfin_ref, *weights,
                                chunk=FF_CHUNK, final_norm=final_norm)


def _ffn_stream(xp, pp, xs, ps, g_ffn, g_ple, g_final, w_gate, w_up, w_down, w_ple, w_pg, TM, final_norm):
    T = xp.shape[0]
    f32, bf16 = jnp.float32, jnp.bfloat16
    kern = functools.partial(_ffn_stream_kernel, final_norm=final_norm)
    row_map = lambda t: (jnp.maximum(t - 1, 0), 0)
    hbm = pl.BlockSpec(memory_space=pl.ANY)
    return pl.pallas_call(
        kern,
        grid=(T // TM + 1,),
        in_specs=[
            pl.BlockSpec((TM, D_MODEL), row_map),
            pl.BlockSpec((TM, PLE_DIM), row_map),
            _const_spec(xs.shape),
            _const_spec(ps.shape),
            _const_spec((1, D_MODEL)),
            _const_spec((1, D_MODEL)),
            _const_spec((1, D_MODEL)),
            hbm, hbm, hbm, hbm, hbm,
        ],
        out_specs=[pl.BlockSpec((TM, D_MODEL), row_map),
                   pl.BlockSpec(xs.shape, lambda t: (0, 0, 0))],
        out_shape=[jax.ShapeDtypeStruct(xp.shape, f32), jax.ShapeDtypeStruct(xs.shape, f32)],
        scratch_shapes=[
            pltpu.VMEM((D_MODEL, D_FF), bf16),
            pltpu.VMEM((D_MODEL, D_FF), bf16),
            pltpu.VMEM((D_FF, D_MODEL), bf16),
            pltpu.VMEM((PLE_DIM, D_MODEL), bf16),
            pltpu.VMEM((D_MODEL, D_MODEL), bf16),
            pltpu.VMEM((2, D_MODEL, FF_STREAM), f32),
            pltpu.VMEM((2, D_MODEL, FF_STREAM), f32),
            pltpu.VMEM((2, FF_STREAM, D_MODEL), f32),
            pltpu.SemaphoreType.DMA((3, 2)),
        ],
        compiler_params=pltpu.CompilerParams(
            dimension_semantics=("arbitrary",), vmem_limit_bytes=VMEM_LIMIT),
        name="ffn",
    )(xp, pp, xs, ps, g_ffn.reshape(1, D_MODEL), g_ple.reshape(1, D_MODEL), g_final.reshape(1, D_MODEL),
      w_gate, w_up, w_down, w_ple, w_pg)


def kernel(x_prompt, x_sample, p_prompt, p_sample, state_C, state_n, state_m, state_conv,
           g_mix, w_in, b_ig, b_fg, g_mh, conv_w, w_out, g_ffn, w_gate, w_up, w_down,
           w_ple, g_ple, w_pg, g_final):
    bf16 = jnp.bfloat16
    depth = g_mix.shape[0]
    B, S, _ = x_prompt.shape
    Bs, Ss, _ = x_sample.shape
    assert Ss == SAMPLE_T and S % PROMPT_TS == 0 and Bs % SAMPLE_NSEQ == 0

    xp = x_prompt
    xs = x_sample
    outs = [[] for _ in range(8)]
    for i in range(depth):
        last = i == depth - 1
        pw, sw = _mixer_weights(g_mix[i], w_in[i], b_ig[i], b_fg[i], g_mh[i], conv_w[i], w_out[i])
        x1p, cp, n_p, mp, cvp = _mixer_prompt(xp, pw, TB=PROMPT_TS, NSUB=PROMPT_NSUB, L=PROMPT_L)
        x1s, cs, n_s, ms, cvs = _mixer_sample(
            xs, state_conv[i], state_C[i], state_n[i].reshape(Bs, 1, QK_W),
            state_m[i].reshape(Bs, 1, NH), sw, SAMPLE_NSEQ)
        xp, xs = _ffn_stream(x1p.reshape(B * S, D_MODEL), p_prompt[i].reshape(B * S, PLE_DIM), x1s,
                             p_sample[i], g_ffn[i], g_ple[i], g_final, w_gate[i], w_up[i], w_down[i],
                             w_ple[i], w_pg[i], FFN_TM, last)
        xp = xp.reshape(B, S, D_MODEL)

        new = (cp, n_p.reshape(B, NH, DK), mp[:, :NH, 0], cvp[:, SUBLANES - (CONV_W - 1):],
               cs, n_s.reshape(Bs, NH, DK), ms.reshape(Bs, NH), cvs)
        for lst, v in zip(outs, new):
            lst.append(v)

    return (xp, xs) + tuple(jnp.stack(l) for l in outs)
```

```python
import functools

import numpy as np
import jax
import jax.numpy as jnp
from jax import lax
from jax.experimental import pallas as pl
from jax.experimental.pallas import tpu as pltpu

D_MODEL = 1024
NH = 4
DK = 64
DV = 128
MLSTM_W = NH * DV
QK_W = NH * DK
CONV_CH = 512
CONV_W = 3
D_FF = 2816
FF_CHUNK = 768
FFN_TM = 512
PLE_DIM = 256
PROMPT_L = 128
PROMPT_TS = 512
PROMPT_NSUB = 2
SAMPLE_T = 4
SAMPLE_NSEQ = 32
GATE_CAP = 15.0
EPS = 1e-6

LANES = 128
SUBLANES = 8
NEG_BIG = -1e30

VMEM_LIMIT = 56 * 1024 * 1024


def _rms(x, g):
    ms = jnp.mean(x * x, axis=-1, keepdims=True)
    return x * lax.rsqrt(ms + EPS) * g


def _bdot(a, b):
    return jnp.dot(a, b, preferred_element_type=jnp.float32)


def _log_sigmoid(x):
    return jnp.minimum(x, 0.0) - jnp.log1p(jnp.exp(-jnp.abs(x)))


def _const_spec(shape):
    nd = len(shape)
    return pl.BlockSpec(shape, lambda *_: (0,) * nd, pipeline_mode=pl.Buffered(1))


_IG0 = 2 * QK_W + MLSTM_W
_OG_IN = _IG0 + 2 * NH
PROJ_IN = _OG_IN + MLSTM_W + 3 * CONV_CH
_Q0, _K0, _V0, _OG0, _BG0, _CG0, _HC0, _GT0 = 0, 256, 512, 1024, 1536, 2048, 2560, 3072
PROJ_COLS = _GT0 + 2 * LANES
_TQ0, _TV0, _TOG0, _TG0 = 0, 256, 768, 1280
TPROJ_ROWS = _TG0 + 2 * SUBLANES
PREP_TK = 256


def _prep_weights_kernel(w_ref, wo_ref, wall_ref, wt_ref, wout_ref):
    bf16 = jnp.bfloat16
    wout_ref[...] = wo_ref[...].astype(bf16)
    wf = w_ref[...]
    tk = wf.shape[1]
    g8 = wf[_IG0:_OG_IN]
    row = lax.broadcasted_iota(jnp.int32, (SUBLANES, tk), 0)
    g_in = jnp.where(row < NH, g8, 0.0)
    g_fg = jnp.where(row < NH, pltpu.roll(g8, SUBLANES - NH, 0), 0.0)

    wt_ref[_TQ0:_TV0, :] = wf[_Q0:_K0].astype(bf16)
    wt_ref[_TV0:_TOG0, :] = wf[_V0:_OG0].astype(bf16)
    wt_ref[_TOG0:_TG0, :] = wf[_OG_IN:_OG_IN + MLSTM_W].astype(bf16)
    wt_ref[_TG0:TPROJ_ROWS, :] = jnp.concatenate([g_in, g_fg], axis=0).astype(bf16)

    wall_ref[:, 0:_OG0] = wf[0:_IG0].T.astype(bf16)
    wall_ref[:, _OG0:_GT0] = wf[_OG_IN:PROJ_IN].T.astype(bf16)
    pad = jnp.zeros((LANES - SUBLANES, tk), jnp.float32)
    wall_ref[:, _GT0:_GT0 + LANES] = jnp.concatenate([g_in, pad], axis=0).T.astype(bf16)
    wall_ref[:, _GT0 + LANES:PROJ_COLS] = jnp.concatenate([g_fg, pad], axis=0).T.astype(bf16)


def _prep_weights(w_in, w_out):
    bf16 = jnp.bfloat16
    return pl.pallas_call(
        _prep_weights_kernel,
        grid=(D_MODEL // PREP_TK,),
        in_specs=[pl.BlockSpec((PROJ_IN, PREP_TK), lambda i: (0, i)),
                  pl.BlockSpec((PREP_TK, D_MODEL), lambda i: (i, 0))],
        out_specs=[pl.BlockSpec((PREP_TK, PROJ_COLS), lambda i: (i, 0)),
                   pl.BlockSpec((TPROJ_ROWS, PREP_TK), lambda i: (0, i)),
                   pl.BlockSpec((PREP_TK, D_MODEL), lambda i: (i, 0))],
        out_shape=[jax.ShapeDtypeStruct((D_MODEL, PROJ_COLS), bf16),
                   jax.ShapeDtypeStruct((TPROJ_ROWS, D_MODEL), bf16),
                   jax.ShapeDtypeStruct((D_MODEL, D_MODEL), bf16)],
        compiler_params=pltpu.CompilerParams(
            dimension_semantics=("arbitrary",), vmem_limit_bytes=VMEM_LIMIT),
        name="prep_weights",
    )(w_in.T, w_out)


ST_ROWS = DV + 2 * SUBLANES


def _lane_scan(x, seg, op, fill):
    pos = lax.broadcasted_iota(jnp.int32, x.shape, 1) & (seg - 1)
    k = 1
    while k < seg:
        x = op(x, jnp.where(pos >= k, pltpu.roll(x, k, 1), fill))
        k *= 2
    return x


PROJ_PIECE = 256


def _run(*phases):
    live = list(phases)
    while live:
        for g in list(live):
            try:
                next(g)
            except StopIteration:
                live.remove(g)


def _mixer_prompt_kernel(x_ref, gmix_ref, wn_ref, wt_ref, gbt_ref, gmhc_ref, cw_ref, wout_ref,
                         x1_ref, c_ref, n_ref, m_ref, cvo_ref,
                         qt_s, vt_s, ogt_s, k_s, mixt_s, z_s, ucol_s, wi_s, c2_s, emt_s, wk_s, ast_s, st_s,
                         *, L, TB, NSUB):
    j = pl.program_id(1)

    @pl.when(j == 0)
    def _():
        z_s[0, 0:SUBLANES, :] = jnp.zeros((SUBLANES, CONV_CH), jnp.float32)
        st_s[...] = jnp.zeros(st_s.shape, jnp.float32)
        m_ref[...] = jnp.zeros(m_ref.shape, jnp.float32)

    m_prev = m_ref[0]
    sub = [dict(sb=sb, x_ref=x_ref, gmix_ref=gmix_ref, wn_ref=wn_ref, wt_ref=wt_ref, gbt_ref=gbt_ref,
                gmhc_ref=gmhc_ref, cw_ref=cw_ref, wout_ref=wout_ref, x1_ref=x1_ref, qt_s=qt_s.at[sb],
                vt_s=vt_s.at[sb], ogt_s=ogt_s.at[sb], k_s=k_s.at[sb], mixt_s=mixt_s.at[sb], z_s=z_s,
                ucol_s=ucol_s.at[sb], wi_s=wi_s.at[sb], c2_s=c2_s.at[sb], emt_s=emt_s.at[sb],
                wk_s=wk_s.at[sb], ast_s=ast_s.at[sb], st_s=st_s, L=L, TS=TB) for sb in range(NSUB)]
    carry = {"m": m_prev}
    _run(_prompt_project(sub[0]))
    for a, b in zip(sub[:-1], sub[1:]):
        _run(_prompt_chunks(a, carry), _prompt_project(b))
    _run(_prompt_chunks(sub[-1], carry), *[_prompt_output(s) for s in sub[:-1]])
    _run(_prompt_output(sub[-1]))
    m_ref[0] = carry["m"]
    cvo_ref[0] = z_s[NSUB - 1, TB:TB + SUBLANES, :]
    z_s[0, 0:SUBLANES, :] = z_s[NSUB - 1, TB:TB + SUBLANES, :]

    @pl.when(j == pl.num_programs(1) - 1)
    def _():
        for pr in range(NH // 2):
            state = st_s[pr]
            c_pair = state[0:DV, :].T
            c_ref[0, 2 * pr] = c_pair[0:DK, :]
            c_ref[0, 2 * pr + 1] = c_pair[DK:, :]
            n_ref[0, pr:pr + 1, :] = state[DV:DV + 1, :]


def _prompt_project(s):
    sb, L, TS = s["sb"], s["L"], s["TS"]
    x_ref, gmix_ref, wn_ref, wt_ref, gbt_ref, cw_ref = (
        s["x_ref"], s["gmix_ref"], s["wn_ref"], s["wt_ref"], s["gbt_ref"], s["cw_ref"])
    qt_s, vt_s, ogt_s, k_s, z_s = s["qt_s"], s["vt_s"], s["ogt_s"], s["k_s"], s["z_s"]
    bf16 = jnp.bfloat16

    x = x_ref[0, sb * TS:(sb + 1) * TS, :]
    a = _rms(x, gmix_ref[...]).astype(bf16)
    nt_dims = (((1,), (1,)), ((), ()))

    gt = lax.dot_general(wt_ref[_TG0:TPROJ_ROWS, :], a, nt_dims,
                         preferred_element_type=jnp.float32) + gbt_ref[...]
    ic = GATE_CAP * jnp.tanh(gt[0:SUBLANES] / GATE_CAP)
    lf = _log_sigmoid(gt[SUBLANES:])
    b = _lane_scan(lf, L, jnp.add, 0.0)
    u = ic - b
    m_loc = b + _lane_scan(u, L, jnp.maximum, -jnp.inf)
    s.update(b=b, u=u, ic=ic, m_loc=m_loc)
    yield

    k_s[...] = _bdot(a, wn_ref[:, _K0:_V0])
    yield
    hc = _bdot(a, wn_ref[:, _HC0:_GT0])
    yield
    z = _bdot(a, wn_ref[:, _CG0:_HC0]) * hc
    if sb > 0:
        z_s[sb, 0:SUBLANES, :] = z_s[sb - 1, TS:TS + SUBLANES, :]
    z_s[sb, SUBLANES:SUBLANES + TS, :] = z
    yield
    yconv = (cw_ref[0:1, :] * z_s[sb, SUBLANES - 2:SUBLANES - 2 + TS, :]
             + cw_ref[1:2, :] * z_s[sb, SUBLANES - 1:SUBLANES - 1 + TS, :]
             + cw_ref[2:3, :] * z)
    s["yc"] = (_bdot(a, wn_ref[:, _BG0:_CG0]) * yconv).astype(bf16)
    yield

    for r0 in range(0, _TG0, PROJ_PIECE):
        ut = lax.dot_general(wt_ref[r0:r0 + PROJ_PIECE, :], a, nt_dims, preferred_element_type=jnp.float32)
        if r0 < _TV0:
            qt_s[r0:r0 + PROJ_PIECE, :] = ut * (DK ** -0.5)
        elif r0 < _TOG0:
            vt_s[r0 - _TV0:r0 - _TV0 + PROJ_PIECE, :] = ut
        else:
            ogt_s[r0 - _TOG0:r0 - _TOG0 + PROJ_PIECE, :] = ut
        yield


def _prompt_chunks(s, carry):
    L, TS = s["L"], s["TS"]
    qt_s, vt_s, ogt_s, k_s, mixt_s, ucol_s, wi_s, c2_s, emt_s, wk_s, ast_s, st_s = (
        s["qt_s"], s["vt_s"], s["ogt_s"], s["k_s"], s["mixt_s"], s["ucol_s"], s["wi_s"],
        s["c2_s"], s["emt_s"], s["wk_s"], s["ast_s"], s["st_s"])
    b, u, ic, m_loc = s["b"], s["u"], s["ic"], s["m_loc"]
    nc = TS // L
    bf16 = jnp.bfloat16
    m_prev = carry["m"]

    for c in range(nc):
        sl = slice(c * L, (c + 1) * L)
        bc, mlc = b[:, sl], m_loc[:, sl]
        b_last = jnp.broadcast_to(bc[:, L - 1:L], bc.shape)
        m_new = jnp.maximum(b_last + m_prev, jnp.broadcast_to(mlc[:, L - 1:L], bc.shape))
        g = bc + m_prev
        mt = jnp.maximum(g, mlc)
        wi_s[:, sl] = jnp.exp(g - mt)
        c2_s[:, sl] = mt - bc
        emt_s[:, sl] = jnp.exp(-mt)
        wk_s[:, sl] = jnp.exp(b_last - bc + ic[:, sl] - m_new)
        ast_s[c] = jnp.exp(b_last + m_prev - m_new)
        upad = jnp.concatenate([u[:, sl], jnp.zeros((L - SUBLANES, L), jnp.float32)], axis=0)
        ucol_s[c * L:(c + 1) * L, :] = upad.T
        m_prev = m_new
    carry["m"] = m_prev
    yield

    s_i = lax.broadcasted_iota(jnp.int32, (L, L), 0)
    t_i = lax.broadcasted_iota(jnp.int32, (L, L), 1)
    causal = s_i <= t_i
    low_half = lax.broadcasted_iota(jnp.int32, (L, LANES), 1) < DK
    zeros_q = jnp.zeros((DK, L), jnp.float32)
    zeros_p = jnp.zeros((L, L), bf16)

    def pair_row(ref, h0, rows):
        return jnp.concatenate([ref[h0:h0 + 1, rows], ref[h0 + 1:h0 + 2, rows]], axis=1)

    for c in range(nc):
        rows = slice(c * L, (c + 1) * L)
        for pr in range(NH // 2):
            h0 = 2 * pr
            hv0 = slice(h0 * DV, (h0 + 1) * DV)
            hv1 = slice((h0 + 1) * DV, (h0 + 2) * DV)
            kp = k_s[rows, pr * LANES:(pr + 1) * LANES]
            q0 = qt_s[h0 * DK:(h0 + 1) * DK, rows]
            q1 = qt_s[(h0 + 1) * DK:(h0 + 2) * DK, rows]
            qbd = jnp.concatenate([jnp.concatenate([q0, zeros_q], axis=1),
                                   jnp.concatenate([zeros_q, q1], axis=1)], axis=0).astype(bf16)
            st = _bdot(kp.astype(bf16), qbd)
            arg = jnp.concatenate(
                [jnp.where(causal, ucol_s[rows, h:h + 1] - c2_s[h:h + 1, rows], -jnp.inf)
                 for h in (h0, h0 + 1)], axis=1)
            pt = st * jnp.exp(arg)
            rs = jnp.sum(pt, axis=0, keepdims=True)
            ptb = pt.astype(bf16)
            pbd = jnp.concatenate([jnp.concatenate([ptb[:, 0:L], zeros_p], axis=1),
                                   jnp.concatenate([zeros_p, ptb[:, L:]], axis=1)], axis=0)
            vt = jnp.concatenate([vt_s[hv0, rows], vt_s[hv1, rows]], axis=1)
            state = st_s[pr]
            sq = _bdot(state.astype(bf16), qbd) * pair_row(wi_s, h0, rows)
            num = _bdot(vt.astype(bf16), pbd) + sq[0:DV]
            den = sq[DV:DV + 1] + rs
            hh = num * (1.0 / jnp.maximum(jnp.abs(den), pair_row(emt_s, h0, rows)))
            hn = hh * lax.rsqrt(jnp.mean(hh * hh, axis=0, keepdims=True) + EPS)
            mixt_s[hv0, rows] = jax.nn.sigmoid(ogt_s[hv0, rows]) * hn[:, 0:L]
            mixt_s[hv1, rows] = jax.nn.sigmoid(ogt_s[hv1, rows]) * hn[:, L:]
            wkr = pair_row(wk_s, h0, rows)
            vw = jnp.concatenate([vt * wkr, jnp.broadcast_to(wkr, (2 * SUBLANES, 2 * L))], axis=0)
            km = jnp.concatenate([jnp.where(low_half, kp, 0.0), jnp.where(low_half, 0.0, kp)], axis=0)
            decay = jnp.where(low_half[0:1, :], ast_s[c][h0:h0 + 1, :], ast_s[c][h0 + 1:h0 + 2, :])
            st_s[pr] = decay * state + _bdot(vw.astype(bf16), km.astype(bf16))
            yield


def _prompt_output(s):
    sb, TS = s["sb"], s["TS"]
    x_ref, wout_ref, x1_ref, mixt_s = s["x_ref"], s["wout_ref"], s["x1_ref"], s["mixt_s"]
    tok = slice(sb * TS, (sb + 1) * TS)
    hm = mixt_s[...].T * s["gmhc_ref"][...]
    mix = jnp.concatenate([hm.astype(jnp.bfloat16), s["yc"]], axis=1)
    yield
    for c0 in range(0, D_MODEL, PROJ_PIECE):
        cols = slice(c0, c0 + PROJ_PIECE)
        x1_ref[0, tok, cols] = x_ref[0, tok, cols] + _bdot(mix, wout_ref[:, cols])
        yield


def _mixer_prompt(x, weights, TB, NSUB, L):
    B, S, _ = x.shape
    TS = TB * NSUB
    nb = S // TS
    f32 = jnp.float32
    kern = functools.partial(_mixer_prompt_kernel, L=L, TB=TB, NSUB=NSUB)
    return pl.pallas_call(
        kern,
        grid=(B, nb),
        in_specs=[
            pl.BlockSpec((1, TS, D_MODEL), lambda b, j: (b, j, 0)),
            _const_spec((1, D_MODEL)),
            _const_spec((D_MODEL, PROJ_COLS)),
            _const_spec((TPROJ_ROWS, D_MODEL)),
            _const_spec((2 * SUBLANES, 1)),
            _const_spec((1, MLSTM_W)),
            _const_spec((CONV_W, CONV_CH)),
            _const_spec((D_MODEL, D_MODEL)),
        ],
        out_specs=[
            pl.BlockSpec((1, TS, D_MODEL), lambda b, j: (b, j, 0)),
            pl.BlockSpec((1, NH, DK, DV), lambda b, j: (b, 0, 0, 0)),
            pl.BlockSpec((1, NH // 2, 2 * DK), lambda b, j: (b, 0, 0)),
            pl.BlockSpec((1, SUBLANES, LANES), lambda b, j: (b, 0, 0)),
            pl.BlockSpec((1, SUBLANES, CONV_CH), lambda b, j: (b, 0, 0)),
        ],
        out_shape=[
            jax.ShapeDtypeStruct((B, S, D_MODEL), f32),
            jax.ShapeDtypeStruct((B, NH, DK, DV), f32),
            jax.ShapeDtypeStruct((B, NH // 2, 2 * DK), f32),
            jax.ShapeDtypeStruct((B, SUBLANES, LANES), f32),
            jax.ShapeDtypeStruct((B, SUBLANES, CONV_CH), f32),
        ],
        scratch_shapes=[
            pltpu.VMEM((NSUB, QK_W, TB), f32),
            pltpu.VMEM((NSUB, MLSTM_W, TB), f32),
            pltpu.VMEM((NSUB, MLSTM_W, TB), f32),
            pltpu.VMEM((NSUB, TB, QK_W), f32),
            pltpu.VMEM((NSUB, MLSTM_W, TB), f32),
            pltpu.VMEM((NSUB, TB + SUBLANES, CONV_CH), f32),
            pltpu.VMEM((NSUB, TB, LANES), f32),
            pltpu.VMEM((NSUB, SUBLANES, TB), f32),
            pltpu.VMEM((NSUB, SUBLANES, TB), f32),
            pltpu.VMEM((NSUB, SUBLANES, TB), f32),
            pltpu.VMEM((NSUB, SUBLANES, TB), f32),
            pltpu.VMEM((NSUB, TB // L, SUBLANES, LANES), f32),
            pltpu.VMEM((NH // 2, ST_ROWS, LANES), f32),
        ],
        compiler_params=pltpu.CompilerParams(
            dimension_semantics=("arbitrary", "arbitrary"), vmem_limit_bytes=VMEM_LIMIT),
        name="mixer_prompt",
    )(x, *weights)


def _row_scan(x, seg, op, fill):
    pos = lax.broadcasted_iota(jnp.int32, x.shape, 0) & (seg - 1)
    k = 1
    while k < seg:
        x = op(x, jnp.where(pos >= k, pltpu.roll(x, k, 0), fill))
        k *= 2
    return x


def _split_dot(x, sel, parts):
    acc = None
    rem = x
    for p in range(parts):
        hi = rem.astype(jnp.bfloat16)
        d = _bdot(hi, sel)
        acc = d if acc is None else acc + d
        if p + 1 < parts:
            rem = rem - hi.astype(jnp.float32)
    return acc


def _selectors():
    h = np.arange(NH)
    seg_qk = np.zeros((QK_W, LANES), np.float32)
    seg_qk[np.arange(QK_W), np.arange(QK_W) // DK] = 1.0
    exp_v = np.zeros((LANES, MLSTM_W), np.float32)
    exp_k = np.zeros((LANES, QK_W), np.float32)
    for i in h:
        exp_v[i, i * DV:(i + 1) * DV] = 1.0
        exp_k[i, i * DK:(i + 1) * DK] = 1.0
    mean_v = np.kron(np.eye(NH, dtype=np.float32), np.full((DV, DV), 1.0 / DV, np.float32))
    return tuple(jnp.asarray(m, jnp.bfloat16) for m in (seg_qk, exp_v, exp_k, mean_v))


def _mixer_sample_kernel(x_ref, cv_ref, c0_ref, n0_ref, m0_ref,
                         gmix_ref, win_ref, gb_ref, gmh_ref, cw_ref, wout_ref,
                         segqk_ref, expv_ref, expk_ref, meanv_ref,
                         x1_ref, cst_ref, nst_ref, mst_ref, cvo_ref,
                         xp_s, z_s, mp_s, qc_s, *, NSEQ):
    R = SUBLANES
    TS = NSEQ * R
    bf16 = jnp.bfloat16
    f32 = jnp.float32

    xp_s[:, SAMPLE_T:, :] = jnp.zeros((NSEQ, R - SAMPLE_T, D_MODEL), f32)
    xp_s[:, 0:SAMPLE_T, :] = x_ref[...]
    x = xp_s[...].reshape(TS, D_MODEL)
    a = _rms(x, gmix_ref[...]).astype(bf16)
    u = _bdot(a, win_ref[...])

    pos = lax.broadcasted_iota(jnp.int32, (TS, 1), 0) & (R - 1)
    real = pos < SAMPLE_T

    z = u[:, _CG0:_HC0] * u[:, _HC0:_GT0]
    z3 = z.reshape(NSEQ, R, CONV_CH)
    z_s[...] = z3
    z_s[:, R - (CONV_W - 1):, :] = cv_ref[...]
    zf = z_s[...].reshape(TS, CONV_CH)
    zm1 = jnp.where(pos >= 1, pltpu.roll(z, 1, 0), pltpu.roll(zf, TS - (R - 1), 0))
    zm2 = jnp.where(pos >= 2, pltpu.roll(z, 2, 0), pltpu.roll(zf, TS - (R - 2), 0))
    yc = u[:, _BG0:_CG0] * (cw_ref[0:1, :] * zm2 + cw_ref[1:2, :] * zm1 + cw_ref[2:3, :] * z)
    cvo_ref[...] = z3[:, SAMPLE_T - (CONV_W - 1):SAMPLE_T, :]

    gates = u[:, _GT0:PROJ_COLS] + gb_ref[...]
    ic = jnp.where(real, GATE_CAP * jnp.tanh(gates[:, :LANES] / GATE_CAP), NEG_BIG)
    lf = jnp.where(real, _log_sigmoid(gates[:, LANES:]), 0.0)
    b = _row_scan(lf, R, jnp.add, 0.0)
    uu = ic - b
    m_loc = b + _row_scan(uu, R, jnp.maximum, -jnp.inf)

    mp_s[...] = jnp.zeros(mp_s.shape, f32)
    mp_s[:, :, 0:NH] = m0_ref[...]
    m_prev = mp_s[...]
    b3 = b.reshape(NSEQ, R, LANES)
    ml3 = m_loc.reshape(NSEQ, R, LANES)
    b_last = b3[:, R - 1:R, :]
    m_new = jnp.maximum(b_last + m_prev, ml3[:, R - 1:R, :])
    g3 = b3 + m_prev
    mt3 = jnp.maximum(g3, ml3)
    wi = jnp.exp(g3 - mt3).reshape(TS, LANES)
    c2 = (mt3 - b3).reshape(TS, LANES)
    emt = jnp.exp(-mt3).reshape(TS, LANES)
    wk = jnp.exp(b_last - b3 + ic.reshape(NSEQ, R, LANES) - m_new).reshape(TS, LANES)
    a_st = jnp.broadcast_to(jnp.exp(b_last + m_prev - m_new), (NSEQ, R, LANES)).reshape(TS, LANES)
    mst_ref[...] = m_new[:, :, 0:NH]

    q = u[:, _Q0:_K0] * (DK ** -0.5)
    k = u[:, _K0:_V0]
    v = u[:, _V0:_OG0]
    rs = jnp.zeros((TS, LANES), f32)
    num = jnp.zeros((TS, MLSTM_W), f32)
    for d in range(SAMPLE_T):
        kd = k if d == 0 else pltpu.roll(k, d, 0)
        vd = v if d == 0 else pltpu.roll(v, d, 0)
        ud = uu if d == 0 else pltpu.roll(uu, d, 0)
        p = _split_dot(q * kd, segqk_ref[...], 2) * jnp.exp(ud - c2)
        rs = rs + p
        num = num + _split_dot(p, expv_ref[...], 2) * vd

    n0x = jnp.broadcast_to(n0_ref[...], (NSEQ, R, QK_W)).reshape(TS, QK_W)
    qn = _split_dot(q * n0x, segqk_ref[...], 2)
    den = wi * qn + rs
    rden = 1.0 / jnp.maximum(jnp.abs(den), emt)
    kw = k * _split_dot(wk, expk_ref[...], 2)
    ax = _split_dot(a_st, expv_ref[...], 3)
    lane_q = lax.broadcasted_iota(jnp.int32, (R, QK_W), 1)
    for i in range(NSEQ):
        rows = slice(i * R, (i + 1) * R)
        qi = q[rows, :]
        lhs = jnp.concatenate(
            [jnp.where((lane_q >= h * DK) & (lane_q < (h + 1) * DK), qi, 0.0) for h in range(NH)], axis=0)
        c0 = c0_ref[i]
        qc = _bdot(lhs.astype(bf16), c0.reshape(QK_W, DV).astype(bf16))
        qc_s[rows, :] = jnp.concatenate([qc[h * R:(h + 1) * R, :] for h in range(NH)], axis=1)
        kwi = kw[rows, :].astype(bf16)
        vi = v[rows, :].astype(bf16)
        for h in range(NH):
            dc = lax.dot_general(kwi[:, h * DK:(h + 1) * DK], vi[:, h * DV:(h + 1) * DV],
                                 (((0,), (0,)), ((), ())), preferred_element_type=f32)
            cst_ref[i, h] = ax[i * R:i * R + 1, h * DV:(h + 1) * DV] * c0[h] + dc

    hh = (_split_dot(wi, expv_ref[...], 2) * qc_s[...] + num) * _split_dot(rden, expv_ref[...], 2)
    ms = _split_dot(hh * hh, meanv_ref[...], 2)
    hm = jax.nn.sigmoid(u[:, _OG0:_BG0]) * (hh * lax.rsqrt(ms + EPS) * gmh_ref[...])
    mix = jnp.concatenate([hm, yc], axis=1).astype(bf16)
    out = x + _bdot(mix, wout_ref[...])
    x1_ref[...] = out.reshape(NSEQ, R, D_MODEL)[:, 0:SAMPLE_T, :]

    a_k = _split_dot(a_st, expk_ref[...], 3).reshape(NSEQ, R, QK_W)[:, 0:1, :]
    nst_ref[...] = a_k * n0_ref[...] + jnp.sum(kw.reshape(NSEQ, R, QK_W), axis=1, keepdims=True)


def _mixer_sample(x, cv, c0, n0, m0, weights, NSEQ):
    Bs = x.shape[0]
    f32 = jnp.float32
    kern = functools.partial(_mixer_sample_kernel, NSEQ=NSEQ)
    x_spec = pl.BlockSpec((NSEQ, SAMPLE_T, D_MODEL), lambda t: (t, 0, 0))
    cv_spec = pl.BlockSpec((NSEQ, CONV_W - 1, CONV_CH), lambda t: (t, 0, 0))
    c_spec = pl.BlockSpec((NSEQ, NH, DK, DV), lambda t: (t, 0, 0, 0))
    n_spec = pl.BlockSpec((NSEQ, 1, QK_W), lambda t: (t, 0, 0))
    m_spec = pl.BlockSpec((NSEQ, 1, NH), lambda t: (t, 0, 0))
    return pl.pallas_call(
        kern,
        grid=(Bs // NSEQ,),
        in_specs=[
            x_spec, cv_spec, c_spec, n_spec, m_spec,
            _const_spec((1, D_MODEL)),
            _const_spec((D_MODEL, PROJ_COLS)),
            _const_spec((1, 2 * LANES)),
            _const_spec((1, MLSTM_W)),
            _const_spec((CONV_W, CONV_CH)),
            _const_spec((D_MODEL, D_MODEL)),
            _const_spec((QK_W, LANES)),
            _const_spec((LANES, MLSTM_W)),
            _const_spec((LANES, QK_W)),
            _const_spec((MLSTM_W, MLSTM_W)),
        ],
        out_specs=[x_spec, c_spec, n_spec, m_spec, cv_spec],
        out_shape=[
            jax.ShapeDtypeStruct((Bs, SAMPLE_T, D_MODEL), f32),
            jax.ShapeDtypeStruct((Bs, NH, DK, DV), f32),
            jax.ShapeDtypeStruct((Bs, 1, QK_W), f32),
            jax.ShapeDtypeStruct((Bs, 1, NH), f32),
            jax.ShapeDtypeStruct((Bs, CONV_W - 1, CONV_CH), f32),
        ],
        scratch_shapes=[
            pltpu.VMEM((NSEQ, SUBLANES, D_MODEL), f32),
            pltpu.VMEM((NSEQ, SUBLANES, CONV_CH), f32),
            pltpu.VMEM((NSEQ, 1, LANES), f32),
            pltpu.VMEM((NSEQ * SUBLANES, MLSTM_W), f32),
        ],
        compiler_params=pltpu.CompilerParams(
            dimension_semantics=("arbitrary",), vmem_limit_bytes=VMEM_LIMIT),
        name="mixer_sample",
    )(x, cv, c0, n0, m0, *weights)


def _mixer_weights(g_mix, w_in, b_ig, b_fg, g_mh, conv_w, w_out):
    f32 = jnp.float32
    w_all, wt, wout = _prep_weights(w_in, w_out)
    gmix = g_mix.reshape(1, D_MODEL)
    zeros = jnp.zeros((SUBLANES - NH,), f32)
    gb_col = jnp.concatenate([b_ig, zeros, b_fg, zeros]).reshape(2 * SUBLANES, 1)
    zeros = jnp.zeros((LANES - NH,), f32)
    gb_row = jnp.concatenate([b_ig, zeros, b_fg, zeros]).reshape(1, 2 * LANES)
    gmh = g_mh.reshape(1, MLSTM_W)
    prompt = (gmix, w_all, wt, gb_col, gmh, conv_w, wout)
    sample = (gmix, w_all, gb_row, gmh, conv_w, wout) + _selectors()
    return prompt, sample


FF_STREAM = 256
_N_FF_PIECES = D_FF // FF_STREAM
_N_PIECES = _N_FF_PIECES + PLE_DIM // FF_STREAM + D_MODEL // FF_STREAM


def _ffn_rows(x, pe, gffn_ref, gple_ref, gfin_ref, wg_s, wu_s, wd_s, wple_s, wpg_s, *, chunk, final_norm,
              before_chunk=None, before_tail=None):
    bf16 = jnp.bfloat16
    f = _rms(x, gffn_ref[...]).astype(bf16)
    e = None
    if before_tail is None:
        e = _rms(_bdot(pe.astype(bf16), wple_s[...]), gple_ref[...])
    for c0 in range(0, D_FF, chunk):
        c1 = min(c0 + chunk, D_FF)
        if before_chunk is not None:
            before_chunk(c0 // chunk)
        gate = _bdot(f, wg_s[:, c0:c1])
        up = _bdot(f, wu_s[:, c0:c1])
        hmid = (gate * jax.nn.sigmoid(gate) * up).astype(bf16)
        x = x + _bdot(hmid, wd_s[c0:c1, :])
    if before_tail is not None:
        before_tail()
        e = _rms(_bdot(pe.astype(bf16), wple_s[...]), gple_ref[...])
    x = x + jax.nn.sigmoid(_bdot(x.astype(bf16), wpg_s[...])) * e
    if final_norm:
        x = _rms(x, gfin_ref[...])
    return x


def _ffn_stream_kernel(xp_ref, pp_ref, xs_ref, ps_ref, gffn_ref, gple_ref, gfin_ref,
                       wg_hbm, wu_hbm, wd_hbm, wple_hbm, wpg_hbm,
                       yp_ref, ys_ref,
                       wg_s, wu_s, wd_s, wple_s, wpg_s, stg_g, stg_u, stg_d, sem, *, final_norm):
    t = pl.program_id(0)
    bf16 = jnp.bfloat16
    weights = (wg_s, wu_s, wd_s, wple_s, wpg_s)

    def copies(k):
        slot = k % 2
        if k < _N_FF_PIECES:
            cols = pl.ds(k * FF_STREAM, FF_STREAM)
            return [pltpu.make_async_copy(wg_hbm.at[:, cols], stg_g.at[slot], sem.at[0, slot]),
                    pltpu.make_async_copy(wu_hbm.at[:, cols], stg_u.at[slot], sem.at[1, slot]),
                    pltpu.make_async_copy(wd_hbm.at[cols, :], stg_d.at[slot], sem.at[2, slot])]
        if k == _N_FF_PIECES:
            return [pltpu.make_async_copy(wple_hbm, stg_d.at[slot], sem.at[2, slot])]
        rows = pl.ds((k - _N_FF_PIECES - 1) * FF_STREAM, FF_STREAM)
        return [pltpu.make_async_copy(wpg_hbm.at[rows, :], stg_d.at[slot], sem.at[2, slot])]

    def land(k):
        if k + 1 < _N_PIECES:
            for cp in copies(k + 1):
                cp.start()
        for cp in copies(k):
            cp.wait()
        slot = k % 2
        if k < _N_FF_PIECES:
            cols = slice(k * FF_STREAM, (k + 1) * FF_STREAM)
            wg_s[:, cols] = stg_g[slot].astype(bf16)
            wu_s[:, cols] = stg_u[slot].astype(bf16)
            wd_s[cols, :] = stg_d[slot].astype(bf16)
        elif k == _N_FF_PIECES:
            wple_s[...] = stg_d[slot].astype(bf16)
        else:
            r0 = (k - _N_FF_PIECES - 1) * FF_STREAM
            wpg_s[r0:r0 + FF_STREAM, :] = stg_d[slot].astype(bf16)

    @pl.when(t == 0)
    def _():
        for cp in copies(0):
            cp.start()
        n_tok = xs_ref.shape[0] * xs_ref.shape[1]

        def tail():
            for k in range(_N_FF_PIECES, _N_PIECES):
                land(k)

        y = _ffn_rows(xs_ref[...].reshape(n_tok, D_MODEL), ps_ref[...].reshape(n_tok, PLE_DIM),
                      gffn_ref, gple_ref, gfin_ref, *weights, chunk=FF_STREAM, final_norm=final_norm,
                      before_chunk=land, before_tail=tail)
        ys_ref[...] = y.reshape(ys_ref.shape)

    @pl.when(t > 0)
    def _():
        yp_ref[...] = _ffn_rows(xp_ref[...], pp_ref[...], gffn_ref, gple_ref, gfin_ref, *weights,
                                chunk=FF_CHUNK, final_norm=final_norm)


def _ffn_stream(xp, pp, xs, ps, g_ffn, g_ple, g_final, w_gate, w_up, w_down, w_ple, w_pg, TM, final_norm):
    T = xp.shape[0]
    f32, bf16 = jnp.float32, jnp.bfloat16
    kern = functools.partial(_ffn_stream_kernel, final_norm=final_norm)
    row_map = lambda t: (jnp.maximum(t - 1, 0), 0)
    hbm = pl.BlockSpec(memory_space=pl.ANY)
    return pl.pallas_call(
        kern,
        grid=(T // TM + 1,),
        in_specs=[
            pl.BlockSpec((TM, D_MODEL), row_map),
            pl.BlockSpec((TM, PLE_DIM), row_map),
            _const_spec(xs.shape),
            _const_spec(ps.shape),
            _const_spec((1, D_MODEL)),
            _const_spec((1, D_MODEL)),
            _const_spec((1, D_MODEL)),
            hbm, hbm, hbm, hbm, hbm,
        ],
        out_specs=[pl.BlockSpec((TM, D_MODEL), row_map),
                   pl.BlockSpec(xs.shape, lambda t: (0, 0, 0))],
        out_shape=[jax.ShapeDtypeStruct(xp.shape, f32), jax.ShapeDtypeStruct(xs.shape, f32)],
        scratch_shapes=[
            pltpu.VMEM((D_MODEL, D_FF), bf16),
            pltpu.VMEM((D_MODEL, D_FF), bf16),
            pltpu.VMEM((D_FF, D_MODEL), bf16),
            pltpu.VMEM((PLE_DIM, D_MODEL), bf16),
            pltpu.VMEM((D_MODEL, D_MODEL), bf16),
            pltpu.VMEM((2, D_MODEL, FF_STREAM), f32),
            pltpu.VMEM((2, D_MODEL, FF_STREAM), f32),
            pltpu.VMEM((2, FF_STREAM, D_MODEL), f32),
            pltpu.SemaphoreType.DMA((3, 2)),
        ],
        compiler_params=pltpu.CompilerParams(
            dimension_semantics=("arbitrary",), vmem_limit_bytes=VMEM_LIMIT),
        name="ffn",
    )(xp, pp, xs, ps, g_ffn.reshape(1, D_MODEL), g_ple.reshape(1, D_MODEL), g_final.reshape(1, D_MODEL),
      w_gate, w_up, w_down, w_ple, w_pg)


def kernel(x_prompt, x_sample, p_prompt, p_sample, state_C, state_n, state_m, state_conv,
           g_mix, w_in, b_ig, b_fg, g_mh, conv_w, w_out, g_ffn, w_gate, w_up, w_down,
           w_ple, g_ple, w_pg, g_final):
    bf16 = jnp.bfloat16
    depth = g_mix.shape[0]
    B, S, _ = x_prompt.shape
    Bs, Ss, _ = x_sample.shape
    assert Ss == SAMPLE_T and S % PROMPT_TS == 0 and Bs % SAMPLE_NSEQ == 0

    xp = x_prompt
    xs = x_sample
    outs = [[] for _ in range(8)]
    for i in range(depth):
        last = i == depth - 1
        pw, sw = _mixer_weights(g_mix[i], w_in[i], b_ig[i], b_fg[i], g_mh[i], conv_w[i], w_out[i])
        x1p, cp, n_p, mp, cvp = _mixer_prompt(xp, pw, TB=PROMPT_TS, NSUB=PROMPT_NSUB, L=PROMPT_L)
        x1s, cs, n_s, ms, cvs = _mixer_sample(
            xs, state_conv[i], state_C[i], state_n[i].reshape(Bs, 1, QK_W),
            state_m[i].reshape(Bs, 1, NH), sw, SAMPLE_NSEQ)
        xp, xs = _ffn_stream(x1p.reshape(B * S, D_MODEL), p_prompt[i].reshape(B * S, PLE_DIM), x1s,
                             p_sample[i], g_ffn[i], g_ple[i], g_final, w_gate[i], w_up[i], w_down[i],
                             w_ple[i], w_pg[i], FFN_TM, last)
        xp = xp.reshape(B, S, D_MODEL)

        new = (cp, n_p.reshape(B, NH, DK), mp[:, :NH, 0], cvp[:, SUBLANES - (CONV_W - 1):],
               cs, n_s.reshape(Bs, NH, DK), ms.reshape(Bs, NH), cvs)
        for lst, v in zip(outs, new):
            lst.append(v)

    return (xp, xs) + tuple(jnp.stack(l) for l in outs)
```

```python
import functools

import numpy as np
import jax
import jax.numpy as jnp
from jax import lax
from jax.experimental import pallas as pl
from jax.experimental.pallas import tpu as pltpu

D_MODEL = 1024
NH = 4
DK = 64
DV = 128
MLSTM_W = NH * DV
QK_W = NH * DK
CONV_CH = 512
CONV_W = 3
D_FF = 2816
FF_CHUNK = 768
FFN_TM = 512
FFN_TAIL_PIECES = 2
PLE_DIM = 256
PROMPT_L = 128
PROMPT_TS = 512
PROMPT_NSUB = 2
SAMPLE_T = 4
SAMPLE_NSEQ = 32
GATE_CAP = 15.0
EPS = 1e-6

LANES = 128
SUBLANES = 8
NEG_BIG = -1e30

VMEM_LIMIT = 56 * 1024 * 1024


def _rms(x, g):
    ms = jnp.mean(x * x, axis=-1, keepdims=True)
    return x * lax.rsqrt(ms + EPS) * g


def _bdot(a, b):
    return jnp.dot(a, b, preferred_element_type=jnp.float32)


def _log_sigmoid(x):
    return jnp.minimum(x, 0.0) - jnp.log1p(jnp.exp(-jnp.abs(x)))


def _const_spec(shape):
    nd = len(shape)
    return pl.BlockSpec(shape, lambda *_: (0,) * nd, pipeline_mode=pl.Buffered(1))


_IG0 = 2 * QK_W + MLSTM_W
_OG_IN = _IG0 + 2 * NH
PROJ_IN = _OG_IN + MLSTM_W + 3 * CONV_CH
_Q0, _K0, _V0, _OG0, _BG0, _CG0, _HC0, _GT0 = 0, 256, 512, 1024, 1536, 2048, 2560, 3072
PROJ_COLS = _GT0 + 2 * LANES
_TQ0, _TV0, _TOG0, _TG0 = 0, 256, 768, 1280
TPROJ_ROWS = _TG0 + 2 * SUBLANES
PREP_TK = 256


def _prep_weights_kernel(w_ref, wo_ref, wall_ref, wt_ref, wout_ref):
    bf16 = jnp.bfloat16
    wout_ref[...] = wo_ref[...].astype(bf16)
    wf = w_ref[...]
    tk = wf.shape[1]
    g8 = wf[_IG0:_OG_IN]
    row = lax.broadcasted_iota(jnp.int32, (SUBLANES, tk), 0)
    g_in = jnp.where(row < NH, g8, 0.0)
    g_fg = jnp.where(row < NH, pltpu.roll(g8, SUBLANES - NH, 0), 0.0)

    wt_ref[_TQ0:_TV0, :] = wf[_Q0:_K0].astype(bf16)
    wt_ref[_TV0:_TOG0, :] = wf[_V0:_OG0].astype(bf16)
    wt_ref[_TOG0:_TG0, :] = wf[_OG_IN:_OG_IN + MLSTM_W].astype(bf16)
    wt_ref[_TG0:TPROJ_ROWS, :] = jnp.concatenate([g_in, g_fg], axis=0).astype(bf16)

    wall_ref[:, 0:_OG0] = wf[0:_IG0].T.astype(bf16)
    wall_ref[:, _OG0:_GT0] = wf[_OG_IN:PROJ_IN].T.astype(bf16)
    pad = jnp.zeros((LANES - SUBLANES, tk), jnp.float32)
    wall_ref[:, _GT0:_GT0 + LANES] = jnp.concatenate([g_in, pad], axis=0).T.astype(bf16)
    wall_ref[:, _GT0 + LANES:PROJ_COLS] = jnp.concatenate([g_fg, pad], axis=0).T.astype(bf16)


def _prep_weights(w_in, w_out):
    bf16 = jnp.bfloat16
    return pl.pallas_call(
        _prep_weights_kernel,
        grid=(D_MODEL // PREP_TK,),
        in_specs=[pl.BlockSpec((PROJ_IN, PREP_TK), lambda i: (0, i)),
                  pl.BlockSpec((PREP_TK, D_MODEL), lambda i: (i, 0))],
        out_specs=[pl.BlockSpec((PREP_TK, PROJ_COLS), lambda i: (i, 0)),
                   pl.BlockSpec((TPROJ_ROWS, PREP_TK), lambda i: (0, i)),
                   pl.BlockSpec((PREP_TK, D_MODEL), lambda i: (i, 0))],
        out_shape=[jax.ShapeDtypeStruct((D_MODEL, PROJ_COLS), bf16),
                   jax.ShapeDtypeStruct((TPROJ_ROWS, D_MODEL), bf16),
                   jax.ShapeDtypeStruct((D_MODEL, D_MODEL), bf16)],
        compiler_params=pltpu.CompilerParams(
            dimension_semantics=("arbitrary",), vmem_limit_bytes=VMEM_LIMIT),
        name="prep_weights",
    )(w_in.T, w_out)


ST_ROWS = DV + 2 * SUBLANES


def _lane_scan(x, seg, op, fill):
    pos = lax.broadcasted_iota(jnp.int32, x.shape, 1) & (seg - 1)
    k = 1
    while k < seg:
        x = op(x, jnp.where(pos >= k, pltpu.roll(x, k, 1), fill))
        k *= 2
    return x


PROJ_PIECE = 256


def _run(*phases):
    live = list(phases)
    while live:
        for g in list(live):
            try:
                next(g)
            except StopIteration:
                live.remove(g)


def _mixer_prompt_kernel(x_ref, gmix_ref, wn_ref, wt_ref, gbt_ref, gmhc_ref, cw_ref, wout_ref,
                         x1_ref, c_ref, n_ref, m_ref, cvo_ref,
                         qt_s, vt_s, ogt_s, k_s, mixt_s, z_s, ucol_s, wi_s, c2_s, emt_s, wk_s, ast_s, st_s,
                         *, L, TB, NSUB):
    j = pl.program_id(1)

    @pl.when(j == 0)
    def _():
        z_s[0, 0:SUBLANES, :] = jnp.zeros((SUBLANES, CONV_CH), jnp.float32)
        st_s[...] = jnp.zeros(st_s.shape, jnp.float32)
        m_ref[...] = jnp.zeros(m_ref.shape, jnp.float32)

    m_prev = m_ref[0]
    sub = [dict(sb=sb, x_ref=x_ref, gmix_ref=gmix_ref, wn_ref=wn_ref, wt_ref=wt_ref, gbt_ref=gbt_ref,
                gmhc_ref=gmhc_ref, cw_ref=cw_ref, wout_ref=wout_ref, x1_ref=x1_ref, qt_s=qt_s.at[sb],
                vt_s=vt_s.at[sb], ogt_s=ogt_s.at[sb], k_s=k_s.at[sb], mixt_s=mixt_s.at[sb], z_s=z_s,
                ucol_s=ucol_s.at[sb], wi_s=wi_s.at[sb], c2_s=c2_s.at[sb], emt_s=emt_s.at[sb],
                wk_s=wk_s.at[sb], ast_s=ast_s.at[sb], st_s=st_s, L=L, TS=TB) for sb in range(NSUB)]
    carry = {"m": m_prev}
    _run(_prompt_project(sub[0]))
    for a, b in zip(sub[:-1], sub[1:]):
        _run(_prompt_chunks(a, carry), _prompt_project(b))
    _run(_prompt_chunks(sub[-1], carry), *[_prompt_output(s) for s in sub[:-1]])
    _run(_prompt_output(sub[-1]))
    m_ref[0] = carry["m"]
    cvo_ref[0] = z_s[NSUB - 1, TB:TB + SUBLANES, :]
    z_s[0, 0:SUBLANES, :] = z_s[NSUB - 1, TB:TB + SUBLANES, :]

    @pl.when(j == pl.num_programs(1) - 1)
    def _():
        for pr in range(NH // 2):
            state = st_s[pr]
            c_pair = state[0:DV, :].T
            c_ref[0, 2 * pr] = c_pair[0:DK, :]
            c_ref[0, 2 * pr + 1] = c_pair[DK:, :]
            n_ref[0, pr:pr + 1, :] = state[DV:DV + 1, :]


def _prompt_project(s):
    sb, L, TS = s["sb"], s["L"], s["TS"]
    x_ref, gmix_ref, wn_ref, wt_ref, gbt_ref, cw_ref = (
        s["x_ref"], s["gmix_ref"], s["wn_ref"], s["wt_ref"], s["gbt_ref"], s["cw_ref"])
    qt_s, vt_s, ogt_s, k_s, z_s = s["qt_s"], s["vt_s"], s["ogt_s"], s["k_s"], s["z_s"]
    bf16 = jnp.bfloat16

    x = x_ref[0, sb * TS:(sb + 1) * TS, :]
    a = _rms(x, gmix_ref[...]).astype(bf16)
    nt_dims = (((1,), (1,)), ((), ()))

    gt = lax.dot_general(wt_ref[_TG0:TPROJ_ROWS, :], a, nt_dims,
                         preferred_element_type=jnp.float32) + gbt_ref[...]
    ic = GATE_CAP * jnp.tanh(gt[0:SUBLANES] / GATE_CAP)
    lf = _log_sigmoid(gt[SUBLANES:])
    b = _lane_scan(lf, L, jnp.add, 0.0)
    u = ic - b
    m_loc = b + _lane_scan(u, L, jnp.maximum, -jnp.inf)
    s.update(b=b, u=u, ic=ic, m_loc=m_loc)
    yield

    k_s[...] = _bdot(a, wn_ref[:, _K0:_V0])
    yield
    hc = _bdot(a, wn_ref[:, _HC0:_GT0])
    yield
    z = _bdot(a, wn_ref[:, _CG0:_HC0]) * hc
    if sb > 0:
        z_s[sb, 0:SUBLANES, :] = z_s[sb - 1, TS:TS + SUBLANES, :]
    z_s[sb, SUBLANES:SUBLANES + TS, :] = z
    yield
    yconv = (cw_ref[0:1, :] * z_s[sb, SUBLANES - 2:SUBLANES - 2 + TS, :]
             + cw_ref[1:2, :] * z_s[sb, SUBLANES - 1:SUBLANES - 1 + TS, :]
             + cw_ref[2:3, :] * z)
    s["yc"] = (_bdot(a, wn_ref[:, _BG0:_CG0]) * yconv).astype(bf16)
    yield

    for r0 in range(0, _TG0, PROJ_PIECE):
        ut = lax.dot_general(wt_ref[r0:r0 + PROJ_PIECE, :], a, nt_dims, preferred_element_type=jnp.float32)
        if r0 < _TV0:
            qt_s[r0:r0 + PROJ_PIECE, :] = ut * (DK ** -0.5)
        elif r0 < _TOG0:
            vt_s[r0 - _TV0:r0 - _TV0 + PROJ_PIECE, :] = ut
        else:
            ogt_s[r0 - _TOG0:r0 - _TOG0 + PROJ_PIECE, :] = ut
        yield


def _prompt_chunks(s, carry):
    L, TS = s["L"], s["TS"]
    qt_s, vt_s, ogt_s, k_s, mixt_s, ucol_s, wi_s, c2_s, emt_s, wk_s, ast_s, st_s = (
        s["qt_s"], s["vt_s"], s["ogt_s"], s["k_s"], s["mixt_s"], s["ucol_s"], s["wi_s"],
        s["c2_s"], s["emt_s"], s["wk_s"], s["ast_s"], s["st_s"])
    b, u, ic, m_loc = s["b"], s["u"], s["ic"], s["m_loc"]
    nc = TS // L
    bf16 = jnp.bfloat16
    m_prev = carry["m"]

    for c in range(nc):
        sl = slice(c * L, (c + 1) * L)
        bc, mlc = b[:, sl], m_loc[:, sl]
        b_last = jnp.broadcast_to(bc[:, L - 1:L], bc.shape)
        m_new = jnp.maximum(b_last + m_prev, jnp.broadcast_to(mlc[:, L - 1:L], bc.shape))
        g = bc + m_prev
        mt = jnp.maximum(g, mlc)
        wi_s[:, sl] = jnp.exp(g - mt)
        c2_s[:, sl] = mt - bc
        emt_s[:, sl] = jnp.exp(-mt)
        wk_s[:, sl] = jnp.exp(b_last - bc + ic[:, sl] - m_new)
        ast_s[c] = jnp.exp(b_last + m_prev - m_new)
        upad = jnp.concatenate([u[:, sl], jnp.zeros((L - SUBLANES, L), jnp.float32)], axis=0)
        ucol_s[c * L:(c + 1) * L, :] = upad.T
        m_prev = m_new
    carry["m"] = m_prev
    yield

    s_i = lax.broadcasted_iota(jnp.int32, (L, L), 0)
    t_i = lax.broadcasted_iota(jnp.int32, (L, L), 1)
    causal = s_i <= t_i
    low_half = lax.broadcasted_iota(jnp.int32, (L, LANES), 1) < DK
    zeros_q = jnp.zeros((DK, L), jnp.float32)
    zeros_p = jnp.zeros((L, L), bf16)

    def pair_row(ref, h0, rows):
        return jnp.concatenate([ref[h0:h0 + 1, rows], ref[h0 + 1:h0 + 2, rows]], axis=1)

    for c in range(nc):
        rows = slice(c * L, (c + 1) * L)
        for pr in range(NH // 2):
            h0 = 2 * pr
            hv0 = slice(h0 * DV, (h0 + 1) * DV)
            hv1 = slice((h0 + 1) * DV, (h0 + 2) * DV)
            kp = k_s[rows, pr * LANES:(pr + 1) * LANES]
            q0 = qt_s[h0 * DK:(h0 + 1) * DK, rows]
            q1 = qt_s[(h0 + 1) * DK:(h0 + 2) * DK, rows]
            qbd = jnp.concatenate([jnp.concatenate([q0, zeros_q], axis=1),
                                   jnp.concatenate([zeros_q, q1], axis=1)], axis=0).astype(bf16)
            st = _bdot(kp.astype(bf16), qbd)
            arg = jnp.concatenate(
                [jnp.where(causal, ucol_s[rows, h:h + 1] - c2_s[h:h + 1, rows], -jnp.inf)
                 for h in (h0, h0 + 1)], axis=1)
            pt = st * jnp.exp(arg)
            rs = jnp.sum(pt, axis=0, keepdims=True)
            ptb = pt.astype(bf16)
            pbd = jnp.concatenate([jnp.concatenate([ptb[:, 0:L], zeros_p], axis=1),
                                   jnp.concatenate([zeros_p, ptb[:, L:]], axis=1)], axis=0)
            vt = jnp.concatenate([vt_s[hv0, rows], vt_s[hv1, rows]], axis=1)
            state = st_s[pr]
            sq = _bdot(state.astype(bf16), qbd) * pair_row(wi_s, h0, rows)
            num = _bdot(vt.astype(bf16), pbd) + sq[0:DV]
            den = sq[DV:DV + 1] + rs
            hh = num * (1.0 / jnp.maximum(jnp.abs(den), pair_row(emt_s, h0, rows)))
            hn = hh * lax.rsqrt(jnp.mean(hh * hh, axis=0, keepdims=True) + EPS)
            mixt_s[hv0, rows] = jax.nn.sigmoid(ogt_s[hv0, rows]) * hn[:, 0:L]
            mixt_s[hv1, rows] = jax.nn.sigmoid(ogt_s[hv1, rows]) * hn[:, L:]
            wkr = pair_row(wk_s, h0, rows)
            vw = jnp.concatenate([vt * wkr, jnp.broadcast_to(wkr, (2 * SUBLANES, 2 * L))], axis=0)
            km = jnp.concatenate([jnp.where(low_half, kp, 0.0), jnp.where(low_half, 0.0, kp)], axis=0)
            decay = jnp.where(low_half[0:1, :], ast_s[c][h0:h0 + 1, :], ast_s[c][h0 + 1:h0 + 2, :])
            st_s[pr] = decay * state + _bdot(vw.astype(bf16), km.astype(bf16))
            yield


def _prompt_output(s):
    sb, TS = s["sb"], s["TS"]
    x_ref, wout_ref, x1_ref, mixt_s = s["x_ref"], s["wout_ref"], s["x1_ref"], s["mixt_s"]
    tok = slice(sb * TS, (sb + 1) * TS)
    hm = mixt_s[...].T * s["gmhc_ref"][...]
    mix = jnp.concatenate([hm.astype(jnp.bfloat16), s["yc"]], axis=1)
    yield
    for c0 in range(0, D_MODEL, PROJ_PIECE):
        cols = slice(c0, c0 + PROJ_PIECE)
        x1_ref[0, tok, cols] = x_ref[0, tok, cols] + _bdot(mix, wout_ref[:, cols])
        yield


def _mixer_prompt(x, weights, TB, NSUB, L):
    B, S, _ = x.shape
    TS = TB * NSUB
    nb = S // TS
    f32 = jnp.float32
    kern = functools.partial(_mixer_prompt_kernel, L=L, TB=TB, NSUB=NSUB)
    return pl.pallas_call(
        kern,
        grid=(B, nb),
        in_specs=[
            pl.BlockSpec((1, TS, D_MODEL), lambda b, j: (b, j, 0)),
            _const_spec((1, D_MODEL)),
            _const_spec((D_MODEL, PROJ_COLS)),
            _const_spec((TPROJ_ROWS, D_MODEL)),
            _const_spec((2 * SUBLANES, 1)),
            _const_spec((1, MLSTM_W)),
            _const_spec((CONV_W, CONV_CH)),
            _const_spec((D_MODEL, D_MODEL)),
        ],
        out_specs=[
            pl.BlockSpec((1, TS, D_MODEL), lambda b, j: (b, j, 0)),
            pl.BlockSpec((1, NH, DK, DV), lambda b, j: (b, 0, 0, 0)),
            pl.BlockSpec((1, NH // 2, 2 * DK), lambda b, j: (b, 0, 0)),
            pl.BlockSpec((1, SUBLANES, LANES), lambda b, j: (b, 0, 0)),
            pl.BlockSpec((1, SUBLANES, CONV_CH), lambda b, j: (b, 0, 0)),
        ],
        out_shape=[
            jax.ShapeDtypeStruct((B, S, D_MODEL), f32),
            jax.ShapeDtypeStruct((B, NH, DK, DV), f32),
            jax.ShapeDtypeStruct((B, NH // 2, 2 * DK), f32),
            jax.ShapeDtypeStruct((B, SUBLANES, LANES), f32),
            jax.ShapeDtypeStruct((B, SUBLANES, CONV_CH), f32),
        ],
        scratch_shapes=[
            pltpu.VMEM((NSUB, QK_W, TB), f32),
            pltpu.VMEM((NSUB, MLSTM_W, TB), f32),
            pltpu.VMEM((NSUB, MLSTM_W, TB), f32),
            pltpu.VMEM((NSUB, TB, QK_W), f32),
            pltpu.VMEM((NSUB, MLSTM_W, TB), f32),
            pltpu.VMEM((NSUB, TB + SUBLANES, CONV_CH), f32),
            pltpu.VMEM((NSUB, TB, LANES), f32),
            pltpu.VMEM((NSUB, SUBLANES, TB), f32),
            pltpu.VMEM((NSUB, SUBLANES, TB), f32),
            pltpu.VMEM((NSUB, SUBLANES, TB), f32),
            pltpu.VMEM((NSUB, SUBLANES, TB), f32),
            pltpu.VMEM((NSUB, TB // L, SUBLANES, LANES), f32),
            pltpu.VMEM((NH // 2, ST_ROWS, LANES), f32),
        ],
        compiler_params=pltpu.CompilerParams(
            dimension_semantics=("arbitrary", "arbitrary"), vmem_limit_bytes=VMEM_LIMIT),
        name="mixer_prompt",
    )(x, *weights)


def _row_scan(x, seg, op, fill):
    pos = lax.broadcasted_iota(jnp.int32, x.shape, 0) & (seg - 1)
    k = 1
    while k < seg:
        x = op(x, jnp.where(pos >= k, pltpu.roll(x, k, 0), fill))
        k *= 2
    return x


def _split_dot(x, sel, parts):
    acc = None
    rem = x
    for p in range(parts):
        hi = rem.astype(jnp.bfloat16)
        d = _bdot(hi, sel)
        acc = d if acc is None else acc + d
        if p + 1 < parts:
            rem = rem - hi.astype(jnp.float32)
    return acc


def _selectors():
    h = np.arange(NH)
    seg_qk = np.zeros((QK_W, LANES), np.float32)
    seg_qk[np.arange(QK_W), np.arange(QK_W) // DK] = 1.0
    exp_v = np.zeros((LANES, MLSTM_W), np.float32)
    exp_k = np.zeros((LANES, QK_W), np.float32)
    for i in h:
        exp_v[i, i * DV:(i + 1) * DV] = 1.0
        exp_k[i, i * DK:(i + 1) * DK] = 1.0
    mean_v = np.kron(np.eye(NH, dtype=np.float32), np.full((DV, DV), 1.0 / DV, np.float32))
    return tuple(jnp.asarray(m, jnp.bfloat16) for m in (seg_qk, exp_v, exp_k, mean_v))


def _mixer_sample_kernel(x_ref, cv_ref, c0_ref, n0_ref, m0_ref,
                         gmix_ref, win_ref, gb_ref, gmh_ref, cw_ref, wout_ref,
                         segqk_ref, expv_ref, expk_ref, meanv_ref,
                         x1_ref, cst_ref, nst_ref, mst_ref, cvo_ref,
                         xp_s, z_s, mp_s, qc_s, *, NSEQ):
    R = SUBLANES
    TS = NSEQ * R
    bf16 = jnp.bfloat16
    f32 = jnp.float32

    xp_s[:, SAMPLE_T:, :] = jnp.zeros((NSEQ, R - SAMPLE_T, D_MODEL), f32)
    xp_s[:, 0:SAMPLE_T, :] = x_ref[...]
    x = xp_s[...].reshape(TS, D_MODEL)
    a = _rms(x, gmix_ref[...]).astype(bf16)
    u = _bdot(a, win_ref[...])

    pos = lax.broadcasted_iota(jnp.int32, (TS, 1), 0) & (R - 1)
    real = pos < SAMPLE_T

    z = u[:, _CG0:_HC0] * u[:, _HC0:_GT0]
    z3 = z.reshape(NSEQ, R, CONV_CH)
    z_s[...] = z3
    z_s[:, R - (CONV_W - 1):, :] = cv_ref[...]
    zf = z_s[...].reshape(TS, CONV_CH)
    zm1 = jnp.where(pos >= 1, pltpu.roll(z, 1, 0), pltpu.roll(zf, TS - (R - 1), 0))
    zm2 = jnp.where(pos >= 2, pltpu.roll(z, 2, 0), pltpu.roll(zf, TS - (R - 2), 0))
    yc = u[:, _BG0:_CG0] * (cw_ref[0:1, :] * zm2 + cw_ref[1:2, :] * zm1 + cw_ref[2:3, :] * z)
    cvo_ref[...] = z3[:, SAMPLE_T - (CONV_W - 1):SAMPLE_T, :]

    gates = u[:, _GT0:PROJ_COLS] + gb_ref[...]
    ic = jnp.where(real, GATE_CAP * jnp.tanh(gates[:, :LANES] / GATE_CAP), NEG_BIG)
    lf = jnp.where(real, _log_sigmoid(gates[:, LANES:]), 0.0)
    b = _row_scan(lf, R, jnp.add, 0.0)
    uu = ic - b
    m_loc = b + _row_scan(uu, R, jnp.maximum, -jnp.inf)

    mp_s[...] = jnp.zeros(mp_s.shape, f32)
    mp_s[:, :, 0:NH] = m0_ref[...]
    m_prev = mp_s[...]
    b3 = b.reshape(NSEQ, R, LANES)
    ml3 = m_loc.reshape(NSEQ, R, LANES)
    b_last = b3[:, R - 1:R, :]
    m_new = jnp.maximum(b_last + m_prev, ml3[:, R - 1:R, :])
    g3 = b3 + m_prev
    mt3 = jnp.maximum(g3, ml3)
    wi = jnp.exp(g3 - mt3).reshape(TS, LANES)
    c2 = (mt3 - b3).reshape(TS, LANES)
    emt = jnp.exp(-mt3).reshape(TS, LANES)
    wk = jnp.exp(b_last - b3 + ic.reshape(NSEQ, R, LANES) - m_new).reshape(TS, LANES)
    a_st = jnp.broadcast_to(jnp.exp(b_last + m_prev - m_new), (NSEQ, R, LANES)).reshape(TS, LANES)
    mst_ref[...] = m_new[:, :, 0:NH]

    q = u[:, _Q0:_K0] * (DK ** -0.5)
    k = u[:, _K0:_V0]
    v = u[:, _V0:_OG0]
    rs = jnp.zeros((TS, LANES), f32)
    num = jnp.zeros((TS, MLSTM_W), f32)
    for d in range(SAMPLE_T):
        kd = k if d == 0 else pltpu.roll(k, d, 0)
        vd = v if d == 0 else pltpu.roll(v, d, 0)
        ud = uu if d == 0 else pltpu.roll(uu, d, 0)
        p = _split_dot(q * kd, segqk_ref[...], 2) * jnp.exp(ud - c2)
        rs = rs + p
        num = num + _split_dot(p, expv_ref[...], 2) * vd

    n0x = jnp.broadcast_to(n0_ref[...], (NSEQ, R, QK_W)).reshape(TS, QK_W)
    qn = _split_dot(q * n0x, segqk_ref[...], 2)
    den = wi * qn + rs
    rden = 1.0 / jnp.maximum(jnp.abs(den), emt)
    kw = k * _split_dot(wk, expk_ref[...], 2)
    ax = _split_dot(a_st, expv_ref[...], 3)
    lane_q = lax.broadcasted_iota(jnp.int32, (R, QK_W), 1)
    for i in range(NSEQ):
        rows = slice(i * R, (i + 1) * R)
        qi = q[rows, :]
        lhs = jnp.concatenate(
            [jnp.where((lane_q >= h * DK) & (lane_q < (h + 1) * DK), qi, 0.0) for h in range(NH)], axis=0)
        c0 = c0_ref[i]
        qc = _bdot(lhs.astype(bf16), c0.reshape(QK_W, DV).astype(bf16))
        qc_s[rows, :] = jnp.concatenate([qc[h * R:(h + 1) * R, :] for h in range(NH)], axis=1)
        kwi = kw[rows, :].astype(bf16)
        vi = v[rows, :].astype(bf16)
        for h in range(NH):
            dc = lax.dot_general(kwi[:, h * DK:(h + 1) * DK], vi[:, h * DV:(h + 1) * DV],
                                 (((0,), (0,)), ((), ())), preferred_element_type=f32)
            cst_ref[i, h] = ax[i * R:i * R + 1, h * DV:(h + 1) * DV] * c0[h] + dc

    hh = (_split_dot(wi, expv_ref[...], 2) * qc_s[...] + num) * _split_dot(rden, expv_ref[...], 2)
    ms = _split_dot(hh * hh, meanv_ref[...], 2)
    hm = jax.nn.sigmoid(u[:, _OG0:_BG0]) * (hh * lax.rsqrt(ms + EPS) * gmh_ref[...])
    mix = jnp.concatenate([hm, yc], axis=1).astype(bf16)
    out = x + _bdot(mix, wout_ref[...])
    x1_ref[...] = out.reshape(NSEQ, R, D_MODEL)[:, 0:SAMPLE_T, :]

    a_k = _split_dot(a_st, expk_ref[...], 3).reshape(NSEQ, R, QK_W)[:, 0:1, :]
    nst_ref[...] = a_k * n0_ref[...] + jnp.sum(kw.reshape(NSEQ, R, QK_W), axis=1, keepdims=True)


def _mixer_sample(x, cv, c0, n0, m0, weights, NSEQ):
    Bs = x.shape[0]
    f32 = jnp.float32
    kern = functools.partial(_mixer_sample_kernel, NSEQ=NSEQ)
    x_spec = pl.BlockSpec((NSEQ, SAMPLE_T, D_MODEL), lambda t: (t, 0, 0))
    cv_spec = pl.BlockSpec((NSEQ, CONV_W - 1, CONV_CH), lambda t: (t, 0, 0))
    c_spec = pl.BlockSpec((NSEQ, NH, DK, DV), lambda t: (t, 0, 0, 0))
    n_spec = pl.BlockSpec((NSEQ, 1, QK_W), lambda t: (t, 0, 0))
    m_spec = pl.BlockSpec((NSEQ, 1, NH), lambda t: (t, 0, 0))
    return pl.pallas_call(
        kern,
        grid=(Bs // NSEQ,),
        in_specs=[
            x_spec, cv_spec, c_spec, n_spec, m_spec,
            _const_spec((1, D_MODEL)),
            _const_spec((D_MODEL, PROJ_COLS)),
            _const_spec((1, 2 * LANES)),
            _const_spec((1, MLSTM_W)),
            _const_spec((CONV_W, CONV_CH)),
            _const_spec((D_MODEL, D_MODEL)),
            _const_spec((QK_W, LANES)),
            _const_spec((LANES, MLSTM_W)),
            _const_spec((LANES, QK_W)),
            _const_spec((MLSTM_W, MLSTM_W)),
        ],
        out_specs=[x_spec, c_spec, n_spec, m_spec, cv_spec],
        out_shape=[
            jax.ShapeDtypeStruct((Bs, SAMPLE_T, D_MODEL), f32),
            jax.ShapeDtypeStruct((Bs, NH, DK, DV), f32),
            jax.ShapeDtypeStruct((Bs, 1, QK_W), f32),
            jax.ShapeDtypeStruct((Bs, 1, NH), f32),
            jax.ShapeDtypeStruct((Bs, CONV_W - 1, CONV_CH), f32),
        ],
        scratch_shapes=[
            pltpu.VMEM((NSEQ, SUBLANES, D_MODEL), f32),
            pltpu.VMEM((NSEQ, SUBLANES, CONV_CH), f32),
            pltpu.VMEM((NSEQ, 1, LANES), f32),
            pltpu.VMEM((NSEQ * SUBLANES, MLSTM_W), f32),
        ],
        compiler_params=pltpu.CompilerParams(
            dimension_semantics=("arbitrary",), vmem_limit_bytes=VMEM_LIMIT),
        name="mixer_sample",
    )(x, cv, c0, n0, m0, *weights)


def _mixer_weights(g_mix, w_in, b_ig, b_fg, g_mh, conv_w, w_out):
    f32 = jnp.float32
    w_all, wt, wout = _prep_weights(w_in, w_out)
    gmix = g_mix.reshape(1, D_MODEL)
    zeros = jnp.zeros((SUBLANES - NH,), f32)
    gb_col = jnp.concatenate([b_ig, zeros, b_fg, zeros]).reshape(2 * SUBLANES, 1)
    zeros = jnp.zeros((LANES - NH,), f32)
    gb_row = jnp.concatenate([b_ig, zeros, b_fg, zeros]).reshape(1, 2 * LANES)
    gmh = g_mh.reshape(1, MLSTM_W)
    prompt = (gmix, w_all, wt, gb_col, gmh, conv_w, wout)
    sample = (gmix, w_all, gb_row, gmh, conv_w, wout) + _selectors()
    return prompt, sample


FF_STREAM = 256
_N_FF_PIECES = D_FF // FF_STREAM
_N_PIECES = _N_FF_PIECES + PLE_DIM // FF_STREAM + D_MODEL // FF_STREAM


def _ffn_rows(x, pe, gffn_ref, gple_ref, gfin_ref, wg_s, wu_s, wd_s, wple_s, wpg_s, *, chunk, final_norm,
              before_chunk=None, before_tail=None):
    bf16 = jnp.bfloat16
    f = _rms(x, gffn_ref[...]).astype(bf16)
    e = None
    if before_tail is None:
        e = _rms(_bdot(pe.astype(bf16), wple_s[...]), gple_ref[...])
    for c0 in range(0, D_FF, chunk):
        c1 = min(c0 + chunk, D_FF)
        if before_chunk is not None:
            before_chunk(c0 // chunk)
        gate = _bdot(f, wg_s[:, c0:c1])
        up = _bdot(f, wu_s[:, c0:c1])
        hmid = (gate * jax.nn.sigmoid(gate) * up).astype(bf16)
        x = x + _bdot(hmid, wd_s[c0:c1, :])
    if before_tail is not None:
        before_tail()
        e = _rms(_bdot(pe.astype(bf16), wple_s[...]), gple_ref[...])
    xb = x.astype(bf16)
    n_rows = x.shape[0]
    outs = []
    for r0 in range(0, n_rows, n_rows // FFN_TAIL_PIECES):
        rows = slice(r0, r0 + n_rows // FFN_TAIL_PIECES)
        xr = x[rows] + jax.nn.sigmoid(_bdot(xb[rows], wpg_s[...])) * e[rows]
        outs.append(_rms(xr, gfin_ref[...]) if final_norm else xr)
    return jnp.concatenate(outs, axis=0)


def _ffn_stream_kernel(xp_ref, pp_ref, xs_ref, ps_ref, gffn_ref, gple_ref, gfin_ref,
                       wg_hbm, wu_hbm, wd_hbm, wple_hbm, wpg_hbm,
                       yp_ref, ys_ref,
                       wg_s, wu_s, wd_s, wple_s, wpg_s, stg_g, stg_u, stg_d, sem, *, final_norm):
    t = pl.program_id(0)
    bf16 = jnp.bfloat16
    weights = (wg_s, wu_s, wd_s, wple_s, wpg_s)

    def copies(k):
        slot = k % 2
        if k < _N_FF_PIECES:
            cols = pl.ds(k * FF_STREAM, FF_STREAM)
            return [pltpu.make_async_copy(wg_hbm.at[:, cols], stg_g.at[slot], sem.at[0, slot]),
                    pltpu.make_async_copy(wu_hbm.at[:, cols], stg_u.at[slot], sem.at[1, slot]),
                    pltpu.make_async_copy(wd_hbm.at[cols, :], stg_d.at[slot], sem.at[2, slot])]
        if k == _N_FF_PIECES:
            return [pltpu.make_async_copy(wple_hbm, stg_d.at[slot], sem.at[2, slot])]
        rows = pl.ds((k - _N_FF_PIECES - 1) * FF_STREAM, FF_STREAM)
        return [pltpu.make_async_copy(wpg_hbm.at[rows, :], stg_d.at[slot], sem.at[2, slot])]

    def land(k):
        if k + 1 < _N_PIECES:
            for cp in copies(k + 1):
                cp.start()
        for cp in copies(k):
            cp.wait()
        slot = k % 2
        if k < _N_FF_PIECES:
            cols = slice(k * FF_STREAM, (k + 1) * FF_STREAM)
            wg_s[:, cols] = stg_g[slot].astype(bf16)
            wu_s[:, cols] = stg_u[slot].astype(bf16)
            wd_s[cols, :] = stg_d[slot].astype(bf16)
        elif k == _N_FF_PIECES:
            wple_s[...] = stg_d[slot].astype(bf16)
        else:
            r0 = (k - _N_FF_PIECES - 1) * FF_STREAM
            wpg_s[r0:r0 + FF_STREAM, :] = stg_d[slot].astype(bf16)

    @pl.when(t == 0)
    def _():
        for cp in copies(0):
            cp.start()
        n_tok = xs_ref.shape[0] * xs_ref.shape[1]

        def tail():
            for k in range(_N_FF_PIECES, _N_PIECES):
                land(k)

        y = _ffn_rows(xs_ref[...].reshape(n_tok, D_MODEL), ps_ref[...].reshape(n_tok, PLE_DIM),
                      gffn_ref, gple_ref, gfin_ref, *weights, chunk=FF_STREAM, final_norm=final_norm,
                      before_chunk=land, before_tail=tail)
        ys_ref[...] = y.reshape(ys_ref.shape)

    @pl.when(t > 0)
    def _():
        yp_ref[...] = _ffn_rows(xp_ref[...], pp_ref[...], gffn_ref, gple_ref, gfin_ref, *weights,
                                chunk=FF_CHUNK, final_norm=final_norm)


def _ffn_stream(xp, pp, xs, ps, g_ffn, g_ple, g_final, w_gate, w_up, w_down, w_ple, w_pg, TM, final_norm):
    T = xp.shape[0]
    f32, bf16 = jnp.float32, jnp.bfloat16
    kern = functools.partial(_ffn_stream_kernel, final_norm=final_norm)
    row_map = lambda t: (jnp.maximum(t - 1, 0), 0)
    hbm = pl.BlockSpec(memory_space=pl.ANY)
    return pl.pallas_call(
        kern,
        grid=(T // TM + 1,),
        in_specs=[
            pl.BlockSpec((TM, D_MODEL), row_map),
            pl.BlockSpec((TM, PLE_DIM), row_map),
            _const_spec(xs.shape),
            _const_spec(ps.shape),
            _const_spec((1, D_MODEL)),
            _const_spec((1, D_MODEL)),
            _const_spec((1, D_MODEL)),
            hbm, hbm, hbm, hbm, hbm,
        ],
        out_specs=[pl.BlockSpec((TM, D_MODEL), row_map),
                   pl.BlockSpec(xs.shape, lambda t: (0, 0, 0))],
        out_shape=[jax.ShapeDtypeStruct(xp.shape, f32), jax.ShapeDtypeStruct(xs.shape, f32)],
        scratch_shapes=[
            pltpu.VMEM((D_MODEL, D_FF), bf16),
            pltpu.VMEM((D_MODEL, D_FF), bf16),
            pltpu.VMEM((D_FF, D_MODEL), bf16),
            pltpu.VMEM((PLE_DIM, D_MODEL), bf16),
            pltpu.VMEM((D_MODEL, D_MODEL), bf16),
            pltpu.VMEM((2, D_MODEL, FF_STREAM), f32),
            pltpu.VMEM((2, D_MODEL, FF_STREAM), f32),
            pltpu.VMEM((2, FF_STREAM, D_MODEL), f32),
            pltpu.SemaphoreType.DMA((3, 2)),
        ],
        compiler_params=pltpu.CompilerParams(
            dimension_semantics=("arbitrary",), vmem_limit_bytes=VMEM_LIMIT),
        name="ffn",
    )(xp, pp, xs, ps, g_ffn.reshape(1, D_MODEL), g_ple.reshape(1, D_MODEL), g_final.reshape(1, D_MODEL),
      w_gate, w_up, w_down, w_ple, w_pg)


def kernel(x_prompt, x_sample, p_prompt, p_sample, state_C, state_n, state_m, state_conv,
           g_mix, w_in, b_ig, b_fg, g_mh, conv_w, w_out, g_ffn, w_gate, w_up, w_down,
           w_ple, g_ple, w_pg, g_final):
    bf16 = jnp.bfloat16
    depth = g_mix.shape[0]
    B, S, _ = x_prompt.shape
    Bs, Ss, _ = x_sample.shape
    assert Ss == SAMPLE_T and S % PROMPT_TS == 0 and Bs % SAMPLE_NSEQ == 0

    xp = x_prompt
    xs = x_sample
    outs = [[] for _ in range(8)]
    for i in range(depth):
        last = i == depth - 1
        pw, sw = _mixer_weights(g_mix[i], w_in[i], b_ig[i], b_fg[i], g_mh[i], conv_w[i], w_out[i])
        x1p, cp, n_p, mp, cvp = _mixer_prompt(xp, pw, TB=PROMPT_TS, NSUB=PROMPT_NSUB, L=PROMPT_L)
        x1s, cs, n_s, ms, cvs = _mixer_sample(
            xs, state_conv[i], state_C[i], state_n[i].reshape(Bs, 1, QK_W),
            state_m[i].reshape(Bs, 1, NH), sw, SAMPLE_NSEQ)
        xp, xs = _ffn_stream(x1p.reshape(B * S, D_MODEL), p_prompt[i].reshape(B * S, PLE_DIM), x1s,
                             p_sample[i], g_ffn[i], g_ple[i], g_final, w_gate[i], w_up[i], w_down[i],
                             w_ple[i], w_pg[i], FFN_TM, last)
        xp = xp.reshape(B, S, D_MODEL)

        new = (cp, n_p.reshape(B, NH, DK), mp[:, :NH, 0], cvp[:, SUBLANES - (CONV_W - 1):],
               cs, n_s.reshape(Bs, NH, DK), ms.reshape(Bs, NH), cvs)
        for lst, v in zip(outs, new):
            lst.append(v)

    return (xp, xs) + tuple(jnp.stack(l) for l in outs)
```

```python
import functools

import numpy as np
import jax
import jax.numpy as jnp
from jax import lax
from jax.experimental import pallas as pl
from jax.experimental.pallas import tpu as pltpu

D_MODEL = 1024
NH = 4
DK = 64
DV = 128
MLSTM_W = NH * DV
QK_W = NH * DK
CONV_CH = 512
CONV_W = 3
D_FF = 2816
FF_CHUNK = 768
FFN_TM = 1024
FFN_SUB = 2
FFN_TAIL_PIECES = 2
PLE_DIM = 256
PROMPT_L = 128
PROMPT_TS = 512
PROMPT_NSUB = 2
SAMPLE_T = 4
SAMPLE_NSEQ = 32
GATE_CAP = 15.0
EPS = 1e-6

LANES = 128
SUBLANES = 8
NEG_BIG = -1e30

VMEM_LIMIT = 56 * 1024 * 1024


def _rms(x, g):
    ms = jnp.mean(x * x, axis=-1, keepdims=True)
    return x * lax.rsqrt(ms + EPS) * g


def _bdot(a, b):
    return jnp.dot(a, b, preferred_element_type=jnp.float32)


def _log_sigmoid(x):
    return jnp.minimum(x, 0.0) - jnp.log1p(jnp.exp(-jnp.abs(x)))


def _const_spec(shape):
    nd = len(shape)
    return pl.BlockSpec(shape, lambda *_: (0,) * nd, pipeline_mode=pl.Buffered(1))


_IG0 = 2 * QK_W + MLSTM_W
_OG_IN = _IG0 + 2 * NH
PROJ_IN = _OG_IN + MLSTM_W + 3 * CONV_CH
_Q0, _K0, _V0, _OG0, _BG0, _CG0, _HC0, _GT0 = 0, 256, 512, 1024, 1536, 2048, 2560, 3072
PROJ_COLS = _GT0 + 2 * LANES
_TQ0, _TV0, _TOG0, _TG0 = 0, 256, 768, 1280
TPROJ_ROWS = _TG0 + 2 * SUBLANES
PREP_TK = 256


def _prep_weights_kernel(w_ref, wo_ref, wall_ref, wt_ref, wout_ref):
    bf16 = jnp.bfloat16
    wout_ref[...] = wo_ref[...].astype(bf16)
    wf = w_ref[...]
    tk = wf.shape[1]
    g8 = wf[_IG0:_OG_IN]
    row = lax.broadcasted_iota(jnp.int32, (SUBLANES, tk), 0)
    g_in = jnp.where(row < NH, g8, 0.0)
    g_fg = jnp.where(row < NH, pltpu.roll(g8, SUBLANES - NH, 0), 0.0)

    wt_ref[_TQ0:_TV0, :] = wf[_Q0:_K0].astype(bf16)
    wt_ref[_TV0:_TOG0, :] = wf[_V0:_OG0].astype(bf16)
    wt_ref[_TOG0:_TG0, :] = wf[_OG_IN:_OG_IN + MLSTM_W].astype(bf16)
    wt_ref[_TG0:TPROJ_ROWS, :] = jnp.concatenate([g_in, g_fg], axis=0).astype(bf16)

    wall_ref[:, 0:_OG0] = wf[0:_IG0].T.astype(bf16)
    wall_ref[:, _OG0:_GT0] = wf[_OG_IN:PROJ_IN].T.astype(bf16)
    pad = jnp.zeros((LANES - SUBLANES, tk), jnp.float32)
    wall_ref[:, _GT0:_GT0 + LANES] = jnp.concatenate([g_in, pad], axis=0).T.astype(bf16)
    wall_ref[:, _GT0 + LANES:PROJ_COLS] = jnp.concatenate([g_fg, pad], axis=0).T.astype(bf16)


def _prep_weights(w_in, w_out):
    bf16 = jnp.bfloat16
    return pl.pallas_call(
        _prep_weights_kernel,
        grid=(D_MODEL // PREP_TK,),
        in_specs=[pl.BlockSpec((PROJ_IN, PREP_TK), lambda i: (0, i)),
                  pl.BlockSpec((PREP_TK, D_MODEL), lambda i: (i, 0))],
        out_specs=[pl.BlockSpec((PREP_TK, PROJ_COLS), lambda i: (i, 0)),
                   pl.BlockSpec((TPROJ_ROWS, PREP_TK), lambda i: (0, i)),
                   pl.BlockSpec((PREP_TK, D_MODEL), lambda i: (i, 0))],
        out_shape=[jax.ShapeDtypeStruct((D_MODEL, PROJ_COLS), bf16),
                   jax.ShapeDtypeStruct((TPROJ_ROWS, D_MODEL), bf16),
                   jax.ShapeDtypeStruct((D_MODEL, D_MODEL), bf16)],
        compiler_params=pltpu.CompilerParams(
            dimension_semantics=("arbitrary",), vmem_limit_bytes=VMEM_LIMIT),
        name="prep_weights",
    )(w_in.T, w_out)


ST_ROWS = DV + 2 * SUBLANES


def _lane_scan(x, seg, op, fill):
    pos = lax.broadcasted_iota(jnp.int32, x.shape, 1) & (seg - 1)
    k = 1
    while k < seg:
        x = op(x, jnp.where(pos >= k, pltpu.roll(x, k, 1), fill))
        k *= 2
    return x


PROJ_PIECE = 256


def _run(*phases):
    live = list(phases)
    while live:
        for g in list(live):
            try:
                next(g)
            except StopIteration:
                live.remove(g)


def _mixer_prompt_kernel(x_ref, gmix_ref, wn_ref, wt_ref, gbt_ref, gmhc_ref, cw_ref, wout_ref,
                         x1_ref, c_ref, n_ref, m_ref, cvo_ref,
                         qt_s, vt_s, ogt_s, k_s, mixt_s, z_s, ucol_s, wi_s, c2_s, emt_s, wk_s, ast_s, st_s,
                         *, L, TB, NSUB):
    j = pl.program_id(1)

    @pl.when(j == 0)
    def _():
        z_s[0, 0:SUBLANES, :] = jnp.zeros((SUBLANES, CONV_CH), jnp.float32)
        st_s[...] = jnp.zeros(st_s.shape, jnp.float32)
        m_ref[...] = jnp.zeros(m_ref.shape, jnp.float32)

    m_prev = m_ref[0]
    sub = [dict(sb=sb, x_ref=x_ref, gmix_ref=gmix_ref, wn_ref=wn_ref, wt_ref=wt_ref, gbt_ref=gbt_ref,
                gmhc_ref=gmhc_ref, cw_ref=cw_ref, wout_ref=wout_ref, x1_ref=x1_ref, qt_s=qt_s.at[sb],
                vt_s=vt_s.at[sb], ogt_s=ogt_s.at[sb], k_s=k_s.at[sb], mixt_s=mixt_s.at[sb], z_s=z_s,
                ucol_s=ucol_s.at[sb], wi_s=wi_s.at[sb], c2_s=c2_s.at[sb], emt_s=emt_s.at[sb],
                wk_s=wk_s.at[sb], ast_s=ast_s.at[sb], st_s=st_s, L=L, TS=TB) for sb in range(NSUB)]
    carry = {"m": m_prev}
    _run(_prompt_project(sub[0]))
    for a, b in zip(sub[:-1], sub[1:]):
        _run(_prompt_chunks(a, carry), _prompt_project(b))
    _run(_prompt_chunks(sub[-1], carry), *[_prompt_output(s) for s in sub[:-1]])
    _run(_prompt_output(sub[-1]))
    m_ref[0] = carry["m"]
    cvo_ref[0] = z_s[NSUB - 1, TB:TB + SUBLANES, :]
    z_s[0, 0:SUBLANES, :] = z_s[NSUB - 1, TB:TB + SUBLANES, :]

    @pl.when(j == pl.num_programs(1) - 1)
    def _():
        for pr in range(NH // 2):
            state = st_s[pr]
            c_pair = state[0:DV, :].T
            c_ref[0, 2 * pr] = c_pair[0:DK, :]
            c_ref[0, 2 * pr + 1] = c_pair[DK:, :]
            n_ref[0, pr:pr + 1, :] = state[DV:DV + 1, :]


def _prompt_project(s):
    sb, L, TS = s["sb"], s["L"], s["TS"]
    x_ref, gmix_ref, wn_ref, wt_ref, gbt_ref, cw_ref = (
        s["x_ref"], s["gmix_ref"], s["wn_ref"], s["wt_ref"], s["gbt_ref"], s["cw_ref"])
    qt_s, vt_s, ogt_s, k_s, z_s = s["qt_s"], s["vt_s"], s["ogt_s"], s["k_s"], s["z_s"]
    bf16 = jnp.bfloat16

    x = x_ref[0, sb * TS:(sb + 1) * TS, :]
    a = _rms(x, gmix_ref[...]).astype(bf16)
    nt_dims = (((1,), (1,)), ((), ()))

    gt = lax.dot_general(wt_ref[_TG0:TPROJ_ROWS, :], a, nt_dims,
                         preferred_element_type=jnp.float32) + gbt_ref[...]
    ic = GATE_CAP * jnp.tanh(gt[0:SUBLANES] / GATE_CAP)
    lf = _log_sigmoid(gt[SUBLANES:])
    b = _lane_scan(lf, L, jnp.add, 0.0)
    u = ic - b
    m_loc = b + _lane_scan(u, L, jnp.maximum, -jnp.inf)
    s.update(b=b, u=u, ic=ic, m_loc=m_loc)
    yield

    k_s[...] = _bdot(a, wn_ref[:, _K0:_V0])
    yield
    hc = _bdot(a, wn_ref[:, _HC0:_GT0])
    yield
    z = _bdot(a, wn_ref[:, _CG0:_HC0]) * hc
    if sb > 0:
        z_s[sb, 0:SUBLANES, :] = z_s[sb - 1, TS:TS + SUBLANES, :]
    z_s[sb, SUBLANES:SUBLANES + TS, :] = z
    yield
    yconv = (cw_ref[0:1, :] * z_s[sb, SUBLANES - 2:SUBLANES - 2 + TS, :]
             + cw_ref[1:2, :] * z_s[sb, SUBLANES - 1:SUBLANES - 1 + TS, :]
             + cw_ref[2:3, :] * z)
    s["yc"] = (_bdot(a, wn_ref[:, _BG0:_CG0]) * yconv).astype(bf16)
    yield

    for r0 in range(0, _TG0, PROJ_PIECE):
        ut = lax.dot_general(wt_ref[r0:r0 + PROJ_PIECE, :], a, nt_dims, preferred_element_type=jnp.float32)
        if r0 < _TV0:
            qt_s[r0:r0 + PROJ_PIECE, :] = ut * (DK ** -0.5)
        elif r0 < _TOG0:
            vt_s[r0 - _TV0:r0 - _TV0 + PROJ_PIECE, :] = ut
        else:
            ogt_s[r0 - _TOG0:r0 - _TOG0 + PROJ_PIECE, :] = ut
        yield


def _prompt_chunks(s, carry):
    L, TS = s["L"], s["TS"]
    qt_s, vt_s, ogt_s, k_s, mixt_s, ucol_s, wi_s, c2_s, emt_s, wk_s, ast_s, st_s = (
        s["qt_s"], s["vt_s"], s["ogt_s"], s["k_s"], s["mixt_s"], s["ucol_s"], s["wi_s"],
        s["c2_s"], s["emt_s"], s["wk_s"], s["ast_s"], s["st_s"])
    b, u, ic, m_loc = s["b"], s["u"], s["ic"], s["m_loc"]
    nc = TS // L
    bf16 = jnp.bfloat16
    m_prev = carry["m"]

    for c in range(nc):
        sl = slice(c * L, (c + 1) * L)
        bc, mlc = b[:, sl], m_loc[:, sl]
        b_last = jnp.broadcast_to(bc[:, L - 1:L], bc.shape)
        m_new = jnp.maximum(b_last + m_prev, jnp.broadcast_to(mlc[:, L - 1:L], bc.shape))
        g = bc + m_prev
        mt = jnp.maximum(g, mlc)
        wi_s[:, sl] = jnp.exp(g - mt)
        c2_s[:, sl] = mt - bc
        emt_s[:, sl] = jnp.exp(-mt)
        wk_s[:, sl] = jnp.exp(b_last - bc + ic[:, sl] - m_new)
        ast_s[c] = jnp.exp(b_last + m_prev - m_new)
        upad = jnp.concatenate([u[:, sl], jnp.zeros((L - SUBLANES, L), jnp.float32)], axis=0)
        ucol_s[c * L:(c + 1) * L, :] = upad.T
        m_prev = m_new
    carry["m"] = m_prev
    yield

    s_i = lax.broadcasted_iota(jnp.int32, (L, L), 0)
    t_i = lax.broadcasted_iota(jnp.int32, (L, L), 1)
    causal = s_i <= t_i
    low_half = lax.broadcasted_iota(jnp.int32, (L, LANES), 1) < DK
    zeros_q = jnp.zeros((DK, L), jnp.float32)
    zeros_p = jnp.zeros((L, L), bf16)

    def pair_row(ref, h0, rows):
        return jnp.concatenate([ref[h0:h0 + 1, rows], ref[h0 + 1:h0 + 2, rows]], axis=1)

    for c in range(nc):
        rows = slice(c * L, (c + 1) * L)
        for pr in range(NH // 2):
            h0 = 2 * pr
            hv0 = slice(h0 * DV, (h0 + 1) * DV)
            hv1 = slice((h0 + 1) * DV, (h0 + 2) * DV)
            kp = k_s[rows, pr * LANES:(pr + 1) * LANES]
            q0 = qt_s[h0 * DK:(h0 + 1) * DK, rows]
            q1 = qt_s[(h0 + 1) * DK:(h0 + 2) * DK, rows]
            qbd = jnp.concatenate([jnp.concatenate([q0, zeros_q], axis=1),
                                   jnp.concatenate([zeros_q, q1], axis=1)], axis=0).astype(bf16)
            st = _bdot(kp.astype(bf16), qbd)
            arg = jnp.concatenate(
                [jnp.where(causal, ucol_s[rows, h:h + 1] - c2_s[h:h + 1, rows], -jnp.inf)
                 for h in (h0, h0 + 1)], axis=1)
            pt = st * jnp.exp(arg)
            rs = jnp.sum(pt, axis=0, keepdims=True)
            ptb = pt.astype(bf16)
            pbd = jnp.concatenate([jnp.concatenate([ptb[:, 0:L], zeros_p], axis=1),
                                   jnp.concatenate([zeros_p, ptb[:, L:]], axis=1)], axis=0)
            vt = jnp.concatenate([vt_s[hv0, rows], vt_s[hv1, rows]], axis=1)
            state = st_s[pr]
            sq = _bdot(state.astype(bf16), qbd) * pair_row(wi_s, h0, rows)
            num = _bdot(vt.astype(bf16), pbd) + sq[0:DV]
            den = sq[DV:DV + 1] + rs
            hh = num * (1.0 / jnp.maximum(jnp.abs(den), pair_row(emt_s, h0, rows)))
            hn = hh * lax.rsqrt(jnp.mean(hh * hh, axis=0, keepdims=True) + EPS)
            mixt_s[hv0, rows] = jax.nn.sigmoid(ogt_s[hv0, rows]) * hn[:, 0:L]
            mixt_s[hv1, rows] = jax.nn.sigmoid(ogt_s[hv1, rows]) * hn[:, L:]
            wkr = pair_row(wk_s, h0, rows)
            vw = jnp.concatenate([vt * wkr, jnp.broadcast_to(wkr, (2 * SUBLANES, 2 * L))], axis=0)
            km = jnp.concatenate([jnp.where(low_half, kp, 0.0), jnp.where(low_half, 0.0, kp)], axis=0)
            decay = jnp.where(low_half[0:1, :], ast_s[c][h0:h0 + 1, :], ast_s[c][h0 + 1:h0 + 2, :])
            st_s[pr] = decay * state + _bdot(vw.astype(bf16), km.astype(bf16))
            yield


def _prompt_output(s):
    sb, TS = s["sb"], s["TS"]
    x_ref, wout_ref, x1_ref, mixt_s = s["x_ref"], s["wout_ref"], s["x1_ref"], s["mixt_s"]
    tok = slice(sb * TS, (sb + 1) * TS)
    hm = mixt_s[...].T * s["gmhc_ref"][...]
    mix = jnp.concatenate([hm.astype(jnp.bfloat16), s["yc"]], axis=1)
    yield
    for c0 in range(0, D_MODEL, PROJ_PIECE):
        cols = slice(c0, c0 + PROJ_PIECE)
        x1_ref[0, tok, cols] = x_ref[0, tok, cols] + _bdot(mix, wout_ref[:, cols])
        yield


def _mixer_prompt(x, weights, TB, NSUB, L):
    B, S, _ = x.shape
    TS = TB * NSUB
    nb = S // TS
    f32 = jnp.float32
    kern = functools.partial(_mixer_prompt_kernel, L=L, TB=TB, NSUB=NSUB)
    return pl.pallas_call(
        kern,
        grid=(B, nb),
        in_specs=[
            pl.BlockSpec((1, TS, D_MODEL), lambda b, j: (b, j, 0)),
            _const_spec((1, D_MODEL)),
            _const_spec((D_MODEL, PROJ_COLS)),
            _const_spec((TPROJ_ROWS, D_MODEL)),
            _const_spec((2 * SUBLANES, 1)),
            _const_spec((1, MLSTM_W)),
            _const_spec((CONV_W, CONV_CH)),
            _const_spec((D_MODEL, D_MODEL)),
        ],
        out_specs=[
            pl.BlockSpec((1, TS, D_MODEL), lambda b, j: (b, j, 0)),
            pl.BlockSpec((1, NH, DK, DV), lambda b, j: (b, 0, 0, 0)),
            pl.BlockSpec((1, NH // 2, 2 * DK), lambda b, j: (b, 0, 0)),
            pl.BlockSpec((1, SUBLANES, LANES), lambda b, j: (b, 0, 0)),
            pl.BlockSpec((1, SUBLANES, CONV_CH), lambda b, j: (b, 0, 0)),
        ],
        out_shape=[
            jax.ShapeDtypeStruct((B, S, D_MODEL), f32),
            jax.ShapeDtypeStruct((B, NH, DK, DV), f32),
            jax.ShapeDtypeStruct((B, NH // 2, 2 * DK), f32),
            jax.ShapeDtypeStruct((B, SUBLANES, LANES), f32),
            jax.ShapeDtypeStruct((B, SUBLANES, CONV_CH), f32),
        ],
        scratch_shapes=[
            pltpu.VMEM((NSUB, QK_W, TB), f32),
            pltpu.VMEM((NSUB, MLSTM_W, TB), f32),
            pltpu.VMEM((NSUB, MLSTM_W, TB), f32),
            pltpu.VMEM((NSUB, TB, QK_W), f32),
            pltpu.VMEM((NSUB, MLSTM_W, TB), f32),
            pltpu.VMEM((NSUB, TB + SUBLANES, CONV_CH), f32),
            pltpu.VMEM((NSUB, TB, LANES), f32),
            pltpu.VMEM((NSUB, SUBLANES, TB), f32),
            pltpu.VMEM((NSUB, SUBLANES, TB), f32),
            pltpu.VMEM((NSUB, SUBLANES, TB), f32),
            pltpu.VMEM((NSUB, SUBLANES, TB), f32),
            pltpu.VMEM((NSUB, TB // L, SUBLANES, LANES), f32),
            pltpu.VMEM((NH // 2, ST_ROWS, LANES), f32),
        ],
        compiler_params=pltpu.CompilerParams(
            dimension_semantics=("arbitrary", "arbitrary"), vmem_limit_bytes=VMEM_LIMIT),
        name="mixer_prompt",
    )(x, *weights)


def _row_scan(x, seg, op, fill):
    pos = lax.broadcasted_iota(jnp.int32, x.shape, 0) & (seg - 1)
    k = 1
    while k < seg:
        x = op(x, jnp.where(pos >= k, pltpu.roll(x, k, 0), fill))
        k *= 2
    return x


def _split_dot(x, sel, parts):
    acc = None
    rem = x
    for p in range(parts):
        hi = rem.astype(jnp.bfloat16)
        d = _bdot(hi, sel)
        acc = d if acc is None else acc + d
        if p + 1 < parts:
            rem = rem - hi.astype(jnp.float32)
    return acc


def _selectors():
    h = np.arange(NH)
    seg_qk = np.zeros((QK_W, LANES), np.float32)
    seg_qk[np.arange(QK_W), np.arange(QK_W) // DK] = 1.0
    exp_v = np.zeros((LANES, MLSTM_W), np.float32)
    exp_k = np.zeros((LANES, QK_W), np.float32)
    for i in h:
        exp_v[i, i * DV:(i + 1) * DV] = 1.0
        exp_k[i, i * DK:(i + 1) * DK] = 1.0
    mean_v = np.kron(np.eye(NH, dtype=np.float32), np.full((DV, DV), 1.0 / DV, np.float32))
    return tuple(jnp.asarray(m, jnp.bfloat16) for m in (seg_qk, exp_v, exp_k, mean_v))


def _mixer_sample_kernel(x_ref, cv_ref, c0_ref, n0_ref, m0_ref,
                         gmix_ref, win_ref, gb_ref, gmh_ref, cw_ref, wout_ref,
                         segqk_ref, expv_ref, expk_ref, meanv_ref,
                         x1_ref, cst_ref, nst_ref, mst_ref, cvo_ref,
                         xp_s, z_s, mp_s, qc_s, *, NSEQ):
    R = SUBLANES
    TS = NSEQ * R
    bf16 = jnp.bfloat16
    f32 = jnp.float32

    xp_s[:, SAMPLE_T:, :] = jnp.zeros((NSEQ, R - SAMPLE_T, D_MODEL), f32)
    xp_s[:, 0:SAMPLE_T, :] = x_ref[...]
    x = xp_s[...].reshape(TS, D_MODEL)
    a = _rms(x, gmix_ref[...]).astype(bf16)
    u = _bdot(a, win_ref[...])

    pos = lax.broadcasted_iota(jnp.int32, (TS, 1), 0) & (R - 1)
    real = pos < SAMPLE_T

    z = u[:, _CG0:_HC0] * u[:, _HC0:_GT0]
    z3 = z.reshape(NSEQ, R, CONV_CH)
    z_s[...] = z3
    z_s[:, R - (CONV_W - 1):, :] = cv_ref[...]
    zf = z_s[...].reshape(TS, CONV_CH)
    zm1 = jnp.where(pos >= 1, pltpu.roll(z, 1, 0), pltpu.roll(zf, TS - (R - 1), 0))
    zm2 = jnp.where(pos >= 2, pltpu.roll(z, 2, 0), pltpu.roll(zf, TS - (R - 2), 0))
    yc = u[:, _BG0:_CG0] * (cw_ref[0:1, :] * zm2 + cw_ref[1:2, :] * zm1 + cw_ref[2:3, :] * z)
    cvo_ref[...] = z3[:, SAMPLE_T - (CONV_W - 1):SAMPLE_T, :]

    gates = u[:, _GT0:PROJ_COLS] + gb_ref[...]
    ic = jnp.where(real, GATE_CAP * jnp.tanh(gates[:, :LANES] / GATE_CAP), NEG_BIG)
    lf = jnp.where(real, _log_sigmoid(gates[:, LANES:]), 0.0)
    b = _row_scan(lf, R, jnp.add, 0.0)
    uu = ic - b
    m_loc = b + _row_scan(uu, R, jnp.maximum, -jnp.inf)

    mp_s[...] = jnp.zeros(mp_s.shape, f32)
    mp_s[:, :, 0:NH] = m0_ref[...]
    m_prev = mp_s[...]
    b3 = b.reshape(NSEQ, R, LANES)
    ml3 = m_loc.reshape(NSEQ, R, LANES)
    b_last = b3[:, R - 1:R, :]
    m_new = jnp.maximum(b_last + m_prev, ml3[:, R - 1:R, :])
    g3 = b3 + m_prev
    mt3 = jnp.maximum(g3, ml3)
    wi = jnp.exp(g3 - mt3).reshape(TS, LANES)
    c2 = (mt3 - b3).reshape(TS, LANES)
    emt = jnp.exp(-mt3).reshape(TS, LANES)
    wk = jnp.exp(b_last - b3 + ic.reshape(NSEQ, R, LANES) - m_new).reshape(TS, LANES)
    a_st = jnp.broadcast_to(jnp.exp(b_last + m_prev - m_new), (NSEQ, R, LANES)).reshape(TS, LANES)
    mst_ref[...] = m_new[:, :, 0:NH]

    q = u[:, _Q0:_K0] * (DK ** -0.5)
    k = u[:, _K0:_V0]
    v = u[:, _V0:_OG0]
    rs = jnp.zeros((TS, LANES), f32)
    num = jnp.zeros((TS, MLSTM_W), f32)
    for d in range(SAMPLE_T):
        kd = k if d == 0 else pltpu.roll(k, d, 0)
        vd = v if d == 0 else pltpu.roll(v, d, 0)
        ud = uu if d == 0 else pltpu.roll(uu, d, 0)
        p = _split_dot(q * kd, segqk_ref[...], 2) * jnp.exp(ud - c2)
        rs = rs + p
        num = num + _split_dot(p, expv_ref[...], 2) * vd

    n0x = jnp.broadcast_to(n0_ref[...], (NSEQ, R, QK_W)).reshape(TS, QK_W)
    qn = _split_dot(q * n0x, segqk_ref[...], 2)
    den = wi * qn + rs
    rden = 1.0 / jnp.maximum(jnp.abs(den), emt)
    kw = k * _split_dot(wk, expk_ref[...], 2)
    ax = _split_dot(a_st, expv_ref[...], 3)
    lane_q = lax.broadcasted_iota(jnp.int32, (R, QK_W), 1)
    for i in range(NSEQ):
        rows = slice(i * R, (i + 1) * R)
        qi = q[rows, :]
        lhs = jnp.concatenate(
            [jnp.where((lane_q >= h * DK) & (lane_q < (h + 1) * DK), qi, 0.0) for h in range(NH)], axis=0)
        c0 = c0_ref[i]
        qc = _bdot(lhs.astype(bf16), c0.reshape(QK_W, DV).astype(bf16))
        qc_s[rows, :] = jnp.concatenate([qc[h * R:(h + 1) * R, :] for h in range(NH)], axis=1)
        kwi = kw[rows, :].astype(bf16)
        vi = v[rows, :].astype(bf16)
        for h in range(NH):
            dc = lax.dot_general(kwi[:, h * DK:(h + 1) * DK], vi[:, h * DV:(h + 1) * DV],
                                 (((0,), (0,)), ((), ())), preferred_element_type=f32)
            cst_ref[i, h] = ax[i * R:i * R + 1, h * DV:(h + 1) * DV] * c0[h] + dc

    hh = (_split_dot(wi, expv_ref[...], 2) * qc_s[...] + num) * _split_dot(rden, expv_ref[...], 2)
    ms = _split_dot(hh * hh, meanv_ref[...], 2)
    hm = jax.nn.sigmoid(u[:, _OG0:_BG0]) * (hh * lax.rsqrt(ms + EPS) * gmh_ref[...])
    mix = jnp.concatenate([hm, yc], axis=1).astype(bf16)
    out = x + _bdot(mix, wout_ref[...])
    x1_ref[...] = out.reshape(NSEQ, R, D_MODEL)[:, 0:SAMPLE_T, :]

    a_k = _split_dot(a_st, expk_ref[...], 3).reshape(NSEQ, R, QK_W)[:, 0:1, :]
    nst_ref[...] = a_k * n0_ref[...] + jnp.sum(kw.reshape(NSEQ, R, QK_W), axis=1, keepdims=True)


def _mixer_sample(x, cv, c0, n0, m0, weights, NSEQ):
    Bs = x.shape[0]
    f32 = jnp.float32
    kern = functools.partial(_mixer_sample_kernel, NSEQ=NSEQ)
    x_spec = pl.BlockSpec((NSEQ, SAMPLE_T, D_MODEL), lambda t: (t, 0, 0))
    cv_spec = pl.BlockSpec((NSEQ, CONV_W - 1, CONV_CH), lambda t: (t, 0, 0))
    c_spec = pl.BlockSpec((NSEQ, NH, DK, DV), lambda t: (t, 0, 0, 0))
    n_spec = pl.BlockSpec((NSEQ, 1, QK_W), lambda t: (t, 0, 0))
    m_spec = pl.BlockSpec((NSEQ, 1, NH), lambda t: (t, 0, 0))
    return pl.pallas_call(
        kern,
        grid=(Bs // NSEQ,),
        in_specs=[
            x_spec, cv_spec, c_spec, n_spec, m_spec,
            _const_spec((1, D_MODEL)),
            _const_spec((D_MODEL, PROJ_COLS)),
            _const_spec((1, 2 * LANES)),
            _const_spec((1, MLSTM_W)),
            _const_spec((CONV_W, CONV_CH)),
            _const_spec((D_MODEL, D_MODEL)),
            _const_spec((QK_W, LANES)),
            _const_spec((LANES, MLSTM_W)),
            _const_spec((LANES, QK_W)),
            _const_spec((MLSTM_W, MLSTM_W)),
        ],
        out_specs=[x_spec, c_spec, n_spec, m_spec, cv_spec],
        out_shape=[
            jax.ShapeDtypeStruct((Bs, SAMPLE_T, D_MODEL), f32),
            jax.ShapeDtypeStruct((Bs, NH, DK, DV), f32),
            jax.ShapeDtypeStruct((Bs, 1, QK_W), f32),
            jax.ShapeDtypeStruct((Bs, 1, NH), f32),
            jax.ShapeDtypeStruct((Bs, CONV_W - 1, CONV_CH), f32),
        ],
        scratch_shapes=[
            pltpu.VMEM((NSEQ, SUBLANES, D_MODEL), f32),
            pltpu.VMEM((NSEQ, SUBLANES, CONV_CH), f32),
            pltpu.VMEM((NSEQ, 1, LANES), f32),
            pltpu.VMEM((NSEQ * SUBLANES, MLSTM_W), f32),
        ],
        compiler_params=pltpu.CompilerParams(
            dimension_semantics=("arbitrary",), vmem_limit_bytes=VMEM_LIMIT),
        name="mixer_sample",
    )(x, cv, c0, n0, m0, *weights)


def _mixer_weights(g_mix, w_in, b_ig, b_fg, g_mh, conv_w, w_out):
    f32 = jnp.float32
    w_all, wt, wout = _prep_weights(w_in, w_out)
    gmix = g_mix.reshape(1, D_MODEL)
    zeros = jnp.zeros((SUBLANES - NH,), f32)
    gb_col = jnp.concatenate([b_ig, zeros, b_fg, zeros]).reshape(2 * SUBLANES, 1)
    zeros = jnp.zeros((LANES - NH,), f32)
    gb_row = jnp.concatenate([b_ig, zeros, b_fg, zeros]).reshape(1, 2 * LANES)
    gmh = g_mh.reshape(1, MLSTM_W)
    prompt = (gmix, w_all, wt, gb_col, gmh, conv_w, wout)
    sample = (gmix, w_all, gb_row, gmh, conv_w, wout) + _selectors()
    return prompt, sample


FF_STREAM = 256
_N_FF_PIECES = D_FF // FF_STREAM
_N_PIECES = _N_FF_PIECES + PLE_DIM // FF_STREAM + D_MODEL // FF_STREAM


def _ffn_rows(x, pe, gffn_ref, gple_ref, gfin_ref, wg_s, wu_s, wd_s, wple_s, wpg_s, *, chunk, final_norm,
              before_chunk=None, before_tail=None):
    bf16 = jnp.bfloat16
    f = _rms(x, gffn_ref[...]).astype(bf16)
    e = None
    if before_tail is None:
        e = _rms(_bdot(pe.astype(bf16), wple_s[...]), gple_ref[...])
    for c0 in range(0, D_FF, chunk):
        c1 = min(c0 + chunk, D_FF)
        if before_chunk is not None:
            before_chunk(c0 // chunk)
        gate = _bdot(f, wg_s[:, c0:c1])
        up = _bdot(f, wu_s[:, c0:c1])
        hmid = (gate * jax.nn.sigmoid(gate) * up).astype(bf16)
        x = x + _bdot(hmid, wd_s[c0:c1, :])
    if before_tail is not None:
        before_tail()
        e = _rms(_bdot(pe.astype(bf16), wple_s[...]), gple_ref[...])
    xb = x.astype(bf16)
    n_rows = x.shape[0]
    outs = []
    for r0 in range(0, n_rows, n_rows // FFN_TAIL_PIECES):
        rows = slice(r0, r0 + n_rows // FFN_TAIL_PIECES)
        xr = x[rows] + jax.nn.sigmoid(_bdot(xb[rows], wpg_s[...])) * e[rows]
        outs.append(_rms(xr, gfin_ref[...]) if final_norm else xr)
    return jnp.concatenate(outs, axis=0)


def _ffn_stream_kernel(xp_ref, pp_ref, xs_ref, ps_ref, gffn_ref, gple_ref, gfin_ref,
                       wg_hbm, wu_hbm, wd_hbm, wple_hbm, wpg_hbm,
                       yp_ref, ys_ref,
                       wg_s, wu_s, wd_s, wple_s, wpg_s, stg_g, stg_u, stg_d, sem, *, final_norm):
    t = pl.program_id(0)
    bf16 = jnp.bfloat16
    weights = (wg_s, wu_s, wd_s, wple_s, wpg_s)

    def copies(k):
        slot = k % 2
        if k < _N_FF_PIECES:
            cols = pl.ds(k * FF_STREAM, FF_STREAM)
            return [pltpu.make_async_copy(wg_hbm.at[:, cols], stg_g.at[slot], sem.at[0, slot]),
                    pltpu.make_async_copy(wu_hbm.at[:, cols], stg_u.at[slot], sem.at[1, slot]),
                    pltpu.make_async_copy(wd_hbm.at[cols, :], stg_d.at[slot], sem.at[2, slot])]
        if k == _N_FF_PIECES:
            return [pltpu.make_async_copy(wple_hbm, stg_d.at[slot], sem.at[2, slot])]
        rows = pl.ds((k - _N_FF_PIECES - 1) * FF_STREAM, FF_STREAM)
        return [pltpu.make_async_copy(wpg_hbm.at[rows, :], stg_d.at[slot], sem.at[2, slot])]

    def land(k):
        if k + 1 < _N_PIECES:
            for cp in copies(k + 1):
                cp.start()
        for cp in copies(k):
            cp.wait()
        slot = k % 2
        if k < _N_FF_PIECES:
            cols = slice(k * FF_STREAM, (k + 1) * FF_STREAM)
            wg_s[:, cols] = stg_g[slot].astype(bf16)
            wu_s[:, cols] = stg_u[slot].astype(bf16)
            wd_s[cols, :] = stg_d[slot].astype(bf16)
        elif k == _N_FF_PIECES:
            wple_s[...] = stg_d[slot].astype(bf16)
        else:
            r0 = (k - _N_FF_PIECES - 1) * FF_STREAM
            wpg_s[r0:r0 + FF_STREAM, :] = stg_d[slot].astype(bf16)

    @pl.when(t == 0)
    def _():
        for cp in copies(0):
            cp.start()
        n_tok = xs_ref.shape[0] * xs_ref.shape[1]

        def tail():
            for k in range(_N_FF_PIECES, _N_PIECES):
                land(k)

        y = _ffn_rows(xs_ref[...].reshape(n_tok, D_MODEL), ps_ref[...].reshape(n_tok, PLE_DIM),
                      gffn_ref, gple_ref, gfin_ref, *weights, chunk=FF_STREAM, final_norm=final_norm,
                      before_chunk=land, before_tail=tail)
        ys_ref[...] = y.reshape(ys_ref.shape)

    @pl.when(t > 0)
    def _():
        sub = xp_ref.shape[0] // FFN_SUB
        for r0 in range(0, xp_ref.shape[0], sub):
            rows = slice(r0, r0 + sub)
            yp_ref[rows, :] = _ffn_rows(xp_ref[rows, :], pp_ref[rows, :], gffn_ref, gple_ref, gfin_ref,
                                        *weights, chunk=FF_CHUNK, final_norm=final_norm)


def _ffn_stream(xp, pp, xs, ps, g_ffn, g_ple, g_final, w_gate, w_up, w_down, w_ple, w_pg, TM, final_norm):
    T = xp.shape[0]
    f32, bf16 = jnp.float32, jnp.bfloat16
    kern = functools.partial(_ffn_stream_kernel, final_norm=final_norm)
    row_map = lambda t: (jnp.maximum(t - 1, 0), 0)
    hbm = pl.BlockSpec(memory_space=pl.ANY)
    return pl.pallas_call(
        kern,
        grid=(T // TM + 1,),
        in_specs=[
            pl.BlockSpec((TM, D_MODEL), row_map),
            pl.BlockSpec((TM, PLE_DIM), row_map),
            _const_spec(xs.shape),
            _const_spec(ps.shape),
            _const_spec((1, D_MODEL)),
            _const_spec((1, D_MODEL)),
            _const_spec((1, D_MODEL)),
            hbm, hbm, hbm, hbm, hbm,
        ],
        out_specs=[pl.BlockSpec((TM, D_MODEL), row_map),
                   pl.BlockSpec(xs.shape, lambda t: (0, 0, 0))],
        out_shape=[jax.ShapeDtypeStruct(xp.shape, f32), jax.ShapeDtypeStruct(xs.shape, f32)],
        scratch_shapes=[
            pltpu.VMEM((D_MODEL, D_FF), bf16),
            pltpu.VMEM((D_MODEL, D_FF), bf16),
            pltpu.VMEM((D_FF, D_MODEL), bf16),
            pltpu.VMEM((PLE_DIM, D_MODEL), bf16),
            pltpu.VMEM((D_MODEL, D_MODEL), bf16),
            pltpu.VMEM((2, D_MODEL, FF_STREAM), f32),
            pltpu.VMEM((2, D_MODEL, FF_STREAM), f32),
            pltpu.VMEM((2, FF_STREAM, D_MODEL), f32),
            pltpu.SemaphoreType.DMA((3, 2)),
        ],
        compiler_params=pltpu.CompilerParams(
            dimension_semantics=("arbitrary",), vmem_limit_bytes=VMEM_LIMIT),
        name="ffn",
    )(xp, pp, xs, ps, g_ffn.reshape(1, D_MODEL), g_ple.reshape(1, D_MODEL), g_final.reshape(1, D_MODEL),
      w_gate, w_up, w_down, w_ple, w_pg)


def kernel(x_prompt, x_sample, p_prompt, p_sample, state_C, state_n, state_m, state_conv,
           g_mix, w_in, b_ig, b_fg, g_mh, conv_w, w_out, g_ffn, w_gate, w_up, w_down,
           w_ple, g_ple, w_pg, g_final):
    bf16 = jnp.bfloat16
    depth = g_mix.shape[0]
    B, S, _ = x_prompt.shape
    Bs, Ss, _ = x_sample.shape
    assert Ss == SAMPLE_T and S % PROMPT_TS == 0 and Bs % SAMPLE_NSEQ == 0

    xp = x_prompt
    xs = x_sample
    outs = [[] for _ in range(8)]
    for i in range(depth):
        last = i == depth - 1
        pw, sw = _mixer_weights(g_mix[i], w_in[i], b_ig[i], b_fg[i], g_mh[i], conv_w[i], w_out[i])
        x1p, cp, n_p, mp, cvp = _mixer_prompt(xp, pw, TB=PROMPT_TS, NSUB=PROMPT_NSUB, L=PROMPT_L)
        x1s, cs, n_s, ms, cvs = _mixer_sample(
            xs, state_conv[i], state_C[i], state_n[i].reshape(Bs, 1, QK_W),
            state_m[i].reshape(Bs, 1, NH), sw, SAMPLE_NSEQ)
        xp, xs = _ffn_stream(x1p.reshape(B * S, D_MODEL), p_prompt[i].reshape(B * S, PLE_DIM), x1s,
                             p_sample[i], g_ffn[i], g_ple[i], g_final, w_gate[i], w_up[i], w_down[i],
                             w_ple[i], w_pg[i], FFN_TM, last)
        xp = xp.reshape(B, S, D_MODEL)

        new = (cp, n_p.reshape(B, NH, DK), mp[:, :NH, 0], cvp[:, SUBLANES - (CONV_W - 1):],
               cs, n_s.reshape(Bs, NH, DK), ms.reshape(Bs, NH), cvs)
        for lst, v in zip(outs, new):
            lst.append(v)

    return (xp, xs) + tuple(jnp.stack(l) for l in outs)
```

```python
import functools

import numpy as np
import jax
import jax.numpy as jnp
from jax import lax
from jax.experimental import pallas as pl
from jax.experimental.pallas import tpu as pltpu

D_MODEL = 1024
NH = 4
DK = 64
DV = 128
MLSTM_W = NH * DV
QK_W = NH * DK
CONV_CH = 512
CONV_W = 3
D_FF = 2816
FF_CHUNK = 768
FFN_TM = 512
FFN_TAIL_PIECES = 2
PLE_DIM = 256
PROMPT_L = 128
PROMPT_TS = 256
PROMPT_NSUB = 2
SAMPLE_T = 4
SAMPLE_NSEQ = 32
GATE_CAP = 15.0
EPS = 1e-6

LANES = 128
SUBLANES = 8
NEG_BIG = -1e30

VMEM_LIMIT = 56 * 1024 * 1024


def _rms(x, g):
    ms = jnp.mean(x * x, axis=-1, keepdims=True)
    return x * lax.rsqrt(ms + EPS) * g


def _bdot(a, b):
    return jnp.dot(a, b, preferred_element_type=jnp.float32)


def _log_sigmoid(x):
    return jnp.minimum(x, 0.0) - jnp.log1p(jnp.exp(-jnp.abs(x)))


def _const_spec(shape):
    nd = len(shape)
    return pl.BlockSpec(shape, lambda *_: (0,) * nd, pipeline_mode=pl.Buffered(1))


_IG0 = 2 * QK_W + MLSTM_W
_OG_IN = _IG0 + 2 * NH
PROJ_IN = _OG_IN + MLSTM_W + 3 * CONV_CH
_Q0, _K0, _V0, _OG0, _BG0, _CG0, _HC0, _GT0 = 0, 256, 512, 1024, 1536, 2048, 2560, 3072
PROJ_COLS = _GT0 + 2 * LANES
_TQ0, _TV0, _TOG0, _TG0 = 0, 256, 768, 1280
TPROJ_ROWS = _TG0 + 2 * SUBLANES
PREP_TK = 256


def _prep_weights_kernel(w_ref, wo_ref, wall_ref, wt_ref, wout_ref):
    bf16 = jnp.bfloat16
    wout_ref[...] = wo_ref[...].astype(bf16)
    wf = w_ref[...]
    tk = wf.shape[1]
    g8 = wf[_IG0:_OG_IN]
    row = lax.broadcasted_iota(jnp.int32, (SUBLANES, tk), 0)
    g_in = jnp.where(row < NH, g8, 0.0)
    g_fg = jnp.where(row < NH, pltpu.roll(g8, SUBLANES - NH, 0), 0.0)

    wt_ref[_TQ0:_TV0, :] = wf[_Q0:_K0].astype(bf16)
    wt_ref[_TV0:_TOG0, :] = wf[_V0:_OG0].astype(bf16)
    wt_ref[_TOG0:_TG0, :] = wf[_OG_IN:_OG_IN + MLSTM_W].astype(bf16)
    wt_ref[_TG0:TPROJ_ROWS, :] = jnp.concatenate([g_in, g_fg], axis=0).astype(bf16)

    wall_ref[:, 0:_OG0] = wf[0:_IG0].T.astype(bf16)
    wall_ref[:, _OG0:_GT0] = wf[_OG_IN:PROJ_IN].T.astype(bf16)
    pad = jnp.zeros((LANES - SUBLANES, tk), jnp.float32)
    wall_ref[:, _GT0:_GT0 + LANES] = jnp.concatenate([g_in, pad], axis=0).T.astype(bf16)
    wall_ref[:, _GT0 + LANES:PROJ_COLS] = jnp.concatenate([g_fg, pad], axis=0).T.astype(bf16)


def _prep_weights(w_in, w_out):
    bf16 = jnp.bfloat16
    return pl.pallas_call(
        _prep_weights_kernel,
        grid=(D_MODEL // PREP_TK,),
        in_specs=[pl.BlockSpec((PROJ_IN, PREP_TK), lambda i: (0, i)),
                  pl.BlockSpec((PREP_TK, D_MODEL), lambda i: (i, 0))],
        out_specs=[pl.BlockSpec((PREP_TK, PROJ_COLS), lambda i: (i, 0)),
                   pl.BlockSpec((TPROJ_ROWS, PREP_TK), lambda i: (0, i)),
                   pl.BlockSpec((PREP_TK, D_MODEL), lambda i: (i, 0))],
        out_shape=[jax.ShapeDtypeStruct((D_MODEL, PROJ_COLS), bf16),
                   jax.ShapeDtypeStruct((TPROJ_ROWS, D_MODEL), bf16),
                   jax.ShapeDtypeStruct((D_MODEL, D_MODEL), bf16)],
        compiler_params=pltpu.CompilerParams(
            dimension_semantics=("arbitrary",), vmem_limit_bytes=VMEM_LIMIT),
        name="prep_weights",
    )(w_in.T, w_out)


ST_ROWS = DV + 2 * SUBLANES


def _lane_scan(x, seg, op, fill):
    pos = lax.broadcasted_iota(jnp.int32, x.shape, 1) & (seg - 1)
    k = 1
    while k < seg:
        x = op(x, jnp.where(pos >= k, pltpu.roll(x, k, 1), fill))
        k *= 2
    return x


PROJ_PIECE = 256


def _run(*phases):
    live = list(phases)
    while live:
        for g in list(live):
            try:
                next(g)
            except StopIteration:
                live.remove(g)


def _mixer_prompt_kernel(x_ref, gmix_ref, wn_ref, wt_ref, gbt_ref, gmhc_ref, cw_ref, wout_ref,
                         x1_ref, c_ref, n_ref, m_ref, cvo_ref,
                         qt_s, vt_s, ogt_s, k_s, mixt_s, z_s, ucol_s, wi_s, c2_s, emt_s, wk_s, ast_s, st_s,
                         *, L, TB, NSUB):
    j = pl.program_id(1)

    @pl.when(j == 0)
    def _():
        z_s[0, 0:SUBLANES, :] = jnp.zeros((SUBLANES, CONV_CH), jnp.float32)
        st_s[...] = jnp.zeros(st_s.shape, jnp.float32)
        m_ref[...] = jnp.zeros(m_ref.shape, jnp.float32)

    m_prev = m_ref[0]
    sub = [dict(sb=sb, x_ref=x_ref, gmix_ref=gmix_ref, wn_ref=wn_ref, wt_ref=wt_ref, gbt_ref=gbt_ref,
                gmhc_ref=gmhc_ref, cw_ref=cw_ref, wout_ref=wout_ref, x1_ref=x1_ref, qt_s=qt_s.at[sb],
                vt_s=vt_s.at[sb], ogt_s=ogt_s.at[sb], k_s=k_s.at[sb], mixt_s=mixt_s.at[sb], z_s=z_s,
                ucol_s=ucol_s.at[sb], wi_s=wi_s.at[sb], c2_s=c2_s.at[sb], emt_s=emt_s.at[sb],
                wk_s=wk_s.at[sb], ast_s=ast_s.at[sb], st_s=st_s, L=L, TS=TB) for sb in range(NSUB)]
    carry = {"m": m_prev}
    _run(_prompt_project(sub[0]))
    for a, b in zip(sub[:-1], sub[1:]):
        _run(_prompt_chunks(a, carry), _prompt_project(b))
    _run(_prompt_chunks(sub[-1], carry), *[_prompt_output(s) for s in sub[:-1]])
    _run(_prompt_output(sub[-1]))
    m_ref[0] = carry["m"]
    cvo_ref[0] = z_s[NSUB - 1, TB:TB + SUBLANES, :]
    z_s[0, 0:SUBLANES, :] = z_s[NSUB - 1, TB:TB + SUBLANES, :]

    @pl.when(j == pl.num_programs(1) - 1)
    def _():
        for pr in range(NH // 2):
            state = st_s[pr]
            c_pair = state[0:DV, :].T
            c_ref[0, 2 * pr] = c_pair[0:DK, :]
            c_ref[0, 2 * pr + 1] = c_pair[DK:, :]
            n_ref[0, pr:pr + 1, :] = state[DV:DV + 1, :]


def _prompt_project(s):
    sb, L, TS = s["sb"], s["L"], s["TS"]
    x_ref, gmix_ref, wn_ref, wt_ref, gbt_ref, cw_ref = (
        s["x_ref"], s["gmix_ref"], s["wn_ref"], s["wt_ref"], s["gbt_ref"], s["cw_ref"])
    qt_s, vt_s, ogt_s, k_s, z_s = s["qt_s"], s["vt_s"], s["ogt_s"], s["k_s"], s["z_s"]
    bf16 = jnp.bfloat16

    x = x_ref[0, sb * TS:(sb + 1) * TS, :]
    a = _rms(x, gmix_ref[...]).astype(bf16)
    nt_dims = (((1,), (1,)), ((), ()))

    gt = lax.dot_general(wt_ref[_TG0:TPROJ_ROWS, :], a, nt_dims,
                         preferred_element_type=jnp.float32) + gbt_ref[...]
    ic = GATE_CAP * jnp.tanh(gt[0:SUBLANES] / GATE_CAP)
    lf = _log_sigmoid(gt[SUBLANES:])
    b = _lane_scan(lf, L, jnp.add, 0.0)
    u = ic - b
    m_loc = b + _lane_scan(u, L, jnp.maximum, -jnp.inf)
    s.update(b=b, u=u, ic=ic, m_loc=m_loc)
    yield

    k_s[...] = _bdot(a, wn_ref[:, _K0:_V0])
    yield
    hc = _bdot(a, wn_ref[:, _HC0:_GT0])
    yield
    z = _bdot(a, wn_ref[:, _CG0:_HC0]) * hc
    if sb > 0:
        z_s[sb, 0:SUBLANES, :] = z_s[sb - 1, TS:TS + SUBLANES, :]
    z_s[sb, SUBLANES:SUBLANES + TS, :] = z
    yield
    yconv = (cw_ref[0:1, :] * z_s[sb, SUBLANES - 2:SUBLANES - 2 + TS, :]
             + cw_ref[1:2, :] * z_s[sb, SUBLANES - 1:SUBLANES - 1 + TS, :]
             + cw_ref[2:3, :] * z)
    s["yc"] = (_bdot(a, wn_ref[:, _BG0:_CG0]) * yconv).astype(bf16)
    yield

    for r0 in range(0, _TG0, PROJ_PIECE):
        ut = lax.dot_general(wt_ref[r0:r0 + PROJ_PIECE, :], a, nt_dims, preferred_element_type=jnp.float32)
        if r0 < _TV0:
            qt_s[r0:r0 + PROJ_PIECE, :] = ut * (DK ** -0.5)
        elif r0 < _TOG0:
            vt_s[r0 - _TV0:r0 - _TV0 + PROJ_PIECE, :] = ut
        else:
            ogt_s[r0 - _TOG0:r0 - _TOG0 + PROJ_PIECE, :] = ut
        yield


def _prompt_chunks(s, carry):
    L, TS = s["L"], s["TS"]
    qt_s, vt_s, ogt_s, k_s, mixt_s, ucol_s, wi_s, c2_s, emt_s, wk_s, ast_s, st_s = (
        s["qt_s"], s["vt_s"], s["ogt_s"], s["k_s"], s["mixt_s"], s["ucol_s"], s["wi_s"],
        s["c2_s"], s["emt_s"], s["wk_s"], s["ast_s"], s["st_s"])
    b, u, ic, m_loc = s["b"], s["u"], s["ic"], s["m_loc"]
    nc = TS // L
    bf16 = jnp.bfloat16
    m_prev = carry["m"]

    for c in range(nc):
        sl = slice(c * L, (c + 1) * L)
        bc, mlc = b[:, sl], m_loc[:, sl]
        b_last = jnp.broadcast_to(bc[:, L - 1:L], bc.shape)
        m_new = jnp.maximum(b_last + m_prev, jnp.broadcast_to(mlc[:, L - 1:L], bc.shape))
        g = bc + m_prev
        mt = jnp.maximum(g, mlc)
        wi_s[:, sl] = jnp.exp(g - mt)
        c2_s[:, sl] = mt - bc
        emt_s[:, sl] = jnp.exp(-mt)
        wk_s[:, sl] = jnp.exp(b_last - bc + ic[:, sl] - m_new)
        ast_s[c] = jnp.exp(b_last + m_prev - m_new)
        upad = jnp.concatenate([u[:, sl], jnp.zeros((L - SUBLANES, L), jnp.float32)], axis=0)
        ucol_s[c * L:(c + 1) * L, :] = upad.T
        m_prev = m_new
    carry["m"] = m_prev
    yield

    s_i = lax.broadcasted_iota(jnp.int32, (L, L), 0)
    t_i = lax.broadcasted_iota(jnp.int32, (L, L), 1)
    causal = s_i <= t_i
    low_half = lax.broadcasted_iota(jnp.int32, (L, LANES), 1) < DK
    zeros_q = jnp.zeros((DK, L), jnp.float32)
    zeros_p = jnp.zeros((L, L), bf16)

    def pair_row(ref, h0, rows):
        return jnp.concatenate([ref[h0:h0 + 1, rows], ref[h0 + 1:h0 + 2, rows]], axis=1)

    for c in range(nc):
        rows = slice(c * L, (c + 1) * L)
        for pr in range(NH // 2):
            h0 = 2 * pr
            hv0 = slice(h0 * DV, (h0 + 1) * DV)
            hv1 = slice((h0 + 1) * DV, (h0 + 2) * DV)
            kp = k_s[rows, pr * LANES:(pr + 1) * LANES]
            q0 = qt_s[h0 * DK:(h0 + 1) * DK, rows]
            q1 = qt_s[(h0 + 1) * DK:(h0 + 2) * DK, rows]
            qbd = jnp.concatenate([jnp.concatenate([q0, zeros_q], axis=1),
                                   jnp.concatenate([zeros_q, q1], axis=1)], axis=0).astype(bf16)
            st = _bdot(kp.astype(bf16), qbd)
            arg = jnp.concatenate(
                [jnp.where(causal, ucol_s[rows, h:h + 1] - c2_s[h:h + 1, rows], -jnp.inf)
                 for h in (h0, h0 + 1)], axis=1)
            pt = st * jnp.exp(arg)
            rs = jnp.sum(pt, axis=0, keepdims=True)
            ptb = pt.astype(bf16)
            pbd = jnp.concatenate([jnp.concatenate([ptb[:, 0:L], zeros_p], axis=1),
                                   jnp.concatenate([zeros_p, ptb[:, L:]], axis=1)], axis=0)
            vt = jnp.concatenate([vt_s[hv0, rows], vt_s[hv1, rows]], axis=1)
            state = st_s[pr]
            sq = _bdot(state.astype(bf16), qbd) * pair_row(wi_s, h0, rows)
            num = _bdot(vt.astype(bf16), pbd) + sq[0:DV]
            den = sq[DV:DV + 1] + rs
            hh = num * (1.0 / jnp.maximum(jnp.abs(den), pair_row(emt_s, h0, rows)))
            hn = hh * lax.rsqrt(jnp.mean(hh * hh, axis=0, keepdims=True) + EPS)
            mixt_s[hv0, rows] = jax.nn.sigmoid(ogt_s[hv0, rows]) * hn[:, 0:L]
            mixt_s[hv1, rows] = jax.nn.sigmoid(ogt_s[hv1, rows]) * hn[:, L:]
            wkr = pair_row(wk_s, h0, rows)
            vw = jnp.concatenate([vt * wkr, jnp.broadcast_to(wkr, (2 * SUBLANES, 2 * L))], axis=0)
            km = jnp.concatenate([jnp.where(low_half, kp, 0.0), jnp.where(low_half, 0.0, kp)], axis=0)
            decay = jnp.where(low_half[0:1, :], ast_s[c][h0:h0 + 1, :], ast_s[c][h0 + 1:h0 + 2, :])
            st_s[pr] = decay * state + _bdot(vw.astype(bf16), km.astype(bf16))
            yield


def _prompt_output(s):
    sb, TS = s["sb"], s["TS"]
    x_ref, wout_ref, x1_ref, mixt_s = s["x_ref"], s["wout_ref"], s["x1_ref"], s["mixt_s"]
    tok = slice(sb * TS, (sb + 1) * TS)
    hm = mixt_s[...].T * s["gmhc_ref"][...]
    mix = jnp.concatenate([hm.astype(jnp.bfloat16), s["yc"]], axis=1)
    yield
    for c0 in range(0, D_MODEL, PROJ_PIECE):
        cols = slice(c0, c0 + PROJ_PIECE)
        x1_ref[0, tok, cols] = x_ref[0, tok, cols] + _bdot(mix, wout_ref[:, cols])
        yield


def _mixer_prompt(x, weights, TB, NSUB, L):
    B, S, _ = x.shape
    TS = TB * NSUB
    nb = S // TS
    f32 = jnp.float32
    kern = functools.partial(_mixer_prompt_kernel, L=L, TB=TB, NSUB=NSUB)
    return pl.pallas_call(
        kern,
        grid=(B, nb),
        in_specs=[
            pl.BlockSpec((1, TS, D_MODEL), lambda b, j: (b, j, 0)),
            _const_spec((1, D_MODEL)),
            _const_spec((D_MODEL, PROJ_COLS)),
            _const_spec((TPROJ_ROWS, D_MODEL)),
            _const_spec((2 * SUBLANES, 1)),
            _const_spec((1, MLSTM_W)),
            _const_spec((CONV_W, CONV_CH)),
            _const_spec((D_MODEL, D_MODEL)),
        ],
        out_specs=[
            pl.BlockSpec((1, TS, D_MODEL), lambda b, j: (b, j, 0)),
            pl.BlockSpec((1, NH, DK, DV), lambda b, j: (b, 0, 0, 0)),
            pl.BlockSpec((1, NH // 2, 2 * DK), lambda b, j: (b, 0, 0)),
            pl.BlockSpec((1, SUBLANES, LANES), lambda b, j: (b, 0, 0)),
            pl.BlockSpec((1, SUBLANES, CONV_CH), lambda b, j: (b, 0, 0)),
        ],
        out_shape=[
            jax.ShapeDtypeStruct((B, S, D_MODEL), f32),
            jax.ShapeDtypeStruct((B, NH, DK, DV), f32),
            jax.ShapeDtypeStruct((B, NH // 2, 2 * DK), f32),
            jax.ShapeDtypeStruct((B, SUBLANES, LANES), f32),
            jax.ShapeDtypeStruct((B, SUBLANES, CONV_CH), f32),
        ],
        scratch_shapes=[
            pltpu.VMEM((NSUB, QK_W, TB), f32),
            pltpu.VMEM((NSUB, MLSTM_W, TB), f32),
            pltpu.VMEM((NSUB, MLSTM_W, TB), f32),
            pltpu.VMEM((NSUB, TB, QK_W), f32),
            pltpu.VMEM((NSUB, MLSTM_W, TB), f32),
            pltpu.VMEM((NSUB, TB + SUBLANES, CONV_CH), f32),
            pltpu.VMEM((NSUB, TB, LANES), f32),
            pltpu.VMEM((NSUB, SUBLANES, TB), f32),
            pltpu.VMEM((NSUB, SUBLANES, TB), f32),
            pltpu.VMEM((NSUB, SUBLANES, TB), f32),
            pltpu.VMEM((NSUB, SUBLANES, TB), f32),
            pltpu.VMEM((NSUB, TB // L, SUBLANES, LANES), f32),
            pltpu.VMEM((NH // 2, ST_ROWS, LANES), f32),
        ],
        compiler_params=pltpu.CompilerParams(
            dimension_semantics=("arbitrary", "arbitrary"), vmem_limit_bytes=VMEM_LIMIT),
        name="mixer_prompt",
    )(x, *weights)


def _row_scan(x, seg, op, fill):
    pos = lax.broadcasted_iota(jnp.int32, x.shape, 0) & (seg - 1)
    k = 1
    while k < seg:
        x = op(x, jnp.where(pos >= k, pltpu.roll(x, k, 0), fill))
        k *= 2
    return x


def _split_dot(x, sel, parts):
    acc = None
    rem = x
    for p in range(parts):
        hi = rem.astype(jnp.bfloat16)
        d = _bdot(hi, sel)
        acc = d if acc is None else acc + d
        if p + 1 < parts:
            rem = rem - hi.astype(jnp.float32)
    return acc


def _selectors():
    h = np.arange(NH)
    seg_qk = np.zeros((QK_W, LANES), np.float32)
    seg_qk[np.arange(QK_W), np.arange(QK_W) // DK] = 1.0
    exp_v = np.zeros((LANES, MLSTM_W), np.float32)
    exp_k = np.zeros((LANES, QK_W), np.float32)
    for i in h:
        exp_v[i, i * DV:(i + 1) * DV] = 1.0
        exp_k[i, i * DK:(i + 1) * DK] = 1.0
    mean_v = np.kron(np.eye(NH, dtype=np.float32), np.full((DV, DV), 1.0 / DV, np.float32))
    return tuple(jnp.asarray(m, jnp.bfloat16) for m in (seg_qk, exp_v, exp_k, mean_v))


def _mixer_sample_kernel(x_ref, cv_ref, c0_ref, n0_ref, m0_ref,
                         gmix_ref, win_ref, gb_ref, gmh_ref, cw_ref, wout_ref,
                         segqk_ref, expv_ref, expk_ref, meanv_ref,
                         x1_ref, cst_ref, nst_ref, mst_ref, cvo_ref,
                         xp_s, z_s, mp_s, qc_s, *, NSEQ):
    R = SUBLANES
    TS = NSEQ * R
    bf16 = jnp.bfloat16
    f32 = jnp.float32

    xp_s[:, SAMPLE_T:, :] = jnp.zeros((NSEQ, R - SAMPLE_T, D_MODEL), f32)
    xp_s[:, 0:SAMPLE_T, :] = x_ref[...]
    x = xp_s[...].reshape(TS, D_MODEL)
    a = _rms(x, gmix_ref[...]).astype(bf16)
    u = _bdot(a, win_ref[...])

    pos = lax.broadcasted_iota(jnp.int32, (TS, 1), 0) & (R - 1)
    real = pos < SAMPLE_T

    z = u[:, _CG0:_HC0] * u[:, _HC0:_GT0]
    z3 = z.reshape(NSEQ, R, CONV_CH)
    z_s[...] = z3
    z_s[:, R - (CONV_W - 1):, :] = cv_ref[...]
    zf = z_s[...].reshape(TS, CONV_CH)
    zm1 = jnp.where(pos >= 1, pltpu.roll(z, 1, 0), pltpu.roll(zf, TS - (R - 1), 0))
    zm2 = jnp.where(pos >= 2, pltpu.roll(z, 2, 0), pltpu.roll(zf, TS - (R - 2), 0))
    yc = u[:, _BG0:_CG0] * (cw_ref[0:1, :] * zm2 + cw_ref[1:2, :] * zm1 + cw_ref[2:3, :] * z)
    cvo_ref[...] = z3[:, SAMPLE_T - (CONV_W - 1):SAMPLE_T, :]

    gates = u[:, _GT0:PROJ_COLS] + gb_ref[...]
    ic = jnp.where(real, GATE_CAP * jnp.tanh(gates[:, :LANES] / GATE_CAP), NEG_BIG)
    lf = jnp.where(real, _log_sigmoid(gates[:, LANES:]), 0.0)
    b = _row_scan(lf, R, jnp.add, 0.0)
    uu = ic - b
    m_loc = b + _row_scan(uu, R, jnp.maximum, -jnp.inf)

    mp_s[...] = jnp.zeros(mp_s.shape, f32)
    mp_s[:, :, 0:NH] = m0_ref[...]
    m_prev = mp_s[...]
    b3 = b.reshape(NSEQ, R, LANES)
    ml3 = m_loc.reshape(NSEQ, R, LANES)
    b_last = b3[:, R - 1:R, :]
    m_new = jnp.maximum(b_last + m_prev, ml3[:, R - 1:R, :])
    g3 = b3 + m_prev
    mt3 = jnp.maximum(g3, ml3)
    wi = jnp.exp(g3 - mt3).reshape(TS, LANES)
    c2 = (mt3 - b3).reshape(TS, LANES)
    emt = jnp.exp(-mt3).reshape(TS, LANES)
    wk = jnp.exp(b_last - b3 + ic.reshape(NSEQ, R, LANES) - m_new).reshape(TS, LANES)
    a_st = jnp.broadcast_to(jnp.exp(b_last + m_prev - m_new), (NSEQ, R, LANES)).reshape(TS, LANES)
    mst_ref[...] = m_new[:, :, 0:NH]

    q = u[:, _Q0:_K0] * (DK ** -0.5)
    k = u[:, _K0:_V0]
    v = u[:, _V0:_OG0]
    rs = jnp.zeros((TS, LANES), f32)
    num = jnp.zeros((TS, MLSTM_W), f32)
    for d in range(SAMPLE_T):
        kd = k if d == 0 else pltpu.roll(k, d, 0)
        vd = v if d == 0 else pltpu.roll(v, d, 0)
        ud = uu if d == 0 else pltpu.roll(uu, d, 0)
        p = _split_dot(q * kd, segqk_ref[...], 2) * jnp.exp(ud - c2)
        rs = rs + p
        num = num + _split_dot(p, expv_ref[...], 2) * vd

    n0x = jnp.broadcast_to(n0_ref[...], (NSEQ, R, QK_W)).reshape(TS, QK_W)
    qn = _split_dot(q * n0x, segqk_ref[...], 2)
    den = wi * qn + rs
    rden = 1.0 / jnp.maximum(jnp.abs(den), emt)
    kw = k * _split_dot(wk, expk_ref[...], 2)
    ax = _split_dot(a_st, expv_ref[...], 3)
    lane_q = lax.broadcasted_iota(jnp.int32, (R, QK_W), 1)
    for i in range(NSEQ):
        rows = slice(i * R, (i + 1) * R)
        qi = q[rows, :]
        lhs = jnp.concatenate(
            [jnp.where((lane_q >= h * DK) & (lane_q < (h + 1) * DK), qi, 0.0) for h in range(NH)], axis=0)
        c0 = c0_ref[i]
        qc = _bdot(lhs.astype(bf16), c0.reshape(QK_W, DV).astype(bf16))
        qc_s[rows, :] = jnp.concatenate([qc[h * R:(h + 1) * R, :] for h in range(NH)], axis=1)
        kwi = kw[rows, :].astype(bf16)
        vi = v[rows, :].astype(bf16)
        for h in range(NH):
            dc = lax.dot_general(kwi[:, h * DK:(h + 1) * DK], vi[:, h * DV:(h + 1) * DV],
                                 (((0,), (0,)), ((), ())), preferred_element_type=f32)
            cst_ref[i, h] = ax[i * R:i * R + 1, h * DV:(h + 1) * DV] * c0[h] + dc

    hh = (_split_dot(wi, expv_ref[...], 2) * qc_s[...] + num) * _split_dot(rden, expv_ref[...], 2)
    ms = _split_dot(hh * hh, meanv_ref[...], 2)
    hm = jax.nn.sigmoid(u[:, _OG0:_BG0]) * (hh * lax.rsqrt(ms + EPS) * gmh_ref[...])
    mix = jnp.concatenate([hm, yc], axis=1).astype(bf16)
    out = x + _bdot(mix, wout_ref[...])
    x1_ref[...] = out.reshape(NSEQ, R, D_MODEL)[:, 0:SAMPLE_T, :]

    a_k = _split_dot(a_st, expk_ref[...], 3).reshape(NSEQ, R, QK_W)[:, 0:1, :]
    nst_ref[...] = a_k * n0_ref[...] + jnp.sum(kw.reshape(NSEQ, R, QK_W), axis=1, keepdims=True)


def _mixer_sample(x, cv, c0, n0, m0, weights, NSEQ):
    Bs = x.shape[0]
    f32 = jnp.float32
    kern = functools.partial(_mixer_sample_kernel, NSEQ=NSEQ)
    x_spec = pl.BlockSpec((NSEQ, SAMPLE_T, D_MODEL), lambda t: (t, 0, 0))
    cv_spec = pl.BlockSpec((NSEQ, CONV_W - 1, CONV_CH), lambda t: (t, 0, 0))
    c_spec = pl.BlockSpec((NSEQ, NH, DK, DV), lambda t: (t, 0, 0, 0))
    n_spec = pl.BlockSpec((NSEQ, 1, QK_W), lambda t: (t, 0, 0))
    m_spec = pl.BlockSpec((NSEQ, 1, NH), lambda t: (t, 0, 0))
    return pl.pallas_call(
        kern,
        grid=(Bs // NSEQ,),
        in_specs=[
            x_spec, cv_spec, c_spec, n_spec, m_spec,
            _const_spec((1, D_MODEL)),
            _const_spec((D_MODEL, PROJ_COLS)),
            _const_spec((1, 2 * LANES)),
            _const_spec((1, MLSTM_W)),
            _const_spec((CONV_W, CONV_CH)),
            _const_spec((D_MODEL, D_MODEL)),
            _const_spec((QK_W, LANES)),
            _const_spec((LANES, MLSTM_W)),
            _const_spec((LANES, QK_W)),
            _const_spec((MLSTM_W, MLSTM_W)),
        ],
        out_specs=[x_spec, c_spec, n_spec, m_spec, cv_spec],
        out_shape=[
            jax.ShapeDtypeStruct((Bs, SAMPLE_T, D_MODEL), f32),
            jax.ShapeDtypeStruct((Bs, NH, DK, DV), f32),
            jax.ShapeDtypeStruct((Bs, 1, QK_W), f32),
            jax.ShapeDtypeStruct((Bs, 1, NH), f32),
            jax.ShapeDtypeStruct((Bs, CONV_W - 1, CONV_CH), f32),
        ],
        scratch_shapes=[
            pltpu.VMEM((NSEQ, SUBLANES, D_MODEL), f32),
            pltpu.VMEM((NSEQ, SUBLANES, CONV_CH), f32),
            pltpu.VMEM((NSEQ, 1, LANES), f32),
            pltpu.VMEM((NSEQ * SUBLANES, MLSTM_W), f32),
        ],
        compiler_params=pltpu.CompilerParams(
            dimension_semantics=("arbitrary",), vmem_limit_bytes=VMEM_LIMIT),
        name="mixer_sample",
    )(x, cv, c0, n0, m0, *weights)


def _mixer_weights(g_mix, w_in, b_ig, b_fg, g_mh, conv_w, w_out):
    f32 = jnp.float32
    w_all, wt, wout = _prep_weights(w_in, w_out)
    gmix = g_mix.reshape(1, D_MODEL)
    zeros = jnp.zeros((SUBLANES - NH,), f32)
    gb_col = jnp.concatenate([b_ig, zeros, b_fg, zeros]).reshape(2 * SUBLANES, 1)
    zeros = jnp.zeros((LANES - NH,), f32)
    gb_row = jnp.concatenate([b_ig, zeros, b_fg, zeros]).reshape(1, 2 * LANES)
    gmh = g_mh.reshape(1, MLSTM_W)
    prompt = (gmix, w_all, wt, gb_col, gmh, conv_w, wout)
    sample = (gmix, w_all, gb_row, gmh, conv_w, wout) + _selectors()
    return prompt, sample


FF_STREAM = 256
_N_FF_PIECES = D_FF // FF_STREAM
_N_PIECES = _N_FF_PIECES + PLE_DIM // FF_STREAM + D_MODEL // FF_STREAM


def _ffn_rows(x, pe, gffn_ref, gple_ref, gfin_ref, wg_s, wu_s, wd_s, wple_s, wpg_s, *, chunk, final_norm,
              before_chunk=None, before_tail=None):
    bf16 = jnp.bfloat16
    f = _rms(x, gffn_ref[...]).astype(bf16)
    e = None
    if before_tail is None:
        e = _rms(_bdot(pe.astype(bf16), wple_s[...]), gple_ref[...])
    for c0 in range(0, D_FF, chunk):
        c1 = min(c0 + chunk, D_FF)
        if before_chunk is not None:
            before_chunk(c0 // chunk)
        gate = _bdot(f, wg_s[:, c0:c1])
        up = _bdot(f, wu_s[:, c0:c1])
        hmid = (gate * jax.nn.sigmoid(gate) * up).astype(bf16)
        x = x + _bdot(hmid, wd_s[c0:c1, :])
    if before_tail is not None:
        before_tail()
        e = _rms(_bdot(pe.astype(bf16), wple_s[...]), gple_ref[...])
    xb = x.astype(bf16)
    n_rows = x.shape[0]
    outs = []
    for r0 in range(0, n_rows, n_rows // FFN_TAIL_PIECES):
        rows = slice(r0, r0 + n_rows // FFN_TAIL_PIECES)
        xr = x[rows] + jax.nn.sigmoid(_bdot(xb[rows], wpg_s[...])) * e[rows]
        outs.append(_rms(xr, gfin_ref[...]) if final_norm else xr)
    return jnp.concatenate(outs, axis=0)


def _ffn_stream_kernel(xp_ref, pp_ref, xs_ref, ps_ref, gffn_ref, gple_ref, gfin_ref,
                       wg_hbm, wu_hbm, wd_hbm, wple_hbm, wpg_hbm,
                       yp_ref, ys_ref,
                       wg_s, wu_s, wd_s, wple_s, wpg_s, stg_g, stg_u, stg_d, sem, *, final_norm):
    t = pl.program_id(0)
    bf16 = jnp.bfloat16
    weights = (wg_s, wu_s, wd_s, wple_s, wpg_s)

    def copies(k):
        slot = k % 2
        if k < _N_FF_PIECES:
            cols = pl.ds(k * FF_STREAM, FF_STREAM)
            return [pltpu.make_async_copy(wg_hbm.at[:, cols], stg_g.at[slot], sem.at[0, slot]),
                    pltpu.make_async_copy(wu_hbm.at[:, cols], stg_u.at[slot], sem.at[1, slot]),
                    pltpu.make_async_copy(wd_hbm.at[cols, :], stg_d.at[slot], sem.at[2, slot])]
        if k == _N_FF_PIECES:
            return [pltpu.make_async_copy(wple_hbm, stg_d.at[slot], sem.at[2, slot])]
        rows = pl.ds((k - _N_FF_PIECES - 1) * FF_STREAM, FF_STREAM)
        return [pltpu.make_async_copy(wpg_hbm.at[rows, :], stg_d.at[slot], sem.at[2, slot])]

    def land(k):
        if k + 1 < _N_PIECES:
            for cp in copies(k + 1):
                cp.start()
        for cp in copies(k):
            cp.wait()
        slot = k % 2
        if k < _N_FF_PIECES:
            cols = slice(k * FF_STREAM, (k + 1) * FF_STREAM)
            wg_s[:, cols] = stg_g[slot].astype(bf16)
            wu_s[:, cols] = stg_u[slot].astype(bf16)
            wd_s[cols, :] = stg_d[slot].astype(bf16)
        elif k == _N_FF_PIECES:
            wple_s[...] = stg_d[slot].astype(bf16)
        else:
            r0 = (k - _N_FF_PIECES - 1) * FF_STREAM
            wpg_s[r0:r0 + FF_STREAM, :] = stg_d[slot].astype(bf16)

    @pl.when(t == 0)
    def _():
        for cp in copies(0):
            cp.start()
        n_tok = xs_ref.shape[0] * xs_ref.shape[1]

        def tail():
            for k in range(_N_FF_PIECES, _N_PIECES):
                land(k)

        y = _ffn_rows(xs_ref[...].reshape(n_tok, D_MODEL), ps_ref[...].reshape(n_tok, PLE_DIM),
                      gffn_ref, gple_ref, gfin_ref, *weights, chunk=FF_STREAM, final_norm=final_norm,
                      before_chunk=land, before_tail=tail)
        ys_ref[...] = y.reshape(ys_ref.shape)

    @pl.when(t > 0)
    def _():
        yp_ref[...] = _ffn_rows(xp_ref[...], pp_ref[...], gffn_ref, gple_ref, gfin_ref, *weights,
                                chunk=FF_CHUNK, final_norm=final_norm)


def _ffn_stream(xp, pp, xs, ps, g_ffn, g_ple, g_final, w_gate, w_up, w_down, w_ple, w_pg, TM, final_norm):
    T = xp.shape[0]
    f32, bf16 = jnp.float32, jnp.bfloat16
    kern = functools.partial(_ffn_stream_kernel, final_norm=final_norm)
    row_map = lambda t: (jnp.maximum(t - 1, 0), 0)
    hbm = pl.BlockSpec(memory_space=pl.ANY)
    return pl.pallas_call(
        kern,
        grid=(T // TM + 1,),
        in_specs=[
            pl.BlockSpec((TM, D_MODEL), row_map),
            pl.BlockSpec((TM, PLE_DIM), row_map),
            _const_spec(xs.shape),
            _const_spec(ps.shape),
            _const_spec((1, D_MODEL)),
            _const_spec((1, D_MODEL)),
            _const_spec((1, D_MODEL)),
            hbm, hbm, hbm, hbm, hbm,
        ],
        out_specs=[pl.BlockSpec((TM, D_MODEL), row_map),
                   pl.BlockSpec(xs.shape, lambda t: (0, 0, 0))],
        out_shape=[jax.ShapeDtypeStruct(xp.shape, f32), jax.ShapeDtypeStruct(xs.shape, f32)],
        scratch_shapes=[
            pltpu.VMEM((D_MODEL, D_FF), bf16),
            pltpu.VMEM((D_MODEL, D_FF), bf16),
            pltpu.VMEM((D_FF, D_MODEL), bf16),
            pltpu.VMEM((PLE_DIM, D_MODEL), bf16),
            pltpu.VMEM((D_MODEL, D_MODEL), bf16),
            pltpu.VMEM((2, D_MODEL, FF_STREAM), f32),
            pltpu.VMEM((2, D_MODEL, FF_STREAM), f32),
            pltpu.VMEM((2, FF_STREAM, D_MODEL), f32),
            pltpu.SemaphoreType.DMA((3, 2)),
        ],
        compiler_params=pltpu.CompilerParams(
            dimension_semantics=("arbitrary",), vmem_limit_bytes=VMEM_LIMIT),
        name="ffn",
    )(xp, pp, xs, ps, g_ffn.reshape(1, D_MODEL), g_ple.reshape(1, D_MODEL), g_final.reshape(1, D_MODEL),
      w_gate, w_up, w_down, w_ple, w_pg)


def kernel(x_prompt, x_sample, p_prompt, p_sample, state_C, state_n, state_m, state_conv,
           g_mix, w_in, b_ig, b_fg, g_mh, conv_w, w_out, g_ffn, w_gate, w_up, w_down,
           w_ple, g_ple, w_pg, g_final):
    bf16 = jnp.bfloat16
    depth = g_mix.shape[0]
    B, S, _ = x_prompt.shape
    Bs, Ss, _ = x_sample.shape
    assert Ss == SAMPLE_T and S % PROMPT_TS == 0 and Bs % SAMPLE_NSEQ == 0

    xp = x_prompt
    xs = x_sample
    outs = [[] for _ in range(8)]
    for i in range(depth):
        last = i == depth - 1
        pw, sw = _mixer_weights(g_mix[i], w_in[i], b_ig[i], b_fg[i], g_mh[i], conv_w[i], w_out[i])
        x1p, cp, n_p, mp, cvp = _mixer_prompt(xp, pw, TB=PROMPT_TS, NSUB=PROMPT_NSUB, L=PROMPT_L)
        x1s, cs, n_s, ms, cvs = _mixer_sample(
            xs, state_conv[i], state_C[i], state_n[i].reshape(Bs, 1, QK_W),
            state_m[i].reshape(Bs, 1, NH), sw, SAMPLE_NSEQ)
        xp, xs = _ffn_stream(x1p.reshape(B * S, D_MODEL), p_prompt[i].reshape(B * S, PLE_DIM), x1s,
                             p_sample[i], g_ffn[i], g_ple[i], g_final, w_gate[i], w_up[i], w_down[i],
                             w_ple[i], w_pg[i], FFN_TM, last)
        xp = xp.reshape(B, S, D_MODEL)

        new = (cp, n_p.reshape(B, NH, DK), mp[:, :NH, 0], cvp[:, SUBLANES - (CONV_W - 1):],
               cs, n_s.reshape(Bs, NH, DK), ms.reshape(Bs, NH), cvs)
        for lst, v in zip(outs, new):
            lst.append(v)

    return (xp, xs) + tuple(jnp.stack(l) for l in outs)
```

```python
import functools

import numpy as np
import jax
import jax.numpy as jnp
from jax import lax
from jax.experimental import pallas as pl
from jax.experimental.pallas import tpu as pltpu

D_MODEL = 1024
NH = 4
DK = 64
DV = 128
MLSTM_W = NH * DV
QK_W = NH * DK
CONV_CH = 512
CONV_W = 3
D_FF = 2816
FF_CHUNK = 768
FFN_TM = 512
FFN_TAIL_PIECES = 2
PLE_DIM = 256
PROMPT_L = 128
PROMPT_TS = 512
PROMPT_NSUB = 2
SAMPLE_T = 4
SAMPLE_NSEQ = 32
GATE_CAP = 15.0
EPS = 1e-6

LANES = 128
SUBLANES = 8
NEG_BIG = -1e30

VMEM_LIMIT = 56 * 1024 * 1024


def _rms(x, g):
    ms = jnp.mean(x * x, axis=-1, keepdims=True)
    return x * lax.rsqrt(ms + EPS) * g


def _bdot(a, b):
    return jnp.dot(a, b, preferred_element_type=jnp.float32)


def _log_sigmoid(x):
    return jnp.minimum(x, 0.0) - jnp.log1p(jnp.exp(-jnp.abs(x)))


def _const_spec(shape):
    nd = len(shape)
    return pl.BlockSpec(shape, lambda *_: (0,) * nd, pipeline_mode=pl.Buffered(1))


_IG0 = 2 * QK_W + MLSTM_W
_OG_IN = _IG0 + 2 * NH
PROJ_IN = _OG_IN + MLSTM_W + 3 * CONV_CH
_Q0, _K0, _V0, _OG0, _BG0, _CG0, _HC0, _GT0 = 0, 256, 512, 1024, 1536, 2048, 2560, 3072
PROJ_COLS = _GT0 + 2 * LANES
_TQ0, _TV0, _TOG0, _TG0 = 0, 256, 768, 1280
TPROJ_ROWS = _TG0 + 2 * SUBLANES
PREP_TK = 256


def _prep_weights_kernel(w_ref, wo_ref, wall_ref, wt_ref, wout_ref):
    bf16 = jnp.bfloat16
    wout_ref[...] = wo_ref[...].astype(bf16)
    wf = w_ref[...]
    tk = wf.shape[1]
    g8 = wf[_IG0:_OG_IN]
    row = lax.broadcasted_iota(jnp.int32, (SUBLANES, tk), 0)
    g_in = jnp.where(row < NH, g8, 0.0)
    g_fg = jnp.where(row < NH, pltpu.roll(g8, SUBLANES - NH, 0), 0.0)

    wt_ref[_TQ0:_TV0, :] = wf[_Q0:_K0].astype(bf16)
    wt_ref[_TV0:_TOG0, :] = wf[_V0:_OG0].astype(bf16)
    wt_ref[_TOG0:_TG0, :] = wf[_OG_IN:_OG_IN + MLSTM_W].astype(bf16)
    wt_ref[_TG0:TPROJ_ROWS, :] = jnp.concatenate([g_in, g_fg], axis=0).astype(bf16)

    wall_ref[:, 0:_OG0] = wf[0:_IG0].T.astype(bf16)
    wall_ref[:, _OG0:_GT0] = wf[_OG_IN:PROJ_IN].T.astype(bf16)
    pad = jnp.zeros((LANES - SUBLANES, tk), jnp.float32)
    wall_ref[:, _GT0:_GT0 + LANES] = jnp.concatenate([g_in, pad], axis=0).T.astype(bf16)
    wall_ref[:, _GT0 + LANES:PROJ_COLS] = jnp.concatenate([g_fg, pad], axis=0).T.astype(bf16)


def _prep_weights(w_in, w_out):
    bf16 = jnp.bfloat16
    return pl.pallas_call(
        _prep_weights_kernel,
        grid=(D_MODEL // PREP_TK,),
        in_specs=[pl.BlockSpec((PROJ_IN, PREP_TK), lambda i: (0, i)),
                  pl.BlockSpec((PREP_TK, D_MODEL), lambda i: (i, 0))],
        out_specs=[pl.BlockSpec((PREP_TK, PROJ_COLS), lambda i: (i, 0)),
                   pl.BlockSpec((TPROJ_ROWS, PREP_TK), lambda i: (0, i)),
                   pl.BlockSpec((PREP_TK, D_MODEL), lambda i: (i, 0))],
        out_shape=[jax.ShapeDtypeStruct((D_MODEL, PROJ_COLS), bf16),
                   jax.ShapeDtypeStruct((TPROJ_ROWS, D_MODEL), bf16),
                   jax.ShapeDtypeStruct((D_MODEL, D_MODEL), bf16)],
        compiler_params=pltpu.CompilerParams(
            dimension_semantics=("arbitrary",), vmem_limit_bytes=VMEM_LIMIT),
        name="prep_weights",
    )(w_in.T, w_out)


ST_ROWS = DV + 2 * SUBLANES


def _lane_scan(x, seg, op, fill):
    pos = lax.broadcasted_iota(jnp.int32, x.shape, 1) & (seg - 1)
    k = 1
    while k < seg:
        x = op(x, jnp.where(pos >= k, pltpu.roll(x, k, 1), fill))
        k *= 2
    return x


PROJ_PIECE = 256


def _run(*phases):
    live = list(phases)
    while live:
        for g in list(live):
            try:
                next(g)
            except StopIteration:
                live.remove(g)


def _mixer_prompt_kernel(x_ref, gmix_ref, wn_ref, wt_ref, gbt_ref, gmhc_ref, cw_ref, wout_ref,
                         x1_ref, c_ref, n_ref, m_ref, cvo_ref,
                         qt_s, vt_s, ogt_s, k_s, mixt_s, z_s, ucol_s, wi_s, c2_s, emt_s, wk_s, ast_s, st_s,
                         *, L, TB, NSUB):
    j = pl.program_id(1)

    @pl.when(j == 0)
    def _():
        z_s[0, 0:SUBLANES, :] = jnp.zeros((SUBLANES, CONV_CH), jnp.float32)
        st_s[...] = jnp.zeros(st_s.shape, jnp.float32)
        m_ref[...] = jnp.zeros(m_ref.shape, jnp.float32)

    m_prev = m_ref[0]
    sub = [dict(sb=sb, x_ref=x_ref, gmix_ref=gmix_ref, wn_ref=wn_ref, wt_ref=wt_ref, gbt_ref=gbt_ref,
                gmhc_ref=gmhc_ref, cw_ref=cw_ref, wout_ref=wout_ref, x1_ref=x1_ref, qt_s=qt_s.at[sb],
                vt_s=vt_s.at[sb], ogt_s=ogt_s.at[sb], k_s=k_s.at[sb], mixt_s=mixt_s.at[sb], z_s=z_s,
                ucol_s=ucol_s.at[sb], wi_s=wi_s.at[sb], c2_s=c2_s.at[sb], emt_s=emt_s.at[sb],
                wk_s=wk_s.at[sb], ast_s=ast_s.at[sb], st_s=st_s, L=L, TS=TB) for sb in range(NSUB)]
    carry = {"m": m_prev}
    _run(_prompt_project(sub[0]))
    for a, b in zip(sub[:-1], sub[1:]):
        _run(_prompt_chunks(a, carry), _prompt_project(b))
    _run(_prompt_chunks(sub[-1], carry), *[_prompt_output(s) for s in sub[:-1]])
    _run(_prompt_output(sub[-1]))
    m_ref[0] = carry["m"]
    cvo_ref[0] = z_s[NSUB - 1, TB + SUBLANES - (CONV_W - 1):TB + SUBLANES, :]
    z_s[0, 0:SUBLANES, :] = z_s[NSUB - 1, TB:TB + SUBLANES, :]

    @pl.when(j == pl.num_programs(1) - 1)
    def _():
        for pr in range(NH // 2):
            state = st_s[pr]
            c_pair = state[0:DV, :].T
            c_ref[0, 2 * pr] = c_pair[0:DK, :]
            c_ref[0, 2 * pr + 1] = c_pair[DK:, :]
            n_ref[0, pr:pr + 1, :] = state[DV:DV + 1, :]


def _prompt_project(s):
    sb, L, TS = s["sb"], s["L"], s["TS"]
    x_ref, gmix_ref, wn_ref, wt_ref, gbt_ref, cw_ref = (
        s["x_ref"], s["gmix_ref"], s["wn_ref"], s["wt_ref"], s["gbt_ref"], s["cw_ref"])
    qt_s, vt_s, ogt_s, k_s, z_s = s["qt_s"], s["vt_s"], s["ogt_s"], s["k_s"], s["z_s"]
    bf16 = jnp.bfloat16

    x = x_ref[0, sb * TS:(sb + 1) * TS, :]
    a = _rms(x, gmix_ref[...]).astype(bf16)
    nt_dims = (((1,), (1,)), ((), ()))

    gt = lax.dot_general(wt_ref[_TG0:TPROJ_ROWS, :], a, nt_dims,
                         preferred_element_type=jnp.float32) + gbt_ref[...]
    ic = GATE_CAP * jnp.tanh(gt[0:SUBLANES] / GATE_CAP)
    lf = _log_sigmoid(gt[SUBLANES:])
    b = _lane_scan(lf, L, jnp.add, 0.0)
    u = ic - b
    m_loc = b + _lane_scan(u, L, jnp.maximum, -jnp.inf)
    s.update(b=b, u=u, ic=ic, m_loc=m_loc)
    yield

    k_s[...] = _bdot(a, wn_ref[:, _K0:_V0])
    yield
    hc = _bdot(a, wn_ref[:, _HC0:_GT0])
    yield
    z = _bdot(a, wn_ref[:, _CG0:_HC0]) * hc
    if sb > 0:
        z_s[sb, 0:SUBLANES, :] = z_s[sb - 1, TS:TS + SUBLANES, :]
    z_s[sb, SUBLANES:SUBLANES + TS, :] = z
    yield
    yconv = (cw_ref[0:1, :] * z_s[sb, SUBLANES - 2:SUBLANES - 2 + TS, :]
             + cw_ref[1:2, :] * z_s[sb, SUBLANES - 1:SUBLANES - 1 + TS, :]
             + cw_ref[2:3, :] * z)
    s["yc"] = (_bdot(a, wn_ref[:, _BG0:_CG0]) * yconv).astype(bf16)
    yield

    for r0 in range(0, _TG0, PROJ_PIECE):
        ut = lax.dot_general(wt_ref[r0:r0 + PROJ_PIECE, :], a, nt_dims, preferred_element_type=jnp.float32)
        if r0 < _TV0:
            qt_s[r0:r0 + PROJ_PIECE, :] = ut * (DK ** -0.5)
        elif r0 < _TOG0:
            vt_s[r0 - _TV0:r0 - _TV0 + PROJ_PIECE, :] = ut
        else:
            ogt_s[r0 - _TOG0:r0 - _TOG0 + PROJ_PIECE, :] = ut
        yield


def _prompt_chunks(s, carry):
    L, TS = s["L"], s["TS"]
    qt_s, vt_s, ogt_s, k_s, mixt_s, ucol_s, wi_s, c2_s, emt_s, wk_s, ast_s, st_s = (
        s["qt_s"], s["vt_s"], s["ogt_s"], s["k_s"], s["mixt_s"], s["ucol_s"], s["wi_s"],
        s["c2_s"], s["emt_s"], s["wk_s"], s["ast_s"], s["st_s"])
    b, u, ic, m_loc = s["b"], s["u"], s["ic"], s["m_loc"]
    nc = TS // L
    bf16 = jnp.bfloat16
    m_prev = carry["m"]

    for c in range(nc):
        sl = slice(c * L, (c + 1) * L)
        bc, mlc = b[:, sl], m_loc[:, sl]
        b_last = jnp.broadcast_to(bc[:, L - 1:L], bc.shape)
        m_new = jnp.maximum(b_last + m_prev, jnp.broadcast_to(mlc[:, L - 1:L], bc.shape))
        g = bc + m_prev
        mt = jnp.maximum(g, mlc)
        wi_s[:, sl] = jnp.exp(g - mt)
        c2_s[:, sl] = mt - bc
        emt_s[:, sl] = jnp.exp(-mt)
        wk_s[:, sl] = jnp.exp(b_last - bc + ic[:, sl] - m_new)
        ast_s[c] = jnp.exp(b_last + m_prev - m_new)
        upad = jnp.concatenate([u[:, sl], jnp.zeros((L - SUBLANES, L), jnp.float32)], axis=0)
        ucol_s[c * L:(c + 1) * L, :] = upad.T
        m_prev = m_new
    carry["m"] = m_prev
    yield

    s_i = lax.broadcasted_iota(jnp.int32, (L, L), 0)
    t_i = lax.broadcasted_iota(jnp.int32, (L, L), 1)
    causal = s_i <= t_i
    low_half = lax.broadcasted_iota(jnp.int32, (L, LANES), 1) < DK
    zeros_q = jnp.zeros((DK, L), jnp.float32)
    zeros_p = jnp.zeros((L, L), bf16)

    def pair_row(ref, h0, rows):
        return jnp.concatenate([ref[h0:h0 + 1, rows], ref[h0 + 1:h0 + 2, rows]], axis=1)

    for c in range(nc):
        rows = slice(c * L, (c + 1) * L)
        for pr in range(NH // 2):
            h0 = 2 * pr
            hv0 = slice(h0 * DV, (h0 + 1) * DV)
            hv1 = slice((h0 + 1) * DV, (h0 + 2) * DV)
            kp = k_s[rows, pr * LANES:(pr + 1) * LANES]
            q0 = qt_s[h0 * DK:(h0 + 1) * DK, rows]
            q1 = qt_s[(h0 + 1) * DK:(h0 + 2) * DK, rows]
            qbd = jnp.concatenate([jnp.concatenate([q0, zeros_q], axis=1),
                                   jnp.concatenate([zeros_q, q1], axis=1)], axis=0).astype(bf16)
            st = _bdot(kp.astype(bf16), qbd)
            arg = jnp.concatenate(
                [jnp.where(causal, ucol_s[rows, h:h + 1] - c2_s[h:h + 1, rows], -jnp.inf)
                 for h in (h0, h0 + 1)], axis=1)
            pt = st * jnp.exp(arg)
            rs = jnp.sum(pt, axis=0, keepdims=True)
            ptb = pt.astype(bf16)
            pbd = jnp.concatenate([jnp.concatenate([ptb[:, 0:L], zeros_p], axis=1),
                                   jnp.concatenate([zeros_p, ptb[:, L:]], axis=1)], axis=0)
            vt = jnp.concatenate([vt_s[hv0, rows], vt_s[hv1, rows]], axis=1)
            state = st_s[pr]
            sq = _bdot(state.astype(bf16), qbd) * pair_row(wi_s, h0, rows)
            num = _bdot(vt.astype(bf16), pbd) + sq[0:DV]
            den = sq[DV:DV + 1] + rs
            hh = num * (1.0 / jnp.maximum(jnp.abs(den), pair_row(emt_s, h0, rows)))
            hn = hh * lax.rsqrt(jnp.mean(hh * hh, axis=0, keepdims=True) + EPS)
            mixt_s[hv0, rows] = jax.nn.sigmoid(ogt_s[hv0, rows]) * hn[:, 0:L]
            mixt_s[hv1, rows] = jax.nn.sigmoid(ogt_s[hv1, rows]) * hn[:, L:]
            wkr = pair_row(wk_s, h0, rows)
            vw = jnp.concatenate([vt * wkr, jnp.broadcast_to(wkr, (2 * SUBLANES, 2 * L))], axis=0)
            km = jnp.concatenate([jnp.where(low_half, kp, 0.0), jnp.where(low_half, 0.0, kp)], axis=0)
            decay = jnp.where(low_half[0:1, :], ast_s[c][h0:h0 + 1, :], ast_s[c][h0 + 1:h0 + 2, :])
            st_s[pr] = decay * state + _bdot(vw.astype(bf16), km.astype(bf16))
            yield


def _prompt_output(s):
    sb, TS = s["sb"], s["TS"]
    x_ref, wout_ref, x1_ref, mixt_s = s["x_ref"], s["wout_ref"], s["x1_ref"], s["mixt_s"]
    tok = slice(sb * TS, (sb + 1) * TS)
    hm = mixt_s[...].T * s["gmhc_ref"][...]
    mix = jnp.concatenate([hm.astype(jnp.bfloat16), s["yc"]], axis=1)
    yield
    for c0 in range(0, D_MODEL, PROJ_PIECE):
        cols = slice(c0, c0 + PROJ_PIECE)
        x1_ref[0, tok, cols] = x_ref[0, tok, cols] + _bdot(mix, wout_ref[:, cols])
        yield


def _mixer_prompt(x, weights, TB, NSUB, L):
    B, S, _ = x.shape
    TS = TB * NSUB
    nb = S // TS
    f32 = jnp.float32
    kern = functools.partial(_mixer_prompt_kernel, L=L, TB=TB, NSUB=NSUB)
    return pl.pallas_call(
        kern,
        grid=(B, nb),
        in_specs=[
            pl.BlockSpec((1, TS, D_MODEL), lambda b, j: (b, j, 0)),
            _const_spec((1, D_MODEL)),
            _const_spec((D_MODEL, PROJ_COLS)),
            _const_spec((TPROJ_ROWS, D_MODEL)),
            _const_spec((2 * SUBLANES, 1)),
            _const_spec((1, MLSTM_W)),
            _const_spec((CONV_W, CONV_CH)),
            _const_spec((D_MODEL, D_MODEL)),
        ],
        out_specs=[
            pl.BlockSpec((1, TS, D_MODEL), lambda b, j: (b, j, 0)),
            pl.BlockSpec((1, NH, DK, DV), lambda b, j: (b, 0, 0, 0)),
            pl.BlockSpec((1, NH // 2, 2 * DK), lambda b, j: (b, 0, 0)),
            pl.BlockSpec((1, SUBLANES, LANES), lambda b, j: (b, 0, 0)),
            pl.BlockSpec((1, CONV_W - 1, CONV_CH), lambda b, j: (b, 0, 0)),
        ],
        out_shape=[
            jax.ShapeDtypeStruct((B, S, D_MODEL), f32),
            jax.ShapeDtypeStruct((B, NH, DK, DV), f32),
            jax.ShapeDtypeStruct((B, NH // 2, 2 * DK), f32),
            jax.ShapeDtypeStruct((B, SUBLANES, LANES), f32),
            jax.ShapeDtypeStruct((B, CONV_W - 1, CONV_CH), f32),
        ],
        scratch_shapes=[
            pltpu.VMEM((NSUB, QK_W, TB), f32),
            pltpu.VMEM((NSUB, MLSTM_W, TB), f32),
            pltpu.VMEM((NSUB, MLSTM_W, TB), f32),
            pltpu.VMEM((NSUB, TB, QK_W), f32),
            pltpu.VMEM((NSUB, MLSTM_W, TB), f32),
            pltpu.VMEM((NSUB, TB + SUBLANES, CONV_CH), f32),
            pltpu.VMEM((NSUB, TB, LANES), f32),
            pltpu.VMEM((NSUB, SUBLANES, TB), f32),
            pltpu.VMEM((NSUB, SUBLANES, TB), f32),
            pltpu.VMEM((NSUB, SUBLANES, TB), f32),
            pltpu.VMEM((NSUB, SUBLANES, TB), f32),
            pltpu.VMEM((NSUB, TB // L, SUBLANES, LANES), f32),
            pltpu.VMEM((NH // 2, ST_ROWS, LANES), f32),
        ],
        compiler_params=pltpu.CompilerParams(
            dimension_semantics=("arbitrary", "arbitrary"), vmem_limit_bytes=VMEM_LIMIT),
        name="mixer_prompt",
    )(x, *weights)


def _row_scan(x, seg, op, fill):
    pos = lax.broadcasted_iota(jnp.int32, x.shape, 0) & (seg - 1)
    k = 1
    while k < seg:
        x = op(x, jnp.where(pos >= k, pltpu.roll(x, k, 0), fill))
        k *= 2
    return x


def _split_dot(x, sel, parts):
    acc = None
    rem = x
    for p in range(parts):
        hi = rem.astype(jnp.bfloat16)
        d = _bdot(hi, sel)
        acc = d if acc is None else acc + d
        if p + 1 < parts:
            rem = rem - hi.astype(jnp.float32)
    return acc


def _selectors():
    h = np.arange(NH)
    seg_qk = np.zeros((QK_W, LANES), np.float32)
    seg_qk[np.arange(QK_W), np.arange(QK_W) // DK] = 1.0
    exp_v = np.zeros((LANES, MLSTM_W), np.float32)
    exp_k = np.zeros((LANES, QK_W), np.float32)
    for i in h:
        exp_v[i, i * DV:(i + 1) * DV] = 1.0
        exp_k[i, i * DK:(i + 1) * DK] = 1.0
    mean_v = np.kron(np.eye(NH, dtype=np.float32), np.full((DV, DV), 1.0 / DV, np.float32))
    return tuple(jnp.asarray(m, jnp.bfloat16) for m in (seg_qk, exp_v, exp_k, mean_v))


def _mixer_sample_kernel(x_ref, cv_ref, c0_ref, n0_ref, m0_ref,
                         gmix_ref, win_ref, gb_ref, gmh_ref, cw_ref, wout_ref,
                         segqk_ref, expv_ref, expk_ref, meanv_ref,
                         x1_ref, cst_ref, nst_ref, mst_ref, cvo_ref,
                         xp_s, z_s, mp_s, qc_s, *, NSEQ):
    R = SUBLANES
    TS = NSEQ * R
    bf16 = jnp.bfloat16
    f32 = jnp.float32

    xp_s[:, SAMPLE_T:, :] = jnp.zeros((NSEQ, R - SAMPLE_T, D_MODEL), f32)
    xp_s[:, 0:SAMPLE_T, :] = x_ref[...]
    x = xp_s[...].reshape(TS, D_MODEL)
    a = _rms(x, gmix_ref[...]).astype(bf16)
    u = _bdot(a, win_ref[...])

    pos = lax.broadcasted_iota(jnp.int32, (TS, 1), 0) & (R - 1)
    real = pos < SAMPLE_T

    z = u[:, _CG0:_HC0] * u[:, _HC0:_GT0]
    z3 = z.reshape(NSEQ, R, CONV_CH)
    z_s[...] = z3
    z_s[:, R - (CONV_W - 1):, :] = cv_ref[...]
    zf = z_s[...].reshape(TS, CONV_CH)
    zm1 = jnp.where(pos >= 1, pltpu.roll(z, 1, 0), pltpu.roll(zf, TS - (R - 1), 0))
    zm2 = jnp.where(pos >= 2, pltpu.roll(z, 2, 0), pltpu.roll(zf, TS - (R - 2), 0))
    yc = u[:, _BG0:_CG0] * (cw_ref[0:1, :] * zm2 + cw_ref[1:2, :] * zm1 + cw_ref[2:3, :] * z)
    cvo_ref[...] = z3[:, SAMPLE_T - (CONV_W - 1):SAMPLE_T, :]

    gates = u[:, _GT0:PROJ_COLS] + gb_ref[...]
    ic = jnp.where(real, GATE_CAP * jnp.tanh(gates[:, :LANES] / GATE_CAP), NEG_BIG)
    lf = jnp.where(real, _log_sigmoid(gates[:, LANES:]), 0.0)
    b = _row_scan(lf, R, jnp.add, 0.0)
    uu = ic - b
    m_loc = b + _row_scan(uu, R, jnp.maximum, -jnp.inf)

    mp_s[...] = jnp.zeros(mp_s.shape, f32)
    mp_s[:, :, 0:NH] = m0_ref[...]
    m_prev = mp_s[...]
    b3 = b.reshape(NSEQ, R, LANES)
    ml3 = m_loc.reshape(NSEQ, R, LANES)
    b_last = b3[:, R - 1:R, :]
    m_new = jnp.maximum(b_last + m_prev, ml3[:, R - 1:R, :])
    g3 = b3 + m_prev
    mt3 = jnp.maximum(g3, ml3)
    wi = jnp.exp(g3 - mt3).reshape(TS, LANES)
    c2 = (mt3 - b3).reshape(TS, LANES)
    emt = jnp.exp(-mt3).reshape(TS, LANES)
    wk = jnp.exp(b_last - b3 + ic.reshape(NSEQ, R, LANES) - m_new).reshape(TS, LANES)
    a_st = jnp.broadcast_to(jnp.exp(b_last + m_prev - m_new), (NSEQ, R, LANES)).reshape(TS, LANES)
    mst_ref[...] = m_new[:, :, 0:NH]

    q = u[:, _Q0:_K0] * (DK ** -0.5)
    k = u[:, _K0:_V0]
    v = u[:, _V0:_OG0]
    rs = jnp.zeros((TS, LANES), f32)
    num = jnp.zeros((TS, MLSTM_W), f32)
    for d in range(SAMPLE_T):
        kd = k if d == 0 else pltpu.roll(k, d, 0)
        vd = v if d == 0 else pltpu.roll(v, d, 0)
        ud = uu if d == 0 else pltpu.roll(uu, d, 0)
        p = _split_dot(q * kd, segqk_ref[...], 2) * jnp.exp(ud - c2)
        rs = rs + p
        num = num + _split_dot(p, expv_ref[...], 2) * vd

    n0x = jnp.broadcast_to(n0_ref[...], (NSEQ, R, QK_W)).reshape(TS, QK_W)
    qn = _split_dot(q * n0x, segqk_ref[...], 2)
    den = wi * qn + rs
    rden = 1.0 / jnp.maximum(jnp.abs(den), emt)
    kw = k * _split_dot(wk, expk_ref[...], 2)
    ax = _split_dot(a_st, expv_ref[...], 3)
    lane_q = lax.broadcasted_iota(jnp.int32, (R, QK_W), 1)
    for i in range(NSEQ):
        rows = slice(i * R, (i + 1) * R)
        qi = q[rows, :]
        lhs = jnp.concatenate(
            [jnp.where((lane_q >= h * DK) & (lane_q < (h + 1) * DK), qi, 0.0) for h in range(NH)], axis=0)
        c0 = c0_ref[i]
        qc = _bdot(lhs.astype(bf16), c0.reshape(QK_W, DV).astype(bf16))
        qc_s[rows, :] = jnp.concatenate([qc[h * R:(h + 1) * R, :] for h in range(NH)], axis=1)
        kwi = kw[rows, :].astype(bf16)
        vi = v[rows, :].astype(bf16)
        for h in range(NH):
            dc = lax.dot_general(kwi[:, h * DK:(h + 1) * DK], vi[:, h * DV:(h + 1) * DV],
                                 (((0,), (0,)), ((), ())), preferred_element_type=f32)
            cst_ref[i, h] = ax[i * R:i * R + 1, h * DV:(h + 1) * DV] * c0[h] + dc

    hh = (_split_dot(wi, expv_ref[...], 2) * qc_s[...] + num) * _split_dot(rden, expv_ref[...], 2)
    ms = _split_dot(hh * hh, meanv_ref[...], 2)
    hm = jax.nn.sigmoid(u[:, _OG0:_BG0]) * (hh * lax.rsqrt(ms + EPS) * gmh_ref[...])
    mix = jnp.concatenate([hm, yc], axis=1).astype(bf16)
    out = x + _bdot(mix, wout_ref[...])
    x1_ref[...] = out.reshape(NSEQ, R, D_MODEL)[:, 0:SAMPLE_T, :]

    a_k = _split_dot(a_st, expk_ref[...], 3).reshape(NSEQ, R, QK_W)[:, 0:1, :]
    nst_ref[...] = a_k * n0_ref[...] + jnp.sum(kw.reshape(NSEQ, R, QK_W), axis=1, keepdims=True)


def _mixer_sample(x, cv, c0, n0, m0, weights, NSEQ):
    Bs = x.shape[0]
    f32 = jnp.float32
    kern = functools.partial(_mixer_sample_kernel, NSEQ=NSEQ)
    x_spec = pl.BlockSpec((NSEQ, SAMPLE_T, D_MODEL), lambda t: (t, 0, 0))
    cv_spec = pl.BlockSpec((NSEQ, CONV_W - 1, CONV_CH), lambda t: (t, 0, 0))
    c_spec = pl.BlockSpec((NSEQ, NH, DK, DV), lambda t: (t, 0, 0, 0))
    n_spec = pl.BlockSpec((NSEQ, 1, QK_W), lambda t: (t, 0, 0))
    m_spec = pl.BlockSpec((NSEQ, 1, NH), lambda t: (t, 0, 0))
    return pl.pallas_call(
        kern,
        grid=(Bs // NSEQ,),
        in_specs=[
            x_spec, cv_spec, c_spec, n_spec, m_spec,
            _const_spec((1, D_MODEL)),
            _const_spec((D_MODEL, PROJ_COLS)),
            _const_spec((1, 2 * LANES)),
            _const_spec((1, MLSTM_W)),
            _const_spec((CONV_W, CONV_CH)),
            _const_spec((D_MODEL, D_MODEL)),
            _const_spec((QK_W, LANES)),
            _const_spec((LANES, MLSTM_W)),
            _const_spec((LANES, QK_W)),
            _const_spec((MLSTM_W, MLSTM_W)),
        ],
        out_specs=[x_spec, c_spec, n_spec, m_spec, cv_spec],
        out_shape=[
            jax.ShapeDtypeStruct((Bs, SAMPLE_T, D_MODEL), f32),
            jax.ShapeDtypeStruct((Bs, NH, DK, DV), f32),
            jax.ShapeDtypeStruct((Bs, 1, QK_W), f32),
            jax.ShapeDtypeStruct((Bs, 1, NH), f32),
            jax.ShapeDtypeStruct((Bs, CONV_W - 1, CONV_CH), f32),
        ],
        scratch_shapes=[
            pltpu.VMEM((NSEQ, SUBLANES, D_MODEL), f32),
            pltpu.VMEM((NSEQ, SUBLANES, CONV_CH), f32),
            pltpu.VMEM((NSEQ, 1, LANES), f32),
            pltpu.VMEM((NSEQ * SUBLANES, MLSTM_W), f32),
        ],
        compiler_params=pltpu.CompilerParams(
            dimension_semantics=("arbitrary",), vmem_limit_bytes=VMEM_LIMIT),
        name="mixer_sample",
    )(x, cv, c0, n0, m0, *weights)


def _mixer_weights(g_mix, w_in, b_ig, b_fg, g_mh, conv_w, w_out):
    f32 = jnp.float32
    w_all, wt, wout = _prep_weights(w_in, w_out)
    gmix = g_mix.reshape(1, D_MODEL)
    zeros = jnp.zeros((SUBLANES - NH,), f32)
    gb_col = jnp.concatenate([b_ig, zeros, b_fg, zeros]).reshape(2 * SUBLANES, 1)
    zeros = jnp.zeros((LANES - NH,), f32)
    gb_row = jnp.concatenate([b_ig, zeros, b_fg, zeros]).reshape(1, 2 * LANES)
    gmh = g_mh.reshape(1, MLSTM_W)
    prompt = (gmix, w_all, wt, gb_col, gmh, conv_w, wout)
    sample = (gmix, w_all, gb_row, gmh, conv_w, wout) + _selectors()
    return prompt, sample


FF_STREAM = 256
_N_FF_PIECES = D_FF // FF_STREAM
_N_PIECES = _N_FF_PIECES + PLE_DIM // FF_STREAM + D_MODEL // FF_STREAM


def _ffn_rows(x, pe, gffn_ref, gple_ref, gfin_ref, wg_s, wu_s, wd_s, wple_s, wpg_s, *, chunk, final_norm,
              before_chunk=None, before_tail=None):
    bf16 = jnp.bfloat16
    f = _rms(x, gffn_ref[...]).astype(bf16)
    e = None
    if before_tail is None:
        e = _rms(_bdot(pe.astype(bf16), wple_s[...]), gple_ref[...])
    for c0 in range(0, D_FF, chunk):
        c1 = min(c0 + chunk, D_FF)
        if before_chunk is not None:
            before_chunk(c0 // chunk)
        gate = _bdot(f, wg_s[:, c0:c1])
        up = _bdot(f, wu_s[:, c0:c1])
        hmid = (gate * jax.nn.sigmoid(gate) * up).astype(bf16)
        x = x + _bdot(hmid, wd_s[c0:c1, :])
    if before_tail is not None:
        before_tail()
        e = _rms(_bdot(pe.astype(bf16), wple_s[...]), gple_ref[...])
    xb = x.astype(bf16)
    n_rows = x.shape[0]
    outs = []
    for r0 in range(0, n_rows, n_rows // FFN_TAIL_PIECES):
        rows = slice(r0, r0 + n_rows // FFN_TAIL_PIECES)
        xr = x[rows] + jax.nn.sigmoid(_bdot(xb[rows], wpg_s[...])) * e[rows]
        outs.append(_rms(xr, gfin_ref[...]) if final_norm else xr)
    return jnp.concatenate(outs, axis=0)


def _ffn_stream_kernel(xp_ref, pp_ref, xs_ref, ps_ref, gffn_ref, gple_ref, gfin_ref,
                       wg_hbm, wu_hbm, wd_hbm, wple_hbm, wpg_hbm,
                       yp_ref, ys_ref,
                       wg_s, wu_s, wd_s, wple_s, wpg_s, stg_g, stg_u, stg_d, sem, *, final_norm):
    t = pl.program_id(0)
    bf16 = jnp.bfloat16
    weights = (wg_s, wu_s, wd_s, wple_s, wpg_s)

    def copies(k):
        slot = k % 2
        if k < _N_FF_PIECES:
            cols = pl.ds(k * FF_STREAM, FF_STREAM)
            return [pltpu.make_async_copy(wg_hbm.at[:, cols], stg_g.at[slot], sem.at[0, slot]),
                    pltpu.make_async_copy(wu_hbm.at[:, cols], stg_u.at[slot], sem.at[1, slot]),
                    pltpu.make_async_copy(wd_hbm.at[cols, :], stg_d.at[slot], sem.at[2, slot])]
        if k == _N_FF_PIECES:
            return [pltpu.make_async_copy(wple_hbm, stg_d.at[slot], sem.at[2, slot])]
        rows = pl.ds((k - _N_FF_PIECES - 1) * FF_STREAM, FF_STREAM)
        return [pltpu.make_async_copy(wpg_hbm.at[rows, :], stg_d.at[slot], sem.at[2, slot])]

    def land(k):
        if k + 1 < _N_PIECES:
            for cp in copies(k + 1):
                cp.start()
        for cp in copies(k):
            cp.wait()
        slot = k % 2
        if k < _N_FF_PIECES:
            cols = slice(k * FF_STREAM, (k + 1) * FF_STREAM)
            wg_s[:, cols] = stg_g[slot].astype(bf16)
            wu_s[:, cols] = stg_u[slot].astype(bf16)
            wd_s[cols, :] = stg_d[slot].astype(bf16)
        elif k == _N_FF_PIECES:
            wple_s[...] = stg_d[slot].astype(bf16)
        else:
            r0 = (k - _N_FF_PIECES - 1) * FF_STREAM
            wpg_s[r0:r0 + FF_STREAM, :] = stg_d[slot].astype(bf16)

    @pl.when(t == 0)
    def _():
        for cp in copies(0):
            cp.start()
        n_tok = xs_ref.shape[0] * xs_ref.shape[1]

        def tail():
            for k in range(_N_FF_PIECES, _N_PIECES):
                land(k)

        y = _ffn_rows(xs_ref[...].reshape(n_tok, D_MODEL), ps_ref[...].reshape(n_tok, PLE_DIM),
                      gffn_ref, gple_ref, gfin_ref, *weights, chunk=FF_STREAM, final_norm=final_norm,
                      before_chunk=land, before_tail=tail)
        ys_ref[...] = y.reshape(ys_ref.shape)

    @pl.when(t > 0)
    def _():
        yp_ref[...] = _ffn_rows(xp_ref[...], pp_ref[...], gffn_ref, gple_ref, gfin_ref, *weights,
                                chunk=FF_CHUNK, final_norm=final_norm)


def _ffn_stream(xp, pp, xs, ps, g_ffn, g_ple, g_final, w_gate, w_up, w_down, w_ple, w_pg, TM, final_norm):
    T = xp.shape[0]
    assert T % TM == 0 and D_FF % FF_STREAM == 0 and D_MODEL % FF_STREAM == 0
    assert w_ple.shape == (FF_STREAM, D_MODEL)
    f32, bf16 = jnp.float32, jnp.bfloat16
    kern = functools.partial(_ffn_stream_kernel, final_norm=final_norm)
    row_map = lambda t: (jnp.maximum(t - 1, 0), 0)
    hbm = pl.BlockSpec(memory_space=pl.ANY)
    return pl.pallas_call(
        kern,
        grid=(T // TM + 1,),
        in_specs=[
            pl.BlockSpec((TM, D_MODEL), row_map),
            pl.BlockSpec((TM, PLE_DIM), row_map),
            _const_spec(xs.shape),
            _const_spec(ps.shape),
            _const_spec((1, D_MODEL)),
            _const_spec((1, D_MODEL)),
            _const_spec((1, D_MODEL)),
            hbm, hbm, hbm, hbm, hbm,
        ],
        out_specs=[pl.BlockSpec((TM, D_MODEL), row_map),
                   pl.BlockSpec(xs.shape, lambda t: (0, 0, 0))],
        out_shape=[jax.ShapeDtypeStruct(xp.shape, f32), jax.ShapeDtypeStruct(xs.shape, f32)],
        scratch_shapes=[
            pltpu.VMEM((D_MODEL, D_FF), bf16),
            pltpu.VMEM((D_MODEL, D_FF), bf16),
            pltpu.VMEM((D_FF, D_MODEL), bf16),
            pltpu.VMEM((PLE_DIM, D_MODEL), bf16),
            pltpu.VMEM((D_MODEL, D_MODEL), bf16),
            pltpu.VMEM((2, D_MODEL, FF_STREAM), f32),
            pltpu.VMEM((2, D_MODEL, FF_STREAM), f32),
            pltpu.VMEM((2, FF_STREAM, D_MODEL), f32),
            pltpu.SemaphoreType.DMA((3, 2)),
        ],
        compiler_params=pltpu.CompilerParams(
            dimension_semantics=("arbitrary",), vmem_limit_bytes=VMEM_LIMIT),
        name="ffn",
    )(xp, pp, xs, ps, g_ffn.reshape(1, D_MODEL), g_ple.reshape(1, D_MODEL), g_final.reshape(1, D_MODEL),
      w_gate, w_up, w_down, w_ple, w_pg)


def kernel(x_prompt, x_sample, p_prompt, p_sample, state_C, state_n, state_m, state_conv,
           g_mix, w_in, b_ig, b_fg, g_mh, conv_w, w_out, g_ffn, w_gate, w_up, w_down,
           w_ple, g_ple, w_pg, g_final):
    bf16 = jnp.bfloat16
    depth = g_mix.shape[0]
    B, S, _ = x_prompt.shape
    Bs, Ss, _ = x_sample.shape
    assert Ss == SAMPLE_T and S % PROMPT_TS == 0 and Bs % SAMPLE_NSEQ == 0

    xp = x_prompt
    xs = x_sample
    outs = [[] for _ in range(8)]
    for i in range(depth):
        last = i == depth - 1
        pw, sw = _mixer_weights(g_mix[i], w_in[i], b_ig[i], b_fg[i], g_mh[i], conv_w[i], w_out[i])
        x1p, cp, n_p, mp, cvp = _mixer_prompt(xp, pw, TB=PROMPT_TS, NSUB=PROMPT_NSUB, L=PROMPT_L)
        x1s, cs, n_s, ms, cvs = _mixer_sample(
            xs, state_conv[i], state_C[i], state_n[i].reshape(Bs, 1, QK_W),
            state_m[i].reshape(Bs, 1, NH), sw, SAMPLE_NSEQ)
        xp, xs = _ffn_stream(x1p.reshape(B * S, D_MODEL), p_prompt[i].reshape(B * S, PLE_DIM), x1s,
                             p_sample[i], g_ffn[i], g_ple[i], g_final, w_gate[i], w_up[i], w_down[i],
                             w_ple[i], w_pg[i], FFN_TM, last)
        xp = xp.reshape(B, S, D_MODEL)

        new = (cp, n_p.reshape(B, NH, DK), mp[:, :NH, 0], cvp,
               cs, n_s.reshape(Bs, NH, DK), ms.reshape(Bs, NH), cvs)
        for lst, v in zip(outs, new):
            lst.append(v)

    return (xp, xs) + tuple(jnp.stack(l) for l in outs)
```

```python
import functools

import numpy as np
import jax
import jax.numpy as jnp
from jax import lax
from jax.experimental import pallas as pl
from jax.experimental.pallas import tpu as pltpu

D_MODEL = 1024
NH = 4
DK = 64
DV = 128
MLSTM_W = NH * DV
QK_W = NH * DK
CONV_CH = 512
CONV_W = 3
D_FF = 2816
FF_CHUNK = 768
FFN_TM = 512
FFN_TAIL_PIECES = 2
PLE_DIM = 256
PROMPT_L = 128
PROMPT_TS = 512
PROMPT_NSUB = 2
SAMPLE_T = 4
SAMPLE_NSEQ = 32
GATE_CAP = 15.0
EPS = 1e-6

LANES = 128
SUBLANES = 8
NEG_BIG = -1e30

VMEM_LIMIT = 56 * 1024 * 1024


def _rms(x, g):
    ms = jnp.mean(x * x, axis=-1, keepdims=True)
    return x * lax.rsqrt(ms + EPS) * g


def _bdot(a, b):
    return jnp.dot(a, b, preferred_element_type=jnp.float32)


def _log_sigmoid(x):
    return jnp.minimum(x, 0.0) - jnp.log1p(jnp.exp(-jnp.abs(x)))


def _const_spec(shape):
    nd = len(shape)
    return pl.BlockSpec(shape, lambda *_: (0,) * nd, pipeline_mode=pl.Buffered(1))


_IG0 = 2 * QK_W + MLSTM_W
_OG_IN = _IG0 + 2 * NH
PROJ_IN = _OG_IN + MLSTM_W + 3 * CONV_CH
_Q0, _K0, _V0, _OG0, _BG0, _CG0, _HC0, _GT0 = 0, 256, 512, 1024, 1536, 2048, 2560, 3072
PROJ_COLS = _GT0 + 2 * LANES
_TQ0, _TV0, _TOG0, _TG0 = 0, 256, 768, 1280
TPROJ_ROWS = _TG0 + 2 * SUBLANES
PREP_TK = 256


def _prep_weights_kernel(w_ref, wo_ref, wall_ref, wt_ref, wout_ref):
    bf16 = jnp.bfloat16
    wout_ref[...] = wo_ref[...].astype(bf16)
    wf = w_ref[...]
    tk = wf.shape[1]
    g8 = wf[_IG0:_OG_IN]
    row = lax.broadcasted_iota(jnp.int32, (SUBLANES, tk), 0)
    g_in = jnp.where(row < NH, g8, 0.0)
    g_fg = jnp.where(row < NH, pltpu.roll(g8, SUBLANES - NH, 0), 0.0)

    wt_ref[_TQ0:_TV0, :] = wf[_Q0:_K0].astype(bf16)
    wt_ref[_TV0:_TOG0, :] = wf[_V0:_OG0].astype(bf16)
    wt_ref[_TOG0:_TG0, :] = wf[_OG_IN:_OG_IN + MLSTM_W].astype(bf16)
    wt_ref[_TG0:TPROJ_ROWS, :] = jnp.concatenate([g_in, g_fg], axis=0).astype(bf16)

    wall_ref[:, 0:_OG0] = wf[0:_IG0].T.astype(bf16)
    wall_ref[:, _OG0:_GT0] = wf[_OG_IN:PROJ_IN].T.astype(bf16)
    pad = jnp.zeros((LANES - SUBLANES, tk), jnp.float32)
    wall_ref[:, _GT0:_GT0 + LANES] = jnp.concatenate([g_in, pad], axis=0).T.astype(bf16)
    wall_ref[:, _GT0 + LANES:PROJ_COLS] = jnp.concatenate([g_fg, pad], axis=0).T.astype(bf16)


def _prep_weights(w_in, w_out):
    bf16 = jnp.bfloat16
    return pl.pallas_call(
        _prep_weights_kernel,
        grid=(D_MODEL // PREP_TK,),
        in_specs=[pl.BlockSpec((PROJ_IN, PREP_TK), lambda i: (0, i)),
                  pl.BlockSpec((PREP_TK, D_MODEL), lambda i: (i, 0))],
        out_specs=[pl.BlockSpec((PREP_TK, PROJ_COLS), lambda i: (i, 0)),
                   pl.BlockSpec((TPROJ_ROWS, PREP_TK), lambda i: (0, i)),
                   pl.BlockSpec((PREP_TK, D_MODEL), lambda i: (i, 0))],
        out_shape=[jax.ShapeDtypeStruct((D_MODEL, PROJ_COLS), bf16),
                   jax.ShapeDtypeStruct((TPROJ_ROWS, D_MODEL), bf16),
                   jax.ShapeDtypeStruct((D_MODEL, D_MODEL), bf16)],
        compiler_params=pltpu.CompilerParams(
            dimension_semantics=("arbitrary",), vmem_limit_bytes=VMEM_LIMIT),
        name="prep_weights",
    )(w_in.T, w_out)


ST_ROWS = DV + 2 * SUBLANES


def _lane_scan(x, seg, op, fill):
    pos = lax.broadcasted_iota(jnp.int32, x.shape, 1) & (seg - 1)
    k = 1
    while k < seg:
        x = op(x, jnp.where(pos >= k, pltpu.roll(x, k, 1), fill))
        k *= 2
    return x


PROJ_PIECE = 256


def _run(*phases):
    live = list(phases)
    while live:
        for g in list(live):
            try:
                next(g)
            except StopIteration:
                live.remove(g)


def _mixer_prompt_kernel(x_ref, gmix_ref, wn_ref, wt_ref, big_ref, bfg_ref, gmhc_ref, cw_ref, wout_ref,
                         x1_ref, c_ref, n_ref, m_ref, cvo_ref,
                         qt_s, vt_s, ogt_s, k_s, mixt_s, z_s, ucol_s, wi_s, c2_s, emt_s, wk_s, ast_s, st_s,
                         pad_s, *, L, TB, NSUB):
    j = pl.program_id(1)

    on_diag = (lax.broadcasted_iota(jnp.int32, (SUBLANES, LANES), 0)
               == lax.broadcasted_iota(jnp.int32, (SUBLANES, LANES), 1))

    def bias_col(b_ref):
        pad_s[...] = jnp.zeros(pad_s.shape, jnp.float32)
        pad_s[:, 0:NH] = b_ref[...]
        tile = jnp.broadcast_to(pad_s[...], (SUBLANES, LANES))
        return jnp.sum(jnp.where(on_diag, tile, 0.0), axis=1, keepdims=True)

    gate_bias = jnp.concatenate([bias_col(big_ref), bias_col(bfg_ref)], axis=0)

    @pl.when(j == 0)
    def _():
        z_s[0, 0:SUBLANES, :] = jnp.zeros((SUBLANES, CONV_CH), jnp.float32)
        st_s[...] = jnp.zeros(st_s.shape, jnp.float32)
        m_ref[...] = jnp.zeros(m_ref.shape, jnp.float32)

    m_prev = m_ref[0]
    sub = [dict(sb=sb, x_ref=x_ref, gmix_ref=gmix_ref, wn_ref=wn_ref, wt_ref=wt_ref, gate_bias=gate_bias,
                gmhc_ref=gmhc_ref, cw_ref=cw_ref, wout_ref=wout_ref, x1_ref=x1_ref, qt_s=qt_s.at[sb],
                vt_s=vt_s.at[sb], ogt_s=ogt_s.at[sb], k_s=k_s.at[sb], mixt_s=mixt_s.at[sb], z_s=z_s,
                ucol_s=ucol_s.at[sb], wi_s=wi_s.at[sb], c2_s=c2_s.at[sb], emt_s=emt_s.at[sb],
                wk_s=wk_s.at[sb], ast_s=ast_s.at[sb], st_s=st_s, L=L, TS=TB) for sb in range(NSUB)]
    carry = {"m": m_prev}
    _run(_prompt_project(sub[0]))
    for a, b in zip(sub[:-1], sub[1:]):
        _run(_prompt_chunks(a, carry), _prompt_project(b))
    _run(_prompt_chunks(sub[-1], carry), *[_prompt_output(s) for s in sub[:-1]])
    _run(_prompt_output(sub[-1]))
    m_ref[0] = carry["m"]
    cvo_ref[0] = z_s[NSUB - 1, TB + SUBLANES - (CONV_W - 1):TB + SUBLANES, :]
    z_s[0, 0:SUBLANES, :] = z_s[NSUB - 1, TB:TB + SUBLANES, :]

    @pl.when(j == pl.num_programs(1) - 1)
    def _():
        for pr in range(NH // 2):
            state = st_s[pr]
            c_pair = state[0:DV, :].T
            c_ref[0, 2 * pr] = c_pair[0:DK, :]
            c_ref[0, 2 * pr + 1] = c_pair[DK:, :]
            n_ref[0, pr:pr + 1, :] = state[DV:DV + 1, :]


def _prompt_project(s):
    sb, L, TS = s["sb"], s["L"], s["TS"]
    x_ref, gmix_ref, wn_ref, wt_ref, cw_ref = (
        s["x_ref"], s["gmix_ref"], s["wn_ref"], s["wt_ref"], s["cw_ref"])
    qt_s, vt_s, ogt_s, k_s, z_s = s["qt_s"], s["vt_s"], s["ogt_s"], s["k_s"], s["z_s"]
    bf16 = jnp.bfloat16

    x = x_ref[0, sb * TS:(sb + 1) * TS, :]
    a = _rms(x, gmix_ref[...]).astype(bf16)
    nt_dims = (((1,), (1,)), ((), ()))

    gt = lax.dot_general(wt_ref[_TG0:TPROJ_ROWS, :], a, nt_dims,
                         preferred_element_type=jnp.float32) + s["gate_bias"]
    ic = GATE_CAP * jnp.tanh(gt[0:SUBLANES] / GATE_CAP)
    lf = _log_sigmoid(gt[SUBLANES:])
    b = _lane_scan(lf, L, jnp.add, 0.0)
    u = ic - b
    m_loc = b + _lane_scan(u, L, jnp.maximum, -jnp.inf)
    s.update(b=b, u=u, ic=ic, m_loc=m_loc)
    yield

    k_s[...] = _bdot(a, wn_ref[:, _K0:_V0])
    yield
    hc = _bdot(a, wn_ref[:, _HC0:_GT0])
    yield
    z = _bdot(a, wn_ref[:, _CG0:_HC0]) * hc
    if sb > 0:
        z_s[sb, 0:SUBLANES, :] = z_s[sb - 1, TS:TS + SUBLANES, :]
    z_s[sb, SUBLANES:SUBLANES + TS, :] = z
    yield
    yconv = (cw_ref[0:1, :] * z_s[sb, SUBLANES - 2:SUBLANES - 2 + TS, :]
             + cw_ref[1:2, :] * z_s[sb, SUBLANES - 1:SUBLANES - 1 + TS, :]
             + cw_ref[2:3, :] * z)
    s["yc"] = (_bdot(a, wn_ref[:, _BG0:_CG0]) * yconv).astype(bf16)
    yield

    for r0 in range(0, _TG0, PROJ_PIECE):
        ut = lax.dot_general(wt_ref[r0:r0 + PROJ_PIECE, :], a, nt_dims, preferred_element_type=jnp.float32)
        if r0 < _TV0:
            qt_s[r0:r0 + PROJ_PIECE, :] = ut * (DK ** -0.5)
        elif r0 < _TOG0:
            vt_s[r0 - _TV0:r0 - _TV0 + PROJ_PIECE, :] = ut
        else:
            ogt_s[r0 - _TOG0:r0 - _TOG0 + PROJ_PIECE, :] = ut
        yield


def _prompt_chunks(s, carry):
    L, TS = s["L"], s["TS"]
    qt_s, vt_s, ogt_s, k_s, mixt_s, ucol_s, wi_s, c2_s, emt_s, wk_s, ast_s, st_s = (
        s["qt_s"], s["vt_s"], s["ogt_s"], s["k_s"], s["mixt_s"], s["ucol_s"], s["wi_s"],
        s["c2_s"], s["emt_s"], s["wk_s"], s["ast_s"], s["st_s"])
    b, u, ic, m_loc = s["b"], s["u"], s["ic"], s["m_loc"]
    nc = TS // L
    bf16 = jnp.bfloat16
    m_prev = carry["m"]

    for c in range(nc):
        sl = slice(c * L, (c + 1) * L)
        bc, mlc = b[:, sl], m_loc[:, sl]
        b_last = jnp.broadcast_to(bc[:, L - 1:L], bc.shape)
        m_new = jnp.maximum(b_last + m_prev, jnp.broadcast_to(mlc[:, L - 1:L], bc.shape))
        g = bc + m_prev
        mt = jnp.maximum(g, mlc)
        wi_s[:, sl] = jnp.exp(g - mt)
        c2_s[:, sl] = mt - bc
        emt_s[:, sl] = jnp.exp(-mt)
        wk_s[:, sl] = jnp.exp(b_last - bc + ic[:, sl] - m_new)
        ast_s[c] = jnp.exp(b_last + m_prev - m_new)
        upad = jnp.concatenate([u[:, sl], jnp.zeros((L - SUBLANES, L), jnp.float32)], axis=0)
        ucol_s[c * L:(c + 1) * L, :] = upad.T
        m_prev = m_new
    carry["m"] = m_prev
    yield

    s_i = lax.broadcasted_iota(jnp.int32, (L, L), 0)
    t_i = lax.broadcasted_iota(jnp.int32, (L, L), 1)
    causal = s_i <= t_i
    low_half = lax.broadcasted_iota(jnp.int32, (L, LANES), 1) < DK
    zeros_q = jnp.zeros((DK, L), jnp.float32)
    zeros_p = jnp.zeros((L, L), bf16)

    def pair_row(ref, h0, rows):
        return jnp.concatenate([ref[h0:h0 + 1, rows], ref[h0 + 1:h0 + 2, rows]], axis=1)

    for c in range(nc):
        rows = slice(c * L, (c + 1) * L)
        for pr in range(NH // 2):
            h0 = 2 * pr
            hv0 = slice(h0 * DV, (h0 + 1) * DV)
            hv1 = slice((h0 + 1) * DV, (h0 + 2) * DV)
            kp = k_s[rows, pr * LANES:(pr + 1) * LANES]
            q0 = qt_s[h0 * DK:(h0 + 1) * DK, rows]
            q1 = qt_s[(h0 + 1) * DK:(h0 + 2) * DK, rows]
            qbd = jnp.concatenate([jnp.concatenate([q0, zeros_q], axis=1),
                                   jnp.concatenate([zeros_q, q1], axis=1)], axis=0).astype(bf16)
            st = _bdot(kp.astype(bf16), qbd)
            arg = jnp.concatenate(
                [jnp.where(causal, ucol_s[rows, h:h + 1] - c2_s[h:h + 1, rows], -jnp.inf)
                 for h in (h0, h0 + 1)], axis=1)
            pt = st * jnp.exp(arg)
            rs = jnp.sum(pt, axis=0, keepdims=True)
            ptb = pt.astype(bf16)
            pbd = jnp.concatenate([jnp.concatenate([ptb[:, 0:L], zeros_p], axis=1),
                                   jnp.concatenate([zeros_p, ptb[:, L:]], axis=1)], axis=0)
            vt = jnp.concatenate([vt_s[hv0, rows], vt_s[hv1, rows]], axis=1)
            state = st_s[pr]
            sq = _bdot(state.astype(bf16), qbd) * pair_row(wi_s, h0, rows)
            num = _bdot(vt.astype(bf16), pbd) + sq[0:DV]
            den = sq[DV:DV + 1] + rs
            hh = num * (1.0 / jnp.maximum(jnp.abs(den), pair_row(emt_s, h0, rows)))
            hn = hh * lax.rsqrt(jnp.mean(hh * hh, axis=0, keepdims=True) + EPS)
            mixt_s[hv0, rows] = jax.nn.sigmoid(ogt_s[hv0, rows]) * hn[:, 0:L]
            mixt_s[hv1, rows] = jax.nn.sigmoid(ogt_s[hv1, rows]) * hn[:, L:]
            wkr = pair_row(wk_s, h0, rows)
            vw = jnp.concatenate([vt * wkr, jnp.broadcast_to(wkr, (2 * SUBLANES, 2 * L))], axis=0)
            km = jnp.concatenate([jnp.where(low_half, kp, 0.0), jnp.where(low_half, 0.0, kp)], axis=0)
            decay = jnp.where(low_half[0:1, :], ast_s[c][h0:h0 + 1, :], ast_s[c][h0 + 1:h0 + 2, :])
            st_s[pr] = decay * state + _bdot(vw.astype(bf16), km.astype(bf16))
            yield


def _prompt_output(s):
    sb, TS = s["sb"], s["TS"]
    x_ref, wout_ref, x1_ref, mixt_s = s["x_ref"], s["wout_ref"], s["x1_ref"], s["mixt_s"]
    tok = slice(sb * TS, (sb + 1) * TS)
    hm = mixt_s[...].T * s["gmhc_ref"][...]
    mix = jnp.concatenate([hm.astype(jnp.bfloat16), s["yc"]], axis=1)
    yield
    for c0 in range(0, D_MODEL, PROJ_PIECE):
        cols = slice(c0, c0 + PROJ_PIECE)
        x1_ref[0, tok, cols] = x_ref[0, tok, cols] + _bdot(mix, wout_ref[:, cols])
        yield


def _mixer_prompt(x, weights, TB, NSUB, L):
    B, S, _ = x.shape
    TS = TB * NSUB
    nb = S // TS
    f32 = jnp.float32
    kern = functools.partial(_mixer_prompt_kernel, L=L, TB=TB, NSUB=NSUB)
    return pl.pallas_call(
        kern,
        grid=(B, nb),
        in_specs=[
            pl.BlockSpec((1, TS, D_MODEL), lambda b, j: (b, j, 0)),
            _const_spec((1, D_MODEL)),
            _const_spec((D_MODEL, PROJ_COLS)),
            _const_spec((TPROJ_ROWS, D_MODEL)),
            _const_spec((1, NH)),
            _const_spec((1, NH)),
            _const_spec((1, MLSTM_W)),
            _const_spec((CONV_W, CONV_CH)),
            _const_spec((D_MODEL, D_MODEL)),
        ],
        out_specs=[
            pl.BlockSpec((1, TS, D_MODEL), lambda b, j: (b, j, 0)),
            pl.BlockSpec((1, NH, DK, DV), lambda b, j: (b, 0, 0, 0)),
            pl.BlockSpec((1, NH // 2, 2 * DK), lambda b, j: (b, 0, 0)),
            pl.BlockSpec((1, SUBLANES, LANES), lambda b, j: (b, 0, 0)),
            pl.BlockSpec((1, CONV_W - 1, CONV_CH), lambda b, j: (b, 0, 0)),
        ],
        out_shape=[
            jax.ShapeDtypeStruct((B, S, D_MODEL), f32),
            jax.ShapeDtypeStruct((B, NH, DK, DV), f32),
            jax.ShapeDtypeStruct((B, NH // 2, 2 * DK), f32),
            jax.ShapeDtypeStruct((B, SUBLANES, LANES), f32),
            jax.ShapeDtypeStruct((B, CONV_W - 1, CONV_CH), f32),
        ],
        scratch_shapes=[
            pltpu.VMEM((NSUB, QK_W, TB), f32),
            pltpu.VMEM((NSUB, MLSTM_W, TB), f32),
            pltpu.VMEM((NSUB, MLSTM_W, TB), f32),
            pltpu.VMEM((NSUB, TB, QK_W), f32),
            pltpu.VMEM((NSUB, MLSTM_W, TB), f32),
            pltpu.VMEM((NSUB, TB + SUBLANES, CONV_CH), f32),
            pltpu.VMEM((NSUB, TB, LANES), f32),
            pltpu.VMEM((NSUB, SUBLANES, TB), f32),
            pltpu.VMEM((NSUB, SUBLANES, TB), f32),
            pltpu.VMEM((NSUB, SUBLANES, TB), f32),
            pltpu.VMEM((NSUB, SUBLANES, TB), f32),
            pltpu.VMEM((NSUB, TB // L, SUBLANES, LANES), f32),
            pltpu.VMEM((NH // 2, ST_ROWS, LANES), f32),
            pltpu.VMEM((1, LANES), f32),
        ],
        compiler_params=pltpu.CompilerParams(
            dimension_semantics=("arbitrary", "arbitrary"), vmem_limit_bytes=VMEM_LIMIT),
        name="mixer_prompt",
    )(x, *weights)


def _row_scan(x, seg, op, fill):
    pos = lax.broadcasted_iota(jnp.int32, x.shape, 0) & (seg - 1)
    k = 1
    while k < seg:
        x = op(x, jnp.where(pos >= k, pltpu.roll(x, k, 0), fill))
        k *= 2
    return x


def _split_dot(x, sel, parts):
    acc = None
    rem = x
    for p in range(parts):
        hi = rem.astype(jnp.bfloat16)
        d = _bdot(hi, sel)
        acc = d if acc is None else acc + d
        if p + 1 < parts:
            rem = rem - hi.astype(jnp.float32)
    return acc


def _selectors():
    h = np.arange(NH)
    seg_qk = np.zeros((QK_W, LANES), np.float32)
    seg_qk[np.arange(QK_W), np.arange(QK_W) // DK] = 1.0
    exp_v = np.zeros((LANES, MLSTM_W), np.float32)
    exp_k = np.zeros((LANES, QK_W), np.float32)
    for i in h:
        exp_v[i, i * DV:(i + 1) * DV] = 1.0
        exp_k[i, i * DK:(i + 1) * DK] = 1.0
    mean_v = np.kron(np.eye(NH, dtype=np.float32), np.full((DV, DV), 1.0 / DV, np.float32))
    return tuple(jnp.asarray(m, jnp.bfloat16) for m in (seg_qk, exp_v, exp_k, mean_v))


def _mixer_sample_kernel(x_ref, cv_ref, c0_ref, n0_ref, m0_ref,
                         gmix_ref, win_ref, big_ref, bfg_ref, gmh_ref, cw_ref, wout_ref,
                         segqk_ref, expv_ref, expk_ref, meanv_ref,
                         x1_ref, cst_ref, nst_ref, mst_ref, cvo_ref,
                         xp_s, z_s, mp_s, qc_s, gb_s, *, NSEQ):
    R = SUBLANES
    TS = NSEQ * R
    bf16 = jnp.bfloat16
    f32 = jnp.float32

    xp_s[:, SAMPLE_T:, :] = jnp.zeros((NSEQ, R - SAMPLE_T, D_MODEL), f32)
    xp_s[:, 0:SAMPLE_T, :] = x_ref[...]
    x = xp_s[...].reshape(TS, D_MODEL)
    a = _rms(x, gmix_ref[...]).astype(bf16)
    u = _bdot(a, win_ref[...])

    pos = lax.broadcasted_iota(jnp.int32, (TS, 1), 0) & (R - 1)
    real = pos < SAMPLE_T

    z = u[:, _CG0:_HC0] * u[:, _HC0:_GT0]
    z3 = z.reshape(NSEQ, R, CONV_CH)
    z_s[...] = z3
    z_s[:, R - (CONV_W - 1):, :] = cv_ref[...]
    zf = z_s[...].reshape(TS, CONV_CH)
    zm1 = jnp.where(pos >= 1, pltpu.roll(z, 1, 0), pltpu.roll(zf, TS - (R - 1), 0))
    zm2 = jnp.where(pos >= 2, pltpu.roll(z, 2, 0), pltpu.roll(zf, TS - (R - 2), 0))
    yc = u[:, _BG0:_CG0] * (cw_ref[0:1, :] * zm2 + cw_ref[1:2, :] * zm1 + cw_ref[2:3, :] * z)
    cvo_ref[...] = z3[:, SAMPLE_T - (CONV_W - 1):SAMPLE_T, :]

    gb_s[...] = jnp.zeros(gb_s.shape, f32)
    gb_s[:, 0:NH] = big_ref[...]
    gb_s[:, LANES:LANES + NH] = bfg_ref[...]
    gates = u[:, _GT0:PROJ_COLS] + gb_s[...]
    ic = jnp.where(real, GATE_CAP * jnp.tanh(gates[:, :LANES] / GATE_CAP), NEG_BIG)
    lf = jnp.where(real, _log_sigmoid(gates[:, LANES:]), 0.0)
    b = _row_scan(lf, R, jnp.add, 0.0)
    uu = ic - b
    m_loc = b + _row_scan(uu, R, jnp.maximum, -jnp.inf)

    mp_s[...] = jnp.zeros(mp_s.shape, f32)
    mp_s[:, :, 0:NH] = m0_ref[...]
    m_prev = mp_s[...]
    b3 = b.reshape(NSEQ, R, LANES)
    ml3 = m_loc.reshape(NSEQ, R, LANES)
    b_last = b3[:, R - 1:R, :]
    m_new = jnp.maximum(b_last + m_prev, ml3[:, R - 1:R, :])
    g3 = b3 + m_prev
    mt3 = jnp.maximum(g3, ml3)
    wi = jnp.exp(g3 - mt3).reshape(TS, LANES)
    c2 = (mt3 - b3).reshape(TS, LANES)
    emt = jnp.exp(-mt3).reshape(TS, LANES)
    wk = jnp.exp(b_last - b3 + ic.reshape(NSEQ, R, LANES) - m_new).reshape(TS, LANES)
    a_st = jnp.broadcast_to(jnp.exp(b_last + m_prev - m_new), (NSEQ, R, LANES)).reshape(TS, LANES)
    mst_ref[...] = m_new[:, :, 0:NH]

    q = u[:, _Q0:_K0] * (DK ** -0.5)
    k = u[:, _K0:_V0]
    v = u[:, _V0:_OG0]
    rs = jnp.zeros((TS, LANES), f32)
    num = jnp.zeros((TS, MLSTM_W), f32)
    for d in range(SAMPLE_T):
        kd = k if d == 0 else pltpu.roll(k, d, 0)
        vd = v if d == 0 else pltpu.roll(v, d, 0)
        ud = uu if d == 0 else pltpu.roll(uu, d, 0)
        p = _split_dot(q * kd, segqk_ref[...], 2) * jnp.exp(ud - c2)
        rs = rs + p
        num = num + _split_dot(p, expv_ref[...], 2) * vd

    n0x = jnp.broadcast_to(n0_ref[...], (NSEQ, R, QK_W)).reshape(TS, QK_W)
    qn = _split_dot(q * n0x, segqk_ref[...], 2)
    den = wi * qn + rs
    rden = 1.0 / jnp.maximum(jnp.abs(den), emt)
    kw = k * _split_dot(wk, expk_ref[...], 2)
    ax = _split_dot(a_st, expv_ref[...], 3)
    lane_q = lax.broadcasted_iota(jnp.int32, (R, QK_W), 1)
    for i in range(NSEQ):
        rows = slice(i * R, (i + 1) * R)
        qi = q[rows, :]
        lhs = jnp.concatenate(
            [jnp.where((lane_q >= h * DK) & (lane_q < (h + 1) * DK), qi, 0.0) for h in range(NH)], axis=0)
        c0 = c0_ref[i]
        qc = _bdot(lhs.astype(bf16), c0.reshape(QK_W, DV).astype(bf16))
        qc_s[rows, :] = jnp.concatenate([qc[h * R:(h + 1) * R, :] for h in range(NH)], axis=1)
        kwi = kw[rows, :].astype(bf16)
        vi = v[rows, :].astype(bf16)
        for h in range(NH):
            dc = lax.dot_general(kwi[:, h * DK:(h + 1) * DK], vi[:, h * DV:(h + 1) * DV],
                                 (((0,), (0,)), ((), ())), preferred_element_type=f32)
            cst_ref[i, h] = ax[i * R:i * R + 1, h * DV:(h + 1) * DV] * c0[h] + dc

    hh = (_split_dot(wi, expv_ref[...], 2) * qc_s[...] + num) * _split_dot(rden, expv_ref[...], 2)
    ms = _split_dot(hh * hh, meanv_ref[...], 2)
    hm = jax.nn.sigmoid(u[:, _OG0:_BG0]) * (hh * lax.rsqrt(ms + EPS) * gmh_ref[...])
    mix = jnp.concatenate([hm, yc], axis=1).astype(bf16)
    out = x + _bdot(mix, wout_ref[...])
    x1_ref[...] = out.reshape(NSEQ, R, D_MODEL)[:, 0:SAMPLE_T, :]

    a_k = _split_dot(a_st, expk_ref[...], 3).reshape(NSEQ, R, QK_W)[:, 0:1, :]
    nst_ref[...] = a_k * n0_ref[...] + jnp.sum(kw.reshape(NSEQ, R, QK_W), axis=1, keepdims=True)


def _mixer_sample(x, cv, c0, n0, m0, weights, NSEQ):
    Bs = x.shape[0]
    f32 = jnp.float32
    kern = functools.partial(_mixer_sample_kernel, NSEQ=NSEQ)
    x_spec = pl.BlockSpec((NSEQ, SAMPLE_T, D_MODEL), lambda t: (t, 0, 0))
    cv_spec = pl.BlockSpec((NSEQ, CONV_W - 1, CONV_CH), lambda t: (t, 0, 0))
    c_spec = pl.BlockSpec((NSEQ, NH, DK, DV), lambda t: (t, 0, 0, 0))
    n_spec = pl.BlockSpec((NSEQ, 1, QK_W), lambda t: (t, 0, 0))
    m_spec = pl.BlockSpec((NSEQ, 1, NH), lambda t: (t, 0, 0))
    return pl.pallas_call(
        kern,
        grid=(Bs // NSEQ,),
        in_specs=[
            x_spec, cv_spec, c_spec, n_spec, m_spec,
            _const_spec((1, D_MODEL)),
            _const_spec((D_MODEL, PROJ_COLS)),
            _const_spec((1, NH)),
            _const_spec((1, NH)),
            _const_spec((1, MLSTM_W)),
            _const_spec((CONV_W, CONV_CH)),
            _const_spec((D_MODEL, D_MODEL)),
            _const_spec((QK_W, LANES)),
            _const_spec((LANES, MLSTM_W)),
            _const_spec((LANES, QK_W)),
            _const_spec((MLSTM_W, MLSTM_W)),
        ],
        out_specs=[x_spec, c_spec, n_spec, m_spec, cv_spec],
        out_shape=[
            jax.ShapeDtypeStruct((Bs, SAMPLE_T, D_MODEL), f32),
            jax.ShapeDtypeStruct((Bs, NH, DK, DV), f32),
            jax.ShapeDtypeStruct((Bs, 1, QK_W), f32),
            jax.ShapeDtypeStruct((Bs, 1, NH), f32),
            jax.ShapeDtypeStruct((Bs, CONV_W - 1, CONV_CH), f32),
        ],
        scratch_shapes=[
            pltpu.VMEM((NSEQ, SUBLANES, D_MODEL), f32),
            pltpu.VMEM((NSEQ, SUBLANES, CONV_CH), f32),
            pltpu.VMEM((NSEQ, 1, LANES), f32),
            pltpu.VMEM((NSEQ * SUBLANES, MLSTM_W), f32),
            pltpu.VMEM((1, 2 * LANES), f32),
        ],
        compiler_params=pltpu.CompilerParams(
            dimension_semantics=("arbitrary",), vmem_limit_bytes=VMEM_LIMIT),
        name="mixer_sample",
    )(x, cv, c0, n0, m0, *weights)


def _mixer_weights(g_mix, w_in, b_ig, b_fg, g_mh, conv_w, w_out):
    w_all, wt, wout = _prep_weights(w_in, w_out)
    gmix = g_mix.reshape(1, D_MODEL)
    big, bfg = b_ig.reshape(1, NH), b_fg.reshape(1, NH)
    gmh = g_mh.reshape(1, MLSTM_W)
    prompt = (gmix, w_all, wt, big, bfg, gmh, conv_w, wout)
    sample = (gmix, w_all, big, bfg, gmh, conv_w, wout) + _selectors()
    return prompt, sample


FF_STREAM = 256
_N_FF_PIECES = D_FF // FF_STREAM
_N_PIECES = _N_FF_PIECES + PLE_DIM // FF_STREAM + D_MODEL // FF_STREAM


def _ffn_rows(x, pe, gffn_ref, gple_ref, gfin_ref, wg_s, wu_s, wd_s, wple_s, wpg_s, *, chunk, final_norm,
              before_chunk=None, before_tail=None):
    bf16 = jnp.bfloat16
    f = _rms(x, gffn_ref[...]).astype(bf16)
    e = None
    if before_tail is None:
        e = _rms(_bdot(pe.astype(bf16), wple_s[...]), gple_ref[...])
    for c0 in range(0, D_FF, chunk):
        c1 = min(c0 + chunk, D_FF)
        if before_chunk is not None:
            before_chunk(c0 // chunk)
        gate = _bdot(f, wg_s[:, c0:c1])
        up = _bdot(f, wu_s[:, c0:c1])
        hmid = (gate * jax.nn.sigmoid(gate) * up).astype(bf16)
        x = x + _bdot(hmid, wd_s[c0:c1, :])
    if before_tail is not None:
        before_tail()
        e = _rms(_bdot(pe.astype(bf16), wple_s[...]), gple_ref[...])
    xb = x.astype(bf16)
    n_rows = x.shape[0]
    outs = []
    for r0 in range(0, n_rows, n_rows // FFN_TAIL_PIECES):
        rows = slice(r0, r0 + n_rows // FFN_TAIL_PIECES)
        xr = x[rows] + jax.nn.sigmoid(_bdot(xb[rows], wpg_s[...])) * e[rows]
        outs.append(_rms(xr, gfin_ref[...]) if final_norm else xr)
    return jnp.concatenate(outs, axis=0)


def _ffn_stream_kernel(xp_ref, pp_ref, xs_ref, ps_ref, gffn_ref, gple_ref, gfin_ref,
                       wg_hbm, wu_hbm, wd_hbm, wple_hbm, wpg_hbm,
                       yp_ref, ys_ref,
                       wg_s, wu_s, wd_s, wple_s, wpg_s, stg_g, stg_u, stg_d, sem, *, final_norm):
    t = pl.program_id(0)
    bf16 = jnp.bfloat16
    weights = (wg_s, wu_s, wd_s, wple_s, wpg_s)

    def copies(k):
        slot = k % 2
        if k < _N_FF_PIECES:
            cols = pl.ds(k * FF_STREAM, FF_STREAM)
            return [pltpu.make_async_copy(wg_hbm.at[:, cols], stg_g.at[slot], sem.at[0, slot]),
                    pltpu.make_async_copy(wu_hbm.at[:, cols], stg_u.at[slot], sem.at[1, slot]),
                    pltpu.make_async_copy(wd_hbm.at[cols, :], stg_d.at[slot], sem.at[2, slot])]
        if k == _N_FF_PIECES:
            return [pltpu.make_async_copy(wple_hbm, stg_d.at[slot], sem.at[2, slot])]
        rows = pl.ds((k - _N_FF_PIECES - 1) * FF_STREAM, FF_STREAM)
        return [pltpu.make_async_copy(wpg_hbm.at[rows, :], stg_d.at[slot], sem.at[2, slot])]

    def land(k):
        if k + 1 < _N_PIECES:
            for cp in copies(k + 1):
                cp.start()
        for cp in copies(k):
            cp.wait()
        slot = k % 2
        if k < _N_FF_PIECES:
            cols = slice(k * FF_STREAM, (k + 1) * FF_STREAM)
            wg_s[:, cols] = stg_g[slot].astype(bf16)
            wu_s[:, cols] = stg_u[slot].astype(bf16)
            wd_s[cols, :] = stg_d[slot].astype(bf16)
        elif k == _N_FF_PIECES:
            wple_s[...] = stg_d[slot].astype(bf16)
        else:
            r0 = (k - _N_FF_PIECES - 1) * FF_STREAM
            wpg_s[r0:r0 + FF_STREAM, :] = stg_d[slot].astype(bf16)

    @pl.when(t == 0)
    def _():
        for cp in copies(0):
            cp.start()
        n_tok = xs_ref.shape[0] * xs_ref.shape[1]

        def tail():
            for k in range(_N_FF_PIECES, _N_PIECES):
                land(k)

        y = _ffn_rows(xs_ref[...].reshape(n_tok, D_MODEL), ps_ref[...].reshape(n_tok, PLE_DIM),
                      gffn_ref, gple_ref, gfin_ref, *weights, chunk=FF_STREAM, final_norm=final_norm,
                      before_chunk=land, before_tail=tail)
        ys_ref[...] = y.reshape(ys_ref.shape)

    @pl.when(t > 0)
    def _():
        yp_ref[...] = _ffn_rows(xp_ref[...], pp_ref[...], gffn_ref, gple_ref, gfin_ref, *weights,
                                chunk=FF_CHUNK, final_norm=final_norm)


def _ffn_stream(xp, pp, xs, ps, g_ffn, g_ple, g_final, w_gate, w_up, w_down, w_ple, w_pg, TM, final_norm):
    T = xp.shape[0]
    assert T % TM == 0 and D_FF % FF_STREAM == 0 and D_MODEL % FF_STREAM == 0
    assert w_ple.shape == (FF_STREAM, D_MODEL)
    f32, bf16 = jnp.float32, jnp.bfloat16
    kern = functools.partial(_ffn_stream_kernel, final_norm=final_norm)
    row_map = lambda t: (jnp.maximum(t - 1, 0), 0)
    hbm = pl.BlockSpec(memory_space=pl.ANY)
    return pl.pallas_call(
        kern,
        grid=(T // TM + 1,),
        in_specs=[
            pl.BlockSpec((TM, D_MODEL), row_map),
            pl.BlockSpec((TM, PLE_DIM), row_map),
            _const_spec(xs.shape),
            _const_spec(ps.shape),
            _const_spec((1, D_MODEL)),
            _const_spec((1, D_MODEL)),
            _const_spec((1, D_MODEL)),
            hbm, hbm, hbm, hbm, hbm,
        ],
        out_specs=[pl.BlockSpec((TM, D_MODEL), row_map),
                   pl.BlockSpec(xs.shape, lambda t: (0, 0, 0))],
        out_shape=[jax.ShapeDtypeStruct(xp.shape, f32), jax.ShapeDtypeStruct(xs.shape, f32)],
        scratch_shapes=[
            pltpu.VMEM((D_MODEL, D_FF), bf16),
            pltpu.VMEM((D_MODEL, D_FF), bf16),
            pltpu.VMEM((D_FF, D_MODEL), bf16),
            pltpu.VMEM((PLE_DIM, D_MODEL), bf16),
            pltpu.VMEM((D_MODEL, D_MODEL), bf16),
            pltpu.VMEM((2, D_MODEL, FF_STREAM), f32),
            pltpu.VMEM((2, D_MODEL, FF_STREAM), f32),
            pltpu.VMEM((2, FF_STREAM, D_MODEL), f32),
            pltpu.SemaphoreType.DMA((3, 2)),
        ],
        compiler_params=pltpu.CompilerParams(
            dimension_semantics=("arbitrary",), vmem_limit_bytes=VMEM_LIMIT),
        name="ffn",
    )(xp, pp, xs, ps, g_ffn.reshape(1, D_MODEL), g_ple.reshape(1, D_MODEL), g_final.reshape(1, D_MODEL),
      w_gate, w_up, w_down, w_ple, w_pg)


def kernel(x_prompt, x_sample, p_prompt, p_sample, state_C, state_n, state_m, state_conv,
           g_mix, w_in, b_ig, b_fg, g_mh, conv_w, w_out, g_ffn, w_gate, w_up, w_down,
           w_ple, g_ple, w_pg, g_final):
    bf16 = jnp.bfloat16
    depth = g_mix.shape[0]
    B, S, _ = x_prompt.shape
    Bs, Ss, _ = x_sample.shape
    assert Ss == SAMPLE_T and S % PROMPT_TS == 0 and Bs % SAMPLE_NSEQ == 0

    xp = x_prompt
    xs = x_sample
    outs = [[] for _ in range(8)]
    for i in range(depth):
        last = i == depth - 1
        pw, sw = _mixer_weights(g_mix[i], w_in[i], b_ig[i], b_fg[i], g_mh[i], conv_w[i], w_out[i])
        x1p, cp, n_p, mp, cvp = _mixer_prompt(xp, pw, TB=PROMPT_TS, NSUB=PROMPT_NSUB, L=PROMPT_L)
        x1s, cs, n_s, ms, cvs = _mixer_sample(
            xs, state_conv[i], state_C[i], state_n[i].reshape(Bs, 1, QK_W),
            state_m[i].reshape(Bs, 1, NH), sw, SAMPLE_NSEQ)
        xp, xs = _ffn_stream(x1p.reshape(B * S, D_MODEL), p_prompt[i].reshape(B * S, PLE_DIM), x1s,
                             p_sample[i], g_ffn[i], g_ple[i], g_final, w_gate[i], w_up[i], w_down[i],
                             w_ple[i], w_pg[i], FFN_TM, last)
        xp = xp.reshape(B, S, D_MODEL)

        new = (cp, n_p.reshape(B, NH, DK), mp[:, :NH, 0], cvp,
               cs, n_s.reshape(Bs, NH, DK), ms.reshape(Bs, NH), cvs)
        for lst, v in zip(outs, new):
            lst.append(v)

    return (xp, xs) + tuple(jnp.stack(l) for l in outs)
```

```python
import functools

import numpy as np
import jax
import jax.numpy as jnp
from jax import lax
from jax.experimental import pallas as pl
from jax.experimental.pallas import tpu as pltpu

D_MODEL = 1024
NH = 4
DK = 64
DV = 128
MLSTM_W = NH * DV
QK_W = NH * DK
CONV_CH = 512
CONV_W = 3
D_FF = 2816
FF_CHUNK = 768
FFN_TM = 512
FFN_TAIL_PIECES = 2
PLE_DIM = 256
PROMPT_L = 128
PROMPT_TS = 512
PROMPT_NSUB = 2
SAMPLE_T = 4
SAMPLE_NSEQ = 32
GATE_CAP = 15.0
EPS = 1e-6

LANES = 128
SUBLANES = 8
NEG_BIG = -1e30

VMEM_LIMIT = 56 * 1024 * 1024


def _rms(x, g):
    ms = jnp.mean(x * x, axis=-1, keepdims=True)
    return x * lax.rsqrt(ms + EPS) * g


def _bdot(a, b):
    return jnp.dot(a, b, preferred_element_type=jnp.float32)


def _log_sigmoid(x):
    return jnp.minimum(x, 0.0) - jnp.log1p(jnp.exp(-jnp.abs(x)))


def _const_spec(shape):
    nd = len(shape)
    return pl.BlockSpec(shape, lambda *_: (0,) * nd, pipeline_mode=pl.Buffered(1))


_IG0 = 2 * QK_W + MLSTM_W
_OG_IN = _IG0 + 2 * NH
PROJ_IN = _OG_IN + MLSTM_W + 3 * CONV_CH
_Q0, _K0, _V0, _OG0, _BG0, _CG0, _HC0, _GT0 = 0, 256, 512, 1024, 1536, 2048, 2560, 3072
PROJ_COLS = _GT0 + 2 * LANES
_TQ0, _TV0, _TOG0, _TG0 = 0, 256, 768, 1280
TPROJ_ROWS = _TG0 + 2 * SUBLANES
PREP_TK = 256


def _prep_weights_kernel(w_ref, wo_ref, wall_ref, wt_ref, wout_ref):
    bf16 = jnp.bfloat16
    wout_ref[...] = wo_ref[...].astype(bf16)
    wf = w_ref[...]
    tk = wf.shape[1]
    g8 = wf[_IG0:_OG_IN]
    row = lax.broadcasted_iota(jnp.int32, (SUBLANES, tk), 0)
    g_in = jnp.where(row < NH, g8, 0.0)
    g_fg = jnp.where(row < NH, pltpu.roll(g8, SUBLANES - NH, 0), 0.0)

    wt_ref[_TQ0:_TV0, :] = wf[_Q0:_K0].astype(bf16)
    wt_ref[_TV0:_TOG0, :] = wf[_V0:_OG0].astype(bf16)
    wt_ref[_TOG0:_TG0, :] = wf[_OG_IN:_OG_IN + MLSTM_W].astype(bf16)
    wt_ref[_TG0:TPROJ_ROWS, :] = jnp.concatenate([g_in, g_fg], axis=0).astype(bf16)

    wall_ref[:, 0:_OG0] = wf[0:_IG0].T.astype(bf16)
    wall_ref[:, _OG0:_GT0] = wf[_OG_IN:PROJ_IN].T.astype(bf16)
    pad = jnp.zeros((LANES - SUBLANES, tk), jnp.float32)
    wall_ref[:, _GT0:_GT0 + LANES] = jnp.concatenate([g_in, pad], axis=0).T.astype(bf16)
    wall_ref[:, _GT0 + LANES:PROJ_COLS] = jnp.concatenate([g_fg, pad], axis=0).T.astype(bf16)


def _prep_weights(w_in, w_out):
    bf16 = jnp.bfloat16
    return pl.pallas_call(
        _prep_weights_kernel,
        grid=(D_MODEL // PREP_TK,),
        in_specs=[pl.BlockSpec((PROJ_IN, PREP_TK), lambda i: (0, i)),
                  pl.BlockSpec((PREP_TK, D_MODEL), lambda i: (i, 0))],
        out_specs=[pl.BlockSpec((PREP_TK, PROJ_COLS), lambda i: (i, 0)),
                   pl.BlockSpec((TPROJ_ROWS, PREP_TK), lambda i: (0, i)),
                   pl.BlockSpec((PREP_TK, D_MODEL), lambda i: (i, 0))],
        out_shape=[jax.ShapeDtypeStruct((D_MODEL, PROJ_COLS), bf16),
                   jax.ShapeDtypeStruct((TPROJ_ROWS, D_MODEL), bf16),
                   jax.ShapeDtypeStruct((D_MODEL, D_MODEL), bf16)],
        compiler_params=pltpu.CompilerParams(
            dimension_semantics=("arbitrary",), vmem_limit_bytes=VMEM_LIMIT),
        name="prep_weights",
    )(w_in.T, w_out)


ST_ROWS = DV + 2 * SUBLANES


def _lane_scan(x, seg, op, fill):
    pos = lax.broadcasted_iota(jnp.int32, x.shape, 1) & (seg - 1)
    k = 1
    while k < seg:
        x = op(x, jnp.where(pos >= k, pltpu.roll(x, k, 1), fill))
        k *= 2
    return x


PROJ_PIECE = 256


def _run(*phases):
    live = list(phases)
    while live:
        for g in list(live):
            try:
                next(g)
            except StopIteration:
                live.remove(g)


def _mixer_prompt_kernel(x_ref, gmix_ref, wn_ref, wt_ref, gbt_ref, gmhc_ref, cw_ref, wout_ref,
                         x1_ref, c_ref, n_ref, m_ref, cvo_ref,
                         qt_s, vt_s, ogt_s, k_s, mixt_s, z_s, ucol_s, wi_s, c2_s, emt_s, wk_s, ast_s, st_s,
                         *, L, TB, NSUB):
    j = pl.program_id(1)

    @pl.when(j == 0)
    def _():
        z_s[0, 0:SUBLANES, :] = jnp.zeros((SUBLANES, CONV_CH), jnp.float32)
        st_s[...] = jnp.zeros(st_s.shape, jnp.float32)
        m_ref[...] = jnp.zeros(m_ref.shape, jnp.float32)

    m_prev = m_ref[0]
    sub = [dict(sb=sb, x_ref=x_ref, gmix_ref=gmix_ref, wn_ref=wn_ref, wt_ref=wt_ref, gbt_ref=gbt_ref,
                gmhc_ref=gmhc_ref, cw_ref=cw_ref, wout_ref=wout_ref, x1_ref=x1_ref, qt_s=qt_s.at[sb],
                vt_s=vt_s.at[sb], ogt_s=ogt_s.at[sb], k_s=k_s.at[sb], mixt_s=mixt_s.at[sb], z_s=z_s,
                ucol_s=ucol_s.at[sb], wi_s=wi_s.at[sb], c2_s=c2_s.at[sb], emt_s=emt_s.at[sb],
                wk_s=wk_s.at[sb], ast_s=ast_s.at[sb], st_s=st_s, L=L, TS=TB) for sb in range(NSUB)]
    carry = {"m": m_prev}
    _run(_prompt_project(sub[0]))
    for a, b in zip(sub[:-1], sub[1:]):
        _run(_prompt_chunks(a, carry), _prompt_project(b))
    _run(_prompt_chunks(sub[-1], carry), *[_prompt_output(s) for s in sub[:-1]])
    _run(_prompt_output(sub[-1]))
    m_ref[0] = carry["m"]
    cvo_ref[0] = z_s[NSUB - 1, TB + SUBLANES - (CONV_W - 1):TB + SUBLANES, :]
    z_s[0, 0:SUBLANES, :] = z_s[NSUB - 1, TB:TB + SUBLANES, :]

    @pl.when(j == pl.num_programs(1) - 1)
    def _():
        for pr in range(NH // 2):
            state = st_s[pr]
            c_pair = state[0:DV, :].T
            c_ref[0, 2 * pr] = c_pair[0:DK, :]
            c_ref[0, 2 * pr + 1] = c_pair[DK:, :]
            n_ref[0, pr:pr + 1, :] = state[DV:DV + 1, :]


def _prompt_project(s):
    sb, L, TS = s["sb"], s["L"], s["TS"]
    x_ref, gmix_ref, wn_ref, wt_ref, gbt_ref, cw_ref = (
        s["x_ref"], s["gmix_ref"], s["wn_ref"], s["wt_ref"], s["gbt_ref"], s["cw_ref"])
    qt_s, vt_s, ogt_s, k_s, z_s = s["qt_s"], s["vt_s"], s["ogt_s"], s["k_s"], s["z_s"]
    bf16 = jnp.bfloat16

    x = x_ref[0, sb * TS:(sb + 1) * TS, :]
    a = _rms(x, gmix_ref[...]).astype(bf16)
    nt_dims = (((1,), (1,)), ((), ()))

    gt = lax.dot_general(wt_ref[_TG0:TPROJ_ROWS, :], a, nt_dims,
                         preferred_element_type=jnp.float32) + gbt_ref[...]
    ic = GATE_CAP * jnp.tanh(gt[0:SUBLANES] / GATE_CAP)
    lf = _log_sigmoid(gt[SUBLANES:])
    b = _lane_scan(lf, L, jnp.add, 0.0)
    u = ic - b
    m_loc = b + _lane_scan(u, L, jnp.maximum, -jnp.inf)
    s.update(b=b, u=u, ic=ic, m_loc=m_loc)
    yield

    k_s[...] = _bdot(a, wn_ref[:, _K0:_V0])
    yield
    hc = _bdot(a, wn_ref[:, _HC0:_GT0])
    yield
    z = _bdot(a, wn_ref[:, _CG0:_HC0]) * hc
    if sb > 0:
        z_s[sb, 0:SUBLANES, :] = z_s[sb - 1, TS:TS + SUBLANES, :]
    z_s[sb, SUBLANES:SUBLANES + TS, :] = z
    yield
    yconv = (cw_ref[0:1, :] * z_s[sb, SUBLANES - 2:SUBLANES - 2 + TS, :]
             + cw_ref[1:2, :] * z_s[sb, SUBLANES - 1:SUBLANES - 1 + TS, :]
             + cw_ref[2:3, :] * z)
    s["yc"] = (_bdot(a, wn_ref[:, _BG0:_CG0]) * yconv).astype(bf16)
    yield

    for r0, r1, dst, scale in ((_TQ0, _TV0, qt_s, DK ** -0.5), (_TV0, _TOG0, vt_s, None), (_TOG0, _TG0, ogt_s, None)):
        ut = lax.dot_general(wt_ref[r0:r1, :], a, nt_dims, preferred_element_type=jnp.float32)
        dst[...] = ut if scale is None else ut * scale
        yield


def _prompt_chunks(s, carry):
    L, TS = s["L"], s["TS"]
    qt_s, vt_s, ogt_s, k_s, mixt_s, ucol_s, wi_s, c2_s, emt_s, wk_s, ast_s, st_s = (
        s["qt_s"], s["vt_s"], s["ogt_s"], s["k_s"], s["mixt_s"], s["ucol_s"], s["wi_s"],
        s["c2_s"], s["emt_s"], s["wk_s"], s["ast_s"], s["st_s"])
    b, u, ic, m_loc = s["b"], s["u"], s["ic"], s["m_loc"]
    nc = TS // L
    bf16 = jnp.bfloat16
    m_prev = carry["m"]

    for c in range(nc):
        sl = slice(c * L, (c + 1) * L)
        bc, mlc = b[:, sl], m_loc[:, sl]
        b_last = jnp.broadcast_to(bc[:, L - 1:L], bc.shape)
        m_new = jnp.maximum(b_last + m_prev, jnp.broadcast_to(mlc[:, L - 1:L], bc.shape))
        g = bc + m_prev
        mt = jnp.maximum(g, mlc)
        wi_s[:, sl] = jnp.exp(g - mt)
        c2_s[:, sl] = mt - bc
        emt_s[:, sl] = jnp.exp(-mt)
        wk_s[:, sl] = jnp.exp(b_last - bc + ic[:, sl] - m_new)
        ast_s[c] = jnp.exp(b_last + m_prev - m_new)
        upad = jnp.concatenate([u[:, sl], jnp.zeros((L - SUBLANES, L), jnp.float32)], axis=0)
        ucol_s[c * L:(c + 1) * L, :] = upad.T
        m_prev = m_new
    carry["m"] = m_prev
    yield

    s_i = lax.broadcasted_iota(jnp.int32, (L, L), 0)
    t_i = lax.broadcasted_iota(jnp.int32, (L, L), 1)
    causal = s_i <= t_i
    low_half = lax.broadcasted_iota(jnp.int32, (L, LANES), 1) < DK
    zeros_q = jnp.zeros((DK, L), jnp.float32)
    zeros_p = jnp.zeros((L, L), bf16)

    def pair_row(ref, h0, rows):
        return jnp.concatenate([ref[h0:h0 + 1, rows], ref[h0 + 1:h0 + 2, rows]], axis=1)

    for c in range(nc):
        rows = slice(c * L, (c + 1) * L)
        for pr in range(NH // 2):
            h0 = 2 * pr
            hv0 = slice(h0 * DV, (h0 + 1) * DV)
            hv1 = slice((h0 + 1) * DV, (h0 + 2) * DV)
            kp = k_s[rows, pr * LANES:(pr + 1) * LANES]
            q0 = qt_s[h0 * DK:(h0 + 1) * DK, rows]
            q1 = qt_s[(h0 + 1) * DK:(h0 + 2) * DK, rows]
            qbd = jnp.concatenate([jnp.concatenate([q0, zeros_q], axis=1),
                                   jnp.concatenate([zeros_q, q1], axis=1)], axis=0).astype(bf16)
            st = _bdot(kp.astype(bf16), qbd)
            arg = jnp.concatenate(
                [jnp.where(causal, ucol_s[rows, h:h + 1] - c2_s[h:h + 1, rows], -jnp.inf)
                 for h in (h0, h0 + 1)], axis=1)
            pt = st * jnp.exp(arg)
            rs = jnp.sum(pt, axis=0, keepdims=True)
            ptb = pt.astype(bf16)
            pbd = jnp.concatenate([jnp.concatenate([ptb[:, 0:L], zeros_p], axis=1),
                                   jnp.concatenate([zeros_p, ptb[:, L:]], axis=1)], axis=0)
            vt = jnp.concatenate([vt_s[hv0, rows], vt_s[hv1, rows]], axis=1)
            state = st_s[pr]
            sq = _bdot(state.astype(bf16), qbd) * pair_row(wi_s, h0, rows)
            num = _bdot(vt.astype(bf16), pbd) + sq[0:DV]
            den = sq[DV:DV + 1] + rs
            hh = num * (1.0 / jnp.maximum(jnp.abs(den), pair_row(emt_s, h0, rows)))
            hn = hh * lax.rsqrt(jnp.mean(hh * hh, axis=0, keepdims=True) + EPS)
            mixt_s[hv0, rows] = jax.nn.sigmoid(ogt_s[hv0, rows]) * hn[:, 0:L]
            mixt_s[hv1, rows] = jax.nn.sigmoid(ogt_s[hv1, rows]) * hn[:, L:]
            wkr = pair_row(wk_s, h0, rows)
            vw = jnp.concatenate([vt * wkr, jnp.broadcast_to(wkr, (2 * SUBLANES, 2 * L))], axis=0)
            km = jnp.concatenate([jnp.where(low_half, kp, 0.0), jnp.where(low_half, 0.0, kp)], axis=0)
            decay = jnp.where(low_half[0:1, :], ast_s[c][h0:h0 + 1, :], ast_s[c][h0 + 1:h0 + 2, :])
            st_s[pr] = decay * state + _bdot(vw.astype(bf16), km.astype(bf16))
            yield


def _prompt_output(s):
    sb, TS = s["sb"], s["TS"]
    x_ref, wout_ref, x1_ref, mixt_s = s["x_ref"], s["wout_ref"], s["x1_ref"], s["mixt_s"]
    tok = slice(sb * TS, (sb + 1) * TS)
    hm = mixt_s[...].T * s["gmhc_ref"][...]
    mix = jnp.concatenate([hm.astype(jnp.bfloat16), s["yc"]], axis=1)
    yield
    for c0 in range(0, D_MODEL, PROJ_PIECE):
        cols = slice(c0, c0 + PROJ_PIECE)
        x1_ref[0, tok, cols] = x_ref[0, tok, cols] + _bdot(mix, wout_ref[:, cols])
        yield


def _mixer_prompt(x, weights, TB, NSUB, L):
    B, S, _ = x.shape
    TS = TB * NSUB
    nb = S // TS
    f32 = jnp.float32
    kern = functools.partial(_mixer_prompt_kernel, L=L, TB=TB, NSUB=NSUB)
    return pl.pallas_call(
        kern,
        grid=(B, nb),
        in_specs=[
            pl.BlockSpec((1, TS, D_MODEL), lambda b, j: (b, j, 0)),
            _const_spec((1, D_MODEL)),
            _const_spec((D_MODEL, PROJ_COLS)),
            _const_spec((TPROJ_ROWS, D_MODEL)),
            _const_spec((2 * SUBLANES, 1)),
            _const_spec((1, MLSTM_W)),
            _const_spec((CONV_W, CONV_CH)),
            _const_spec((D_MODEL, D_MODEL)),
        ],
        out_specs=[
            pl.BlockSpec((1, TS, D_MODEL), lambda b, j: (b, j, 0)),
            pl.BlockSpec((1, NH, DK, DV), lambda b, j: (b, 0, 0, 0)),
            pl.BlockSpec((1, NH // 2, 2 * DK), lambda b, j: (b, 0, 0)),
            pl.BlockSpec((1, SUBLANES, LANES), lambda b, j: (b, 0, 0)),
            pl.BlockSpec((1, CONV_W - 1, CONV_CH), lambda b, j: (b, 0, 0)),
        ],
        out_shape=[
            jax.ShapeDtypeStruct((B, S, D_MODEL), f32),
            jax.ShapeDtypeStruct((B, NH, DK, DV), f32),
            jax.ShapeDtypeStruct((B, NH // 2, 2 * DK), f32),
            jax.ShapeDtypeStruct((B, SUBLANES, LANES), f32),
            jax.ShapeDtypeStruct((B, CONV_W - 1, CONV_CH), f32),
        ],
        scratch_shapes=[
            pltpu.VMEM((NSUB, QK_W, TB), f32),
            pltpu.VMEM((NSUB, MLSTM_W, TB), f32),
            pltpu.VMEM((NSUB, MLSTM_W, TB), f32),
            pltpu.VMEM((NSUB, TB, QK_W), f32),
            pltpu.VMEM((NSUB, MLSTM_W, TB), f32),
            pltpu.VMEM((NSUB, TB + SUBLANES, CONV_CH), f32),
            pltpu.VMEM((NSUB, TB, LANES), f32),
            pltpu.VMEM((NSUB, SUBLANES, TB), f32),
            pltpu.VMEM((NSUB, SUBLANES, TB), f32),
            pltpu.VMEM((NSUB, SUBLANES, TB), f32),
            pltpu.VMEM((NSUB, SUBLANES, TB), f32),
            pltpu.VMEM((NSUB, TB // L, SUBLANES, LANES), f32),
            pltpu.VMEM((NH // 2, ST_ROWS, LANES), f32),
        ],
        compiler_params=pltpu.CompilerParams(
            dimension_semantics=("arbitrary", "arbitrary"), vmem_limit_bytes=VMEM_LIMIT),
        name="mixer_prompt",
    )(x, *weights)


def _row_scan(x, seg, op, fill):
    pos = lax.broadcasted_iota(jnp.int32, x.shape, 0) & (seg - 1)
    k = 1
    while k < seg:
        x = op(x, jnp.where(pos >= k, pltpu.roll(x, k, 0), fill))
        k *= 2
    return x


def _split_dot(x, sel, parts):
    acc = None
    rem = x
    for p in range(parts):
        hi = rem.astype(jnp.bfloat16)
        d = _bdot(hi, sel)
        acc = d if acc is None else acc + d
        if p + 1 < parts:
            rem = rem - hi.astype(jnp.float32)
    return acc


def _selectors():
    h = np.arange(NH)
    seg_qk = np.zeros((QK_W, LANES), np.float32)
    seg_qk[np.arange(QK_W), np.arange(QK_W) // DK] = 1.0
    exp_v = np.zeros((LANES, MLSTM_W), np.float32)
    exp_k = np.zeros((LANES, QK_W), np.float32)
    for i in h:
        exp_v[i, i * DV:(i + 1) * DV] = 1.0
        exp_k[i, i * DK:(i + 1) * DK] = 1.0
    mean_v = np.kron(np.eye(NH, dtype=np.float32), np.full((DV, DV), 1.0 / DV, np.float32))
    return tuple(jnp.asarray(m, jnp.bfloat16) for m in (seg_qk, exp_v, exp_k, mean_v))


def _mixer_sample_kernel(x_ref, cv_ref, c0_ref, n0_ref, m0_ref,
                         gmix_ref, win_ref, gb_ref, gmh_ref, cw_ref, wout_ref,
                         segqk_ref, expv_ref, expk_ref, meanv_ref,
                         x1_ref, cst_ref, nst_ref, mst_ref, cvo_ref,
                         xp_s, z_s, mp_s, qc_s, *, NSEQ):
    R = SUBLANES
    TS = NSEQ * R
    bf16 = jnp.bfloat16
    f32 = jnp.float32

    xp_s[:, SAMPLE_T:, :] = jnp.zeros((NSEQ, R - SAMPLE_T, D_MODEL), f32)
    xp_s[:, 0:SAMPLE_T, :] = x_ref[...]
    x = xp_s[...].reshape(TS, D_MODEL)
    a = _rms(x, gmix_ref[...]).astype(bf16)
    u = _bdot(a, win_ref[...])

    pos = lax.broadcasted_iota(jnp.int32, (TS, 1), 0) & (R - 1)
    real = pos < SAMPLE_T

    z = u[:, _CG0:_HC0] * u[:, _HC0:_GT0]
    z3 = z.reshape(NSEQ, R, CONV_CH)
    z_s[...] = z3
    z_s[:, R - (CONV_W - 1):, :] = cv_ref[...]
    zf = z_s[...].reshape(TS, CONV_CH)
    zm1 = jnp.where(pos >= 1, pltpu.roll(z, 1, 0), pltpu.roll(zf, TS - (R - 1), 0))
    zm2 = jnp.where(pos >= 2, pltpu.roll(z, 2, 0), pltpu.roll(zf, TS - (R - 2), 0))
    yc = u[:, _BG0:_CG0] * (cw_ref[0:1, :] * zm2 + cw_ref[1:2, :] * zm1 + cw_ref[2:3, :] * z)
    cvo_ref[...] = z3[:, SAMPLE_T - (CONV_W - 1):SAMPLE_T, :]

    gates = u[:, _GT0:PROJ_COLS] + gb_ref[...]
    ic = jnp.where(real, GATE_CAP * jnp.tanh(gates[:, :LANES] / GATE_CAP), NEG_BIG)
    lf = jnp.where(real, _log_sigmoid(gates[:, LANES:]), 0.0)
    b = _row_scan(lf, R, jnp.add, 0.0)
    uu = ic - b
    m_loc = b + _row_scan(uu, R, jnp.maximum, -jnp.inf)

    mp_s[...] = jnp.zeros(mp_s.shape, f32)
    mp_s[:, :, 0:NH] = m0_ref[...]
    m_prev = mp_s[...]
    b3 = b.reshape(NSEQ, R, LANES)
    ml3 = m_loc.reshape(NSEQ, R, LANES)
    b_last = b3[:, R - 1:R, :]
    m_new = jnp.maximum(b_last + m_prev, ml3[:, R - 1:R, :])
    g3 = b3 + m_prev
    mt3 = jnp.maximum(g3, ml3)
    wi = jnp.exp(g3 - mt3).reshape(TS, LANES)
    c2 = (mt3 - b3).reshape(TS, LANES)
    emt = jnp.exp(-mt3).reshape(TS, LANES)
    wk = jnp.exp(b_last - b3 + ic.reshape(NSEQ, R, LANES) - m_new).reshape(TS, LANES)
    a_st = jnp.broadcast_to(jnp.exp(b_last + m_prev - m_new), (NSEQ, R, LANES)).reshape(TS, LANES)
    mst_ref[...] = m_new[:, :, 0:NH]

    q = u[:, _Q0:_K0] * (DK ** -0.5)
    k = u[:, _K0:_V0]
    v = u[:, _V0:_OG0]
    rs = jnp.zeros((TS, LANES), f32)
    num = jnp.zeros((TS, MLSTM_W), f32)
    for d in range(SAMPLE_T):
        kd = k if d == 0 else pltpu.roll(k, d, 0)
        vd = v if d == 0 else pltpu.roll(v, d, 0)
        ud = uu if d == 0 else pltpu.roll(uu, d, 0)
        p = _split_dot(q * kd, segqk_ref[...], 2) * jnp.exp(ud - c2)
        rs = rs + p
        num = num + _split_dot(p, expv_ref[...], 2) * vd

    n0x = jnp.broadcast_to(n0_ref[...], (NSEQ, R, QK_W)).reshape(TS, QK_W)
    qn = _split_dot(q * n0x, segqk_ref[...], 2)
    den = wi * qn + rs
    rden = 1.0 / jnp.maximum(jnp.abs(den), emt)
    kw = k * _split_dot(wk, expk_ref[...], 2)
    ax = _split_dot(a_st, expv_ref[...], 3)
    lane_q = lax.broadcasted_iota(jnp.int32, (R, QK_W), 1)
    for i in range(NSEQ):
        rows = slice(i * R, (i + 1) * R)
        qi = q[rows, :]
        lhs = jnp.concatenate(
            [jnp.where((lane_q >= h * DK) & (lane_q < (h + 1) * DK), qi, 0.0) for h in range(NH)], axis=0)
        c0 = c0_ref[i]
        qc = _bdot(lhs.astype(bf16), c0.reshape(QK_W, DV).astype(bf16))
        qc_s[rows, :] = jnp.concatenate([qc[h * R:(h + 1) * R, :] for h in range(NH)], axis=1)
        kwi = kw[rows, :].astype(bf16)
        vi = v[rows, :].astype(bf16)
        for h in range(NH):
            dc = lax.dot_general(kwi[:, h * DK:(h + 1) * DK], vi[:, h * DV:(h + 1) * DV],
                                 (((0,), (0,)), ((), ())), preferred_element_type=f32)
            cst_ref[i, h] = ax[i * R:i * R + 1, h * DV:(h + 1) * DV] * c0[h] + dc

    hh = (_split_dot(wi, expv_ref[...], 2) * qc_s[...] + num) * _split_dot(rden, expv_ref[...], 2)
    ms = _split_dot(hh * hh, meanv_ref[...], 2)
    hm = jax.nn.sigmoid(u[:, _OG0:_BG0]) * (hh * lax.rsqrt(ms + EPS) * gmh_ref[...])
    mix = jnp.concatenate([hm, yc], axis=1).astype(bf16)
    out = x + _bdot(mix, wout_ref[...])
    x1_ref[...] = out.reshape(NSEQ, R, D_MODEL)[:, 0:SAMPLE_T, :]

    a_k = _split_dot(a_st, expk_ref[...], 3).reshape(NSEQ, R, QK_W)[:, 0:1, :]
    nst_ref[...] = a_k * n0_ref[...] + jnp.sum(kw.reshape(NSEQ, R, QK_W), axis=1, keepdims=True)


def _mixer_sample(x, cv, c0, n0, m0, weights, NSEQ):
    Bs = x.shape[0]
    f32 = jnp.float32
    kern = functools.partial(_mixer_sample_kernel, NSEQ=NSEQ)
    x_spec = pl.BlockSpec((NSEQ, SAMPLE_T, D_MODEL), lambda t: (t, 0, 0))
    cv_spec = pl.BlockSpec((NSEQ, CONV_W - 1, CONV_CH), lambda t: (t, 0, 0))
    c_spec = pl.BlockSpec((NSEQ, NH, DK, DV), lambda t: (t, 0, 0, 0))
    n_spec = pl.BlockSpec((NSEQ, 1, QK_W), lambda t: (t, 0, 0))
    m_spec = pl.BlockSpec((NSEQ, 1, NH), lambda t: (t, 0, 0))
    return pl.pallas_call(
        kern,
        grid=(Bs // NSEQ,),
        in_specs=[
            x_spec, cv_spec, c_spec, n_spec, m_spec,
            _const_spec((1, D_MODEL)),
            _const_spec((D_MODEL, PROJ_COLS)),
            _const_spec((1, 2 * LANES)),
            _const_spec((1, MLSTM_W)),
            _const_spec((CONV_W, CONV_CH)),
            _const_spec((D_MODEL, D_MODEL)),
            _const_spec((QK_W, LANES)),
            _const_spec((LANES, MLSTM_W)),
            _const_spec((LANES, QK_W)),
            _const_spec((MLSTM_W, MLSTM_W)),
        ],
        out_specs=[x_spec, c_spec, n_spec, m_spec, cv_spec],
        out_shape=[
            jax.ShapeDtypeStruct((Bs, SAMPLE_T, D_MODEL), f32),
            jax.ShapeDtypeStruct((Bs, NH, DK, DV), f32),
            jax.ShapeDtypeStruct((Bs, 1, QK_W), f32),
            jax.ShapeDtypeStruct((Bs, 1, NH), f32),
            jax.ShapeDtypeStruct((Bs, CONV_W - 1, CONV_CH), f32),
        ],
        scratch_shapes=[
            pltpu.VMEM((NSEQ, SUBLANES, D_MODEL), f32),
            pltpu.VMEM((NSEQ, SUBLANES, CONV_CH), f32),
            pltpu.VMEM((NSEQ, 1, LANES), f32),
            pltpu.VMEM((NSEQ * SUBLANES, MLSTM_W), f32),
        ],
        compiler_params=pltpu.CompilerParams(
            dimension_semantics=("arbitrary",), vmem_limit_bytes=VMEM_LIMIT),
        name="mixer_sample",
    )(x, cv, c0, n0, m0, *weights)


def _mixer_weights(g_mix, w_in, b_ig, b_fg, g_mh, conv_w, w_out):
    f32 = jnp.float32
    w_all, wt, wout = _prep_weights(w_in, w_out)
    gmix = g_mix.reshape(1, D_MODEL)
    zeros = jnp.zeros((SUBLANES - NH,), f32)
    gb_col = jnp.concatenate([b_ig, zeros, b_fg, zeros]).reshape(2 * SUBLANES, 1)
    zeros = jnp.zeros((LANES - NH,), f32)
    gb_row = jnp.concatenate([b_ig, zeros, b_fg, zeros]).reshape(1, 2 * LANES)
    gmh = g_mh.reshape(1, MLSTM_W)
    prompt = (gmix, w_all, wt, gb_col, gmh, conv_w, wout)
    sample = (gmix, w_all, gb_row, gmh, conv_w, wout) + _selectors()
    return prompt, sample


FF_STREAM = 256
_N_FF_PIECES = D_FF // FF_STREAM
_N_PIECES = _N_FF_PIECES + PLE_DIM // FF_STREAM + D_MODEL // FF_STREAM


def _ffn_rows(x, pe, gffn_ref, gple_ref, gfin_ref, wg_s, wu_s, wd_s, wple_s, wpg_s, *, chunk, final_norm,
              before_chunk=None, before_tail=None):
    bf16 = jnp.bfloat16
    f = _rms(x, gffn_ref[...]).astype(bf16)
    e = None
    if before_tail is None:
        e = _rms(_bdot(pe.astype(bf16), wple_s[...]), gple_ref[...])
    for c0 in range(0, D_FF, chunk):
        c1 = min(c0 + chunk, D_FF)
        if before_chunk is not None:
            before_chunk(c0 // chunk)
        gate = _bdot(f, wg_s[:, c0:c1])
        up = _bdot(f, wu_s[:, c0:c1])
        hmid = (gate * jax.nn.sigmoid(gate) * up).astype(bf16)
        x = x + _bdot(hmid, wd_s[c0:c1, :])
    if before_tail is not None:
        before_tail()
        e = _rms(_bdot(pe.astype(bf16), wple_s[...]), gple_ref[...])
    xb = x.astype(bf16)
    n_rows = x.shape[0]
    outs = []
    for r0 in range(0, n_rows, n_rows // FFN_TAIL_PIECES):
        rows = slice(r0, r0 + n_rows // FFN_TAIL_PIECES)
        xr = x[rows] + jax.nn.sigmoid(_bdot(xb[rows], wpg_s[...])) * e[rows]
        outs.append(_rms(xr, gfin_ref[...]) if final_norm else xr)
    return jnp.concatenate(outs, axis=0)


def _ffn_stream_kernel(xp_ref, pp_ref, xs_ref, ps_ref, gffn_ref, gple_ref, gfin_ref,
                       wg_hbm, wu_hbm, wd_hbm, wple_hbm, wpg_hbm,
                       yp_ref, ys_ref,
                       wg_s, wu_s, wd_s, wple_s, wpg_s, stg_g, stg_u, stg_d, sem, *, final_norm):
    t = pl.program_id(0)
    bf16 = jnp.bfloat16
    weights = (wg_s, wu_s, wd_s, wple_s, wpg_s)

    def copies(k):
        slot = k % 2
        if k < _N_FF_PIECES:
            cols = pl.ds(k * FF_STREAM, FF_STREAM)
            return [pltpu.make_async_copy(wg_hbm.at[:, cols], stg_g.at[slot], sem.at[0, slot]),
                    pltpu.make_async_copy(wu_hbm.at[:, cols], stg_u.at[slot], sem.at[1, slot]),
                    pltpu.make_async_copy(wd_hbm.at[cols, :], stg_d.at[slot], sem.at[2, slot])]
        if k == _N_FF_PIECES:
            return [pltpu.make_async_copy(wple_hbm, stg_d.at[slot], sem.at[2, slot])]
        rows = pl.ds((k - _N_FF_PIECES - 1) * FF_STREAM, FF_STREAM)
        return [pltpu.make_async_copy(wpg_hbm.at[rows, :], stg_d.at[slot], sem.at[2, slot])]

    def land(k):
        if k + 1 < _N_PIECES:
            for cp in copies(k + 1):
                cp.start()
        for cp in copies(k):
            cp.wait()
        slot = k % 2
        if k < _N_FF_PIECES:
            cols = slice(k * FF_STREAM, (k + 1) * FF_STREAM)
            wg_s[:, cols] = stg_g[slot].astype(bf16)
            wu_s[:, cols] = stg_u[slot].astype(bf16)
            wd_s[cols, :] = stg_d[slot].astype(bf16)
        elif k == _N_FF_PIECES:
            wple_s[...] = stg_d[slot].astype(bf16)
        else:
            r0 = (k - _N_FF_PIECES - 1) * FF_STREAM
            wpg_s[r0:r0 + FF_STREAM, :] = stg_d[slot].astype(bf16)

    @pl.when(t == 0)
    def _():
        for cp in copies(0):
            cp.start()
        n_tok = xs_ref.shape[0] * xs_ref.shape[1]

        def tail():
            for k in range(_N_FF_PIECES, _N_PIECES):
                land(k)

        y = _ffn_rows(xs_ref[...].reshape(n_tok, D_MODEL), ps_ref[...].reshape(n_tok, PLE_DIM),
                      gffn_ref, gple_ref, gfin_ref, *weights, chunk=FF_STREAM, final_norm=final_norm,
                      before_chunk=land, before_tail=tail)
        ys_ref[...] = y.reshape(ys_ref.shape)

    @pl.when(t > 0)
    def _():
        yp_ref[...] = _ffn_rows(xp_ref[...], pp_ref[...], gffn_ref, gple_ref, gfin_ref, *weights,
                                chunk=FF_CHUNK, final_norm=final_norm)


def _ffn_stream(xp, pp, xs, ps, g_ffn, g_ple, g_final, w_gate, w_up, w_down, w_ple, w_pg, TM, final_norm):
    T = xp.shape[0]
    assert T % TM == 0 and D_FF % FF_STREAM == 0 and D_MODEL % FF_STREAM == 0
    assert w_ple.shape == (FF_STREAM, D_MODEL)
    f32, bf16 = jnp.float32, jnp.bfloat16
    kern = functools.partial(_ffn_stream_kernel, final_norm=final_norm)
    row_map = lambda t: (jnp.maximum(t - 1, 0), 0)
    hbm = pl.BlockSpec(memory_space=pl.ANY)
    return pl.pallas_call(
        kern,
        grid=(T // TM + 1,),
        in_specs=[
            pl.BlockSpec((TM, D_MODEL), row_map),
            pl.BlockSpec((TM, PLE_DIM), row_map),
            _const_spec(xs.shape),
            _const_spec(ps.shape),
            _const_spec((1, D_MODEL)),
            _const_spec((1, D_MODEL)),
            _const_spec((1, D_MODEL)),
            hbm, hbm, hbm, hbm, hbm,
        ],
        out_specs=[pl.BlockSpec((TM, D_MODEL), row_map),
                   pl.BlockSpec(xs.shape, lambda t: (0, 0, 0))],
        out_shape=[jax.ShapeDtypeStruct(xp.shape, f32), jax.ShapeDtypeStruct(xs.shape, f32)],
        scratch_shapes=[
            pltpu.VMEM((D_MODEL, D_FF), bf16),
            pltpu.VMEM((D_MODEL, D_FF), bf16),
            pltpu.VMEM((D_FF, D_MODEL), bf16),
            pltpu.VMEM((PLE_DIM, D_MODEL), bf16),
            pltpu.VMEM((D_MODEL, D_MODEL), bf16),
            pltpu.VMEM((2, D_MODEL, FF_STREAM), f32),
            pltpu.VMEM((2, D_MODEL, FF_STREAM), f32),
            pltpu.VMEM((2, FF_STREAM, D_MODEL), f32),
            pltpu.SemaphoreType.DMA((3, 2)),
        ],
        compiler_params=pltpu.CompilerParams(
            dimension_semantics=("arbitrary",), vmem_limit_bytes=VMEM_LIMIT),
        name="ffn",
    )(xp, pp, xs, ps, g_ffn.reshape(1, D_MODEL), g_ple.reshape(1, D_MODEL), g_final.reshape(1, D_MODEL),
      w_gate, w_up, w_down, w_ple, w_pg)


def kernel(x_prompt, x_sample, p_prompt, p_sample, state_C, state_n, state_m, state_conv,
           g_mix, w_in, b_ig, b_fg, g_mh, conv_w, w_out, g_ffn, w_gate, w_up, w_down,
           w_ple, g_ple, w_pg, g_final):
    bf16 = jnp.bfloat16
    depth = g_mix.shape[0]
    B, S, _ = x_prompt.shape
    Bs, Ss, _ = x_sample.shape
    assert Ss == SAMPLE_T and S % PROMPT_TS == 0 and Bs % SAMPLE_NSEQ == 0

    xp = x_prompt
    xs = x_sample
    outs = [[] for _ in range(8)]
    for i in range(depth):
        last = i == depth - 1
        pw, sw = _mixer_weights(g_mix[i], w_in[i], b_ig[i], b_fg[i], g_mh[i], conv_w[i], w_out[i])
        x1p, cp, n_p, mp, cvp = _mixer_prompt(xp, pw, TB=PROMPT_TS, NSUB=PROMPT_NSUB, L=PROMPT_L)
        x1s, cs, n_s, ms, cvs = _mixer_sample(
            xs, state_conv[i], state_C[i], state_n[i].reshape(Bs, 1, QK_W),
            state_m[i].reshape(Bs, 1, NH), sw, SAMPLE_NSEQ)
        xp, xs = _ffn_stream(x1p.reshape(B * S, D_MODEL), p_prompt[i].reshape(B * S, PLE_DIM), x1s,
                             p_sample[i], g_ffn[i], g_ple[i], g_final, w_gate[i], w_up[i], w_down[i],
                             w_ple[i], w_pg[i], FFN_TM, last)
        xp = xp.reshape(B, S, D_MODEL)

        new = (cp, n_p.reshape(B, NH, DK), mp[:, :NH, 0], cvp,
               cs, n_s.reshape(Bs, NH, DK), ms.reshape(Bs, NH), cvs)
        for lst, v in zip(outs, new):
            lst.append(v)

    return (xp, xs) + tuple(jnp.stack(l) for l in outs)
```

```python
import functools

import numpy as np
import jax
import jax.numpy as jnp
from jax import lax
from jax.experimental import pallas as pl
from jax.experimental.pallas import tpu as pltpu

D_MODEL = 1024
NH = 4
DK = 64
DV = 128
MLSTM_W = NH * DV
QK_W = NH * DK
CONV_CH = 512
CONV_W = 3
D_FF = 2816
FF_CHUNK = 768
FFN_TM = 512
FFN_TAIL_PIECES = 2
PLE_DIM = 256
PROMPT_L = 128
PROMPT_TS = 512
PROMPT_NSUB = 2
SAMPLE_T = 4
SAMPLE_NSEQ = 32
GATE_CAP = 15.0
EPS = 1e-6

LANES = 128
SUBLANES = 8
NEG_BIG = -1e30

VMEM_LIMIT = 56 * 1024 * 1024


def _rms(x, g):
    ms = jnp.mean(x * x, axis=-1, keepdims=True)
    return x * lax.rsqrt(ms + EPS) * g


def _bdot(a, b):
    return jnp.dot(a, b, preferred_element_type=jnp.float32)


def _log_sigmoid(x):
    return jnp.minimum(x, 0.0) - jnp.log1p(jnp.exp(-jnp.abs(x)))


def _const_spec(shape):
    nd = len(shape)
    return pl.BlockSpec(shape, lambda *_: (0,) * nd, pipeline_mode=pl.Buffered(1))


_IG0 = 2 * QK_W + MLSTM_W
_OG_IN = _IG0 + 2 * NH
PROJ_IN = _OG_IN + MLSTM_W + 3 * CONV_CH
_Q0, _K0, _V0, _OG0, _BG0, _CG0, _HC0, _GT0 = 0, 256, 512, 1024, 1536, 2048, 2560, 3072
PROJ_COLS = _GT0 + 2 * LANES
_TQ0, _TV0, _TOG0, _TG0 = 0, 256, 768, 1280
TPROJ_ROWS = _TG0 + 2 * SUBLANES
PREP_TK = 256


def _prep_weights_kernel(w_ref, wo_ref, wall_ref, wt_ref, wout_ref):
    bf16 = jnp.bfloat16
    wout_ref[...] = wo_ref[...].astype(bf16)
    wf = w_ref[...]
    tk = wf.shape[1]
    g8 = wf[_IG0:_OG_IN]
    row = lax.broadcasted_iota(jnp.int32, (SUBLANES, tk), 0)
    g_in = jnp.where(row < NH, g8, 0.0)
    g_fg = jnp.where(row < NH, pltpu.roll(g8, SUBLANES - NH, 0), 0.0)

    wt_ref[_TQ0:_TV0, :] = wf[_Q0:_K0].astype(bf16)
    wt_ref[_TV0:_TOG0, :] = wf[_V0:_OG0].astype(bf16)
    wt_ref[_TOG0:_TG0, :] = wf[_OG_IN:_OG_IN + MLSTM_W].astype(bf16)
    wt_ref[_TG0:TPROJ_ROWS, :] = jnp.concatenate([g_in, g_fg], axis=0).astype(bf16)

    wall_ref[:, 0:_OG0] = wf[0:_IG0].T.astype(bf16)
    wall_ref[:, _OG0:_GT0] = wf[_OG_IN:PROJ_IN].T.astype(bf16)
    pad = jnp.zeros((LANES - SUBLANES, tk), jnp.float32)
    wall_ref[:, _GT0:_GT0 + LANES] = jnp.concatenate([g_in, pad], axis=0).T.astype(bf16)
    wall_ref[:, _GT0 + LANES:PROJ_COLS] = jnp.concatenate([g_fg, pad], axis=0).T.astype(bf16)


def _prep_weights(w_in, w_out):
    bf16 = jnp.bfloat16
    return pl.pallas_call(
        _prep_weights_kernel,
        grid=(D_MODEL // PREP_TK,),
        in_specs=[pl.BlockSpec((PROJ_IN, PREP_TK), lambda i: (0, i)),
                  pl.BlockSpec((PREP_TK, D_MODEL), lambda i: (i, 0))],
        out_specs=[pl.BlockSpec((PREP_TK, PROJ_COLS), lambda i: (i, 0)),
                   pl.BlockSpec((TPROJ_ROWS, PREP_TK), lambda i: (0, i)),
                   pl.BlockSpec((PREP_TK, D_MODEL), lambda i: (i, 0))],
        out_shape=[jax.ShapeDtypeStruct((D_MODEL, PROJ_COLS), bf16),
                   jax.ShapeDtypeStruct((TPROJ_ROWS, D_MODEL), bf16),
                   jax.ShapeDtypeStruct((D_MODEL, D_MODEL), bf16)],
        compiler_params=pltpu.CompilerParams(
            dimension_semantics=("arbitrary",), vmem_limit_bytes=VMEM_LIMIT),
        name="prep_weights",
    )(w_in.T, w_out)


ST_ROWS = DV + 2 * SUBLANES


def _lane_scan(x, seg, op, fill):
    pos = lax.broadcasted_iota(jnp.int32, x.shape, 1) & (seg - 1)
    k = 1
    while k < seg:
        x = op(x, jnp.where(pos >= k, pltpu.roll(x, k, 1), fill))
        k *= 2
    return x


PROJ_PIECE = 256


def _run(*phases):
    live = list(phases)
    while live:
        for g in list(live):
            try:
                next(g)
            except StopIteration:
                live.remove(g)


def _mixer_prompt_kernel(x_ref, gmix_ref, wn_ref, wt_ref, gbt_ref, gmhc_ref, cw_ref, wout_ref,
                         x1_ref, c_ref, n_ref, m_ref, cvo_ref,
                         qt_s, vt_s, ogt_s, k_s, mixt_s, z_s, ucol_s, wi_s, c2_s, emt_s, wk_s, ast_s, st_s,
                         *, L, TB, NSUB):
    j = pl.program_id(1)

    @pl.when(j == 0)
    def _():
        z_s[0, 0:SUBLANES, :] = jnp.zeros((SUBLANES, CONV_CH), jnp.float32)
        st_s[...] = jnp.zeros(st_s.shape, jnp.float32)
        m_ref[...] = jnp.zeros(m_ref.shape, jnp.float32)

    m_prev = m_ref[0]
    sub = [dict(sb=sb, x_ref=x_ref, gmix_ref=gmix_ref, wn_ref=wn_ref, wt_ref=wt_ref, gbt_ref=gbt_ref,
                gmhc_ref=gmhc_ref, cw_ref=cw_ref, wout_ref=wout_ref, x1_ref=x1_ref, qt_s=qt_s.at[sb],
                vt_s=vt_s.at[sb], ogt_s=ogt_s.at[sb], k_s=k_s.at[sb], mixt_s=mixt_s.at[sb], z_s=z_s,
                ucol_s=ucol_s.at[sb], wi_s=wi_s.at[sb], c2_s=c2_s.at[sb], emt_s=emt_s.at[sb],
                wk_s=wk_s.at[sb], ast_s=ast_s.at[sb], st_s=st_s, L=L, TS=TB) for sb in range(NSUB)]
    carry = {"m": m_prev}
    _run(_prompt_project(sub[0]))
    for a, b in zip(sub[:-1], sub[1:]):
        _run(_prompt_chunks(a, carry), _prompt_project(b))
    _run(_prompt_chunks(sub[-1], carry), *[_prompt_output(s) for s in sub[:-1]])
    _run(_prompt_output(sub[-1]))
    m_ref[0] = carry["m"]
    cvo_ref[0] = z_s[NSUB - 1, TB + SUBLANES - (CONV_W - 1):TB + SUBLANES, :]
    z_s[0, 0:SUBLANES, :] = z_s[NSUB - 1, TB:TB + SUBLANES, :]

    @pl.when(j == pl.num_programs(1) - 1)
    def _():
        for pr in range(NH // 2):
            state = st_s[pr]
            c_pair = state[0:DV, :].T
            c_ref[0, 2 * pr] = c_pair[0:DK, :]
            c_ref[0, 2 * pr + 1] = c_pair[DK:, :]
            n_ref[0, pr:pr + 1, :] = state[DV:DV + 1, :]


def _prompt_project(s):
    sb, L, TS = s["sb"], s["L"], s["TS"]
    x_ref, gmix_ref, wn_ref, wt_ref, gbt_ref, cw_ref = (
        s["x_ref"], s["gmix_ref"], s["wn_ref"], s["wt_ref"], s["gbt_ref"], s["cw_ref"])
    qt_s, vt_s, ogt_s, k_s, z_s = s["qt_s"], s["vt_s"], s["ogt_s"], s["k_s"], s["z_s"]
    bf16 = jnp.bfloat16

    x = x_ref[0, sb * TS:(sb + 1) * TS, :]
    a = _rms(x, gmix_ref[...]).astype(bf16)
    nt_dims = (((1,), (1,)), ((), ()))

    gt = lax.dot_general(wt_ref[_TG0:TPROJ_ROWS, :], a, nt_dims,
                         preferred_element_type=jnp.float32) + gbt_ref[...]
    ic = GATE_CAP * jnp.tanh(gt[0:SUBLANES] / GATE_CAP)
    lf = _log_sigmoid(gt[SUBLANES:])
    b = _lane_scan(lf, L, jnp.add, 0.0)
    u = ic - b
    m_loc = b + _lane_scan(u, L, jnp.maximum, -jnp.inf)
    s.update(b=b, u=u, ic=ic, m_loc=m_loc)
    yield

    k_s[...] = _bdot(a, wn_ref[:, _K0:_V0])
    yield
    hc = _bdot(a, wn_ref[:, _HC0:_GT0])
    yield
    z = _bdot(a, wn_ref[:, _CG0:_HC0]) * hc
    if sb > 0:
        z_s[sb, 0:SUBLANES, :] = z_s[sb - 1, TS:TS + SUBLANES, :]
    z_s[sb, SUBLANES:SUBLANES + TS, :] = z
    yield
    yconv = (cw_ref[0:1, :] * z_s[sb, SUBLANES - 2:SUBLANES - 2 + TS, :]
             + cw_ref[1:2, :] * z_s[sb, SUBLANES - 1:SUBLANES - 1 + TS, :]
             + cw_ref[2:3, :] * z)
    s["yc"] = (_bdot(a, wn_ref[:, _BG0:_CG0]) * yconv).astype(bf16)
    yield

    for r0 in range(0, _TG0, PROJ_PIECE):
        ut = lax.dot_general(wt_ref[r0:r0 + PROJ_PIECE, :], a, nt_dims, preferred_element_type=jnp.float32)
        if r0 < _TV0:
            qt_s[r0:r0 + PROJ_PIECE, :] = ut * (DK ** -0.5)
        elif r0 < _TOG0:
            vt_s[r0 - _TV0:r0 - _TV0 + PROJ_PIECE, :] = ut
        else:
            ogt_s[r0 - _TOG0:r0 - _TOG0 + PROJ_PIECE, :] = ut
        yield


def _prompt_chunks(s, carry):
    L, TS = s["L"], s["TS"]
    qt_s, vt_s, ogt_s, k_s, mixt_s, ucol_s, wi_s, c2_s, emt_s, wk_s, ast_s, st_s = (
        s["qt_s"], s["vt_s"], s["ogt_s"], s["k_s"], s["mixt_s"], s["ucol_s"], s["wi_s"],
        s["c2_s"], s["emt_s"], s["wk_s"], s["ast_s"], s["st_s"])
    b, u, ic, m_loc = s["b"], s["u"], s["ic"], s["m_loc"]
    nc = TS // L
    bf16 = jnp.bfloat16
    m_prev = carry["m"]

    for c in range(nc):
        sl = slice(c * L, (c + 1) * L)
        bc, mlc = b[:, sl], m_loc[:, sl]
        b_last = jnp.broadcast_to(bc[:, L - 1:L], bc.shape)
        m_new = jnp.maximum(b_last + m_prev, jnp.broadcast_to(mlc[:, L - 1:L], bc.shape))
        g = bc + m_prev
        mt = jnp.maximum(g, mlc)
        wi_s[:, sl] = jnp.exp(g - mt)
        c2_s[:, sl] = mt - bc
        emt_s[:, sl] = jnp.exp(-mt)
        wk_s[:, sl] = jnp.exp(b_last - bc + ic[:, sl] - m_new)
        ast_s[c] = jnp.exp(b_last + m_prev - m_new)
        upad = jnp.concatenate([u[:, sl], jnp.zeros((L - SUBLANES, L), jnp.float32)], axis=0)
        ucol_s[c * L:(c + 1) * L, :] = upad.T
        m_prev = m_new
    carry["m"] = m_prev
    yield

    s_i = lax.broadcasted_iota(jnp.int32, (L, L), 0)
    t_i = lax.broadcasted_iota(jnp.int32, (L, L), 1)
    causal = s_i <= t_i
    low_half = lax.broadcasted_iota(jnp.int32, (L, LANES), 1) < DK
    zeros_q = jnp.zeros((DK, L), jnp.float32)
    zeros_p = jnp.zeros((L, L), bf16)

    def pair_row(ref, h0, rows):
        return jnp.concatenate([ref[h0:h0 + 1, rows], ref[h0 + 1:h0 + 2, rows]], axis=1)

    for c in range(nc):
        rows = slice(c * L, (c + 1) * L)
        for pr in range(NH // 2):
            h0 = 2 * pr
            hv0 = slice(h0 * DV, (h0 + 1) * DV)
            hv1 = slice((h0 + 1) * DV, (h0 + 2) * DV)
            kp = k_s[rows, pr * LANES:(pr + 1) * LANES]
            q0 = qt_s[h0 * DK:(h0 + 1) * DK, rows]
            q1 = qt_s[(h0 + 1) * DK:(h0 + 2) * DK, rows]
            qbd = jnp.concatenate([jnp.concatenate([q0, zeros_q], axis=1),
                                   jnp.concatenate([zeros_q, q1], axis=1)], axis=0).astype(bf16)
            st = _bdot(kp.astype(bf16), qbd)
            arg = jnp.concatenate(
                [jnp.where(causal, ucol_s[rows, h:h + 1] - c2_s[h:h + 1, rows], -jnp.inf)
                 for h in (h0, h0 + 1)], axis=1)
            pt = st * jnp.exp(arg)
            rs = jnp.sum(pt, axis=0, keepdims=True)
            ptb = pt.astype(bf16)
            pbd = jnp.concatenate([jnp.concatenate([ptb[:, 0:L], zeros_p], axis=1),
                                   jnp.concatenate([zeros_p, ptb[:, L:]], axis=1)], axis=0)
            vt = jnp.concatenate([vt_s[hv0, rows], vt_s[hv1, rows]], axis=1)
            state = st_s[pr]
            sq = _bdot(state.astype(bf16), qbd) * pair_row(wi_s, h0, rows)
            num = _bdot(vt.astype(bf16), pbd) + sq[0:DV]
            den = sq[DV:DV + 1] + rs
            hh = num * (1.0 / jnp.maximum(jnp.abs(den), pair_row(emt_s, h0, rows)))
            hn = hh * lax.rsqrt(jnp.mean(hh * hh, axis=0, keepdims=True) + EPS)
            mixt_s[hv0, rows] = jax.nn.sigmoid(ogt_s[hv0, rows]) * hn[:, 0:L]
            mixt_s[hv1, rows] = jax.nn.sigmoid(ogt_s[hv1, rows]) * hn[:, L:]
            wkr = pair_row(wk_s, h0, rows)
            vw = jnp.concatenate([vt * wkr, jnp.broadcast_to(wkr, (2 * SUBLANES, 2 * L))], axis=0)
            km = jnp.concatenate([jnp.where(low_half, kp, 0.0), jnp.where(low_half, 0.0, kp)], axis=0)
            decay = jnp.where(low_half[0:1, :], ast_s[c][h0:h0 + 1, :], ast_s[c][h0 + 1:h0 + 2, :])
            st_s[pr] = decay * state + _bdot(vw.astype(bf16), km.astype(bf16))
            yield


def _prompt_output(s):
    sb, TS = s["sb"], s["TS"]
    x_ref, wout_ref, x1_ref, mixt_s = s["x_ref"], s["wout_ref"], s["x1_ref"], s["mixt_s"]
    tok = slice(sb * TS, (sb + 1) * TS)
    hm = mixt_s[...].T * s["gmhc_ref"][...]
    mix = jnp.concatenate([hm.astype(jnp.bfloat16), s["yc"]], axis=1)
    yield
    for c0 in range(0, D_MODEL, PROJ_PIECE):
        cols = slice(c0, c0 + PROJ_PIECE)
        x1_ref[0, tok, cols] = x_ref[0, tok, cols] + _bdot(mix, wout_ref[:, cols])
        yield


def _mixer_prompt(x, weights, TB, NSUB, L):
    B, S, _ = x.shape
    TS = TB * NSUB
    nb = S // TS
    f32 = jnp.float32
    kern = functools.partial(_mixer_prompt_kernel, L=L, TB=TB, NSUB=NSUB)
    return pl.pallas_call(
        kern,
        grid=(B, nb),
        in_specs=[
            pl.BlockSpec((1, TS, D_MODEL), lambda b, j: (b, j, 0)),
            _const_spec((1, D_MODEL)),
            _const_spec((D_MODEL, PROJ_COLS)),
            _const_spec((TPROJ_ROWS, D_MODEL)),
            _const_spec((2 * SUBLANES, 1)),
            _const_spec((1, MLSTM_W)),
            _const_spec((CONV_W, CONV_CH)),
            _const_spec((D_MODEL, D_MODEL)),
        ],
        out_specs=[
            pl.BlockSpec((1, TS, D_MODEL), lambda b, j: (b, j, 0)),
            pl.BlockSpec((1, NH, DK, DV), lambda b, j: (b, 0, 0, 0)),
            pl.BlockSpec((1, NH // 2, 2 * DK), lambda b, j: (b, 0, 0)),
            pl.BlockSpec((1, SUBLANES, LANES), lambda b, j: (b, 0, 0)),
            pl.BlockSpec((1, CONV_W - 1, CONV_CH), lambda b, j: (b, 0, 0)),
        ],
        out_shape=[
            jax.ShapeDtypeStruct((B, S, D_MODEL), f32),
            jax.ShapeDtypeStruct((B, NH, DK, DV), f32),
            jax.ShapeDtypeStruct((B, NH // 2, 2 * DK), f32),
            jax.ShapeDtypeStruct((B, SUBLANES, LANES), f32),
            jax.ShapeDtypeStruct((B, CONV_W - 1, CONV_CH), f32),
        ],
        scratch_shapes=[
            pltpu.VMEM((NSUB, QK_W, TB), f32),
            pltpu.VMEM((NSUB, MLSTM_W, TB), f32),
            pltpu.VMEM((NSUB, MLSTM_W, TB), f32),
            pltpu.VMEM((NSUB, TB, QK_W), f32),
            pltpu.VMEM((NSUB, MLSTM_W, TB), f32),
            pltpu.VMEM((NSUB, TB + SUBLANES, CONV_CH), f32),
            pltpu.VMEM((NSUB, TB, LANES), f32),
            pltpu.VMEM((NSUB, SUBLANES, TB), f32),
            pltpu.VMEM((NSUB, SUBLANES, TB), f32),
            pltpu.VMEM((NSUB, SUBLANES, TB), f32),
            pltpu.VMEM((NSUB, SUBLANES, TB), f32),
            pltpu.VMEM((NSUB, TB // L, SUBLANES, LANES), f32),
            pltpu.VMEM((NH // 2, ST_ROWS, LANES), f32),
        ],
        compiler_params=pltpu.CompilerParams(
            dimension_semantics=("arbitrary", "arbitrary"), vmem_limit_bytes=VMEM_LIMIT),
        name="mixer_prompt",
    )(x, *weights)


def _row_scan(x, seg, op, fill):
    pos = lax.broadcasted_iota(jnp.int32, x.shape, 0) & (seg - 1)
    k = 1
    while k < seg:
        x = op(x, jnp.where(pos >= k, pltpu.roll(x, k, 0), fill))
        k *= 2
    return x


def _split_dot(x, sel, parts):
    acc = None
    rem = x
    for p in range(parts):
        hi = rem.astype(jnp.bfloat16)
        d = _bdot(hi, sel)
        acc = d if acc is None else acc + d
        if p + 1 < parts:
            rem = rem - hi.astype(jnp.float32)
    return acc


def _selectors():
    h = np.arange(NH)
    seg_qk = np.zeros((QK_W, LANES), np.float32)
    seg_qk[np.arange(QK_W), np.arange(QK_W) // DK] = 1.0
    exp_v = np.zeros((LANES, MLSTM_W), np.float32)
    exp_k = np.zeros((LANES, QK_W), np.float32)
    for i in h:
        exp_v[i, i * DV:(i + 1) * DV] = 1.0
        exp_k[i, i * DK:(i + 1) * DK] = 1.0
    mean_v = np.kron(np.eye(NH, dtype=np.float32), np.full((DV, DV), 1.0 / DV, np.float32))
    return tuple(jnp.asarray(m, jnp.bfloat16) for m in (seg_qk, exp_v, exp_k, mean_v))


def _mixer_sample_kernel(x_ref, cv_ref, c0_ref, n0_ref, m0_ref,
                         gmix_ref, win_ref, gb_ref, gmh_ref, cw_ref, wout_ref,
                         segqk_ref, expv_ref, expk_ref, meanv_ref,
                         x1_ref, cst_ref, nst_ref, mst_ref, cvo_ref,
                         xp_s, z_s, mp_s, qc_s, *, NSEQ):
    R = SUBLANES
    TS = NSEQ * R
    bf16 = jnp.bfloat16
    f32 = jnp.float32

    xp_s[:, SAMPLE_T:, :] = jnp.zeros((NSEQ, R - SAMPLE_T, D_MODEL), f32)
    xp_s[:, 0:SAMPLE_T, :] = x_ref[...]
    x = xp_s[...].reshape(TS, D_MODEL)
    a = _rms(x, gmix_ref[...]).astype(bf16)
    u = _bdot(a, win_ref[...])

    pos = lax.broadcasted_iota(jnp.int32, (TS, 1), 0) & (R - 1)
    real = pos < SAMPLE_T

    z = u[:, _CG0:_HC0] * u[:, _HC0:_GT0]
    z3 = z.reshape(NSEQ, R, CONV_CH)
    z_s[...] = z3
    z_s[:, R - (CONV_W - 1):, :] = cv_ref[...]
    zf = z_s[...].reshape(TS, CONV_CH)
    zm1 = jnp.where(pos >= 1, pltpu.roll(z, 1, 0), pltpu.roll(zf, TS - (R - 1), 0))
    zm2 = jnp.where(pos >= 2, pltpu.roll(z, 2, 0), pltpu.roll(zf, TS - (R - 2), 0))
    yc = u[:, _BG0:_CG0] * (cw_ref[0:1, :] * zm2 + cw_ref[1:2, :] * zm1 + cw_ref[2:3, :] * z)
    cvo_ref[...] = z3[:, SAMPLE_T - (CONV_W - 1):SAMPLE_T, :]

    gates = u[:, _GT0:PROJ_COLS] + gb_ref[...]
    ic = jnp.where(real, GATE_CAP * jnp.tanh(gates[:, :LANES] / GATE_CAP), NEG_BIG)
    lf = jnp.where(real, _log_sigmoid(gates[:, LANES:]), 0.0)
    b = _row_scan(lf, R, jnp.add, 0.0)
    uu = ic - b
    m_loc = b + _row_scan(uu, R, jnp.maximum, -jnp.inf)

    mp_s[...] = jnp.zeros(mp_s.shape, f32)
    mp_s[:, :, 0:NH] = m0_ref[...]
    m_prev = mp_s[...]
    b3 = b.reshape(NSEQ, R, LANES)
    ml3 = m_loc.reshape(NSEQ, R, LANES)
    b_last = b3[:, R - 1:R, :]
    m_new = jnp.maximum(b_last + m_prev, ml3[:, R - 1:R, :])
    g3 = b3 + m_prev
    mt3 = jnp.maximum(g3, ml3)
    wi = jnp.exp(g3 - mt3).reshape(TS, LANES)
    c2 = (mt3 - b3).reshape(TS, LANES)
    emt = jnp.exp(-mt3).reshape(TS, LANES)
    wk = jnp.exp(b_last - b3 + ic.reshape(NSEQ, R, LANES) - m_new).reshape(TS, LANES)
    a_st = jnp.broadcast_to(jnp.exp(b_last + m_prev - m_new), (NSEQ, R, LANES)).reshape(TS, LANES)
    mst_ref[...] = m_new[:, :, 0:NH]

    q = u[:, _Q0:_K0] * (DK ** -0.5)
    k = u[:, _K0:_V0]
    v = u[:, _V0:_OG0]
    acc = {"rs": jnp.zeros((TS, LANES), f32), "num": jnp.zeros((TS, MLSTM_W), f32)}

    def pair_step(d):
        kd = k if d == 0 else pltpu.roll(k, d, 0)
        vd = v if d == 0 else pltpu.roll(v, d, 0)
        ud = uu if d == 0 else pltpu.roll(uu, d, 0)
        p = _split_dot(q * kd, segqk_ref[...], 2) * jnp.exp(ud - c2)
        acc["rs"] = acc["rs"] + p
        acc["num"] = acc["num"] + _split_dot(p, expv_ref[...], 2) * vd

    n0x = jnp.broadcast_to(n0_ref[...], (NSEQ, R, QK_W)).reshape(TS, QK_W)
    qn = _split_dot(q * n0x, segqk_ref[...], 2)
    kw = k * _split_dot(wk, expk_ref[...], 2)
    ax = _split_dot(a_st, expv_ref[...], 3)
    lane_q = lax.broadcasted_iota(jnp.int32, (R, QK_W), 1)
    group = NSEQ // SAMPLE_T
    for i in range(NSEQ):
        if i % group == 0:
            pair_step(i // group)
        rows = slice(i * R, (i + 1) * R)
        qi = q[rows, :]
        lhs = jnp.concatenate(
            [jnp.where((lane_q >= h * DK) & (lane_q < (h + 1) * DK), qi, 0.0) for h in range(NH)], axis=0)
        c0 = c0_ref[i]
        qc = _bdot(lhs.astype(bf16), c0.reshape(QK_W, DV).astype(bf16))
        qc_s[rows, :] = jnp.concatenate([qc[h * R:(h + 1) * R, :] for h in range(NH)], axis=1)
        kwi = kw[rows, :].astype(bf16)
        vi = v[rows, :].astype(bf16)
        for h in range(NH):
            dc = lax.dot_general(kwi[:, h * DK:(h + 1) * DK], vi[:, h * DV:(h + 1) * DV],
                                 (((0,), (0,)), ((), ())), preferred_element_type=f32)
            cst_ref[i, h] = ax[i * R:i * R + 1, h * DV:(h + 1) * DV] * c0[h] + dc

    rs, num = acc["rs"], acc["num"]
    den = wi * qn + rs
    rden = 1.0 / jnp.maximum(jnp.abs(den), emt)
    hh = (_split_dot(wi, expv_ref[...], 2) * qc_s[...] + num) * _split_dot(rden, expv_ref[...], 2)
    ms = _split_dot(hh * hh, meanv_ref[...], 2)
    hm = jax.nn.sigmoid(u[:, _OG0:_BG0]) * (hh * lax.rsqrt(ms + EPS) * gmh_ref[...])
    mix = jnp.concatenate([hm, yc], axis=1).astype(bf16)
    out = x + _bdot(mix, wout_ref[...])
    x1_ref[...] = out.reshape(NSEQ, R, D_MODEL)[:, 0:SAMPLE_T, :]

    a_k = _split_dot(a_st, expk_ref[...], 3).reshape(NSEQ, R, QK_W)[:, 0:1, :]
    nst_ref[...] = a_k * n0_ref[...] + jnp.sum(kw.reshape(NSEQ, R, QK_W), axis=1, keepdims=True)


def _mixer_sample(x, cv, c0, n0, m0, weights, NSEQ):
    Bs = x.shape[0]
    f32 = jnp.float32
    kern = functools.partial(_mixer_sample_kernel, NSEQ=NSEQ)
    x_spec = pl.BlockSpec((NSEQ, SAMPLE_T, D_MODEL), lambda t: (t, 0, 0))
    cv_spec = pl.BlockSpec((NSEQ, CONV_W - 1, CONV_CH), lambda t: (t, 0, 0))
    c_spec = pl.BlockSpec((NSEQ, NH, DK, DV), lambda t: (t, 0, 0, 0))
    n_spec = pl.BlockSpec((NSEQ, 1, QK_W), lambda t: (t, 0, 0))
    m_spec = pl.BlockSpec((NSEQ, 1, NH), lambda t: (t, 0, 0))
    return pl.pallas_call(
        kern,
        grid=(Bs // NSEQ,),
        in_specs=[
            x_spec, cv_spec, c_spec, n_spec, m_spec,
            _const_spec((1, D_MODEL)),
            _const_spec((D_MODEL, PROJ_COLS)),
            _const_spec((1, 2 * LANES)),
            _const_spec((1, MLSTM_W)),
            _const_spec((CONV_W, CONV_CH)),
            _const_spec((D_MODEL, D_MODEL)),
            _const_spec((QK_W, LANES)),
            _const_spec((LANES, MLSTM_W)),
            _const_spec((LANES, QK_W)),
            _const_spec((MLSTM_W, MLSTM_W)),
        ],
        out_specs=[x_spec, c_spec, n_spec, m_spec, cv_spec],
        out_shape=[
            jax.ShapeDtypeStruct((Bs, SAMPLE_T, D_MODEL), f32),
            jax.ShapeDtypeStruct((Bs, NH, DK, DV), f32),
            jax.ShapeDtypeStruct((Bs, 1, QK_W), f32),
            jax.ShapeDtypeStruct((Bs, 1, NH), f32),
            jax.ShapeDtypeStruct((Bs, CONV_W - 1, CONV_CH), f32),
        ],
        scratch_shapes=[
            pltpu.VMEM((NSEQ, SUBLANES, D_MODEL), f32),
            pltpu.VMEM((NSEQ, SUBLANES, CONV_CH), f32),
            pltpu.VMEM((NSEQ, 1, LANES), f32),
            pltpu.VMEM((NSEQ * SUBLANES, MLSTM_W), f32),
        ],
        compiler_params=pltpu.CompilerParams(
            dimension_semantics=("arbitrary",), vmem_limit_bytes=VMEM_LIMIT),
        name="mixer_sample",
    )(x, cv, c0, n0, m0, *weights)


def _mixer_weights(g_mix, w_in, b_ig, b_fg, g_mh, conv_w, w_out):
    f32 = jnp.float32
    w_all, wt, wout = _prep_weights(w_in, w_out)
    gmix = g_mix.reshape(1, D_MODEL)
    zeros = jnp.zeros((SUBLANES - NH,), f32)
    gb_col = jnp.concatenate([b_ig, zeros, b_fg, zeros]).reshape(2 * SUBLANES, 1)
    zeros = jnp.zeros((LANES - NH,), f32)
    gb_row = jnp.concatenate([b_ig, zeros, b_fg, zeros]).reshape(1, 2 * LANES)
    gmh = g_mh.reshape(1, MLSTM_W)
    prompt = (gmix, w_all, wt, gb_col, gmh, conv_w, wout)
    sample = (gmix, w_all, gb_row, gmh, conv_w, wout) + _selectors()
    return prompt, sample


FF_STREAM = 256
_N_FF_PIECES = D_FF // FF_STREAM
_N_PIECES = _N_FF_PIECES + PLE_DIM // FF_STREAM + D_MODEL // FF_STREAM


def _ffn_rows(x, pe, gffn_ref, gple_ref, gfin_ref, wg_s, wu_s, wd_s, wple_s, wpg_s, *, chunk, final_norm,
              before_chunk=None, before_tail=None):
    bf16 = jnp.bfloat16
    f = _rms(x, gffn_ref[...]).astype(bf16)
    e = None
    if before_tail is None:
        e = _rms(_bdot(pe.astype(bf16), wple_s[...]), gple_ref[...])
    for c0 in range(0, D_FF, chunk):
        c1 = min(c0 + chunk, D_FF)
        if before_chunk is not None:
            before_chunk(c0 // chunk)
        gate = _bdot(f, wg_s[:, c0:c1])
        up = _bdot(f, wu_s[:, c0:c1])
        hmid = (gate * jax.nn.sigmoid(gate) * up).astype(bf16)
        x = x + _bdot(hmid, wd_s[c0:c1, :])
    if before_tail is not None:
        before_tail()
        e = _rms(_bdot(pe.astype(bf16), wple_s[...]), gple_ref[...])
    xb = x.astype(bf16)
    n_rows = x.shape[0]
    outs = []
    for r0 in range(0, n_rows, n_rows // FFN_TAIL_PIECES):
        rows = slice(r0, r0 + n_rows // FFN_TAIL_PIECES)
        xr = x[rows] + jax.nn.sigmoid(_bdot(xb[rows], wpg_s[...])) * e[rows]
        outs.append(_rms(xr, gfin_ref[...]) if final_norm else xr)
    return jnp.concatenate(outs, axis=0)


def _ffn_stream_kernel(xp_ref, pp_ref, xs_ref, ps_ref, gffn_ref, gple_ref, gfin_ref,
                       wg_hbm, wu_hbm, wd_hbm, wple_hbm, wpg_hbm,
                       yp_ref, ys_ref,
                       wg_s, wu_s, wd_s, wple_s, wpg_s, stg_g, stg_u, stg_d, sem, *, final_norm):
    t = pl.program_id(0)
    bf16 = jnp.bfloat16
    weights = (wg_s, wu_s, wd_s, wple_s, wpg_s)

    def copies(k):
        slot = k % 2
        if k < _N_FF_PIECES:
            cols = pl.ds(k * FF_STREAM, FF_STREAM)
            return [pltpu.make_async_copy(wg_hbm.at[:, cols], stg_g.at[slot], sem.at[0, slot]),
                    pltpu.make_async_copy(wu_hbm.at[:, cols], stg_u.at[slot], sem.at[1, slot]),
                    pltpu.make_async_copy(wd_hbm.at[cols, :], stg_d.at[slot], sem.at[2, slot])]
        if k == _N_FF_PIECES:
            return [pltpu.make_async_copy(wple_hbm, stg_d.at[slot], sem.at[2, slot])]
        rows = pl.ds((k - _N_FF_PIECES - 1) * FF_STREAM, FF_STREAM)
        return [pltpu.make_async_copy(wpg_hbm.at[rows, :], stg_d.at[slot], sem.at[2, slot])]

    def land(k):
        if k + 1 < _N_PIECES:
            for cp in copies(k + 1):
                cp.start()
        for cp in copies(k):
            cp.wait()
        slot = k % 2
        if k < _N_FF_PIECES:
            cols = slice(k * FF_STREAM, (k + 1) * FF_STREAM)
            wg_s[:, cols] = stg_g[slot].astype(bf16)
            wu_s[:, cols] = stg_u[slot].astype(bf16)
            wd_s[cols, :] = stg_d[slot].astype(bf16)
        elif k == _N_FF_PIECES:
            wple_s[...] = stg_d[slot].astype(bf16)
        else:
            r0 = (k - _N_FF_PIECES - 1) * FF_STREAM
            wpg_s[r0:r0 + FF_STREAM, :] = stg_d[slot].astype(bf16)

    @pl.when(t == 0)
    def _():
        for cp in copies(0):
            cp.start()
        n_tok = xs_ref.shape[0] * xs_ref.shape[1]

        def tail():
            for k in range(_N_FF_PIECES, _N_PIECES):
                land(k)

        y = _ffn_rows(xs_ref[...].reshape(n_tok, D_MODEL), ps_ref[...].reshape(n_tok, PLE_DIM),
                      gffn_ref, gple_ref, gfin_ref, *weights, chunk=FF_STREAM, final_norm=final_norm,
                      before_chunk=land, before_tail=tail)
        ys_ref[...] = y.reshape(ys_ref.shape)

    @pl.when(t > 0)
    def _():
        yp_ref[...] = _ffn_rows(xp_ref[...], pp_ref[...], gffn_ref, gple_ref, gfin_ref, *weights,
                                chunk=FF_CHUNK, final_norm=final_norm)


def _ffn_stream(xp, pp, xs, ps, g_ffn, g_ple, g_final, w_gate, w_up, w_down, w_ple, w_pg, TM, final_norm):
    T = xp.shape[0]
    assert T % TM == 0 and D_FF % FF_STREAM == 0 and D_MODEL % FF_STREAM == 0
    assert w_ple.shape == (FF_STREAM, D_MODEL)
    f32, bf16 = jnp.float32, jnp.bfloat16
    kern = functools.partial(_ffn_stream_kernel, final_norm=final_norm)
    row_map = lambda t: (jnp.maximum(t - 1, 0), 0)
    hbm = pl.BlockSpec(memory_space=pl.ANY)
    return pl.pallas_call(
        kern,
        grid=(T // TM + 1,),
        in_specs=[
            pl.BlockSpec((TM, D_MODEL), row_map),
            pl.BlockSpec((TM, PLE_DIM), row_map),
            _const_spec(xs.shape),
            _const_spec(ps.shape),
            _const_spec((1, D_MODEL)),
            _const_spec((1, D_MODEL)),
            _const_spec((1, D_MODEL)),
            hbm, hbm, hbm, hbm, hbm,
        ],
        out_specs=[pl.BlockSpec((TM, D_MODEL), row_map),
                   pl.BlockSpec(xs.shape, lambda t: (0, 0, 0))],
        out_shape=[jax.ShapeDtypeStruct(xp.shape, f32), jax.ShapeDtypeStruct(xs.shape, f32)],
        scratch_shapes=[
            pltpu.VMEM((D_MODEL, D_FF), bf16),
            pltpu.VMEM((D_MODEL, D_FF), bf16),
            pltpu.VMEM((D_FF, D_MODEL), bf16),
            pltpu.VMEM((PLE_DIM, D_MODEL), bf16),
            pltpu.VMEM((D_MODEL, D_MODEL), bf16),
            pltpu.VMEM((2, D_MODEL, FF_STREAM), f32),
            pltpu.VMEM((2, D_MODEL, FF_STREAM), f32),
            pltpu.VMEM((2, FF_STREAM, D_MODEL), f32),
            pltpu.SemaphoreType.DMA((3, 2)),
        ],
        compiler_params=pltpu.CompilerParams(
            dimension_semantics=("arbitrary",), vmem_limit_bytes=VMEM_LIMIT),
        name="ffn",
    )(xp, pp, xs, ps, g_ffn.reshape(1, D_MODEL), g_ple.reshape(1, D_MODEL), g_final.reshape(1, D_MODEL),
      w_gate, w_up, w_down, w_ple, w_pg)


def kernel(x_prompt, x_sample, p_prompt, p_sample, state_C, state_n, state_m, state_conv,
           g_mix, w_in, b_ig, b_fg, g_mh, conv_w, w_out, g_ffn, w_gate, w_up, w_down,
           w_ple, g_ple, w_pg, g_final):
    bf16 = jnp.bfloat16
    depth = g_mix.shape[0]
    B, S, _ = x_prompt.shape
    Bs, Ss, _ = x_sample.shape
    assert Ss == SAMPLE_T and S % PROMPT_TS == 0 and Bs % SAMPLE_NSEQ == 0

    xp = x_prompt
    xs = x_sample
    outs = [[] for _ in range(8)]
    for i in range(depth):
        last = i == depth - 1
        pw, sw = _mixer_weights(g_mix[i], w_in[i], b_ig[i], b_fg[i], g_mh[i], conv_w[i], w_out[i])
        x1p, cp, n_p, mp, cvp = _mixer_prompt(xp, pw, TB=PROMPT_TS, NSUB=PROMPT_NSUB, L=PROMPT_L)
        x1s, cs, n_s, ms, cvs = _mixer_sample(
            xs, state_conv[i], state_C[i], state_n[i].reshape(Bs, 1, QK_W),
            state_m[i].reshape(Bs, 1, NH), sw, SAMPLE_NSEQ)
        xp, xs = _ffn_stream(x1p.reshape(B * S, D_MODEL), p_prompt[i].reshape(B * S, PLE_DIM), x1s,
                             p_sample[i], g_ffn[i], g_ple[i], g_final, w_gate[i], w_up[i], w_down[i],
                             w_ple[i], w_pg[i], FFN_TM, last)
        xp = xp.reshape(B, S, D_MODEL)

        new = (cp, n_p.reshape(B, NH, DK), mp[:, :NH, 0], cvp,
               cs, n_s.reshape(Bs, NH, DK), ms.reshape(Bs, NH), cvs)
        for lst, v in zip(outs, new):
            lst.append(v)

    return (xp, xs) + tuple(jnp.stack(l) for l in outs)
```

```python
import functools

import numpy as np
import jax
import jax.numpy as jnp
from jax import lax
from jax.experimental import pallas as pl
from jax.experimental.pallas import tpu as pltpu

D_MODEL = 1024
NH = 4
DK = 64
DV = 128
MLSTM_W = NH * DV
QK_W = NH * DK
CONV_CH = 512
CONV_W = 3
D_FF = 2816
FF_CHUNK = 768
FFN_TM = 512
FFN_TAIL_PIECES = 2
PLE_DIM = 256
PROMPT_L = 128
PROMPT_TS = 512
PROMPT_NSUB = 2
SAMPLE_T = 4
SAMPLE_NSEQ = 32
GATE_CAP = 15.0
EPS = 1e-6

LANES = 128
SUBLANES = 8
NEG_BIG = -1e30

VMEM_LIMIT = 56 * 1024 * 1024


def _rms(x, g):
    ms = jnp.mean(x * x, axis=-1, keepdims=True)
    return x * lax.rsqrt(ms + EPS) * g


def _bdot(a, b):
    return jnp.dot(a, b, preferred_element_type=jnp.float32)


def _log_sigmoid(x):
    return jnp.minimum(x, 0.0) - jnp.log1p(jnp.exp(-jnp.abs(x)))


def _const_spec(shape):
    nd = len(shape)
    return pl.BlockSpec(shape, lambda *_: (0,) * nd, pipeline_mode=pl.Buffered(1))


_IG0 = 2 * QK_W + MLSTM_W
_OG_IN = _IG0 + 2 * NH
PROJ_IN = _OG_IN + MLSTM_W + 3 * CONV_CH
_Q0, _K0, _V0, _OG0, _BG0, _CG0, _HC0, _GT0 = 0, 256, 512, 1024, 1536, 2048, 2560, 3072
PROJ_COLS = _GT0 + 2 * LANES
_TQ0, _TV0, _TOG0, _TG0 = 0, 256, 768, 1280
TPROJ_ROWS = _TG0 + 2 * SUBLANES
PREP_TK = 256


def _prep_weights_kernel(w_ref, wo_ref, wall_ref, wt_ref, wout_ref):
    bf16 = jnp.bfloat16
    wout_ref[...] = wo_ref[...].astype(bf16)
    wf = w_ref[...]
    tk = wf.shape[1]
    g8 = wf[_IG0:_OG_IN]
    row = lax.broadcasted_iota(jnp.int32, (SUBLANES, tk), 0)
    g_in = jnp.where(row < NH, g8, 0.0)
    g_fg = jnp.where(row < NH, pltpu.roll(g8, SUBLANES - NH, 0), 0.0)

    wt_ref[_TQ0:_TV0, :] = wf[_Q0:_K0].astype(bf16)
    wt_ref[_TV0:_TOG0, :] = wf[_V0:_OG0].astype(bf16)
    wt_ref[_TOG0:_TG0, :] = wf[_OG_IN:_OG_IN + MLSTM_W].astype(bf16)
    wt_ref[_TG0:TPROJ_ROWS, :] = jnp.concatenate([g_in, g_fg], axis=0).astype(bf16)

    wall_ref[:, 0:_OG0] = wf[0:_IG0].T.astype(bf16)
    wall_ref[:, _OG0:_GT0] = wf[_OG_IN:PROJ_IN].T.astype(bf16)
    pad = jnp.zeros((LANES - SUBLANES, tk), jnp.float32)
    wall_ref[:, _GT0:_GT0 + LANES] = jnp.concatenate([g_in, pad], axis=0).T.astype(bf16)
    wall_ref[:, _GT0 + LANES:PROJ_COLS] = jnp.concatenate([g_fg, pad], axis=0).T.astype(bf16)


def _prep_weights(w_in, w_out):
    bf16 = jnp.bfloat16
    return pl.pallas_call(
        _prep_weights_kernel,
        grid=(D_MODEL // PREP_TK,),
        in_specs=[pl.BlockSpec((PROJ_IN, PREP_TK), lambda i: (0, i)),
                  pl.BlockSpec((PREP_TK, D_MODEL), lambda i: (i, 0))],
        out_specs=[pl.BlockSpec((PREP_TK, PROJ_COLS), lambda i: (i, 0)),
                   pl.BlockSpec((TPROJ_ROWS, PREP_TK), lambda i: (0, i)),
                   pl.BlockSpec((PREP_TK, D_MODEL), lambda i: (i, 0))],
        out_shape=[jax.ShapeDtypeStruct((D_MODEL, PROJ_COLS), bf16),
                   jax.ShapeDtypeStruct((TPROJ_ROWS, D_MODEL), bf16),
                   jax.ShapeDtypeStruct((D_MODEL, D_MODEL), bf16)],
        compiler_params=pltpu.CompilerParams(
            dimension_semantics=("arbitrary",), vmem_limit_bytes=VMEM_LIMIT),
        name="prep_weights",
    )(w_in.T, w_out)


ST_ROWS = DV + 2 * SUBLANES


def _lane_scan(x, seg, op, fill):
    pos = lax.broadcasted_iota(jnp.int32, x.shape, 1) & (seg - 1)
    k = 1
    while k < seg:
        x = op(x, jnp.where(pos >= k, pltpu.roll(x, k, 1), fill))
        k *= 2
    return x


PROJ_PIECE = 256


def _run(*phases):
    live = list(phases)
    while live:
        for g in list(live):
            try:
                next(g)
            except StopIteration:
                live.remove(g)


def _mixer_prompt_kernel(x_ref, gmix_ref, wn_ref, wt_ref, gbt_ref, gmhc_ref, cw_ref, wout_ref,
                         x1_ref, c_ref, n_ref, m_ref, cvo_ref,
                         qt_s, vt_s, ogt_s, k_s, mixt_s, z_s, ucol_s, wi_s, c2_s, emt_s, wk_s, ast_s, st_s,
                         *, L, TB, NSUB):
    j = pl.program_id(1)

    @pl.when(j == 0)
    def _():
        z_s[0, 0:SUBLANES, :] = jnp.zeros((SUBLANES, CONV_CH), jnp.float32)
        st_s[...] = jnp.zeros(st_s.shape, jnp.float32)
        m_ref[...] = jnp.zeros(m_ref.shape, jnp.float32)

    m_prev = m_ref[0]
    sub = [dict(sb=sb, x_ref=x_ref, gmix_ref=gmix_ref, wn_ref=wn_ref, wt_ref=wt_ref, gbt_ref=gbt_ref,
                gmhc_ref=gmhc_ref, cw_ref=cw_ref, wout_ref=wout_ref, x1_ref=x1_ref, qt_s=qt_s.at[sb],
                vt_s=vt_s.at[sb], ogt_s=ogt_s.at[sb], k_s=k_s.at[sb], mixt_s=mixt_s.at[sb], z_s=z_s,
                ucol_s=ucol_s.at[sb], wi_s=wi_s.at[sb], c2_s=c2_s.at[sb], emt_s=emt_s.at[sb],
                wk_s=wk_s.at[sb], ast_s=ast_s.at[sb], st_s=st_s, L=L, TS=TB) for sb in range(NSUB)]
    carry = {"m": m_prev}
    _run(_prompt_project(sub[0]))
    for a, b in zip(sub[:-1], sub[1:]):
        _run(_prompt_chunks(a, carry), _prompt_project(b))
    _run(_prompt_chunks(sub[-1], carry), *[_prompt_output(s) for s in sub[:-1]])
    _run(_prompt_output(sub[-1]))
    m_ref[0] = carry["m"]
    cvo_ref[0] = z_s[NSUB - 1, TB + SUBLANES - (CONV_W - 1):TB + SUBLANES, :]
    z_s[0, 0:SUBLANES, :] = z_s[NSUB - 1, TB:TB + SUBLANES, :]

    @pl.when(j == pl.num_programs(1) - 1)
    def _():
        for pr in range(NH // 2):
            state = st_s[pr]
            c_pair = state[0:DV, :].T
            c_ref[0, 2 * pr] = c_pair[0:DK, :]
            c_ref[0, 2 * pr + 1] = c_pair[DK:, :]
            n_ref[0, pr:pr + 1, :] = state[DV:DV + 1, :]


def _prompt_project(s):
    sb, L, TS = s["sb"], s["L"], s["TS"]
    x_ref, gmix_ref, wn_ref, wt_ref, gbt_ref, cw_ref = (
        s["x_ref"], s["gmix_ref"], s["wn_ref"], s["wt_ref"], s["gbt_ref"], s["cw_ref"])
    qt_s, vt_s, ogt_s, k_s, z_s = s["qt_s"], s["vt_s"], s["ogt_s"], s["k_s"], s["z_s"]
    bf16 = jnp.bfloat16

    x = x_ref[0, sb * TS:(sb + 1) * TS, :]
    a = _rms(x, gmix_ref[...]).astype(bf16)
    nt_dims = (((1,), (1,)), ((), ()))

    gt = lax.dot_general(wt_ref[_TG0:TPROJ_ROWS, :], a, nt_dims,
                         preferred_element_type=jnp.float32) + gbt_ref[...]
    ic = GATE_CAP * jnp.tanh(gt[0:SUBLANES] / GATE_CAP)
    lf = _log_sigmoid(gt[SUBLANES:])
    b = _lane_scan(lf, L, jnp.add, 0.0)
    u = ic - b
    m_loc = b + _lane_scan(u, L, jnp.maximum, -jnp.inf)
    s.update(b=b, u=u, ic=ic, m_loc=m_loc)
    yield

    k_s[...] = _bdot(a, wn_ref[:, _K0:_V0])
    yield
    hc = _bdot(a, wn_ref[:, _HC0:_GT0])
    yield
    z = _bdot(a, wn_ref[:, _CG0:_HC0]) * hc
    if sb > 0:
        z_s[sb, 0:SUBLANES, :] = z_s[sb - 1, TS:TS + SUBLANES, :]
    z_s[sb, SUBLANES:SUBLANES + TS, :] = z
    yield
    yconv = (cw_ref[0:1, :] * z_s[sb, SUBLANES - 2:SUBLANES - 2 + TS, :]
             + cw_ref[1:2, :] * z_s[sb, SUBLANES - 1:SUBLANES - 1 + TS, :]
             + cw_ref[2:3, :] * z)
    s["yc"] = (_bdot(a, wn_ref[:, _BG0:_CG0]) * yconv).astype(bf16)
    yield

    for r0 in range(0, _TG0, PROJ_PIECE):
        ut = lax.dot_general(wt_ref[r0:r0 + PROJ_PIECE, :], a, nt_dims, preferred_element_type=jnp.float32)
        if r0 < _TV0:
            qt_s[r0:r0 + PROJ_PIECE, :] = ut * (DK ** -0.5)
        elif r0 < _TOG0:
            vt_s[r0 - _TV0:r0 - _TV0 + PROJ_PIECE, :] = ut
        else:
            ogt_s[r0 - _TOG0:r0 - _TOG0 + PROJ_PIECE, :] = ut
        yield


def _prompt_chunks(s, carry):
    L, TS = s["L"], s["TS"]
    qt_s, vt_s, ogt_s, k_s, mixt_s, ucol_s, wi_s, c2_s, emt_s, wk_s, ast_s, st_s = (
        s["qt_s"], s["vt_s"], s["ogt_s"], s["k_s"], s["mixt_s"], s["ucol_s"], s["wi_s"],
        s["c2_s"], s["emt_s"], s["wk_s"], s["ast_s"], s["st_s"])
    b, u, ic, m_loc = s["b"], s["u"], s["ic"], s["m_loc"]
    nc = TS // L
    bf16 = jnp.bfloat16
    m_prev = carry["m"]

    for c in range(nc):
        sl = slice(c * L, (c + 1) * L)
        bc, mlc = b[:, sl], m_loc[:, sl]
        b_last = jnp.broadcast_to(bc[:, L - 1:L], bc.shape)
        m_new = jnp.maximum(b_last + m_prev, jnp.broadcast_to(mlc[:, L - 1:L], bc.shape))
        g = bc + m_prev
        mt = jnp.maximum(g, mlc)
        wi_s[:, sl] = jnp.exp(g - mt)
        c2_s[:, sl] = mt - bc
        emt_s[:, sl] = jnp.exp(-mt)
        wk_s[:, sl] = jnp.exp(b_last - bc + ic[:, sl] - m_new)
        ast_s[c] = jnp.exp(b_last + m_prev - m_new)
        upad = jnp.concatenate([u[:, sl], jnp.zeros((L - SUBLANES, L), jnp.float32)], axis=0)
        ucol_s[c * L:(c + 1) * L, :] = upad.T
        m_prev = m_new
    carry["m"] = m_prev
    yield

    s_i = lax.broadcasted_iota(jnp.int32, (L, L), 0)
    t_i = lax.broadcasted_iota(jnp.int32, (L, L), 1)
    causal = s_i <= t_i
    low_half = lax.broadcasted_iota(jnp.int32, (L, LANES), 1) < DK
    zeros_q = jnp.zeros((DK, L), jnp.float32)
    zeros_p = jnp.zeros((L, L), bf16)

    def pair_row(ref, h0, rows):
        return jnp.concatenate([ref[h0:h0 + 1, rows], ref[h0 + 1:h0 + 2, rows]], axis=1)

    for c in range(nc):
        rows = slice(c * L, (c + 1) * L)
        for pr in range(NH // 2):
            h0 = 2 * pr
            hv0 = slice(h0 * DV, (h0 + 1) * DV)
            hv1 = slice((h0 + 1) * DV, (h0 + 2) * DV)
            kp = k_s[rows, pr * LANES:(pr + 1) * LANES]
            q0 = qt_s[h0 * DK:(h0 + 1) * DK, rows]
            q1 = qt_s[(h0 + 1) * DK:(h0 + 2) * DK, rows]
            qbd = jnp.concatenate([jnp.concatenate([q0, zeros_q], axis=1),
                                   jnp.concatenate([zeros_q, q1], axis=1)], axis=0).astype(bf16)
            st = _bdot(kp.astype(bf16), qbd)
            arg = jnp.concatenate(
                [jnp.where(causal, ucol_s[rows, h:h + 1] - c2_s[h:h + 1, rows], -jnp.inf)
                 for h in (h0, h0 + 1)], axis=1)
            pt = st * jnp.exp(arg)
            rs = jnp.sum(pt, axis=0, keepdims=True)
            ptb = pt.astype(bf16)
            pbd = jnp.concatenate([jnp.concatenate([ptb[:, 0:L], zeros_p], axis=1),
                                   jnp.concatenate([zeros_p, ptb[:, L:]], axis=1)], axis=0)
            vt = jnp.concatenate([vt_s[hv0, rows], vt_s[hv1, rows]], axis=1)
            state = st_s[pr]
            sq = _bdot(state.astype(bf16), qbd) * pair_row(wi_s, h0, rows)
            num = _bdot(vt.astype(bf16), pbd) + sq[0:DV]
            den = sq[DV:DV + 1] + rs
            hh = num * (1.0 / jnp.maximum(jnp.abs(den), pair_row(emt_s, h0, rows)))
            hn = hh * lax.rsqrt(jnp.mean(hh * hh, axis=0, keepdims=True) + EPS)
            mixt_s[hv0, rows] = jax.nn.sigmoid(ogt_s[hv0, rows]) * hn[:, 0:L]
            mixt_s[hv1, rows] = jax.nn.sigmoid(ogt_s[hv1, rows]) * hn[:, L:]
            wkr = pair_row(wk_s, h0, rows)
            vw = jnp.concatenate([vt * wkr, jnp.broadcast_to(wkr, (2 * SUBLANES, 2 * L))], axis=0)
            km = jnp.concatenate([jnp.where(low_half, kp, 0.0), jnp.where(low_half, 0.0, kp)], axis=0)
            decay = jnp.where(low_half[0:1, :], ast_s[c][h0:h0 + 1, :], ast_s[c][h0 + 1:h0 + 2, :])
            st_s[pr] = decay * state + _bdot(vw.astype(bf16), km.astype(bf16))
            yield


def _prompt_output(s):
    sb, TS = s["sb"], s["TS"]
    x_ref, wout_ref, x1_ref, mixt_s = s["x_ref"], s["wout_ref"], s["x1_ref"], s["mixt_s"]
    tok = slice(sb * TS, (sb + 1) * TS)
    hm = mixt_s[...].T * s["gmhc_ref"][...]
    mix = jnp.concatenate([hm.astype(jnp.bfloat16), s["yc"]], axis=1)
    yield
    for c0 in range(0, D_MODEL, PROJ_PIECE):
        cols = slice(c0, c0 + PROJ_PIECE)
        x1_ref[0, tok, cols] = x_ref[0, tok, cols] + _bdot(mix, wout_ref[:, cols])
        yield


def _mixer_prompt(x, weights, TB, NSUB, L):
    B, S, _ = x.shape
    TS = TB * NSUB
    nb = S // TS
    f32 = jnp.float32
    kern = functools.partial(_mixer_prompt_kernel, L=L, TB=TB, NSUB=NSUB)
    return pl.pallas_call(
        kern,
        grid=(B, nb),
        in_specs=[
            pl.BlockSpec((1, TS, D_MODEL), lambda b, j: (b, j, 0)),
            _const_spec((1, D_MODEL)),
            _const_spec((D_MODEL, PROJ_COLS)),
            _const_spec((TPROJ_ROWS, D_MODEL)),
            _const_spec((2 * SUBLANES, 1)),
            _const_spec((1, MLSTM_W)),
            _const_spec((CONV_W, CONV_CH)),
            _const_spec((D_MODEL, D_MODEL)),
        ],
        out_specs=[
            pl.BlockSpec((1, TS, D_MODEL), lambda b, j: (b, j, 0)),
            pl.BlockSpec((1, NH, DK, DV), lambda b, j: (b, 0, 0, 0)),
            pl.BlockSpec((1, NH // 2, 2 * DK), lambda b, j: (b, 0, 0)),
            pl.BlockSpec((1, SUBLANES, LANES), lambda b, j: (b, 0, 0)),
            pl.BlockSpec((1, CONV_W - 1, CONV_CH), lambda b, j: (b, 0, 0)),
        ],
        out_shape=[
            jax.ShapeDtypeStruct((B, S, D_MODEL), f32),
            jax.ShapeDtypeStruct((B, NH, DK, DV), f32),
            jax.ShapeDtypeStruct((B, NH // 2, 2 * DK), f32),
            jax.ShapeDtypeStruct((B, SUBLANES, LANES), f32),
            jax.ShapeDtypeStruct((B, CONV_W - 1, CONV_CH), f32),
        ],
        scratch_shapes=[
            pltpu.VMEM((NSUB, QK_W, TB), f32),
            pltpu.VMEM((NSUB, MLSTM_W, TB), f32),
            pltpu.VMEM((NSUB, MLSTM_W, TB), f32),
            pltpu.VMEM((NSUB, TB, QK_W), f32),
            pltpu.VMEM((NSUB, MLSTM_W, TB), f32),
            pltpu.VMEM((NSUB, TB + SUBLANES, CONV_CH), f32),
            pltpu.VMEM((NSUB, TB, LANES), f32),
            pltpu.VMEM((NSUB, SUBLANES, TB), f32),
            pltpu.VMEM((NSUB, SUBLANES, TB), f32),
            pltpu.VMEM((NSUB, SUBLANES, TB), f32),
            pltpu.VMEM((NSUB, SUBLANES, TB), f32),
            pltpu.VMEM((NSUB, TB // L, SUBLANES, LANES), f32),
            pltpu.VMEM((NH // 2, ST_ROWS, LANES), f32),
        ],
        compiler_params=pltpu.CompilerParams(
            dimension_semantics=("arbitrary", "arbitrary"), vmem_limit_bytes=VMEM_LIMIT),
        name="mixer_prompt",
    )(x, *weights)


def _row_scan(x, seg, op, fill):
    pos = lax.broadcasted_iota(jnp.int32, x.shape, 0) & (seg - 1)
    k = 1
    while k < seg:
        x = op(x, jnp.where(pos >= k, pltpu.roll(x, k, 0), fill))
        k *= 2
    return x


def _split_dot(x, sel, parts):
    acc = None
    rem = x
    for p in range(parts):
        hi = rem.astype(jnp.bfloat16)
        d = _bdot(hi, sel)
        acc = d if acc is None else acc + d
        if p + 1 < parts:
            rem = rem - hi.astype(jnp.float32)
    return acc


def _selectors():
    h = np.arange(NH)
    seg_qk = np.zeros((QK_W, LANES), np.float32)
    seg_qk[np.arange(QK_W), np.arange(QK_W) // DK] = 1.0
    exp_v = np.zeros((LANES, MLSTM_W), np.float32)
    exp_k = np.zeros((LANES, QK_W), np.float32)
    for i in h:
        exp_v[i, i * DV:(i + 1) * DV] = 1.0
        exp_k[i, i * DK:(i + 1) * DK] = 1.0
    mean_v = np.kron(np.eye(NH, dtype=np.float32), np.full((DV, DV), 1.0 / DV, np.float32))
    return tuple(jnp.asarray(m, jnp.bfloat16) for m in (seg_qk, exp_v, exp_k, mean_v))


def _mixer_sample_kernel(x_ref, cv_ref, c0_ref, n0_ref, m0_ref,
                         gmix_ref, win_ref, gb_ref, gmh_ref, cw_ref, wout_ref,
                         segqk_ref, expv_ref, expk_ref, meanv_ref,
                         x1_ref, cst_ref, nst_ref, mst_ref, cvo_ref,
                         xp_s, z_s, mp_s, qc_s, *, NSEQ):
    R = SUBLANES
    TS = NSEQ * R
    bf16 = jnp.bfloat16
    f32 = jnp.float32

    xp_s[:, SAMPLE_T:, :] = jnp.zeros((NSEQ, R - SAMPLE_T, D_MODEL), f32)
    xp_s[:, 0:SAMPLE_T, :] = x_ref[...]
    x = xp_s[...].reshape(TS, D_MODEL)
    a = _rms(x, gmix_ref[...]).astype(bf16)
    u = _bdot(a, win_ref[...])

    pos = lax.broadcasted_iota(jnp.int32, (TS, 1), 0) & (R - 1)
    real = pos < SAMPLE_T

    z = u[:, _CG0:_HC0] * u[:, _HC0:_GT0]
    z3 = z.reshape(NSEQ, R, CONV_CH)
    z_s[...] = z3
    z_s[:, R - (CONV_W - 1):, :] = cv_ref[...]
    zf = z_s[...].reshape(TS, CONV_CH)
    zm1 = jnp.where(pos >= 1, pltpu.roll(z, 1, 0), pltpu.roll(zf, TS - (R - 1), 0))
    zm2 = jnp.where(pos >= 2, pltpu.roll(z, 2, 0), pltpu.roll(zf, TS - (R - 2), 0))
    yc = u[:, _BG0:_CG0] * (cw_ref[0:1, :] * zm2 + cw_ref[1:2, :] * zm1 + cw_ref[2:3, :] * z)
    cvo_ref[...] = z3[:, SAMPLE_T - (CONV_W - 1):SAMPLE_T, :]

    gates = u[:, _GT0:PROJ_COLS] + gb_ref[...]
    ic = jnp.where(real, GATE_CAP * jnp.tanh(gates[:, :LANES] / GATE_CAP), NEG_BIG)
    lf = jnp.where(real, _log_sigmoid(gates[:, LANES:]), 0.0)
    b = _row_scan(lf, R, jnp.add, 0.0)
    uu = ic - b
    m_loc = b + _row_scan(uu, R, jnp.maximum, -jnp.inf)

    mp_s[...] = jnp.zeros(mp_s.shape, f32)
    mp_s[:, :, 0:NH] = m0_ref[...]
    m_prev = mp_s[...]
    b3 = b.reshape(NSEQ, R, LANES)
    ml3 = m_loc.reshape(NSEQ, R, LANES)
    b_last = b3[:, R - 1:R, :]
    m_new = jnp.maximum(b_last + m_prev, ml3[:, R - 1:R, :])
    g3 = b3 + m_prev
    mt3 = jnp.maximum(g3, ml3)
    wi = jnp.exp(g3 - mt3).reshape(TS, LANES)
    c2 = (mt3 - b3).reshape(TS, LANES)
    emt = jnp.exp(-mt3).reshape(TS, LANES)
    wk = jnp.exp(b_last - b3 + ic.reshape(NSEQ, R, LANES) - m_new).reshape(TS, LANES)
    a_st = jnp.broadcast_to(jnp.exp(b_last + m_prev - m_new), (NSEQ, R, LANES)).reshape(TS, LANES)
    mst_ref[...] = m_new[:, :, 0:NH]

    q = u[:, _Q0:_K0] * (DK ** -0.5)
    k = u[:, _K0:_V0]
    v = u[:, _V0:_OG0]
    acc = {"rs": jnp.zeros((TS, LANES), f32), "num": jnp.zeros((TS, MLSTM_W), f32)}

    def pair_step(d):
        kd = k if d == 0 else pltpu.roll(k, d, 0)
        vd = v if d == 0 else pltpu.roll(v, d, 0)
        ud = uu if d == 0 else pltpu.roll(uu, d, 0)
        p = _split_dot(q * kd, segqk_ref[...], 2) * jnp.exp(ud - c2)
        acc["rs"] = acc["rs"] + p
        acc["num"] = acc["num"] + _split_dot(p, expv_ref[...], 2) * vd

    n0x = jnp.broadcast_to(n0_ref[...], (NSEQ, R, QK_W)).reshape(TS, QK_W)
    qn = _split_dot(q * n0x, segqk_ref[...], 2)
    kw = k * _split_dot(wk, expk_ref[...], 2)
    ax = _split_dot(a_st, expv_ref[...], 3)
    lane_q = lax.broadcasted_iota(jnp.int32, (R, QK_W), 1)
    group = NSEQ // SAMPLE_T
    for i in range(NSEQ):
        if i % group == 0:
            pair_step(i // group)
        rows = slice(i * R, (i + 1) * R)
        qi = q[rows, :]
        lhs = jnp.concatenate(
            [jnp.where((lane_q >= h * DK) & (lane_q < (h + 1) * DK), qi, 0.0) for h in range(NH)], axis=0)
        c0 = c0_ref[i]
        qc = _bdot(lhs.astype(bf16), c0.reshape(QK_W, DV).astype(bf16))
        qc_s[rows, :] = jnp.concatenate([qc[h * R:(h + 1) * R, :] for h in range(NH)], axis=1)
        kwi = kw[rows, :].astype(bf16)
        vi = v[rows, :].astype(bf16)
        for h in range(NH):
            dc = lax.dot_general(kwi[:, h * DK:(h + 1) * DK], vi[:, h * DV:(h + 1) * DV],
                                 (((0,), (0,)), ((), ())), preferred_element_type=f32)
            cst_ref[i, h] = ax[i * R:i * R + 1, h * DV:(h + 1) * DV] * c0[h] + dc

    rs, num = acc["rs"], acc["num"]
    den = wi * qn + rs
    rden = 1.0 / jnp.maximum(jnp.abs(den), emt)
    hh = (_split_dot(wi, expv_ref[...], 2) * qc_s[...] + num) * _split_dot(rden, expv_ref[...], 2)
    ms = _split_dot(hh * hh, meanv_ref[...], 2)
    hm = jax.nn.sigmoid(u[:, _OG0:_BG0]) * (hh * lax.rsqrt(ms + EPS) * gmh_ref[...])
    mix = jnp.concatenate([hm, yc], axis=1).astype(bf16)
    out = x + _bdot(mix, wout_ref[...])
    x1_ref[...] = out.reshape(NSEQ, R, D_MODEL)[:, 0:SAMPLE_T, :]

    a_k = _split_dot(a_st, expk_ref[...], 3).reshape(NSEQ, R, QK_W)[:, 0:1, :]
    nst_ref[...] = a_k * n0_ref[...] + jnp.sum(kw.reshape(NSEQ, R, QK_W), axis=1, keepdims=True)


def _mixer_sample(x, cv, c0, n0, m0, weights, NSEQ):
    Bs = x.shape[0]
    f32 = jnp.float32
    kern = functools.partial(_mixer_sample_kernel, NSEQ=NSEQ)
    x_spec = pl.BlockSpec((NSEQ, SAMPLE_T, D_MODEL), lambda t: (t, 0, 0))
    cv_spec = pl.BlockSpec((NSEQ, CONV_W - 1, CONV_CH), lambda t: (t, 0, 0))
    c_spec = pl.BlockSpec((NSEQ, NH, DK, DV), lambda t: (t, 0, 0, 0))
    n_spec = pl.BlockSpec((NSEQ, 1, QK_W), lambda t: (t, 0, 0))
    m_spec = pl.BlockSpec((NSEQ, 1, NH), lambda t: (t, 0, 0))
    return pl.pallas_call(
        kern,
        grid=(Bs // NSEQ,),
        in_specs=[
            x_spec, cv_spec, c_spec, n_spec, m_spec,
            _const_spec((1, D_MODEL)),
            _const_spec((D_MODEL, PROJ_COLS)),
            _const_spec((1, 2 * LANES)),
            _const_spec((1, MLSTM_W)),
            _const_spec((CONV_W, CONV_CH)),
            _const_spec((D_MODEL, D_MODEL)),
            _const_spec((QK_W, LANES)),
            _const_spec((LANES, MLSTM_W)),
            _const_spec((LANES, QK_W)),
            _const_spec((MLSTM_W, MLSTM_W)),
        ],
        out_specs=[x_spec, c_spec, n_spec, m_spec, cv_spec],
        out_shape=[
            jax.ShapeDtypeStruct((Bs, SAMPLE_T, D_MODEL), f32),
            jax.ShapeDtypeStruct((Bs, NH, DK, DV), f32),
            jax.ShapeDtypeStruct((Bs, 1, QK_W), f32),
            jax.ShapeDtypeStruct((Bs, 1, NH), f32),
            jax.ShapeDtypeStruct((Bs, CONV_W - 1, CONV_CH), f32),
        ],
        scratch_shapes=[
            pltpu.VMEM((NSEQ, SUBLANES, D_MODEL), f32),
            pltpu.VMEM((NSEQ, SUBLANES, CONV_CH), f32),
            pltpu.VMEM((NSEQ, 1, LANES), f32),
            pltpu.VMEM((NSEQ * SUBLANES, MLSTM_W), f32),
        ],
        compiler_params=pltpu.CompilerParams(
            dimension_semantics=("arbitrary",), vmem_limit_bytes=VMEM_LIMIT),
        name="mixer_sample",
    )(x, cv, c0, n0, m0, *weights)


def _mixer_weights(g_mix, w_in, b_ig, b_fg, g_mh, conv_w, w_out):
    f32 = jnp.float32
    w_all, wt, wout = _prep_weights(w_in, w_out)
    gmix = g_mix.reshape(1, D_MODEL)
    zeros = jnp.zeros((SUBLANES - NH,), f32)
    gb_col = jnp.concatenate([b_ig, zeros, b_fg, zeros]).reshape(2 * SUBLANES, 1)
    zeros = jnp.zeros((LANES - NH,), f32)
    gb_row = jnp.concatenate([b_ig, zeros, b_fg, zeros]).reshape(1, 2 * LANES)
    gmh = g_mh.reshape(1, MLSTM_W)
    prompt = (gmix, w_all, wt, gb_col, gmh, conv_w, wout)
    sample = (gmix, w_all, gb_row, gmh, conv_w, wout) + _selectors()
    return prompt, sample


FF_STREAM = 256
_N_FF_PIECES = D_FF // FF_STREAM
_N_PIECES = _N_FF_PIECES + PLE_DIM // FF_STREAM + D_MODEL // FF_STREAM


def _ffn_rows(x, pe, gffn_ref, gple_ref, gfin_ref, wg_s, wu_s, wd_s, wple_s, wpg_s, *, chunk, final_norm,
              before_chunk=None, before_tail=None):
    bf16 = jnp.bfloat16
    f = _rms(x, gffn_ref[...]).astype(bf16)
    e = None
    if before_tail is None:
        e = _rms(_bdot(pe.astype(bf16), wple_s[...]), gple_ref[...])
    for c0 in range(0, D_FF, chunk):
        c1 = min(c0 + chunk, D_FF)
        if before_chunk is not None:
            before_chunk(c0 // chunk)
        gate = _bdot(f, wg_s[:, c0:c1])
        up = _bdot(f, wu_s[:, c0:c1])
        hmid = (gate * jax.nn.sigmoid(gate) * up).astype(bf16)
        x = x + _bdot(hmid, wd_s[c0:c1, :])
    if before_tail is not None:
        before_tail()
        e = _rms(_bdot(pe.astype(bf16), wple_s[...]), gple_ref[...])
    xb = x.astype(bf16)
    n_rows = x.shape[0]
    outs = []
    for r0 in range(0, n_rows, n_rows // FFN_TAIL_PIECES):
        rows = slice(r0, r0 + n_rows // FFN_TAIL_PIECES)
        xr = x[rows] + jax.nn.sigmoid(_bdot(xb[rows], wpg_s[...])) * e[rows]
        outs.append(_rms(xr, gfin_ref[...]) if final_norm else xr)
    return jnp.concatenate(outs, axis=0)


def _ffn_stream_kernel(xp_ref, pp_ref, xs_ref, ps_ref, gffn_ref, gple_ref, gfin_ref,
                       wg_hbm, wu_hbm, wd_hbm, wple_hbm, wpg_hbm,
                       yp_ref, ys_ref,
                       wg_s, wu_s, wd_s, wple_s, wpg_s, stg_g, stg_u, stg_d, sem, xring_s, xsem,
                       *, final_norm):
    t = pl.program_id(0)
    bf16 = jnp.bfloat16
    weights = (wg_s, wu_s, wd_s, wple_s, wpg_s)

    def copies(k):
        slot = k % 2
        if k < _N_FF_PIECES:
            cols = pl.ds(k * FF_STREAM, FF_STREAM)
            return [pltpu.make_async_copy(wg_hbm.at[:, cols], stg_g.at[slot], sem.at[0, slot]),
                    pltpu.make_async_copy(wu_hbm.at[:, cols], stg_u.at[slot], sem.at[1, slot]),
                    pltpu.make_async_copy(wd_hbm.at[cols, :], stg_d.at[slot], sem.at[2, slot])]
        if k == _N_FF_PIECES:
            return [pltpu.make_async_copy(wple_hbm, stg_d.at[slot], sem.at[2, slot])]
        rows = pl.ds((k - _N_FF_PIECES - 1) * FF_STREAM, FF_STREAM)
        return [pltpu.make_async_copy(wpg_hbm.at[rows, :], stg_d.at[slot], sem.at[2, slot])]

    def land(k):
        if k + 1 < _N_PIECES:
            for cp in copies(k + 1):
                cp.start()
        for cp in copies(k):
            cp.wait()
        slot = k % 2
        if k < _N_FF_PIECES:
            cols = slice(k * FF_STREAM, (k + 1) * FF_STREAM)
            wg_s[:, cols] = stg_g[slot].astype(bf16)
            wu_s[:, cols] = stg_u[slot].astype(bf16)
            wd_s[cols, :] = stg_d[slot].astype(bf16)
        elif k == _N_FF_PIECES:
            wple_s[...] = stg_d[slot].astype(bf16)
        else:
            r0 = (k - _N_FF_PIECES - 1) * FF_STREAM
            wpg_s[r0:r0 + FF_STREAM, :] = stg_d[slot].astype(bf16)

    @pl.when(t == 0)
    def _():
        for cp in copies(0):
            cp.start()
        n_tok = xs_ref.shape[0] * xs_ref.shape[1]

        def tail():
            for k in range(_N_FF_PIECES, _N_PIECES):
                land(k)

        y = _ffn_rows(xs_ref[...].reshape(n_tok, D_MODEL), ps_ref[...].reshape(n_tok, PLE_DIM),
                      gffn_ref, gple_ref, gfin_ref, *weights, chunk=FF_STREAM, final_norm=final_norm,
                      before_chunk=land, before_tail=tail)
        ys_ref[...] = y.reshape(ys_ref.shape)

    tm = yp_ref.shape[0]
    n_blk = pl.num_programs(0) - 1

    def x_copy(b):
        slot = b % 3
        return pltpu.make_async_copy(xp_ref.at[pl.ds(b * tm, tm), :], xring_s.at[slot], xsem.at[slot])

    @pl.when(t == 0)
    def _():
        x_copy(0).start()
        x_copy(1).start()

    @pl.when(t > 0)
    def _():
        @pl.when(t + 1 < n_blk)
        def _():
            x_copy(t + 1).start()

        x_copy(t - 1).wait()
        yp_ref[...] = _ffn_rows(xring_s[(t - 1) % 3], pp_ref[...], gffn_ref, gple_ref, gfin_ref, *weights,
                                chunk=FF_CHUNK, final_norm=final_norm)


def _ffn_stream(xp, pp, xs, ps, g_ffn, g_ple, g_final, w_gate, w_up, w_down, w_ple, w_pg, TM, final_norm):
    T = xp.shape[0]
    assert T % TM == 0 and D_FF % FF_STREAM == 0 and D_MODEL % FF_STREAM == 0
    assert w_ple.shape == (FF_STREAM, D_MODEL)
    f32, bf16 = jnp.float32, jnp.bfloat16
    kern = functools.partial(_ffn_stream_kernel, final_norm=final_norm)
    row_map = lambda t: (jnp.maximum(t - 1, 0), 0)
    hbm = pl.BlockSpec(memory_space=pl.ANY)
    return pl.pallas_call(
        kern,
        grid=(T // TM + 1,),
        in_specs=[
            hbm,
            pl.BlockSpec((TM, PLE_DIM), row_map),
            _const_spec(xs.shape),
            _const_spec(ps.shape),
            _const_spec((1, D_MODEL)),
            _const_spec((1, D_MODEL)),
            _const_spec((1, D_MODEL)),
            hbm, hbm, hbm, hbm, hbm,
        ],
        out_specs=[pl.BlockSpec((TM, D_MODEL), row_map),
                   pl.BlockSpec(xs.shape, lambda t: (0, 0, 0))],
        out_shape=[jax.ShapeDtypeStruct(xp.shape, f32), jax.ShapeDtypeStruct(xs.shape, f32)],
        scratch_shapes=[
            pltpu.VMEM((D_MODEL, D_FF), bf16),
            pltpu.VMEM((D_MODEL, D_FF), bf16),
            pltpu.VMEM((D_FF, D_MODEL), bf16),
            pltpu.VMEM((PLE_DIM, D_MODEL), bf16),
            pltpu.VMEM((D_MODEL, D_MODEL), bf16),
            pltpu.VMEM((2, D_MODEL, FF_STREAM), f32),
            pltpu.VMEM((2, D_MODEL, FF_STREAM), f32),
            pltpu.VMEM((2, FF_STREAM, D_MODEL), f32),
            pltpu.SemaphoreType.DMA((3, 2)),
            pltpu.VMEM((3, TM, D_MODEL), f32),
            pltpu.SemaphoreType.DMA((3,)),
        ],
        compiler_params=pltpu.CompilerParams(
            dimension_semantics=("arbitrary",), vmem_limit_bytes=VMEM_LIMIT),
        name="ffn",
    )(xp, pp, xs, ps, g_ffn.reshape(1, D_MODEL), g_ple.reshape(1, D_MODEL), g_final.reshape(1, D_MODEL),
      w_gate, w_up, w_down, w_ple, w_pg)


def kernel(x_prompt, x_sample, p_prompt, p_sample, state_C, state_n, state_m, state_conv,
           g_mix, w_in, b_ig, b_fg, g_mh, conv_w, w_out, g_ffn, w_gate, w_up, w_down,
           w_ple, g_ple, w_pg, g_final):
    bf16 = jnp.bfloat16
    depth = g_mix.shape[0]
    B, S, _ = x_prompt.shape
    Bs, Ss, _ = x_sample.shape
    assert Ss == SAMPLE_T and S % PROMPT_TS == 0 and Bs % SAMPLE_NSEQ == 0

    xp = x_prompt
    xs = x_sample
    outs = [[] for _ in range(8)]
    for i in range(depth):
        last = i == depth - 1
        pw, sw = _mixer_weights(g_mix[i], w_in[i], b_ig[i], b_fg[i], g_mh[i], conv_w[i], w_out[i])
        x1p, cp, n_p, mp, cvp = _mixer_prompt(xp, pw, TB=PROMPT_TS, NSUB=PROMPT_NSUB, L=PROMPT_L)
        x1s, cs, n_s, ms, cvs = _mixer_sample(
            xs, state_conv[i], state_C[i], state_n[i].reshape(Bs, 1, QK_W),
            state_m[i].reshape(Bs, 1, NH), sw, SAMPLE_NSEQ)
        xp, xs = _ffn_stream(x1p.reshape(B * S, D_MODEL), p_prompt[i].reshape(B * S, PLE_DIM), x1s,
                             p_sample[i], g_ffn[i], g_ple[i], g_final, w_gate[i], w_up[i], w_down[i],
                             w_ple[i], w_pg[i], FFN_TM, last)
        xp = xp.reshape(B, S, D_MODEL)

        new = (cp, n_p.reshape(B, NH, DK), mp[:, :NH, 0], cvp,
               cs, n_s.reshape(Bs, NH, DK), ms.reshape(Bs, NH), cvs)
        for lst, v in zip(outs, new):
            lst.append(v)

    return (xp, xs) + tuple(jnp.stack(l) for l in outs)
```
